```python
import jax, jax.numpy as jnp
from jax import lax
import numpy as np

D_MODEL = 1024
BATCH = 32
SEQ = 2048
DEPTH = 1

ATT_HEADS = 8
ATT_HEAD_DIM = 64
ATT_WIDTH = ATT_HEADS * ATT_HEAD_DIM
Q_BLOCK = 128
FORGET_BIAS_INIT = 3.0
RNN_WIDTH = D_MODEL
RNN_BLOCKS = 16
RNN_BLOCK_DIM = RNN_WIDTH // RNN_BLOCKS
CONV_WIDTH = 4
RGLRU_C = 8.0
N_EXPERTS = 64
TOP_K = 6
N_GROUPS = 8
TOPK_GROUPS = 4
EXPERT_FF = D_MODEL // 4
SHARED_FF = EXPERT_FF
ROUTE_SCALE = 2.5
DISPATCH_ROWS = 256
NORM_EPS = 1e-6
IN_SPLITS = (ATT_WIDTH, ATT_WIDTH, ATT_WIDTH, ATT_HEADS, RNN_WIDTH, RNN_WIDTH, D_MODEL, D_MODEL)
IN_COLS = sum(IN_SPLITS)
IN_OFFSETS = tuple(int(o) for o in np.cumsum(IN_SPLITS)[:-1])

kernel_name = 'hybrid_fox_rglru_moe_block'


def rmsnorm(x, g):
    xf = x.astype(jnp.float32)
    y = xf * lax.rsqrt(jnp.mean(xf * xf, axis=-1, keepdims=True) + NORM_EPS)
    return (y * g.astype(jnp.float32)).astype(x.dtype)


def swiglu(x, w_gate, w_up, w_down):
    return (jax.nn.silu(x @ w_gate) * (x @ w_up)) @ w_down


def forgetting_attention(q, k, v, f_logit):
    S = q.shape[1]
    log_f = jax.nn.log_sigmoid(f_logit.astype(jnp.float32))
    cum = jnp.transpose(jnp.cumsum(log_f, axis=1), (0, 2, 1))
    scale = ATT_HEAD_DIM ** -0.5
    outs = []
    for i in range(S // Q_BLOCK):
        q0, q1 = i * Q_BLOCK, (i + 1) * Q_BLOCK
        kb, vb = k[:, :q1], v[:, :q1]
        s = jnp.einsum('bqhd,bkhd->bhqk', q[:, q0:q1], kb, preferred_element_type=jnp.float32) * scale
        s = s + cum[:, :, q0:q1, None] - cum[:, :, None, :q1]
        causal = jnp.arange(q1)[None, :] <= jnp.arange(q0, q1)[:, None]
        s = jnp.where(causal, s, -jnp.inf)
        p = jax.nn.softmax(s, axis=-1)
        outs.append(jnp.einsum('bhqk,bkhd->bqhd', p.astype(v.dtype), vb))
    return jnp.concatenate(outs, axis=1)


def causal_depthwise_conv(x, w, b):
    S = x.shape[1]
    xp = jnp.pad(x, ((0, 0), (CONV_WIDTH - 1, 0), (0, 0)))
    y = b
    for j in range(CONV_WIDTH):
        y = y + w[j] * xp[:, j:j + S]
    return y


def rg_lru(u, w_rg, b_rg, w_ig, b_ig, lam):
    B, S, _ = u.shape
    ub = u.reshape(B, S, RNN_BLOCKS, RNN_BLOCK_DIM)
    r = jax.nn.sigmoid((jnp.einsum('bsni,nij->bsnj', ub, w_rg).reshape(B, S, RNN_WIDTH) + b_rg).astype(jnp.float32))
    ig = jax.nn.sigmoid((jnp.einsum('bsni,nij->bsnj', ub, w_ig).reshape(B, S, RNN_WIDTH) + b_ig).astype(jnp.float32))
    log_a = -RGLRU_C * r * jax.nn.softplus(-lam.astype(jnp.float32))
    a = jnp.exp(log_a)
    mult = jnp.sqrt(-jnp.expm1(2.0 * log_a))
    b_in = mult * ig * u.astype(jnp.float32)

    def combine(left, right):
        a1, b1 = left
        a2, b2 = right
        return a1 * a2, a2 * b1 + b2

    _, h = lax.associative_scan(combine, (a, b_in), axis=1)
    return h.astype(u.dtype)


def route(h_flat, w_router, router_bias):
    T = h_flat.shape[0]
    scores = jax.nn.sigmoid(h_flat.astype(jnp.float32) @ w_router.astype(jnp.float32))
    sel = scores + router_bias.astype(jnp.float32)
    grp = sel.reshape(T, N_GROUPS, N_EXPERTS // N_GROUPS)
    grp_score = lax.top_k(grp, 2)[0].sum(-1)
    _, gidx = lax.top_k(grp_score, TOPK_GROUPS)
    gmask = jnp.any(gidx[..., None] == jnp.arange(N_GROUPS)[None, None, :], axis=1)
    emask = jnp.repeat(gmask, N_EXPERTS // N_GROUPS, axis=1)
    _, eidx = lax.top_k(jnp.where(emask, sel, -jnp.inf), TOP_K)
    w = jnp.take_along_axis(scores, eidx, axis=-1)
    w = w / jnp.sum(w, axis=-1, keepdims=True) * ROUTE_SCALE
    return eidx.astype(jnp.int32), w


def routed_experts(h_flat, eidx, ew, w_gate, w_up, w_down):
    T, D = h_flat.shape
    P = T * TOP_K
    C = DISPATCH_ROWS
    n_blocks = -(-P // C) + N_EXPERTS
    flat_e = eidx.reshape(-1)
    flat_tok = jnp.repeat(jnp.arange(T, dtype=jnp.int32), TOP_K)
    flat_w = ew.reshape(-1)
    order = jnp.argsort(flat_e)
    se, stok, sw = flat_e[order], flat_tok[order], flat_w[order]
    counts = jnp.bincount(flat_e, length=N_EXPERTS).astype(jnp.int32)
    start = jnp.cumsum(counts) - counts
    padded = ((counts + C - 1) // C) * C
    pend = jnp.cumsum(padded)
    pstart = pend - padded
    dest = pstart[se] + (jnp.arange(P, dtype=jnp.int32) - start[se])
    buf_tok = jnp.full((n_blocks * C,), T, jnp.int32).at[dest].set(stok)
    buf_w = jnp.zeros((n_blocks * C,), jnp.float32).at[dest].set(sw)
    block_e = jnp.minimum(jnp.searchsorted(pend, jnp.arange(n_blocks, dtype=jnp.int32) * C, side='right'),
                          N_EXPERTS - 1).astype(jnp.int32)
    h_pad = jnp.concatenate([h_flat, jnp.zeros((1, D), h_flat.dtype)], axis=0)

    def expert_block(args):
        tok, wt, e = args
        y = swiglu(h_pad[tok], w_gate[e], w_up[e], w_down[e])
        return y * wt[:, None].astype(y.dtype)

    y_buf = lax.map(expert_block, (buf_tok.reshape(n_blocks, C), buf_w.reshape(n_blocks, C), block_e))
    return jax.ops.segment_sum(y_buf.reshape(-1, D), buf_tok, num_segments=T + 1)[:T]


def setup_inputs(seed: int = 0) -> dict:
    key = jax.random.key(seed)
    ks = jax.random.split(key, 28)
    f32 = jnp.float32
    L, D = DEPTH, D_MODEL

    def nrm(k, shape, scale):
        return jax.random.normal(k, shape, f32) * scale

    a0 = jax.random.uniform(ks[16], (L, RNN_WIDTH), f32, 0.9, 0.999)
    return {
        'x': nrm(ks[0], (BATCH, SEQ, D), 1.0),
        'c': nrm(ks[1], (BATCH, D), 1.0),
        'w_ada': nrm(ks[2], (L, D, 6 * D), 0.5 * D ** -0.5),
        'b_ada': nrm(ks[3], (L, 6 * D), 0.02),
        'g_pre_mix': 1.0 + nrm(ks[4], (L, D), 0.1),
        'g_post_mix': 1.0 + nrm(ks[5], (L, D), 0.1),
        'g_pre_ffn': 1.0 + nrm(ks[6], (L, D), 0.1),
        'g_post_ffn': 1.0 + nrm(ks[7], (L, D), 0.1),
        'w_in': nrm(ks[8], (L, D, IN_COLS), D ** -0.5),
        'b_forget': FORGET_BIAS_INIT + nrm(ks[9], (L, ATT_HEADS), 0.5),
        'w_conv': nrm(ks[10], (L, CONV_WIDTH, RNN_WIDTH), CONV_WIDTH ** -0.5),
        'b_conv': nrm(ks[11], (L, RNN_WIDTH), 0.02),
        'w_rg': nrm(ks[12], (L, RNN_BLOCKS, RNN_BLOCK_DIM, RNN_BLOCK_DIM), RNN_BLOCK_DIM ** -0.5),
        'b_rg': nrm(ks[13], (L, RNN_WIDTH), 0.02),
        'w_ig': nrm(ks[14], (L, RNN_BLOCKS, RNN_BLOCK_DIM, RNN_BLOCK_DIM), RNN_BLOCK_DIM ** -0.5),
        'b_ig': nrm(ks[15], (L, RNN_WIDTH), 0.02),
        'rglru_lambda': jnp.log(a0) - jnp.log1p(-a0),
        'w_branch_attn': nrm(ks[17], (L, ATT_WIDTH, D), ATT_WIDTH ** -0.5),
        'w_branch_rnn': nrm(ks[18], (L, RNN_WIDTH, D), RNN_WIDTH ** -0.5),
        'w_out': nrm(ks[19], (L, D, D), D ** -0.5),
        'w_router': nrm(ks[20], (L, D, N_EXPERTS), D ** -0.5),
        'router_bias': nrm(ks[21], (L, N_EXPERTS), 0.01),
        'w_exp_gate': nrm(ks[22], (L, N_EXPERTS, D, EXPERT_FF), D ** -0.5),
        'w_exp_up': nrm(ks[23], (L, N_EXPERTS, D, EXPERT_FF), D ** -0.5),
        'w_exp_down': nrm(ks[24], (L, N_EXPERTS, EXPERT_FF, D), EXPERT_FF ** -0.5),
        'w_sh_gate': nrm(ks[25], (L, D, SHARED_FF), D ** -0.5),
        'w_sh_up': nrm(ks[26], (L, D, SHARED_FF), D ** -0.5),
        'w_sh_down': nrm(ks[27], (L, SHARED_FF, D), SHARED_FF ** -0.5),
    }


def reference(x, c, w_ada, b_ada, g_pre_mix, g_post_mix, g_pre_ffn, g_post_ffn, w_in, b_forget,
              w_conv, b_conv, w_rg, b_rg, w_ig, b_ig, rglru_lambda, w_branch_attn, w_branch_rnn, w_out,
              w_router, router_bias, w_exp_gate, w_exp_up, w_exp_down, w_sh_gate, w_sh_up, w_sh_down):
    B, S, _ = x.shape
    for l in range(DEPTH):
        mod = jax.nn.silu(c) @ w_ada[l] + b_ada[l]
        shift_m, scale_m, gate_m, shift_f, scale_f, gate_f = jnp.split(mod[:, None, :], 6, axis=-1)

        h = rmsnorm(x, g_pre_mix[l]) * (1.0 + scale_m) + shift_m
        proj = h @ w_in[l]
        q, k, v, f_logit, x_rnn, g_rnn, gate_a, gate_r = jnp.split(proj, IN_OFFSETS, axis=-1)
        y_attn = forgetting_attention(q.reshape(B, S, ATT_HEADS, ATT_HEAD_DIM),
                                      k.reshape(B, S, ATT_HEADS, ATT_HEAD_DIM),
                                      v.reshape(B, S, ATT_HEADS, ATT_HEAD_DIM),
                                      f_logit + b_forget[l]).reshape(B, S, ATT_WIDTH)
        u = causal_depthwise_conv(x_rnn, w_conv[l], b_conv[l])
        y_rnn = rg_lru(u, w_rg[l], b_rg[l], w_ig[l], b_ig[l], rglru_lambda[l]) * jax.nn.gelu(g_rnn)
        merged = (jax.nn.sigmoid(gate_a) * (y_attn @ w_branch_attn[l])
                  + jax.nn.sigmoid(gate_r) * (y_rnn @ w_branch_rnn[l]))
        x = x + gate_m * rmsnorm(merged @ w_out[l], g_post_mix[l])

        h = rmsnorm(x, g_pre_ffn[l]) * (1.0 + scale_f) + shift_f
        h_flat = h.reshape(B * S, D_MODEL)
        eidx, ew = route(h_flat, w_router[l], router_bias[l])
        routed = routed_experts(h_flat, eidx, ew, w_exp_gate[l], w_exp_up[l], w_exp_down[l])
        shared = swiglu(h_flat, w_sh_gate[l], w_sh_up[l], w_sh_down[l])
        x = x + gate_f * rmsnorm((routed + shared).reshape(B, S, D_MODEL), g_post_ffn[l])
    return x
```

```python
import functools

import jax
import jax.numpy as jnp
from jax import lax
from jax.experimental import pallas as pl
from jax.experimental.pallas import tpu as pltpu

F32 = jnp.float32
BF16 = jnp.bfloat16
I32 = jnp.int32

D_MODEL = 1024
ATT_HEADS = 8
ATT_HEAD_DIM = 64
ATT_WIDTH = ATT_HEADS * ATT_HEAD_DIM
RNN_WIDTH = D_MODEL
RNN_BLOCKS = 16
RNN_BLOCK_DIM = RNN_WIDTH // RNN_BLOCKS
CONV_WIDTH = 4
RGLRU_C = 8.0
N_EXPERTS = 64
TOP_K = 6
N_GROUPS = 8
GROUP_SIZE = N_EXPERTS // N_GROUPS
TOPK_GROUPS = 4
EXPERT_FF = D_MODEL // 4
ROUTE_SCALE = 2.5
NORM_EPS = 1e-6

LANES = 128
SUBLANES = 8
MXU_DIM = 256
VMEM_LIMIT_BYTES = 56 * 1024 * 1024

NEG_BIG = -1e30
RANK_BITS = 20
RANK_MASK = (1 << RANK_BITS) - 1


def _sigmoid(x):
    return 1.0 / (1.0 + jnp.exp(-x))


def _rms_scale(x):
    return x * lax.rsqrt(jnp.mean(x * x, axis=-1, keepdims=True) + NORM_EPS)


def _dot(a, b):
    return jnp.dot(a, b, preferred_element_type=F32)


def _dot_nt(a, b):
    return lax.dot_general(a, b, (((1,), (1,)), ((), ())), preferred_element_type=F32)


def _split3(a):
    hi = a.astype(BF16)
    r1 = a - hi.astype(F32)
    mid = r1.astype(BF16)
    lo = (r1 - mid.astype(F32)).astype(BF16)
    return hi, mid, lo


def _params(*sem):
    return pltpu.CompilerParams(dimension_semantics=sem, vmem_limit_bytes=VMEM_LIMIT_BYTES)


def _const_spec(shape):
    nd = len(shape)
    return pl.BlockSpec(shape, lambda *_: (0,) * nd)


def _ada_kernel(c_ref, w_ref, b_ref, o_ref):
    c = c_ref[...]
    sc = c * _sigmoid(c)
    s_hi, s_mid, _ = _split3(sc)
    w_hi, w_mid, _ = _split3(w_ref[...])
    acc = _dot(s_hi, w_hi) + _dot(s_hi, w_mid) + _dot(s_mid, w_hi)
    o_ref[...] = acc + b_ref[...]


def _ada_call(c, w_ada, b_ada):
    bsz, d = c.shape
    n = w_ada.shape[1]
    tn = d
    return pl.pallas_call(
        _ada_kernel,
        grid=(n // tn,),
        in_specs=[
            pl.BlockSpec((bsz, d), lambda j: (0, 0)),
            pl.BlockSpec((d, tn), lambda j: (0, j)),
            pl.BlockSpec((1, tn), lambda j: (0, j)),
        ],
        out_specs=pl.BlockSpec((bsz, tn), lambda j: (0, j)),
        out_shape=jax.ShapeDtypeStruct((bsz, n), F32),
        compiler_params=_params("arbitrary"),
        name="ada_mod",
    )(c, w_ada, b_ada.reshape(1, n))


def _inproj_kernel(x_ref, sh_ref, sc_ref, g_ref, wqkv_ref, wf_ref, wr_ref,
                   q_ref, k_ref, v_ref, f_ref, xr_ref, gr_ref, ga_ref, gb_ref):
    x = x_ref[0]
    h = (_rms_scale(x) * g_ref[...]) * (1.0 + sc_ref[0]) + sh_ref[0]
    hb = h.astype(BF16)
    for j, ref in enumerate((q_ref, k_ref, v_ref)):
        ref[0] = _dot(hb, wqkv_ref[:, j * ATT_WIDTH:(j + 1) * ATT_WIDTH]).astype(BF16)
    f_ref[0] = _dot(hb, wf_ref[...])
    half = D_MODEL // 2
    for j, ref in enumerate((xr_ref, gr_ref, ga_ref, gb_ref)):
        for c in range(2):
            lo = j * D_MODEL + c * half
            ref[0, :, c * half:(c + 1) * half] = _dot(hb, wr_ref[:, lo:lo + half]).astype(BF16)


def _inproj_call(x, shift, scale, g, wqkv, wf, wr, tm):
    bsz, s, d = x.shape
    nt = s // tm
    tok = lambda w: pl.BlockSpec((1, tm, w), lambda b, i: (b, i, 0))
    mod = pl.BlockSpec((1, 1, d), lambda b, i: (b, 0, 0))
    out_w = (ATT_WIDTH,) * 3 + (LANES,) + (d,) * 4
    out_dt = (BF16,) * 3 + (F32,) + (BF16,) * 4
    return pl.pallas_call(
        _inproj_kernel,
        grid=(bsz, nt),
        in_specs=[tok(d), mod, mod, _const_spec((1, d)),
                  _const_spec(wqkv.shape), _const_spec(wf.shape), _const_spec(wr.shape)],
        out_specs=[tok(w) for w in out_w],
        out_shape=[jax.ShapeDtypeStruct((bsz, s, w), dt) for w, dt in zip(out_w, out_dt)],
        compiler_params=_params("arbitrary", "arbitrary"),
        name="in_proj",
    )(x, shift, scale, g, wqkv, wf, wr)


def _cum_kernel(f_ref, b_ref, cum_ref, cumt_ref, *, blk):
    s = f_ref.shape[1]
    row = lax.broadcasted_iota(I32, (blk, blk), 0)
    col = lax.broadcasted_iota(I32, (blk, blk), 1)
    tri = (row >= col).astype(BF16)
    carry = jnp.zeros((1, LANES), F32)
    for i in range(s // blk):
        z = f_ref[0, i * blk:(i + 1) * blk, :] + b_ref[...]
        lf = jnp.minimum(z, 0.0) - jnp.log1p(jnp.exp(-jnp.abs(z)))
        hi, mid, lo = _split3(lf)
        c = _dot(tri, hi) + _dot(tri, mid) + _dot(tri, lo) + carry
        cum_ref[0, i * blk:(i + 1) * blk, :] = c
        cumt_ref[0, :, i * blk:(i + 1) * blk] = c.T[:SUBLANES, :]
        carry = c[blk - 1:blk, :]


def _cum_call(f_logit, b_forget_pad, blk):
    bsz, s, _ = f_logit.shape
    return pl.pallas_call(
        functools.partial(_cum_kernel, blk=blk),
        grid=(bsz,),
        in_specs=[pl.BlockSpec((1, s, LANES), lambda b: (b, 0, 0)), _const_spec((1, LANES))],
        out_specs=[pl.BlockSpec((1, s, LANES), lambda b: (b, 0, 0)),
                   pl.BlockSpec((1, SUBLANES, s), lambda b: (b, 0, 0))],
        out_shape=[jax.ShapeDtypeStruct((bsz, s, LANES), F32),
                   jax.ShapeDtypeStruct((bsz, SUBLANES, s), F32)],
        compiler_params=_params("arbitrary"),
        name="forget_cumsum",
    )(f_logit, b_forget_pad)


def _attn_kernel(q_ref, k_ref, v_ref, cum_ref, cumt_ref, o_ref, m_sc, l_sc, acc_sc, *, tq):
    s = q_ref.shape[1]
    nq = s // tq
    pair = pl.program_id(1)
    lane = lax.broadcasted_iota(I32, (1, LANES), 1)
    row = lax.broadcasted_iota(I32, (tq, tq), 0)
    col = lax.broadcasted_iota(I32, (tq, tq), 1)
    causal = col <= row
    scale = ATT_HEAD_DIM ** -0.5

    for hh in range(2):
        head = 2 * pair + hh
        lmask = (lane >= ATT_HEAD_DIM * hh) & (lane < ATT_HEAD_DIM * (hh + 1))

        def q_body(qi, carry, hh=hh, head=head, lmask=lmask):
            q0 = pl.multiple_of(qi * tq, tq)
            qb = q_ref[0, pl.ds(q0, tq), :]
            qm = jnp.where(lmask, qb, jnp.zeros_like(qb))
            cq = jnp.sum(jnp.where(lane == head, cum_ref[0, pl.ds(q0, tq), :], 0.0),
                         axis=-1, keepdims=True)
            m_sc[...] = jnp.full(m_sc.shape, NEG_BIG, F32)
            l_sc[...] = jnp.zeros(l_sc.shape, F32)
            acc_sc[...] = jnp.zeros(acc_sc.shape, F32)

            def kv_step(k0, diagonal):
                kb = k_ref[0, pl.ds(k0, tq), :]
                vb = v_ref[0, pl.ds(k0, tq), :]
                ck = cumt_ref[0, pl.ds(head, 1), pl.ds(k0, tq)]
                sc = _dot_nt(qm, kb) * scale + (cq - ck)
                if diagonal:
                    sc = jnp.where(causal, sc, NEG_BIG)
                m_old = m_sc[...]
                m_new = jnp.maximum(m_old, jnp.max(sc, axis=-1, keepdims=True))
                alpha = jnp.exp(m_old - m_new)
                p = jnp.exp(sc - m_new)
                l_sc[...] = alpha * l_sc[...] + jnp.sum(p, axis=-1, keepdims=True)
                acc_sc[...] = alpha * acc_sc[...] + _dot(p.astype(BF16), vb)
                m_sc[...] = m_new

            def kv_body(kj, c2):
                kv_step(pl.multiple_of(kj * tq, tq), False)
                return c2

            lax.fori_loop(0, qi, kv_body, 0)
            kv_step(q0, True)
            out = (acc_sc[...] / l_sc[...]).astype(BF16)
            if hh == 0:
                o_ref[0, pl.ds(q0, tq), :] = out
            else:
                cur = o_ref[0, pl.ds(q0, tq), :]
                o_ref[0, pl.ds(q0, tq), :] = jnp.where(lmask, out, cur)
            return carry

        lax.fori_loop(0, nq, q_body, 0)


def _attn_call(q, k, v, cum, cumt, tq):
    bsz, s, _ = q.shape
    qkv = pl.BlockSpec((1, s, LANES), lambda b, p: (b, 0, p))
    return pl.pallas_call(
        functools.partial(_attn_kernel, tq=tq),
        grid=(bsz, ATT_HEADS // 2),
        in_specs=[qkv, qkv, qkv,
                  pl.BlockSpec((1, s, LANES), lambda b, p: (b, 0, 0)),
                  pl.BlockSpec((1, SUBLANES, s), lambda b, p: (b, 0, 0))],
        out_specs=qkv,
        out_shape=jax.ShapeDtypeStruct((bsz, s, ATT_WIDTH), BF16),
        scratch_shapes=[pltpu.VMEM((tq, 1), F32), pltpu.VMEM((tq, 1), F32),
                        pltpu.VMEM((tq, LANES), F32)],
        compiler_params=_params("arbitrary", "arbitrary"),
        name="fox_attention",
    )(q, k, v, cum, cumt)


def _gelu_tanh(x):
    c = 0.7978845608028654
    return 0.5 * x * (1.0 + jnp.tanh(c * (x + 0.044715 * (x * x * x))))


def _rnn_kernel(xr_ref, gr_ref, wc_ref, bc_ref, wbd_ref, brg_ref, big_ref, lam_ref, o_ref,
                tail_sc, hc_sc, a_sc, b_sc, h_sc, *, ts):
    @pl.when(pl.program_id(1) == 0)
    def _():
        tail_sc[...] = jnp.zeros(tail_sc.shape, F32)
        hc_sc[...] = jnp.zeros(hc_sc.shape, F32)

    x = xr_ref[0].astype(F32)
    full = jnp.concatenate([tail_sc[...], x], axis=0)
    u = bc_ref[...] + wc_ref[3:4, :] * x
    for j in range(CONV_WIDTH - 1):
        off = SUBLANES - (CONV_WIDTH - 1) + j
        u = u + wc_ref[j:j + 1, :] * full[off:off + ts, :]
    tail_sc[...] = x[ts - SUBLANES:ts, :]

    lam = lam_ref[...]
    neg_sp = -(jnp.maximum(-lam, 0.0) + jnp.log1p(jnp.exp(-jnp.abs(lam))))
    for jt in range(RNN_WIDTH // MXU_DIM):
        cs = slice(jt * MXU_DIM, (jt + 1) * MXU_DIM)
        uc = u[:, cs]
        g = _dot(uc.astype(BF16), wbd_ref[jt])
        r = _sigmoid(g[:, :MXU_DIM] + brg_ref[:, cs])
        ig = _sigmoid(g[:, MXU_DIM:] + big_ref[:, cs])
        log_a = RGLRU_C * r * neg_sp[:, cs]
        a_sc[:, cs] = jnp.exp(log_a)
        b_sc[:, cs] = jnp.sqrt(1.0 - jnp.exp(2.0 * log_a)) * ig * uc

    srow = lax.broadcasted_iota(I32, (SUBLANES, RNN_WIDTH), 0)

    def scan_body(i, carry):
        r0 = pl.multiple_of(i * SUBLANES, SUBLANES)
        a = a_sc[pl.ds(r0, SUBLANES), :]
        b = b_sc[pl.ds(r0, SUBLANES), :]
        for sh in (1, 2, 4):
            keep = srow >= sh
            a_prev = pltpu.roll(a, sh, 0)
            b_prev = pltpu.roll(b, sh, 0)
            b = jnp.where(keep, b + a * b_prev, b)
            a = jnp.where(keep, a * a_prev, a)
        h = b + a * carry
        h_sc[pl.ds(r0, SUBLANES), :] = h
        return jnp.broadcast_to(h[SUBLANES - 1:SUBLANES, :], (SUBLANES, RNN_WIDTH))

    hc_sc[...] = lax.fori_loop(0, ts // SUBLANES, scan_body, hc_sc[...])
    o_ref[0] = (h_sc[...] * _gelu_tanh(gr_ref[0].astype(F32))).astype(BF16)


def _rnn_call(x_rnn, g_rnn, w_conv, b_conv, wbd, b_rg, b_ig, lam, ts):
    bsz, s, w = x_rnn.shape
    tok = pl.BlockSpec((1, ts, w), lambda b, i: (b, i, 0))
    return pl.pallas_call(
        functools.partial(_rnn_kernel, ts=ts),
        grid=(bsz, s // ts),
        in_specs=[tok, tok, _const_spec(w_conv.shape), _const_spec((1, w)), _const_spec(wbd.shape),
                  _const_spec((1, w)), _const_spec((1, w)), _const_spec((1, w))],
        out_specs=tok,
        out_shape=jax.ShapeDtypeStruct((bsz, s, w), BF16),
        scratch_shapes=[pltpu.VMEM((SUBLANES, w), F32), pltpu.VMEM((SUBLANES, w), F32),
                        pltpu.VMEM((ts, w), F32), pltpu.VMEM((ts, w), F32), pltpu.VMEM((ts, w), F32)],
        compiler_params=_params("arbitrary", "arbitrary"),
        name="rglru",
    )(x_rnn, g_rnn, w_conv, b_conv, wbd, b_rg, b_ig, lam)


def _route(h2, wr_hi_ref, wr_mid_ref, rbias_ref, cnt_ref, code_ref, wtok_ref, tm):
    h_hi, h_mid, _ = _split3(h2)
    logits = _dot_nt(wr_hi_ref[...], h_hi) + _dot_nt(wr_hi_ref[...], h_mid) + _dot_nt(wr_mid_ref[...], h_hi)
    scores = _sigmoid(logits)
    sel = scores + rbias_ref[...]

    giota = lax.broadcasted_iota(I32, (N_GROUPS, tm), 0)
    gs = jnp.zeros((N_GROUPS, tm), F32)
    for g in range(N_GROUPS):
        blk = sel[g * GROUP_SIZE:(g + 1) * GROUP_SIZE, :]
        m1 = jnp.max(blk, axis=0, keepdims=True)
        i1 = jnp.min(jnp.where(blk == m1, giota, N_EXPERTS), axis=0, keepdims=True)
        m2 = jnp.max(jnp.where(giota == i1, -jnp.inf, blk), axis=0, keepdims=True)
        gs = jnp.where(giota == g, m1 + m2, gs)

    gsel = jnp.zeros((N_GROUPS, tm), F32)
    for _ in range(TOPK_GROUPS):
        m = jnp.max(gs, axis=0, keepdims=True)
        idx = jnp.min(jnp.where(gs == m, giota, N_EXPERTS), axis=0, keepdims=True)
        hit = giota == idx
        gsel = jnp.where(hit, 1.0, gsel)
        gs = jnp.where(hit, -jnp.inf, gs)

    masked = jnp.concatenate(
        [jnp.where(gsel[g:g + 1, :] > 0.0, sel[g * GROUP_SIZE:(g + 1) * GROUP_SIZE, :], -jnp.inf)
         for g in range(N_GROUPS)], axis=0)

    eiota = lax.broadcasted_iota(I32, (N_EXPERTS, tm), 0)
    chosen = jnp.zeros((N_EXPERTS, tm), F32)
    idxs, wts = [], []
    for _ in range(TOP_K):
        m = jnp.max(masked, axis=0, keepdims=True)
        idx = jnp.min(jnp.where(masked == m, eiota, N_EXPERTS), axis=0, keepdims=True)
        hit = eiota == idx
        wts.append(jnp.sum(jnp.where(hit, scores, 0.0), axis=0, keepdims=True))
        idxs.append(idx)
        chosen = jnp.where(hit, 1.0, chosen)
        masked = jnp.where(hit, -jnp.inf, masked)
    wsum = wts[0]
    for w in wts[1:]:
        wsum = wsum + w

    srow = lax.broadcasted_iota(I32, (tm, tm), 0)
    scol = lax.broadcasted_iota(I32, (tm, tm), 1)
    earlier = (srow < scol).astype(BF16)
    rank = cnt_ref[...] + _dot(chosen.astype(BF16), earlier)
    cnt_ref[...] = cnt_ref[...] + jnp.sum(chosen, axis=1, keepdims=True)

    riota = lax.broadcasted_iota(I32, (SUBLANES, tm), 0)
    code = jnp.zeros((SUBLANES, tm), I32)
    wrow = jnp.zeros((SUBLANES, tm), F32)
    for r in range(TOP_K):
        rk = jnp.sum(jnp.where(eiota == idxs[r], rank, 0.0), axis=0, keepdims=True)
        code = jnp.where(riota == r, idxs[r] * (1 << RANK_BITS) + rk.astype(I32), code)
        wrow = jnp.where(riota == r, wts[r] / wsum * ROUTE_SCALE, wrow)
    code_ref[0] = code
    wpad = jnp.concatenate([wrow, jnp.zeros((LANES - SUBLANES, tm), F32)], axis=0)
    wtok_ref[...] = wpad.T


ROW_CHUNKS = D_MODEL // LANES


def _store_token_rows(rows_ref, val, n):
    for c in range(ROW_CHUNKS):
        rows_ref[pl.ds(c, n, stride=ROW_CHUNKS), :] = val[:, c * LANES:(c + 1) * LANES]


def _load_token_rows(rows_ref, n):
    return jnp.concatenate([rows_ref[pl.ds(c, n, stride=ROW_CHUNKS), :] for c in range(ROW_CHUNKS)], axis=-1)


def _merge_kernel(ya_ref, yr_ref, ga_ref, gb_ref, x_ref, gm_ref, shf_ref, scf_ref, gpost_ref, gpre_ref,
                  wba_ref, wbr_ref, wout_ref, wr_hi_ref, wr_mid_ref, rbias_ref,
                  x1_ref, h2_ref, h2rows_ref, code_ref, wtok_ref, cnt_ref, *, tm):
    @pl.when((pl.program_id(0) == 0) & (pl.program_id(1) == 0))
    def _():
        cnt_ref[...] = jnp.zeros(cnt_ref.shape, F32)

    pa = _dot(ya_ref[0], wba_ref[...])
    pr = _dot(yr_ref[0], wbr_ref[...])
    merged = _sigmoid(ga_ref[0].astype(F32)) * pa + _sigmoid(gb_ref[0].astype(F32)) * pr
    o = _dot(merged.astype(BF16), wout_ref[...])
    x1 = x_ref[0] + gm_ref[0] * (_rms_scale(o) * gpost_ref[...])
    x1_ref[0] = x1
    h2 = (_rms_scale(x1) * gpre_ref[...]) * (1.0 + scf_ref[0]) + shf_ref[0]
    h2_ref[0] = h2.astype(BF16)
    _store_token_rows(h2rows_ref, h2, tm)
    _route(h2, wr_hi_ref, wr_mid_ref, rbias_ref, cnt_ref, code_ref, wtok_ref, tm)


def _merge_call(y_attn, y_rnn, gate_a, gate_r, x, gate_m, shift_f, scale_f, g_post, g_pre,
                wba, wbr, wout, wr_hi, wr_mid, rbias, tm):
    bsz, s, d = x.shape
    nt = s // tm
    tok = lambda w: pl.BlockSpec((1, tm, w), lambda b, i: (b, i, 0))
    mod = pl.BlockSpec((1, 1, d), lambda b, i: (b, 0, 0))
    return pl.pallas_call(
        functools.partial(_merge_kernel, tm=tm),
        grid=(bsz, nt),
        in_specs=[tok(ATT_WIDTH), tok(d), tok(d), tok(d), tok(d), mod, mod, mod,
                  _const_spec((1, d)), _const_spec((1, d)),
                  _const_spec(wba.shape), _const_spec(wbr.shape), _const_spec(wout.shape),
                  _const_spec(wr_hi.shape), _const_spec(wr_mid.shape), _const_spec(rbias.shape)],
        out_specs=[tok(d), tok(d),
                   pl.BlockSpec((tm * ROW_CHUNKS, LANES), lambda b, i: (b * nt + i, 0)),
                   pl.BlockSpec((1, SUBLANES, tm), lambda b, i: (b * nt + i, 0, 0)),
                   pl.BlockSpec((tm, LANES), lambda b, i: (b * nt + i, 0)),
                   pl.BlockSpec((N_EXPERTS, 1), lambda b, i: (0, 0))],
        out_shape=[jax.ShapeDtypeStruct((bsz, s, d), F32), jax.ShapeDtypeStruct((bsz, s, d), BF16),
                   jax.ShapeDtypeStruct((bsz * s * ROW_CHUNKS, LANES), F32),
                   jax.ShapeDtypeStruct((bsz * nt, SUBLANES, tm), I32),
                   jax.ShapeDtypeStruct((bsz * s, LANES), F32),
                   jax.ShapeDtypeStruct((N_EXPERTS, 1), F32)],
        compiler_params=_params("arbitrary", "arbitrary"),
        name="merge_route",
    )(y_attn, y_rnn, gate_a, gate_r, x, gate_m, shift_f, scale_f, g_post, g_pre,
      wba, wbr, wout, wr_hi, wr_mid, rbias)


def _row_dest(code, pstart_ref):
    slot = pstart_ref[code >> RANK_BITS] + (code & RANK_MASK)
    return pl.multiple_of(slot * ROW_CHUNKS, ROW_CHUNKS)


def _scatter_kernel(code_ref, pstart_ref, pend_ref, h_ref, xs_hbm, zero_sc, sem, *, tm, rows, n_blocks):
    blk = rows * ROW_CHUNKS

    @pl.when(pl.program_id(0) == 0)
    def _():
        zero_sc[...] = jnp.zeros(zero_sc.shape, F32)
        n_used = pend_ref[N_EXPERTS - 1] // rows

        def zero_block(b):
            return pltpu.make_async_copy(zero_sc, xs_hbm.at[pl.ds(pl.multiple_of(b * blk, blk), blk)], sem)

        def zstart(e, c):
            @pl.when(pend_ref[e] > pstart_ref[e])
            def _():
                zero_block(pend_ref[e] // rows - 1).start()
            return c

        def zwait(e, c):
            @pl.when(pend_ref[e] > pstart_ref[e])
            def _():
                zero_block(0).wait()
            return c

        def tstart(b, c):
            zero_block(b).start()
            return c

        def twait(b, c):
            zero_block(0).wait()
            return c

        lax.fori_loop(0, N_EXPERTS, zstart, 0)
        lax.fori_loop(n_used, n_blocks, tstart, 0)
        lax.fori_loop(0, N_EXPERTS, zwait, 0)
        lax.fori_loop(n_used, n_blocks, twait, 0)

    def issue(i, c):
        src = h_ref.at[pl.ds(pl.multiple_of(i * ROW_CHUNKS, ROW_CHUNKS), ROW_CHUNKS)]
        for k in range(TOP_K):
            d = _row_dest(code_ref[0, k, i], pstart_ref)
            pltpu.make_async_copy(src, xs_hbm.at[pl.ds(d, ROW_CHUNKS)], sem).start()
        return c

    lax.fori_loop(0, tm, issue, 0)
    for k in range(TOP_K):
        pltpu.make_async_copy(h_ref, xs_hbm.at[pl.ds(0, tm * ROW_CHUNKS)], sem).wait()


def _scatter_call(code, pstart, pend, h2_rows, n_rows, tm, rows):
    t = h2_rows.shape[0] // ROW_CHUNKS
    smem = functools.partial(pl.BlockSpec, memory_space=pltpu.SMEM)
    return pl.pallas_call(
        functools.partial(_scatter_kernel, tm=tm, rows=rows, n_blocks=n_rows // rows),
        grid=(t // tm,),
        in_specs=[smem((1, SUBLANES, tm), lambda i: (i, 0, 0)), smem(), smem(),
                  pl.BlockSpec((tm * ROW_CHUNKS, LANES), lambda i: (i, 0))],
        out_specs=pl.BlockSpec(memory_space=pl.ANY),
        out_shape=jax.ShapeDtypeStruct((n_rows * ROW_CHUNKS, LANES), F32),
        scratch_shapes=[pltpu.VMEM((rows * ROW_CHUNKS, LANES), F32), pltpu.SemaphoreType.DMA],
        compiler_params=_params("arbitrary"),
        name="moe_dispatch",
    )(code, pstart, pend, h2_rows)


def _expert_kernel(be_ref, nused_ref, xs_ref, wg_ref, wu_ref, wd_ref, ys_ref, *, rows):
    @pl.when(pl.program_id(0) < nused_ref[0])
    def _():
        xb = _load_token_rows(xs_ref, rows).astype(BF16)
        g = _dot(xb, wg_ref[0])
        u = _dot(xb, wu_ref[0])
        act = (g * _sigmoid(g)) * u
        _store_token_rows(ys_ref, _dot(act.astype(BF16), wd_ref[0]), rows)

    @pl.when(pl.program_id(0) >= nused_ref[0])
    def _():
        ys_ref[...] = jnp.zeros(ys_ref.shape, F32)


def _expert_call(block_e, n_used, xs, wg, wu, wd, rows):
    n_rows = xs.shape[0] // ROW_CHUNKS
    _, d, ff = wg.shape
    row_spec = pl.BlockSpec((rows * ROW_CHUNKS, LANES), lambda i, be, nu: (jnp.minimum(i, nu[0] - 1), 0))
    grid_spec = pltpu.PrefetchScalarGridSpec(
        num_scalar_prefetch=2,
        grid=(n_rows // rows,),
        in_specs=[row_spec,
                  pl.BlockSpec((1, d, ff), lambda i, be, nu: (be[i], 0, 0)),
                  pl.BlockSpec((1, d, ff), lambda i, be, nu: (be[i], 0, 0)),
                  pl.BlockSpec((1, ff, d), lambda i, be, nu: (be[i], 0, 0))],
        out_specs=pl.BlockSpec((rows * ROW_CHUNKS, LANES), lambda i, be, nu: (i, 0)),
    )
    return pl.pallas_call(
        functools.partial(_expert_kernel, rows=rows),
        grid_spec=grid_spec,
        out_shape=jax.ShapeDtypeStruct(xs.shape, F32),
        compiler_params=_params("arbitrary"),
        name="moe_experts",
    )(block_e, n_used, xs, wg, wu, wd)


def _final_kernel(code_ref, pstart_ref, x1_ref, h2_ref, wtok_ref, gf_ref, gpost_ref,
                  wsg_ref, wsu_ref, wsd_ref, ys_hbm, o_ref, gat_sc, sem, *, tm):
    def issue(i, c):
        r0 = pl.multiple_of(i * ROW_CHUNKS, ROW_CHUNKS)
        for k in range(TOP_K):
            d = _row_dest(code_ref[0, k, i], pstart_ref)
            pltpu.make_async_copy(ys_hbm.at[pl.ds(d, ROW_CHUNKS)], gat_sc.at[k, pl.ds(r0, ROW_CHUNKS)], sem).start()
        return c

    lax.fori_loop(0, tm, issue, 0)

    hb = h2_ref[0]
    g = _dot(hb, wsg_ref[...])
    u = _dot(hb, wsu_ref[...])
    tot = _dot(((g * _sigmoid(g)) * u).astype(BF16), wsd_ref[...])

    for k in range(TOP_K):
        pltpu.make_async_copy(ys_hbm.at[pl.ds(0, tm * ROW_CHUNKS)], gat_sc.at[k], sem).wait()
    w = wtok_ref[...]
    routed = w[:, 0:1] * _load_token_rows(gat_sc.at[0], tm)
    for k in range(1, TOP_K):
        routed = routed + w[:, k:k + 1] * _load_token_rows(gat_sc.at[k], tm)
    tot = routed + tot
    o_ref[0] = x1_ref[0] + gf_ref[0] * (_rms_scale(tot) * gpost_ref[...])


def _final_call(code, pstart, x1, h2, wtok, gate_f, g_post, wsg, wsu, wsd, ys, tm):
    bsz, s, d = x1.shape
    nt = s // tm
    smem = functools.partial(pl.BlockSpec, memory_space=pltpu.SMEM)
    tok = pl.BlockSpec((1, tm, d), lambda b, i: (b, i, 0))
    return pl.pallas_call(
        functools.partial(_final_kernel, tm=tm),
        grid=(bsz, nt),
        in_specs=[smem((1, SUBLANES, tm), lambda b, i: (b * nt + i, 0, 0)), smem(),
                  tok, tok,
                  pl.BlockSpec((tm, LANES), lambda b, i: (b * nt + i, 0)),
                  pl.BlockSpec((1, 1, d), lambda b, i: (b, 0, 0)),
                  _const_spec((1, d)),
                  _const_spec(wsg.shape), _const_spec(wsu.shape), _const_spec(wsd.shape),
                  pl.BlockSpec(memory_space=pl.ANY)],
        out_specs=tok,
        out_shape=jax.ShapeDtypeStruct((bsz, s, d), F32),
        scratch_shapes=[pltpu.VMEM((TOP_K, tm * ROW_CHUNKS, LANES), F32), pltpu.SemaphoreType.DMA],
        compiler_params=_params("arbitrary", "arbitrary"),
        name="moe_combine",
    )(code, pstart, x1, h2, wtok, gate_f, g_post, wsg, wsu, wsd, ys)


def _block_diag_tiles(w):
    per = MXU_DIM // RNN_BLOCK_DIM
    nt = RNN_BLOCKS // per
    w4 = w.reshape(nt, per, RNN_BLOCK_DIM, RNN_BLOCK_DIM)
    eye = jnp.eye(per, dtype=w.dtype)
    return jnp.einsum("tnij,nm->tnimj", w4, eye).reshape(nt, MXU_DIM, MXU_DIM)


def _tile(n, pref):
    t = min(n, pref)
    assert n % t == 0, (n, t)
    return t


def _layer(x, c, w_ada, b_ada, g_pre_mix, g_post_mix, g_pre_ffn, g_post_ffn, w_in, b_forget,
           w_conv, b_conv, w_rg, b_rg, w_ig, b_ig, lam, w_branch_attn, w_branch_rnn, w_out,
           w_router, router_bias, w_exp_gate, w_exp_up, w_exp_down, w_sh_gate, w_sh_up, w_sh_down):
    bsz, s, d = x.shape
    t = bsz * s
    row = lambda v: v.reshape(1, -1)

    mod = _ada_call(c, w_ada, b_ada)
    shift_m, scale_m, gate_m, shift_f, scale_f, gate_f = [
        mod[:, j * d:(j + 1) * d].reshape(bsz, 1, d) for j in range(6)]

    o_f = 3 * ATT_WIDTH
    o_r = o_f + ATT_HEADS
    wqkv = w_in[:, :o_f].astype(BF16)
    wf = jnp.pad(w_in[:, o_f:o_r], ((0, 0), (0, LANES - ATT_HEADS))).astype(BF16)
    wr = w_in[:, o_r:].astype(BF16)
    q, k, v, f_logit, x_rnn, g_rnn, gate_a, gate_r = _inproj_call(
        x, shift_m, scale_m, row(g_pre_mix), wqkv, wf, wr, _tile(s, 512))

    bf_pad = jnp.pad(b_forget, (0, LANES - ATT_HEADS)).reshape(1, LANES)
    cum, cumt = _cum_call(f_logit, bf_pad, _tile(s, 256))
    y_attn = _attn_call(q, k, v, cum, cumt, _tile(s, 256))

    wbd = jnp.concatenate([_block_diag_tiles(w_rg), _block_diag_tiles(w_ig)], axis=-1).astype(BF16)
    y_rnn = _rnn_call(x_rnn, g_rnn, w_conv, row(b_conv), wbd, row(b_rg), row(b_ig), row(lam), _tile(s, 256))

    wr_t = w_router.T
    wr_hi = wr_t.astype(BF16)
    wr_mid = (wr_t - wr_hi.astype(F32)).astype(BF16)
    tm_r = _tile(s, 512)
    x1, h2, h2_rows, code, wtok, counts = _merge_call(
        y_attn, y_rnn, gate_a, gate_r, x, gate_m, shift_f, scale_f, row(g_post_mix), row(g_pre_ffn),
        w_branch_attn.astype(BF16), w_branch_rnn.astype(BF16), w_out.astype(BF16),
        wr_hi, wr_mid, router_bias.reshape(N_EXPERTS, 1), tm_r)

    rows = 256
    n_blocks = -(-(t * TOP_K) // rows) + N_EXPERTS
    cnt = counts[:, 0].astype(I32)
    padded = ((cnt + rows - 1) // rows) * rows
    pend = jnp.cumsum(padded).astype(I32)
    pstart = pend - padded
    n_used = jnp.maximum(pend[-1] // rows, 1)
    blk = jnp.minimum(jnp.arange(n_blocks, dtype=I32), n_used - 1)
    block_e = jnp.minimum(jnp.searchsorted(pend, blk * rows, side="right"), N_EXPERTS - 1).astype(I32)

    xs = _scatter_call(code, pstart, pend, h2_rows, n_blocks * rows, tm_r, rows)
    ys = _expert_call(block_e, n_used.reshape(1), xs,
                      w_exp_gate.astype(BF16), w_exp_up.astype(BF16), w_exp_down.astype(BF16), rows)

    tm_f = _tile(tm_r, 128)
    code_f = code.reshape(bsz * s // tm_r, SUBLANES, tm_r // tm_f, tm_f)
    code_f = jnp.transpose(code_f, (0, 2, 1, 3)).reshape(t // tm_f, SUBLANES, tm_f)
    return _final_call(code_f, pstart, x1, h2, wtok, gate_f, row(g_post_ffn),
                       w_sh_gate.astype(BF16), w_sh_up.astype(BF16), w_sh_down.astype(BF16), ys, tm_f)


def kernel(x, c, w_ada, b_ada, g_pre_mix, g_post_mix, g_pre_ffn, g_post_ffn, w_in, b_forget, w_conv, b_conv, w_rg, b_rg, w_ig, b_ig, rglru_lambda, w_branch_attn, w_branch_rnn, w_out, w_router, router_bias, w_exp_gate, w_exp_up, w_exp_down, w_sh_gate, w_sh_up, w_sh_down):
    depth = w_ada.shape[0]
    for l in range(depth):
        x = _layer(x, c, w_ada[l], b_ada[l], g_pre_mix[l], g_post_mix[l], g_pre_ffn[l], g_post_ffn[l],
                   w_in[l], b_forget[l], w_conv[l], b_conv[l], w_rg[l], b_rg[l], w_ig[l], b_ig[l],
                   rglru_lambda[l], w_branch_attn[l], w_branch_rnn[l], w_out[l], w_router[l],
                   router_bias[l], w_exp_gate[l], w_exp_up[l], w_exp_down[l],
                   w_sh_gate[l], w_sh_up[l], w_sh_down[l])
    return x
```

```python
import functools

import jax
import jax.numpy as jnp
from jax import lax
from jax.experimental import pallas as pl
from jax.experimental.pallas import tpu as pltpu

F32 = jnp.float32
BF16 = jnp.bfloat16
I32 = jnp.int32

D_MODEL = 1024
ATT_HEADS = 8
ATT_HEAD_DIM = 64
ATT_WIDTH = ATT_HEADS * ATT_HEAD_DIM
RNN_WIDTH = D_MODEL
RNN_BLOCKS = 16
RNN_BLOCK_DIM = RNN_WIDTH // RNN_BLOCKS
CONV_WIDTH = 4
RGLRU_C = 8.0
N_EXPERTS = 64
TOP_K = 6
N_GROUPS = 8
GROUP_SIZE = N_EXPERTS // N_GROUPS
TOPK_GROUPS = 4
EXPERT_FF = D_MODEL // 4
ROUTE_SCALE = 2.5
NORM_EPS = 1e-6

LANES = 128
SUBLANES = 8
MXU_DIM = 256
VMEM_LIMIT_BYTES = 56 * 1024 * 1024

NEG_BIG = -1e30
RANK_BITS = 20
RANK_MASK = (1 << RANK_BITS) - 1


def _sigmoid(x):
    return 1.0 / (1.0 + jnp.exp(-x))


def _rms_scale(x):
    return x * lax.rsqrt(jnp.mean(x * x, axis=-1, keepdims=True) + NORM_EPS)


def _dot(a, b):
    return jnp.dot(a, b, preferred_element_type=F32)


def _dot_nt(a, b):
    return lax.dot_general(a, b, (((1,), (1,)), ((), ())), preferred_element_type=F32)


def _split3(a):
    hi = a.astype(BF16)
    r1 = a - hi.astype(F32)
    mid = r1.astype(BF16)
    lo = (r1 - mid.astype(F32)).astype(BF16)
    return hi, mid, lo


def _params(*sem):
    return pltpu.CompilerParams(dimension_semantics=sem, vmem_limit_bytes=VMEM_LIMIT_BYTES)


def _const_spec(shape):
    nd = len(shape)
    return pl.BlockSpec(shape, lambda *_: (0,) * nd)


def _ada_kernel(c_ref, w_ref, b_ref, o_ref):
    c = c_ref[...]
    sc = c * _sigmoid(c)
    s_hi, s_mid, _ = _split3(sc)
    w_hi, w_mid, _ = _split3(w_ref[...])
    acc = _dot(s_hi, w_hi) + _dot(s_hi, w_mid) + _dot(s_mid, w_hi)
    o_ref[...] = acc + b_ref[...]


def _ada_call(c, w_ada, b_ada):
    bsz, d = c.shape
    n = w_ada.shape[1]
    tn = d
    return pl.pallas_call(
        _ada_kernel,
        grid=(n // tn,),
        in_specs=[
            pl.BlockSpec((bsz, d), lambda j: (0, 0)),
            pl.BlockSpec((d, tn), lambda j: (0, j)),
            pl.BlockSpec((1, tn), lambda j: (0, j)),
        ],
        out_specs=pl.BlockSpec((bsz, tn), lambda j: (0, j)),
        out_shape=jax.ShapeDtypeStruct((bsz, n), F32),
        compiler_params=_params("arbitrary"),
        name="ada_mod",
    )(c, w_ada, b_ada.reshape(1, n))


def _inproj_kernel(x_ref, sh_ref, sc_ref, g_ref, wqkv_ref, wf_ref, wr_ref,
                   q_ref, k_ref, v_ref, f_ref, xr_ref, gr_ref, ga_ref, gb_ref):
    x = x_ref[0]
    h = (_rms_scale(x) * g_ref[...]) * (1.0 + sc_ref[0]) + sh_ref[0]
    hb = h.astype(BF16)
    for j, ref in enumerate((q_ref, k_ref, v_ref)):
        ref[0] = _dot(hb, wqkv_ref[:, j * ATT_WIDTH:(j + 1) * ATT_WIDTH]).astype(BF16)
    f_ref[0] = _dot(hb, wf_ref[...])
    half = D_MODEL // 2
    for j, ref in enumerate((xr_ref, gr_ref, ga_ref, gb_ref)):
        for c in range(2):
            lo = j * D_MODEL + c * half
            ref[0, :, c * half:(c + 1) * half] = _dot(hb, wr_ref[:, lo:lo + half]).astype(BF16)


def _inproj_call(x, shift, scale, g, wqkv, wf, wr, tm):
    bsz, s, d = x.shape
    nt = s // tm
    tok = lambda w: pl.BlockSpec((1, tm, w), lambda b, i: (b, i, 0))
    mod = pl.BlockSpec((1, 1, d), lambda b, i: (b, 0, 0))
    out_w = (ATT_WIDTH,) * 3 + (LANES,) + (d,) * 4
    out_dt = (BF16,) * 3 + (F32,) + (BF16,) * 4
    return pl.pallas_call(
        _inproj_kernel,
        grid=(bsz, nt),
        in_specs=[tok(d), mod, mod, _const_spec((1, d)),
                  _const_spec(wqkv.shape), _const_spec(wf.shape), _const_spec(wr.shape)],
        out_specs=[tok(w) for w in out_w],
        out_shape=[jax.ShapeDtypeStruct((bsz, s, w), dt) for w, dt in zip(out_w, out_dt)],
        compiler_params=_params("arbitrary", "arbitrary"),
        name="in_proj",
    )(x, shift, scale, g, wqkv, wf, wr)


N_BIAS_PIECES = 3


def _bias_placement():
    h = jnp.arange(LANES)[:, None]
    lane = jnp.arange(ATT_WIDTH)[None, :]
    mats = []
    for piece in range(N_BIAS_PIECES):
        target = (h // 2) * LANES + ATT_HEAD_DIM * (1 - h % 2) + piece
        mats.append(((lane == target) & (h < ATT_HEADS)).astype(BF16))
    return jnp.stack(mats)


def _cum_kernel(f_ref, b_ref, place_ref, kb_ref, *, blk):
    s = f_ref.shape[1]
    row = lax.broadcasted_iota(I32, (blk, blk), 0)
    col = lax.broadcasted_iota(I32, (blk, blk), 1)
    tri = (row >= col).astype(BF16)
    carry = jnp.zeros((1, LANES), F32)
    for i in range(s // blk):
        z = f_ref[0, i * blk:(i + 1) * blk, :] + b_ref[...]
        lf = jnp.minimum(z, 0.0) - jnp.log1p(jnp.exp(-jnp.abs(z)))
        hi, mid, lo = _split3(lf)
        c = _dot(tri, hi) + _dot(tri, mid) + _dot(tri, lo) + carry
        carry = c[blk - 1:blk, :]
        pieces = _split3(-c)
        kb = _dot(pieces[0], place_ref[0])
        for j in range(1, N_BIAS_PIECES):
            kb = kb + _dot(pieces[j], place_ref[j])
        kb_ref[0, i * blk:(i + 1) * blk, :] = kb.astype(BF16)


def _cum_call(f_logit, b_forget_pad, blk):
    bsz, s, _ = f_logit.shape
    place = _bias_placement()
    return pl.pallas_call(
        functools.partial(_cum_kernel, blk=blk),
        grid=(bsz,),
        in_specs=[pl.BlockSpec((1, s, LANES), lambda b: (b, 0, 0)), _const_spec((1, LANES)),
                  _const_spec(place.shape)],
        out_specs=pl.BlockSpec((1, s, ATT_WIDTH), lambda b: (b, 0, 0)),
        out_shape=jax.ShapeDtypeStruct((bsz, s, ATT_WIDTH), BF16),
        compiler_params=_params("arbitrary"),
        name="forget_cumsum",
    )(f_logit, b_forget_pad, place)


ATT_VT_ROWS = ATT_HEAD_DIM + 16


def _attn_kernel(q_ref, k_ref, v_ref, kb_ref, o_ref, vt_sc, st_sc, m_sc, acc_sc, *, tq, tk):
    s = q_ref.shape[1]
    nq, nk, ratio = s // tq, s // tk, tq // tk
    lane = lax.broadcasted_iota(I32, (1, LANES), 1)
    head_lanes = (lane < ATT_HEAD_DIM, lane >= ATT_HEAD_DIM)
    one_lanes = ((lane >= ATT_HEAD_DIM) & (lane < ATT_HEAD_DIM + N_BIAS_PIECES), lane < N_BIAS_PIECES)
    key_minus_query = (lax.broadcasted_iota(I32, (tk, tq), 0) - lax.broadcasted_iota(I32, (tk, tq), 1))

    vt = v_ref[0].astype(F32).T
    ones = jnp.ones((ATT_VT_ROWS - ATT_HEAD_DIM, s), F32)
    for h in range(2):
        vth = jnp.concatenate([vt[h * ATT_HEAD_DIM:(h + 1) * ATT_HEAD_DIM, :], ones], axis=0).astype(BF16)
        for j in range(nk):
            vt_sc[h, j] = vth[:, j * tk:(j + 1) * tk]

    def q_body(qi, carry):
        q0 = pl.multiple_of(qi * tq, tq)
        qs = q_ref[0, pl.ds(q0, tq), :] * (ATT_HEAD_DIM ** -0.5)
        qa = [jnp.where(head_lanes[h], qs, one_lanes[h].astype(BF16)) for h in range(2)]
        m_sc[...] = jnp.full(m_sc.shape, NEG_BIG, F32)
        acc_sc[...] = jnp.zeros(acc_sc.shape, F32)

        def scores(kj):
            k0 = pl.multiple_of(kj * tk, tk)
            kb = k_ref[0, pl.ds(k0, tk), :]
            bias = kb_ref[0, pl.ds(k0, tk), :]
            return [_dot_nt(jnp.where(head_lanes[h], kb, bias), qa[h]) for h in range(2)]

        def accumulate(kj, masked):
            for h in range(2):
                st = st_sc[h]
                if masked:
                    st = jnp.where(key_minus_query <= q0 - kj * tk, st, NEG_BIG)
                m_old = m_sc[h]
                m_new = jnp.maximum(m_old, jnp.max(st, axis=0, keepdims=True))
                p = jnp.exp(st - m_new).astype(BF16)
                acc_sc[h] = jnp.exp(m_old - m_new) * acc_sc[h] + _dot(vt_sc[h, kj], p)
                m_sc[h] = m_new

        def put(sts):
            for h in range(2):
                st_sc[h] = sts[h]

        n_full = qi * ratio
        put(scores(0))

        def kv_body(kj, c2):
            nxt = scores(kj + 1)
            accumulate(kj, False)
            put(nxt)
            return c2

        lax.fori_loop(0, n_full, kv_body, 0)
        for d in range(ratio):
            nxt = scores(n_full + d + 1) if d + 1 < ratio else None
            accumulate(n_full + d, True)
            if nxt is not None:
                put(nxt)

        outs = []
        for h in range(2):
            acc = acc_sc[h]
            outs.append(acc[:ATT_HEAD_DIM, :] * (1.0 / acc[ATT_HEAD_DIM:ATT_HEAD_DIM + 1, :]))
        o_ref[0, pl.ds(q0, tq), :] = jnp.concatenate(outs, axis=0).T.astype(BF16)
        return carry

    lax.fori_loop(0, nq, q_body, 0)


def _attn_call(q, k, v, kbias, tq, tk):
    bsz, s, _ = q.shape
    qkv = pl.BlockSpec((1, s, LANES), lambda b, p: (b, 0, p))
    return pl.pallas_call(
        functools.partial(_attn_kernel, tq=tq, tk=tk),
        grid=(bsz, ATT_HEADS // 2),
        in_specs=[qkv, qkv, qkv, qkv],
        out_specs=qkv,
        out_shape=jax.ShapeDtypeStruct((bsz, s, ATT_WIDTH), BF16),
        scratch_shapes=[pltpu.VMEM((2, s // tk, ATT_VT_ROWS, tk), BF16), pltpu.VMEM((2, tk, tq), F32),
                        pltpu.VMEM((2, 1, tq), F32), pltpu.VMEM((2, ATT_VT_ROWS, tq), F32)],
        compiler_params=_params("arbitrary", "arbitrary"),
        name="fox_attention",
    )(q, k, v, kbias)


def _gelu_tanh(x):
    c = 0.7978845608028654
    return 0.5 * x * (1.0 + jnp.tanh(c * (x + 0.044715 * (x * x * x))))


def _rnn_kernel(xr_ref, gr_ref, wc_ref, bc_ref, wbd_ref, brg_ref, big_ref, lam_ref, o_ref,
                tail_sc, hc_sc, a_sc, b_sc, h_sc, *, ts):
    @pl.when(pl.program_id(1) == 0)
    def _():
        tail_sc[...] = jnp.zeros(tail_sc.shape, F32)
        hc_sc[...] = jnp.zeros(hc_sc.shape, F32)

    x = xr_ref[0].astype(F32)
    full = jnp.concatenate([tail_sc[...], x], axis=0)
    u = bc_ref[...] + wc_ref[3:4, :] * x
    for j in range(CONV_WIDTH - 1):
        off = SUBLANES - (CONV_WIDTH - 1) + j
        u = u + wc_ref[j:j + 1, :] * full[off:off + ts, :]
    tail_sc[...] = x[ts - SUBLANES:ts, :]

    lam = lam_ref[...]
    neg_sp = -(jnp.maximum(-lam, 0.0) + jnp.log1p(jnp.exp(-jnp.abs(lam))))
    for jt in range(RNN_WIDTH // MXU_DIM):
        cs = slice(jt * MXU_DIM, (jt + 1) * MXU_DIM)
        uc = u[:, cs]
        g = _dot(uc.astype(BF16), wbd_ref[jt])
        r = _sigmoid(g[:, :MXU_DIM] + brg_ref[:, cs])
        ig = _sigmoid(g[:, MXU_DIM:] + big_ref[:, cs])
        log_a = RGLRU_C * r * neg_sp[:, cs]
        a_sc[:, cs] = jnp.exp(log_a)
        b_sc[:, cs] = jnp.sqrt(1.0 - jnp.exp(2.0 * log_a)) * ig * uc

    srow = lax.broadcasted_iota(I32, (SUBLANES, RNN_WIDTH), 0)

    def scan_body(i, carry):
        r0 = pl.multiple_of(i * SUBLANES, SUBLANES)
        a = a_sc[pl.ds(r0, SUBLANES), :]
        b = b_sc[pl.ds(r0, SUBLANES), :]
        for sh in (1, 2, 4):
            keep = srow >= sh
            a_prev = pltpu.roll(a, sh, 0)
            b_prev = pltpu.roll(b, sh, 0)
            b = jnp.where(keep, b + a * b_prev, b)
            a = jnp.where(keep, a * a_prev, a)
        h = b + a * carry
        h_sc[pl.ds(r0, SUBLANES), :] = h
        return jnp.broadcast_to(h[SUBLANES - 1:SUBLANES, :], (SUBLANES, RNN_WIDTH))

    hc_sc[...] = lax.fori_loop(0, ts // SUBLANES, scan_body, hc_sc[...])
    o_ref[0] = (h_sc[...] * _gelu_tanh(gr_ref[0].astype(F32))).astype(BF16)


def _rnn_call(x_rnn, g_rnn, w_conv, b_conv, wbd, b_rg, b_ig, lam, ts):
    bsz, s, w = x_rnn.shape
    tok = pl.BlockSpec((1, ts, w), lambda b, i: (b, i, 0))
    return pl.pallas_call(
        functools.partial(_rnn_kernel, ts=ts),
        grid=(bsz, s // ts),
        in_specs=[tok, tok, _const_spec(w_conv.shape), _const_spec((1, w)), _const_spec(wbd.shape),
                  _const_spec((1, w)), _const_spec((1, w)), _const_spec((1, w))],
        out_specs=tok,
        out_shape=jax.ShapeDtypeStruct((bsz, s, w), BF16),
        scratch_shapes=[pltpu.VMEM((SUBLANES, w), F32), pltpu.VMEM((SUBLANES, w), F32),
                        pltpu.VMEM((ts, w), F32), pltpu.VMEM((ts, w), F32), pltpu.VMEM((ts, w), F32)],
        compiler_params=_params("arbitrary", "arbitrary"),
        name="rglru",
    )(x_rnn, g_rnn, w_conv, b_conv, wbd, b_rg, b_ig, lam)


def _route(h2, wr_hi_ref, wr_mid_ref, rbias_ref, cnt_ref, code_ref, wtok_ref, tm):
    h_hi, h_mid, _ = _split3(h2)
    logits = _dot_nt(wr_hi_ref[...], h_hi) + _dot_nt(wr_hi_ref[...], h_mid) + _dot_nt(wr_mid_ref[...], h_hi)
    scores = _sigmoid(logits)
    sel = scores + rbias_ref[...]

    giota = lax.broadcasted_iota(I32, (N_GROUPS, tm), 0)
    gs = jnp.zeros((N_GROUPS, tm), F32)
    for g in range(N_GROUPS):
        blk = sel[g * GROUP_SIZE:(g + 1) * GROUP_SIZE, :]
        m1 = jnp.max(blk, axis=0, keepdims=True)
        i1 = jnp.min(jnp.where(blk == m1, giota, N_EXPERTS), axis=0, keepdims=True)
        m2 = jnp.max(jnp.where(giota == i1, -jnp.inf, blk), axis=0, keepdims=True)
        gs = jnp.where(giota == g, m1 + m2, gs)

    gsel = jnp.zeros((N_GROUPS, tm), F32)
    for _ in range(TOPK_GROUPS):
        m = jnp.max(gs, axis=0, keepdims=True)
        idx = jnp.min(jnp.where(gs == m, giota, N_EXPERTS), axis=0, keepdims=True)
        hit = giota == idx
        gsel = jnp.where(hit, 1.0, gsel)
        gs = jnp.where(hit, -jnp.inf, gs)

    masked = jnp.concatenate(
        [jnp.where(gsel[g:g + 1, :] > 0.0, sel[g * GROUP_SIZE:(g + 1) * GROUP_SIZE, :], -jnp.inf)
         for g in range(N_GROUPS)], axis=0)

    eiota = lax.broadcasted_iota(I32, (N_EXPERTS, tm), 0)
    chosen = jnp.zeros((N_EXPERTS, tm), F32)
    idxs, wts = [], []
    for _ in range(TOP_K):
        m = jnp.max(masked, axis=0, keepdims=True)
        idx = jnp.min(jnp.where(masked == m, eiota, N_EXPERTS), axis=0, keepdims=True)
        hit = eiota == idx
        wts.append(jnp.sum(jnp.where(hit, scores, 0.0), axis=0, keepdims=True))
        idxs.append(idx)
        chosen = jnp.where(hit, 1.0, chosen)
        masked = jnp.where(hit, -jnp.inf, masked)
    wsum = wts[0]
    for w in wts[1:]:
        wsum = wsum + w

    srow = lax.broadcasted_iota(I32, (tm, tm), 0)
    scol = lax.broadcasted_iota(I32, (tm, tm), 1)
    earlier = (srow < scol).astype(BF16)
    rank = cnt_ref[...] + _dot(chosen.astype(BF16), earlier)
    cnt_ref[...] = cnt_ref[...] + jnp.sum(chosen, axis=1, keepdims=True)

    riota = lax.broadcasted_iota(I32, (SUBLANES, tm), 0)
    code = jnp.zeros((SUBLANES, tm), I32)
    wrow = jnp.zeros((SUBLANES, tm), F32)
    for r in range(TOP_K):
        rk = jnp.sum(jnp.where(eiota == idxs[r], rank, 0.0), axis=0, keepdims=True)
        code = jnp.where(riota == r, idxs[r] * (1 << RANK_BITS) + rk.astype(I32), code)
        wrow = jnp.where(riota == r, wts[r] / wsum * ROUTE_SCALE, wrow)
    code_ref[0] = code
    wpad = jnp.concatenate([wrow, jnp.zeros((LANES - SUBLANES, tm), F32)], axis=0)
    wtok_ref[...] = wpad.T


ROW_CHUNKS = D_MODEL // LANES


def _store_token_rows(rows_ref, val, n):
    for c in range(ROW_CHUNKS):
        rows_ref[pl.ds(c, n, stride=ROW_CHUNKS), :] = val[:, c * LANES:(c + 1) * LANES]


def _load_token_rows(rows_ref, n):
    return jnp.concatenate([rows_ref[pl.ds(c, n, stride=ROW_CHUNKS), :] for c in range(ROW_CHUNKS)], axis=-1)


def _merge_kernel(ya_ref, yr_ref, ga_ref, gb_ref, x_ref, gm_ref, shf_ref, scf_ref, gpost_ref, gpre_ref,
                  wba_ref, wbr_ref, wout_ref, wr_hi_ref, wr_mid_ref, rbias_ref,
                  x1_ref, h2_ref, h2rows_ref, code_ref, wtok_ref, cnt_ref, *, tm):
    @pl.when((pl.program_id(0) == 0) & (pl.program_id(1) == 0))
    def _():
        cnt_ref[...] = jnp.zeros(cnt_ref.shape, F32)

    pa = _dot(ya_ref[0], wba_ref[...])
    pr = _dot(yr_ref[0], wbr_ref[...])
    merged = _sigmoid(ga_ref[0].astype(F32)) * pa + _sigmoid(gb_ref[0].astype(F32)) * pr
    o = _dot(merged.astype(BF16), wout_ref[...])
    x1 = x_ref[0] + gm_ref[0] * (_rms_scale(o) * gpost_ref[...])
    x1_ref[0] = x1
    h2 = (_rms_scale(x1) * gpre_ref[...]) * (1.0 + scf_ref[0]) + shf_ref[0]
    h2_ref[0] = h2.astype(BF16)
    _store_token_rows(h2rows_ref, h2, tm)
    _route(h2, wr_hi_ref, wr_mid_ref, rbias_ref, cnt_ref, code_ref, wtok_ref, tm)


def _merge_call(y_attn, y_rnn, gate_a, gate_r, x, gate_m, shift_f, scale_f, g_post, g_pre,
                wba, wbr, wout, wr_hi, wr_mid, rbias, tm):
    bsz, s, d = x.shape
    nt = s // tm
    tok = lambda w: pl.BlockSpec((1, tm, w), lambda b, i: (b, i, 0))
    mod = pl.BlockSpec((1, 1, d), lambda b, i: (b, 0, 0))
    return pl.pallas_call(
        functools.partial(_merge_kernel, tm=tm),
        grid=(bsz, nt),
        in_specs=[tok(ATT_WIDTH), tok(d), tok(d), tok(d), tok(d), mod, mod, mod,
                  _const_spec((1, d)), _const_spec((1, d)),
                  _const_spec(wba.shape), _const_spec(wbr.shape), _const_spec(wout.shape),
                  _const_spec(wr_hi.shape), _const_spec(wr_mid.shape), _const_spec(rbias.shape)],
        out_specs=[tok(d), tok(d),
                   pl.BlockSpec((tm * ROW_CHUNKS, LANES), lambda b, i: (b * nt + i, 0)),
                   pl.BlockSpec((1, SUBLANES, tm), lambda b, i: (b * nt + i, 0, 0)),
                   pl.BlockSpec((tm, LANES), lambda b, i: (b * nt + i, 0)),
                   pl.BlockSpec((N_EXPERTS, 1), lambda b, i: (0, 0))],
        out_shape=[jax.ShapeDtypeStruct((bsz, s, d), F32), jax.ShapeDtypeStruct((bsz, s, d), BF16),
                   jax.ShapeDtypeStruct((bsz * s * ROW_CHUNKS, LANES), F32),
                   jax.ShapeDtypeStruct((bsz * nt, SUBLANES, tm), I32),
                   jax.ShapeDtypeStruct((bsz * s, LANES), F32),
                   jax.ShapeDtypeStruct((N_EXPERTS, 1), F32)],
        compiler_params=_params("arbitrary", "arbitrary"),
        name="merge_route",
    )(y_attn, y_rnn, gate_a, gate_r, x, gate_m, shift_f, scale_f, g_post, g_pre,
      wba, wbr, wout, wr_hi, wr_mid, rbias)


def _row_dest(code, pstart_ref):
    slot = pstart_ref[code >> RANK_BITS] + (code & RANK_MASK)
    return pl.multiple_of(slot * ROW_CHUNKS, ROW_CHUNKS)


def _scatter_kernel(code_ref, pstart_ref, pend_ref, h_ref, xs_hbm, zero_sc, sem, *, tm, rows, n_blocks):
    blk = rows * ROW_CHUNKS

    @pl.when(pl.program_id(0) == 0)
    def _():
        zero_sc[...] = jnp.zeros(zero_sc.shape, F32)
        n_used = pend_ref[N_EXPERTS - 1] // rows

        def zero_block(b):
            return pltpu.make_async_copy(zero_sc, xs_hbm.at[pl.ds(pl.multiple_of(b * blk, blk), blk)], sem)

        def zstart(e, c):
            @pl.when(pend_ref[e] > pstart_ref[e])
            def _():
                zero_block(pend_ref[e] // rows - 1).start()
            return c

        def zwait(e, c):
            @pl.when(pend_ref[e] > pstart_ref[e])
            def _():
                zero_block(0).wait()
            return c

        def tstart(b, c):
            zero_block(b).start()
            return c

        def twait(b, c):
            zero_block(0).wait()
            return c

        lax.fori_loop(0, N_EXPERTS, zstart, 0)
        lax.fori_loop(n_used, n_blocks, tstart, 0)
        lax.fori_loop(0, N_EXPERTS, zwait, 0)
        lax.fori_loop(n_used, n_blocks, twait, 0)

    def issue(i, c):
        src = h_ref.at[pl.ds(pl.multiple_of(i * ROW_CHUNKS, ROW_CHUNKS), ROW_CHUNKS)]
        for k in range(TOP_K):
            d = _row_dest(code_ref[0, k, i], pstart_ref)
            pltpu.make_async_copy(src, xs_hbm.at[pl.ds(d, ROW_CHUNKS)], sem).start()
        return c

    lax.fori_loop(0, tm, issue, 0)
    for k in range(TOP_K):
        pltpu.make_async_copy(h_ref, xs_hbm.at[pl.ds(0, tm * ROW_CHUNKS)], sem).wait()


def _scatter_call(code, pstart, pend, h2_rows, n_rows, tm, rows):
    t = h2_rows.shape[0] // ROW_CHUNKS
    smem = functools.partial(pl.BlockSpec, memory_space=pltpu.SMEM)
    return pl.pallas_call(
        functools.partial(_scatter_kernel, tm=tm, rows=rows, n_blocks=n_rows // rows),
        grid=(t // tm,),
        in_specs=[smem((1, SUBLANES, tm), lambda i: (i, 0, 0)), smem(), smem(),
                  pl.BlockSpec((tm * ROW_CHUNKS, LANES), lambda i: (i, 0))],
        out_specs=pl.BlockSpec(memory_space=pl.ANY),
        out_shape=jax.ShapeDtypeStruct((n_rows * ROW_CHUNKS, LANES), F32),
        scratch_shapes=[pltpu.VMEM((rows * ROW_CHUNKS, LANES), F32), pltpu.SemaphoreType.DMA],
        compiler_params=_params("arbitrary"),
        name="moe_dispatch",
    )(code, pstart, pend, h2_rows)


def _expert_kernel(be_ref, nused_ref, xs_ref, wg_ref, wu_ref, wd_ref, ys_ref, *, rows):
    @pl.when(pl.program_id(0) < nused_ref[0])
    def _():
        xb = _load_token_rows(xs_ref, rows).astype(BF16)
        g = _dot(xb, wg_ref[0])
        u = _dot(xb, wu_ref[0])
        act = (g * _sigmoid(g)) * u
        _store_token_rows(ys_ref, _dot(act.astype(BF16), wd_ref[0]), rows)

    @pl.when(pl.program_id(0) >= nused_ref[0])
    def _():
        ys_ref[...] = jnp.zeros(ys_ref.shape, F32)


def _expert_call(block_e, n_used, xs, wg, wu, wd, rows):
    n_rows = xs.shape[0] // ROW_CHUNKS
    _, d, ff = wg.shape
    row_spec = pl.BlockSpec((rows * ROW_CHUNKS, LANES), lambda i, be, nu: (jnp.minimum(i, nu[0] - 1), 0))
    grid_spec = pltpu.PrefetchScalarGridSpec(
        num_scalar_prefetch=2,
        grid=(n_rows // rows,),
        in_specs=[row_spec,
                  pl.BlockSpec((1, d, ff), lambda i, be, nu: (be[i], 0, 0)),
                  pl.BlockSpec((1, d, ff), lambda i, be, nu: (be[i], 0, 0)),
                  pl.BlockSpec((1, ff, d), lambda i, be, nu: (be[i], 0, 0))],
        out_specs=pl.BlockSpec((rows * ROW_CHUNKS, LANES), lambda i, be, nu: (i, 0)),
    )
    return pl.pallas_call(
        functools.partial(_expert_kernel, rows=rows),
        grid_spec=grid_spec,
        out_shape=jax.ShapeDtypeStruct(xs.shape, F32),
        compiler_params=_params("arbitrary"),
        name="moe_experts",
    )(block_e, n_used, xs, wg, wu, wd)


def _final_kernel(code_ref, pstart_ref, x1_ref, h2_ref, wtok_ref, gf_ref, gpost_ref,
                  wsg_ref, wsu_ref, wsd_ref, ys_hbm, o_ref, gat_sc, sem, *, tm):
    def issue(i, c):
        r0 = pl.multiple_of(i * ROW_CHUNKS, ROW_CHUNKS)
        for k in range(TOP_K):
            d = _row_dest(code_ref[0, k, i], pstart_ref)
            pltpu.make_async_copy(ys_hbm.at[pl.ds(d, ROW_CHUNKS)], gat_sc.at[k, pl.ds(r0, ROW_CHUNKS)], sem).start()
        return c

    lax.fori_loop(0, tm, issue, 0)

    hb = h2_ref[0]
    g = _dot(hb, wsg_ref[...])
    u = _dot(hb, wsu_ref[...])
    tot = _dot(((g * _sigmoid(g)) * u).astype(BF16), wsd_ref[...])

    for k in range(TOP_K):
        pltpu.make_async_copy(ys_hbm.at[pl.ds(0, tm * ROW_CHUNKS)], gat_sc.at[k], sem).wait()
    w = wtok_ref[...]
    routed = w[:, 0:1] * _load_token_rows(gat_sc.at[0], tm)
    for k in range(1, TOP_K):
        routed = routed + w[:, k:k + 1] * _load_token_rows(gat_sc.at[k], tm)
    tot = routed + tot
    o_ref[0] = x1_ref[0] + gf_ref[0] * (_rms_scale(tot) * gpost_ref[...])


def _final_call(code, pstart, x1, h2, wtok, gate_f, g_post, wsg, wsu, wsd, ys, tm):
    bsz, s, d = x1.shape
    nt = s // tm
    smem = functools.partial(pl.BlockSpec, memory_space=pltpu.SMEM)
    tok = pl.BlockSpec((1, tm, d), lambda b, i: (b, i, 0))
    return pl.pallas_call(
        functools.partial(_final_kernel, tm=tm),
        grid=(bsz, nt),
        in_specs=[smem((1, SUBLANES, tm), lambda b, i: (b * nt + i, 0, 0)), smem(),
                  tok, tok,
                  pl.BlockSpec((tm, LANES), lambda b, i: (b * nt + i, 0)),
                  pl.BlockSpec((1, 1, d), lambda b, i: (b, 0, 0)),
                  _const_spec((1, d)),
                  _const_spec(wsg.shape), _const_spec(wsu.shape), _const_spec(wsd.shape),
                  pl.BlockSpec(memory_space=pl.ANY)],
        out_specs=tok,
        out_shape=jax.ShapeDtypeStruct((bsz, s, d), F32),
        scratch_shapes=[pltpu.VMEM((TOP_K, tm * ROW_CHUNKS, LANES), F32), pltpu.SemaphoreType.DMA],
        compiler_params=_params("arbitrary", "arbitrary"),
        name="moe_combine",
    )(code, pstart, x1, h2, wtok, gate_f, g_post, wsg, wsu, wsd, ys)


def _block_diag_tiles(w):
    per = MXU_DIM // RNN_BLOCK_DIM
    nt = RNN_BLOCKS // per
    w4 = w.reshape(nt, per, RNN_BLOCK_DIM, RNN_BLOCK_DIM)
    eye = jnp.eye(per, dtype=w.dtype)
    return jnp.einsum("tnij,nm->tnimj", w4, eye).reshape(nt, MXU_DIM, MXU_DIM)


def _tile(n, pref):
    t = min(n, pref)
    assert n % t == 0, (n, t)
    return t


def _layer(x, c, w_ada, b_ada, g_pre_mix, g_post_mix, g_pre_ffn, g_post_ffn, w_in, b_forget,
           w_conv, b_conv, w_rg, b_rg, w_ig, b_ig, lam, w_branch_attn, w_branch_rnn, w_out,
           w_router, router_bias, w_exp_gate, w_exp_up, w_exp_down, w_sh_gate, w_sh_up, w_sh_down):
    bsz, s, d = x.shape
    t = bsz * s
    row = lambda v: v.reshape(1, -1)

    mod = _ada_call(c, w_ada, b_ada)
    shift_m, scale_m, gate_m, shift_f, scale_f, gate_f = [
        mod[:, j * d:(j + 1) * d].reshape(bsz, 1, d) for j in range(6)]

    o_f = 3 * ATT_WIDTH
    o_r = o_f + ATT_HEADS
    wqkv = w_in[:, :o_f].astype(BF16)
    wf = jnp.pad(w_in[:, o_f:o_r], ((0, 0), (0, LANES - ATT_HEADS))).astype(BF16)
    wr = w_in[:, o_r:].astype(BF16)
    q, k, v, f_logit, x_rnn, g_rnn, gate_a, gate_r = _inproj_call(
        x, shift_m, scale_m, row(g_pre_mix), wqkv, wf, wr, _tile(s, 512))

    bf_pad = jnp.pad(b_forget, (0, LANES - ATT_HEADS)).reshape(1, LANES)
    kbias = _cum_call(f_logit, bf_pad, _tile(s, 256))
    tq = _tile(s, 512)
    y_attn = _attn_call(q, k, v, kbias, tq, _tile(tq, 256))

    wbd = jnp.concatenate([_block_diag_tiles(w_rg), _block_diag_tiles(w_ig)], axis=-1).astype(BF16)
    y_rnn = _rnn_call(x_rnn, g_rnn, w_conv, row(b_conv), wbd, row(b_rg), row(b_ig), row(lam), _tile(s, 256))

    wr_t = w_router.T
    wr_hi = wr_t.astype(BF16)
    wr_mid = (wr_t - wr_hi.astype(F32)).astype(BF16)
    tm_r = _tile(s, 512)
    x1, h2, h2_rows, code, wtok, counts = _merge_call(
        y_attn, y_rnn, gate_a, gate_r, x, gate_m, shift_f, scale_f, row(g_post_mix), row(g_pre_ffn),
        w_branch_attn.astype(BF16), w_branch_rnn.astype(BF16), w_out.astype(BF16),
        wr_hi, wr_mid, router_bias.reshape(N_EXPERTS, 1), tm_r)

    rows = 256
    n_blocks = -(-(t * TOP_K) // rows) + N_EXPERTS
    cnt = counts[:, 0].astype(I32)
    padded = ((cnt + rows - 1) // rows) * rows
    pend = jnp.cumsum(padded).astype(I32)
    pstart = pend - padded
    n_used = jnp.maximum(pend[-1] // rows, 1)
    blk = jnp.minimum(jnp.arange(n_blocks, dtype=I32), n_used - 1)
    block_e = jnp.minimum(jnp.sum((pend[None, :] <= (blk * rows)[:, None]).astype(I32), axis=1), N_EXPERTS - 1)

    xs = _scatter_call(code, pstart, pend, h2_rows, n_blocks * rows, tm_r, rows)
    ys = _expert_call(block_e, n_used.reshape(1), xs,
                      w_exp_gate.astype(BF16), w_exp_up.astype(BF16), w_exp_down.astype(BF16), rows)

    tm_f = _tile(tm_r, 128)
    code_f = code.reshape(bsz * s // tm_r, SUBLANES, tm_r // tm_f, tm_f)
    code_f = jnp.transpose(code_f, (0, 2, 1, 3)).reshape(t // tm_f, SUBLANES, tm_f)
    return _final_call(code_f, pstart, x1, h2, wtok, gate_f, row(g_post_ffn),
                       w_sh_gate.astype(BF16), w_sh_up.astype(BF16), w_sh_down.astype(BF16), ys, tm_f)


def kernel(x, c, w_ada, b_ada, g_pre_mix, g_post_mix, g_pre_ffn, g_post_ffn, w_in, b_forget, w_conv, b_conv, w_rg, b_rg, w_ig, b_ig, rglru_lambda, w_branch_attn, w_branch_rnn, w_out, w_router, router_bias, w_exp_gate, w_exp_up, w_exp_down, w_sh_gate, w_sh_up, w_sh_down):
    depth = w_ada.shape[0]
    for l in range(depth):
        x = _layer(x, c, w_ada[l], b_ada[l], g_pre_mix[l], g_post_mix[l], g_pre_ffn[l], g_post_ffn[l],
                   w_in[l], b_forget[l], w_conv[l], b_conv[l], w_rg[l], b_rg[l], w_ig[l], b_ig[l],
                   rglru_lambda[l], w_branch_attn[l], w_branch_rnn[l], w_out[l], w_router[l],
                   router_bias[l], w_exp_gate[l], w_exp_up[l], w_exp_down[l],
                   w_sh_gate[l], w_sh_up[l], w_sh_down[l])
    return x
```

```python
import functools

import jax
import jax.numpy as jnp
from jax import lax
from jax.experimental import pallas as pl
from jax.experimental.pallas import tpu as pltpu

F32 = jnp.float32
BF16 = jnp.bfloat16
I32 = jnp.int32

D_MODEL = 1024
ATT_HEADS = 8
ATT_HEAD_DIM = 64
ATT_WIDTH = ATT_HEADS * ATT_HEAD_DIM
RNN_WIDTH = D_MODEL
RNN_BLOCKS = 16
RNN_BLOCK_DIM = RNN_WIDTH // RNN_BLOCKS
CONV_WIDTH = 4
RGLRU_C = 8.0
N_EXPERTS = 64
TOP_K = 6
N_GROUPS = 8
GROUP_SIZE = N_EXPERTS // N_GROUPS
TOPK_GROUPS = 4
EXPERT_FF = D_MODEL // 4
ROUTE_SCALE = 2.5
NORM_EPS = 1e-6

LANES = 128
SUBLANES = 8
MXU_DIM = 256
VMEM_LIMIT_BYTES = 56 * 1024 * 1024

NEG_BIG = -1e30
RANK_BITS = 20
RANK_MASK = (1 << RANK_BITS) - 1


def _sigmoid(x):
    return 1.0 / (1.0 + jnp.exp(-x))


def _rms_scale(x):
    return x * lax.rsqrt(jnp.mean(x * x, axis=-1, keepdims=True) + NORM_EPS)


def _dot(a, b):
    return jnp.dot(a, b, preferred_element_type=F32)


def _dot_nt(a, b):
    return lax.dot_general(a, b, (((1,), (1,)), ((), ())), preferred_element_type=F32)


def _split3(a):
    hi = a.astype(BF16)
    r1 = a - hi.astype(F32)
    mid = r1.astype(BF16)
    lo = (r1 - mid.astype(F32)).astype(BF16)
    return hi, mid, lo


def _params(*sem):
    return pltpu.CompilerParams(dimension_semantics=sem, vmem_limit_bytes=VMEM_LIMIT_BYTES)


def _const_spec(shape):
    nd = len(shape)
    return pl.BlockSpec(shape, lambda *_: (0,) * nd)


def _ada_kernel(c_ref, w_ref, b_ref, o_ref):
    c = c_ref[...]
    sc = c * _sigmoid(c)
    s_hi, s_mid, _ = _split3(sc)
    w_hi, w_mid, _ = _split3(w_ref[...])
    acc = _dot(s_hi, w_hi) + _dot(s_hi, w_mid) + _dot(s_mid, w_hi)
    o_ref[...] = acc + b_ref[...]


def _ada_call(c, w_ada, b_ada):
    bsz, d = c.shape
    n = w_ada.shape[1]
    tn = d
    return pl.pallas_call(
        _ada_kernel,
        grid=(n // tn,),
        in_specs=[
            pl.BlockSpec((bsz, d), lambda j: (0, 0)),
            pl.BlockSpec((d, tn), lambda j: (0, j)),
            pl.BlockSpec((1, tn), lambda j: (0, j)),
        ],
        out_specs=pl.BlockSpec((bsz, tn), lambda j: (0, j)),
        out_shape=jax.ShapeDtypeStruct((bsz, n), F32),
        compiler_params=_params("arbitrary"),
        name="ada_mod",
    )(c, w_ada, b_ada.reshape(1, n))


def _inproj_kernel(x_ref, sh_ref, sc_ref, g_ref, wqkv_ref, wf_ref, wr_ref,
                   q_ref, k_ref, v_ref, f_ref, xr_ref, gr_ref, ga_ref, gb_ref):
    x = x_ref[0]
    h = (_rms_scale(x) * g_ref[...]) * (1.0 + sc_ref[0]) + sh_ref[0]
    hb = h.astype(BF16)
    for j, ref in enumerate((q_ref, k_ref, v_ref)):
        ref[0] = _dot(hb, wqkv_ref[:, j * ATT_WIDTH:(j + 1) * ATT_WIDTH]).astype(BF16)
    f_ref[0] = _dot(hb, wf_ref[...])
    half = D_MODEL // 2
    for j, ref in enumerate((xr_ref, gr_ref, ga_ref, gb_ref)):
        for c in range(2):
            lo = j * D_MODEL + c * half
            ref[0, :, c * half:(c + 1) * half] = _dot(hb, wr_ref[:, lo:lo + half]).astype(BF16)


def _inproj_call(x, shift, scale, g, wqkv, wf, wr, tm):
    bsz, s, d = x.shape
    nt = s // tm
    tok = lambda w: pl.BlockSpec((1, tm, w), lambda b, i: (b, i, 0))
    mod = pl.BlockSpec((1, 1, d), lambda b, i: (b, 0, 0))
    out_w = (ATT_WIDTH,) * 3 + (LANES,) + (d,) * 4
    out_dt = (BF16,) * 3 + (F32,) + (BF16,) * 4
    return pl.pallas_call(
        _inproj_kernel,
        grid=(bsz, nt),
        in_specs=[tok(d), mod, mod, _const_spec((1, d)),
                  _const_spec(wqkv.shape), _const_spec(wf.shape), _const_spec(wr.shape)],
        out_specs=[tok(w) for w in out_w],
        out_shape=[jax.ShapeDtypeStruct((bsz, s, w), dt) for w, dt in zip(out_w, out_dt)],
        compiler_params=_params("arbitrary", "arbitrary"),
        name="in_proj",
    )(x, shift, scale, g, wqkv, wf, wr)


N_BIAS_PIECES = 3


def _bias_placement():
    h = jnp.arange(LANES)[:, None]
    lane = jnp.arange(ATT_WIDTH)[None, :]
    mats = []
    for piece in range(N_BIAS_PIECES):
        target = (h // 2) * LANES + ATT_HEAD_DIM * (1 - h % 2) + piece
        mats.append(((lane == target) & (h < ATT_HEADS)).astype(BF16))
    return jnp.stack(mats)


def _cum_kernel(f_ref, b_ref, place_ref, kb_ref, *, blk):
    s = f_ref.shape[1]
    row = lax.broadcasted_iota(I32, (blk, blk), 0)
    col = lax.broadcasted_iota(I32, (blk, blk), 1)
    tri = (row >= col).astype(BF16)
    carry = jnp.zeros((1, LANES), F32)
    for i in range(s // blk):
        z = f_ref[0, i * blk:(i + 1) * blk, :] + b_ref[...]
        lf = jnp.minimum(z, 0.0) - jnp.log1p(jnp.exp(-jnp.abs(z)))
        hi, mid, lo = _split3(lf)
        c = _dot(tri, hi) + _dot(tri, mid) + _dot(tri, lo) + carry
        carry = c[blk - 1:blk, :]
        pieces = _split3(-c)
        kb = _dot(pieces[0], place_ref[0])
        for j in range(1, N_BIAS_PIECES):
            kb = kb + _dot(pieces[j], place_ref[j])
        kb_ref[0, i * blk:(i + 1) * blk, :] = kb.astype(BF16)


def _cum_call(f_logit, b_forget_pad, blk):
    bsz, s, _ = f_logit.shape
    place = _bias_placement()
    return pl.pallas_call(
        functools.partial(_cum_kernel, blk=blk),
        grid=(bsz,),
        in_specs=[pl.BlockSpec((1, s, LANES), lambda b: (b, 0, 0)), _const_spec((1, LANES)),
                  _const_spec(place.shape)],
        out_specs=pl.BlockSpec((1, s, ATT_WIDTH), lambda b: (b, 0, 0)),
        out_shape=jax.ShapeDtypeStruct((bsz, s, ATT_WIDTH), BF16),
        compiler_params=_params("arbitrary"),
        name="forget_cumsum",
    )(f_logit, b_forget_pad, place)


ATT_VT_ROWS = ATT_HEAD_DIM + 16


def _attn_kernel(q_ref, k_ref, v_ref, kb_ref, o_ref, vt_sc, st_sc, m_sc, acc_sc, *, tq, tk):
    s = q_ref.shape[1]
    nq, nk, ratio = s // tq, s // tk, tq // tk
    lane = lax.broadcasted_iota(I32, (1, LANES), 1)
    head_lanes = (lane < ATT_HEAD_DIM, lane >= ATT_HEAD_DIM)
    one_lanes = ((lane >= ATT_HEAD_DIM) & (lane < ATT_HEAD_DIM + N_BIAS_PIECES), lane < N_BIAS_PIECES)
    key_minus_query = (lax.broadcasted_iota(I32, (tk, tq), 0) - lax.broadcasted_iota(I32, (tk, tq), 1))

    vt = v_ref[0].astype(F32).T
    ones = jnp.ones((ATT_VT_ROWS - ATT_HEAD_DIM, s), F32)
    for h in range(2):
        vth = jnp.concatenate([vt[h * ATT_HEAD_DIM:(h + 1) * ATT_HEAD_DIM, :], ones], axis=0).astype(BF16)
        for j in range(nk):
            vt_sc[h, j] = vth[:, j * tk:(j + 1) * tk]

    def q_body(qi, carry):
        q0 = pl.multiple_of(qi * tq, tq)
        qs = q_ref[0, pl.ds(q0, tq), :] * (ATT_HEAD_DIM ** -0.5)
        qa = [jnp.where(head_lanes[h], qs, one_lanes[h].astype(BF16)) for h in range(2)]
        m_sc[...] = jnp.full(m_sc.shape, NEG_BIG, F32)
        acc_sc[...] = jnp.zeros(acc_sc.shape, F32)

        def scores(kj):
            k0 = pl.multiple_of(kj * tk, tk)
            kb = k_ref[0, pl.ds(k0, tk), :]
            bias = kb_ref[0, pl.ds(k0, tk), :]
            return [_dot_nt(jnp.where(head_lanes[h], kb, bias), qa[h]) for h in range(2)]

        def accumulate(kj, masked):
            for h in range(2):
                st = st_sc[h]
                if masked:
                    st = jnp.where(key_minus_query <= q0 - kj * tk, st, NEG_BIG)
                m_old = m_sc[h]
                m_new = jnp.maximum(m_old, jnp.max(st, axis=0, keepdims=True))
                p = jnp.exp(st - m_new).astype(BF16)
                acc_sc[h] = jnp.exp(m_old - m_new) * acc_sc[h] + _dot(vt_sc[h, kj], p)
                m_sc[h] = m_new

        def put(sts):
            for h in range(2):
                st_sc[h] = sts[h]

        n_full = qi * ratio
        put(scores(0))

        def kv_body(kj, c2):
            nxt = scores(kj + 1)
            accumulate(kj, False)
            put(nxt)
            return c2

        lax.fori_loop(0, n_full, kv_body, 0)
        for d in range(ratio):
            nxt = scores(n_full + d + 1) if d + 1 < ratio else None
            accumulate(n_full + d, True)
            if nxt is not None:
                put(nxt)

        outs = []
        for h in range(2):
            acc = acc_sc[h]
            outs.append(acc[:ATT_HEAD_DIM, :] * (1.0 / acc[ATT_HEAD_DIM:ATT_HEAD_DIM + 1, :]))
        o_ref[0, pl.ds(q0, tq), :] = jnp.concatenate(outs, axis=0).T.astype(BF16)
        return carry

    lax.fori_loop(0, nq, q_body, 0)


def _attn_call(q, k, v, kbias, tq, tk):
    bsz, s, _ = q.shape
    qkv = pl.BlockSpec((1, s, LANES), lambda b, p: (b, 0, p))
    return pl.pallas_call(
        functools.partial(_attn_kernel, tq=tq, tk=tk),
        grid=(bsz, ATT_HEADS // 2),
        in_specs=[qkv, qkv, qkv, qkv],
        out_specs=qkv,
        out_shape=jax.ShapeDtypeStruct((bsz, s, ATT_WIDTH), BF16),
        scratch_shapes=[pltpu.VMEM((2, s // tk, ATT_VT_ROWS, tk), BF16), pltpu.VMEM((2, tk, tq), F32),
                        pltpu.VMEM((2, 1, tq), F32), pltpu.VMEM((2, ATT_VT_ROWS, tq), F32)],
        compiler_params=_params("arbitrary", "arbitrary"),
        name="fox_attention",
    )(q, k, v, kbias)


def _gelu_tanh(x):
    c = 0.7978845608028654
    return 0.5 * x * (1.0 + jnp.tanh(c * (x + 0.044715 * (x * x * x))))


def _rnn_kernel(xr_ref, gr_ref, wc_ref, bc_ref, wbd_ref, brg_ref, big_ref, lam_ref, o_ref,
                tail_sc, hc_sc, a_sc, b_sc, h_sc, *, ts):
    @pl.when(pl.program_id(1) == 0)
    def _():
        tail_sc[...] = jnp.zeros(tail_sc.shape, F32)
        hc_sc[...] = jnp.zeros(hc_sc.shape, F32)

    x = xr_ref[0].astype(F32)
    full = jnp.concatenate([tail_sc[...], x], axis=0)
    u = bc_ref[...] + wc_ref[3:4, :] * x
    for j in range(CONV_WIDTH - 1):
        off = SUBLANES - (CONV_WIDTH - 1) + j
        u = u + wc_ref[j:j + 1, :] * full[off:off + ts, :]
    tail_sc[...] = x[ts - SUBLANES:ts, :]

    lam = lam_ref[...]
    neg_sp = -(jnp.maximum(-lam, 0.0) + jnp.log1p(jnp.exp(-jnp.abs(lam))))
    for jt in range(RNN_WIDTH // MXU_DIM):
        cs = slice(jt * MXU_DIM, (jt + 1) * MXU_DIM)
        uc = u[:, cs]
        g = _dot(uc.astype(BF16), wbd_ref[jt])
        r = _sigmoid(g[:, :MXU_DIM] + brg_ref[:, cs])
        ig = _sigmoid(g[:, MXU_DIM:] + big_ref[:, cs])
        log_a = RGLRU_C * r * neg_sp[:, cs]
        a_sc[:, cs] = jnp.exp(log_a)
        b_sc[:, cs] = jnp.sqrt(1.0 - jnp.exp(2.0 * log_a)) * ig * uc

    srow = lax.broadcasted_iota(I32, (SUBLANES, RNN_WIDTH), 0)

    def scan_body(i, carry):
        r0 = pl.multiple_of(i * SUBLANES, SUBLANES)
        a = a_sc[pl.ds(r0, SUBLANES), :]
        b = b_sc[pl.ds(r0, SUBLANES), :]
        for sh in (1, 2, 4):
            keep = srow >= sh
            a_prev = pltpu.roll(a, sh, 0)
            b_prev = pltpu.roll(b, sh, 0)
            b = jnp.where(keep, b + a * b_prev, b)
            a = jnp.where(keep, a * a_prev, a)
        h = b + a * carry
        h_sc[pl.ds(r0, SUBLANES), :] = h
        return jnp.broadcast_to(h[SUBLANES - 1:SUBLANES, :], (SUBLANES, RNN_WIDTH))

    hc_sc[...] = lax.fori_loop(0, ts // SUBLANES, scan_body, hc_sc[...])
    o_ref[0] = (h_sc[...] * _gelu_tanh(gr_ref[0].astype(F32))).astype(BF16)


def _rnn_call(x_rnn, g_rnn, w_conv, b_conv, wbd, b_rg, b_ig, lam, ts):
    bsz, s, w = x_rnn.shape
    tok = pl.BlockSpec((1, ts, w), lambda b, i: (b, i, 0))
    return pl.pallas_call(
        functools.partial(_rnn_kernel, ts=ts),
        grid=(bsz, s // ts),
        in_specs=[tok, tok, _const_spec(w_conv.shape), _const_spec((1, w)), _const_spec(wbd.shape),
                  _const_spec((1, w)), _const_spec((1, w)), _const_spec((1, w))],
        out_specs=tok,
        out_shape=jax.ShapeDtypeStruct((bsz, s, w), BF16),
        scratch_shapes=[pltpu.VMEM((SUBLANES, w), F32), pltpu.VMEM((SUBLANES, w), F32),
                        pltpu.VMEM((ts, w), F32), pltpu.VMEM((ts, w), F32), pltpu.VMEM((ts, w), F32)],
        compiler_params=_params("arbitrary", "arbitrary"),
        name="rglru",
    )(x_rnn, g_rnn, w_conv, b_conv, wbd, b_rg, b_ig, lam)


def _route(h2, wr_hi_ref, wr_mid_ref, rbias_ref, cnt_ref, code_ref, wtok_ref, tm):
    h_hi, h_mid, _ = _split3(h2)
    logits = _dot_nt(wr_hi_ref[...], h_hi) + _dot_nt(wr_hi_ref[...], h_mid) + _dot_nt(wr_mid_ref[...], h_hi)
    scores = _sigmoid(logits)
    sel = scores + rbias_ref[...]

    giota = lax.broadcasted_iota(I32, (N_GROUPS, tm), 0)
    gs = jnp.zeros((N_GROUPS, tm), F32)
    for g in range(N_GROUPS):
        blk = sel[g * GROUP_SIZE:(g + 1) * GROUP_SIZE, :]
        m1 = jnp.max(blk, axis=0, keepdims=True)
        i1 = jnp.min(jnp.where(blk == m1, giota, N_EXPERTS), axis=0, keepdims=True)
        m2 = jnp.max(jnp.where(giota == i1, -jnp.inf, blk), axis=0, keepdims=True)
        gs = jnp.where(giota == g, m1 + m2, gs)

    gsel = jnp.zeros((N_GROUPS, tm), F32)
    for _ in range(TOPK_GROUPS):
        m = jnp.max(gs, axis=0, keepdims=True)
        idx = jnp.min(jnp.where(gs == m, giota, N_EXPERTS), axis=0, keepdims=True)
        hit = giota == idx
        gsel = jnp.where(hit, 1.0, gsel)
        gs = jnp.where(hit, -jnp.inf, gs)

    masked = jnp.concatenate(
        [jnp.where(gsel[g:g + 1, :] > 0.0, sel[g * GROUP_SIZE:(g + 1) * GROUP_SIZE, :], -jnp.inf)
         for g in range(N_GROUPS)], axis=0)

    eiota = lax.broadcasted_iota(I32, (N_EXPERTS, tm), 0)
    chosen = jnp.zeros((N_EXPERTS, tm), F32)
    idxs, wts = [], []
    for _ in range(TOP_K):
        m = jnp.max(masked, axis=0, keepdims=True)
        idx = jnp.min(jnp.where(masked == m, eiota, N_EXPERTS), axis=0, keepdims=True)
        hit = eiota == idx
        wts.append(jnp.sum(jnp.where(hit, scores, 0.0), axis=0, keepdims=True))
        idxs.append(idx)
        chosen = jnp.where(hit, 1.0, chosen)
        masked = jnp.where(hit, -jnp.inf, masked)
    wsum = wts[0]
    for w in wts[1:]:
        wsum = wsum + w

    srow = lax.broadcasted_iota(I32, (tm, tm), 0)
    scol = lax.broadcasted_iota(I32, (tm, tm), 1)
    earlier = (srow < scol).astype(BF16)
    rank = cnt_ref[...] + _dot(chosen.astype(BF16), earlier)
    cnt_ref[...] = cnt_ref[...] + jnp.sum(chosen, axis=1, keepdims=True)

    riota = lax.broadcasted_iota(I32, (SUBLANES, tm), 0)
    code = jnp.zeros((SUBLANES, tm), I32)
    wrow = jnp.zeros((SUBLANES, tm), F32)
    for r in range(TOP_K):
        rk = jnp.sum(jnp.where(eiota == idxs[r], rank, 0.0), axis=0, keepdims=True)
        code = jnp.where(riota == r, idxs[r] * (1 << RANK_BITS) + rk.astype(I32), code)
        wrow = jnp.where(riota == r, wts[r] / wsum * ROUTE_SCALE, wrow)
    code_ref[0] = code
    wpad = jnp.concatenate([wrow, jnp.zeros((LANES - SUBLANES, tm), F32)], axis=0)
    wtok_ref[...] = wpad.T


ROW_CHUNKS = D_MODEL // LANES


def _store_token_rows(rows_ref, val, n):
    for c in range(ROW_CHUNKS):
        rows_ref[pl.ds(c, n, stride=ROW_CHUNKS), :] = val[:, c * LANES:(c + 1) * LANES]


def _load_token_rows(rows_ref, n):
    return jnp.concatenate([rows_ref[pl.ds(c, n, stride=ROW_CHUNKS), :] for c in range(ROW_CHUNKS)], axis=-1)


def _merge_kernel(ya_ref, yr_ref, ga_ref, gb_ref, x_ref, gm_ref, shf_ref, scf_ref, gpost_ref, gpre_ref,
                  wba_ref, wbr_ref, wout_ref, wr_hi_ref, wr_mid_ref, rbias_ref,
                  x1_ref, h2_ref, h2rows_ref, code_ref, wtok_ref, cnt_ref, *, tm):
    @pl.when((pl.program_id(0) == 0) & (pl.program_id(1) == 0))
    def _():
        cnt_ref[...] = jnp.zeros(cnt_ref.shape, F32)

    pa = _dot(ya_ref[0], wba_ref[...])
    pr = _dot(yr_ref[0], wbr_ref[...])
    merged = _sigmoid(ga_ref[0].astype(F32)) * pa + _sigmoid(gb_ref[0].astype(F32)) * pr
    o = _dot(merged.astype(BF16), wout_ref[...])
    x1 = x_ref[0] + gm_ref[0] * (_rms_scale(o) * gpost_ref[...])
    x1_ref[0] = x1
    h2 = (_rms_scale(x1) * gpre_ref[...]) * (1.0 + scf_ref[0]) + shf_ref[0]
    h2_ref[0] = h2.astype(BF16)
    _store_token_rows(h2rows_ref, h2, tm)
    _route(h2, wr_hi_ref, wr_mid_ref, rbias_ref, cnt_ref, code_ref, wtok_ref, tm)


def _merge_call(y_attn, y_rnn, gate_a, gate_r, x, gate_m, shift_f, scale_f, g_post, g_pre,
                wba, wbr, wout, wr_hi, wr_mid, rbias, tm):
    bsz, s, d = x.shape
    nt = s // tm
    tok = lambda w: pl.BlockSpec((1, tm, w), lambda b, i: (b, i, 0))
    mod = pl.BlockSpec((1, 1, d), lambda b, i: (b, 0, 0))
    return pl.pallas_call(
        functools.partial(_merge_kernel, tm=tm),
        grid=(bsz, nt),
        in_specs=[tok(ATT_WIDTH), tok(d), tok(d), tok(d), tok(d), mod, mod, mod,
                  _const_spec((1, d)), _const_spec((1, d)),
                  _const_spec(wba.shape), _const_spec(wbr.shape), _const_spec(wout.shape),
                  _const_spec(wr_hi.shape), _const_spec(wr_mid.shape), _const_spec(rbias.shape)],
        out_specs=[tok(d), tok(d),
                   pl.BlockSpec((tm * ROW_CHUNKS, LANES), lambda b, i: (b * nt + i, 0)),
                   pl.BlockSpec((1, SUBLANES, tm), lambda b, i: (b * nt + i, 0, 0)),
                   pl.BlockSpec((tm, LANES), lambda b, i: (b * nt + i, 0)),
                   pl.BlockSpec((N_EXPERTS, 1), lambda b, i: (0, 0))],
        out_shape=[jax.ShapeDtypeStruct((bsz, s, d), F32), jax.ShapeDtypeStruct((bsz, s, d), BF16),
                   jax.ShapeDtypeStruct((bsz * s * ROW_CHUNKS, LANES), F32),
                   jax.ShapeDtypeStruct((bsz * nt, SUBLANES, tm), I32),
                   jax.ShapeDtypeStruct((bsz * s, LANES), F32),
                   jax.ShapeDtypeStruct((N_EXPERTS, 1), F32)],
        compiler_params=_params("arbitrary", "arbitrary"),
        name="merge_route",
    )(y_attn, y_rnn, gate_a, gate_r, x, gate_m, shift_f, scale_f, g_post, g_pre,
      wba, wbr, wout, wr_hi, wr_mid, rbias)


ISSUE_UNROLL = 4
N_DMA_PRIORITIES = 2


def _slots_kernel(pstart_ref, code_ref, o_ref):
    code = code_ref[...]
    e = code >> RANK_BITS
    base = jnp.zeros_like(code)
    for j in range(N_EXPERTS):
        base = jnp.where(e == j, pstart_ref[j], base)
    o_ref[...] = (base + (code & RANK_MASK)) * ROW_CHUNKS


def _slots_call(pstart, code):
    return pl.pallas_call(
        _slots_kernel,
        in_specs=[pl.BlockSpec(memory_space=pltpu.SMEM), pl.BlockSpec(memory_space=pltpu.VMEM)],
        out_specs=pl.BlockSpec(memory_space=pltpu.VMEM),
        out_shape=jax.ShapeDtypeStruct(code.shape, I32),
        compiler_params=pltpu.CompilerParams(vmem_limit_bytes=VMEM_LIMIT_BYTES),
        name="moe_slots",
    )(pstart, code)


def _issue_token_rows(n_tokens, make_copies):
    def body(i, c):
        for u in range(ISSUE_UNROLL):
            for k, cp in enumerate(make_copies(i * ISSUE_UNROLL + u)):
                cp.start(priority=k % N_DMA_PRIORITIES)
        return c

    lax.fori_loop(0, n_tokens // ISSUE_UNROLL, body, 0)


def _scatter_kernel(slot_ref, pstart_ref, pend_ref, h_ref, xs_hbm, zero_sc, sem, *, tm, rows, n_blocks):
    blk = rows * ROW_CHUNKS

    @pl.when(pl.program_id(0) == 0)
    def _():
        zero_sc[...] = jnp.zeros(zero_sc.shape, F32)
        n_used = pend_ref[N_EXPERTS - 1] // rows

        def zero_block(b):
            return pltpu.make_async_copy(zero_sc, xs_hbm.at[pl.ds(pl.multiple_of(b * blk, blk), blk)], sem)

        def zstart(e, c):
            @pl.when(pend_ref[e] > pstart_ref[e])
            def _():
                zero_block(pend_ref[e] // rows - 1).start()
            return c

        def zwait(e, c):
            @pl.when(pend_ref[e] > pstart_ref[e])
            def _():
                zero_block(0).wait()
            return c

        def tstart(b, c):
            zero_block(b).start()
            return c

        def twait(b, c):
            zero_block(0).wait()
            return c

        lax.fori_loop(0, N_EXPERTS, zstart, 0)
        lax.fori_loop(n_used, n_blocks, tstart, 0)
        lax.fori_loop(0, N_EXPERTS, zwait, 0)
        lax.fori_loop(n_used, n_blocks, twait, 0)

    def copies(t):
        src = h_ref.at[pl.ds(pl.multiple_of(t * ROW_CHUNKS, ROW_CHUNKS), ROW_CHUNKS)]
        return [pltpu.make_async_copy(
            src, xs_hbm.at[pl.ds(pl.multiple_of(slot_ref[0, k, t], ROW_CHUNKS), ROW_CHUNKS)], sem)
            for k in range(TOP_K)]

    _issue_token_rows(tm, copies)
    for k in range(TOP_K):
        pltpu.make_async_copy(h_ref, xs_hbm.at[pl.ds(0, tm * ROW_CHUNKS)], sem).wait()


def _scatter_call(slots, pstart, pend, h2_rows, n_rows, tm, rows):
    t = h2_rows.shape[0] // ROW_CHUNKS
    smem = functools.partial(pl.BlockSpec, memory_space=pltpu.SMEM)
    return pl.pallas_call(
        functools.partial(_scatter_kernel, tm=tm, rows=rows, n_blocks=n_rows // rows),
        grid=(t // tm,),
        in_specs=[smem((1, SUBLANES, tm), lambda i: (i, 0, 0)), smem(), smem(),
                  pl.BlockSpec((tm * ROW_CHUNKS, LANES), lambda i: (i, 0))],
        out_specs=pl.BlockSpec(memory_space=pl.ANY),
        out_shape=jax.ShapeDtypeStruct((n_rows * ROW_CHUNKS, LANES), F32),
        scratch_shapes=[pltpu.VMEM((rows * ROW_CHUNKS, LANES), F32), pltpu.SemaphoreType.DMA],
        compiler_params=_params("arbitrary"),
        name="moe_dispatch",
    )(slots, pstart, pend, h2_rows)


def _expert_kernel(be_ref, nused_ref, xs_ref, wg_ref, wu_ref, wd_ref, ys_ref, *, rows):
    @pl.when(pl.program_id(0) < nused_ref[0])
    def _():
        xb = _load_token_rows(xs_ref, rows).astype(BF16)
        g = _dot(xb, wg_ref[0])
        u = _dot(xb, wu_ref[0])
        act = (g * _sigmoid(g)) * u
        _store_token_rows(ys_ref, _dot(act.astype(BF16), wd_ref[0]), rows)

    @pl.when(pl.program_id(0) >= nused_ref[0])
    def _():
        ys_ref[...] = jnp.zeros(ys_ref.shape, F32)


def _expert_call(block_e, n_used, xs, wg, wu, wd, rows):
    n_rows = xs.shape[0] // ROW_CHUNKS
    _, d, ff = wg.shape
    row_spec = pl.BlockSpec((rows * ROW_CHUNKS, LANES), lambda i, be, nu: (jnp.minimum(i, nu[0] - 1), 0))
    grid_spec = pltpu.PrefetchScalarGridSpec(
        num_scalar_prefetch=2,
        grid=(n_rows // rows,),
        in_specs=[row_spec,
                  pl.BlockSpec((1, d, ff), lambda i, be, nu: (be[i], 0, 0)),
                  pl.BlockSpec((1, d, ff), lambda i, be, nu: (be[i], 0, 0)),
                  pl.BlockSpec((1, ff, d), lambda i, be, nu: (be[i], 0, 0))],
        out_specs=pl.BlockSpec((rows * ROW_CHUNKS, LANES), lambda i, be, nu: (i, 0)),
    )
    return pl.pallas_call(
        functools.partial(_expert_kernel, rows=rows),
        grid_spec=grid_spec,
        out_shape=jax.ShapeDtypeStruct(xs.shape, F32),
        compiler_params=_params("arbitrary"),
        name="moe_experts",
    )(block_e, n_used, xs, wg, wu, wd)


def _final_kernel(slot0_ref, slotn_ref, x1_ref, h2_ref, wtok_ref, gf_ref, gpost_ref,
                  wsg_ref, wsu_ref, wsd_ref, ys_hbm, o_ref, gat_sc, sem, *, tm):
    nt = pl.num_programs(1)
    g = pl.program_id(0) * nt + pl.program_id(1)
    n_tiles = pl.num_programs(0) * nt
    cur = lax.rem(g, 2)

    def issue(slot_ref, buf):
        def copies(t):
            r0 = pl.multiple_of(t * ROW_CHUNKS, ROW_CHUNKS)
            return [pltpu.make_async_copy(
                ys_hbm.at[pl.ds(pl.multiple_of(slot_ref[0, k, t], ROW_CHUNKS), ROW_CHUNKS)],
                gat_sc.at[buf, k, pl.ds(r0, ROW_CHUNKS)], sem.at[buf]) for k in range(TOP_K)]
        _issue_token_rows(tm, copies)

    @pl.when(g == 0)
    def _():
        issue(slot0_ref, 0)

    @pl.when(g + 1 < n_tiles)
    def _():
        issue(slotn_ref, 1 - cur)

    hb = h2_ref[0]
    a = _dot(hb, wsg_ref[...])
    u = _dot(hb, wsu_ref[...])
    tot = _dot(((a * _sigmoid(a)) * u).astype(BF16), wsd_ref[...])

    for k in range(TOP_K):
        pltpu.make_async_copy(ys_hbm.at[pl.ds(0, tm * ROW_CHUNKS)], gat_sc.at[cur, k], sem.at[cur]).wait()
    w = wtok_ref[...]
    routed = w[:, 0:1] * _load_token_rows(gat_sc.at[cur, 0], tm)
    for k in range(1, TOP_K):
        routed = routed + w[:, k:k + 1] * _load_token_rows(gat_sc.at[cur, k], tm)
    tot = routed + tot
    o_ref[0] = x1_ref[0] + gf_ref[0] * (_rms_scale(tot) * gpost_ref[...])


def _final_call(slots, x1, h2, wtok, gate_f, g_post, wsg, wsu, wsd, ys, tm):
    bsz, s, d = x1.shape
    nt = s // tm
    n_tiles = bsz * nt
    smem = functools.partial(pl.BlockSpec, memory_space=pltpu.SMEM)
    tok = pl.BlockSpec((1, tm, d), lambda b, i: (b, i, 0))
    return pl.pallas_call(
        functools.partial(_final_kernel, tm=tm),
        grid=(bsz, nt),
        in_specs=[smem((1, SUBLANES, tm), lambda b, i: (0, 0, 0)),
                  smem((1, SUBLANES, tm), lambda b, i: (jnp.minimum(b * nt + i + 1, n_tiles - 1), 0, 0)),
                  tok, tok,
                  pl.BlockSpec((tm, LANES), lambda b, i: (b * nt + i, 0)),
                  pl.BlockSpec((1, 1, d), lambda b, i: (b, 0, 0)),
                  _const_spec((1, d)),
                  _const_spec(wsg.shape), _const_spec(wsu.shape), _const_spec(wsd.shape),
                  pl.BlockSpec(memory_space=pl.ANY)],
        out_specs=tok,
        out_shape=jax.ShapeDtypeStruct((bsz, s, d), F32),
        scratch_shapes=[pltpu.VMEM((2, TOP_K, tm * ROW_CHUNKS, LANES), F32), pltpu.SemaphoreType.DMA((2,))],
        compiler_params=_params("arbitrary", "arbitrary"),
        name="moe_combine",
    )(slots, slots, x1, h2, wtok, gate_f, g_post, wsg, wsu, wsd, ys)


def _block_diag_tiles(w):
    per = MXU_DIM // RNN_BLOCK_DIM
    nt = RNN_BLOCKS // per
    w4 = w.reshape(nt, per, RNN_BLOCK_DIM, RNN_BLOCK_DIM)
    eye = jnp.eye(per, dtype=w.dtype)
    return jnp.einsum("tnij,nm->tnimj", w4, eye).reshape(nt, MXU_DIM, MXU_DIM)


def _tile(n, pref):
    t = min(n, pref)
    assert n % t == 0, (n, t)
    return t


def _layer(x, c, w_ada, b_ada, g_pre_mix, g_post_mix, g_pre_ffn, g_post_ffn, w_in, b_forget,
           w_conv, b_conv, w_rg, b_rg, w_ig, b_ig, lam, w_branch_attn, w_branch_rnn, w_out,
           w_router, router_bias, w_exp_gate, w_exp_up, w_exp_down, w_sh_gate, w_sh_up, w_sh_down):
    bsz, s, d = x.shape
    t = bsz * s
    row = lambda v: v.reshape(1, -1)

    mod = _ada_call(c, w_ada, b_ada)
    shift_m, scale_m, gate_m, shift_f, scale_f, gate_f = [
        mod[:, j * d:(j + 1) * d].reshape(bsz, 1, d) for j in range(6)]

    o_f = 3 * ATT_WIDTH
    o_r = o_f + ATT_HEADS
    wqkv = w_in[:, :o_f].astype(BF16)
    wf = jnp.pad(w_in[:, o_f:o_r], ((0, 0), (0, LANES - ATT_HEADS))).astype(BF16)
    wr = w_in[:, o_r:].astype(BF16)
    q, k, v, f_logit, x_rnn, g_rnn, gate_a, gate_r = _inproj_call(
        x, shift_m, scale_m, row(g_pre_mix), wqkv, wf, wr, _tile(s, 512))

    bf_pad = jnp.pad(b_forget, (0, LANES - ATT_HEADS)).reshape(1, LANES)
    kbias = _cum_call(f_logit, bf_pad, _tile(s, 256))
    tq = _tile(s, 512)
    y_attn = _attn_call(q, k, v, kbias, tq, _tile(tq, 256))

    wbd = jnp.concatenate([_block_diag_tiles(w_rg), _block_diag_tiles(w_ig)], axis=-1).astype(BF16)
    y_rnn = _rnn_call(x_rnn, g_rnn, w_conv, row(b_conv), wbd, row(b_rg), row(b_ig), row(lam), _tile(s, 256))

    wr_t = w_router.T
    wr_hi = wr_t.astype(BF16)
    wr_mid = (wr_t - wr_hi.astype(F32)).astype(BF16)
    tm_r = _tile(s, 512)
    x1, h2, h2_rows, code, wtok, counts = _merge_call(
        y_attn, y_rnn, gate_a, gate_r, x, gate_m, shift_f, scale_f, row(g_post_mix), row(g_pre_ffn),
        w_branch_attn.astype(BF16), w_branch_rnn.astype(BF16), w_out.astype(BF16),
        wr_hi, wr_mid, router_bias.reshape(N_EXPERTS, 1), tm_r)

    rows = 256
    n_blocks = -(-(t * TOP_K) // rows) + N_EXPERTS
    cnt = counts[:, 0].astype(I32)
    padded = ((cnt + rows - 1) // rows) * rows
    pend = jnp.cumsum(padded).astype(I32)
    pstart = pend - padded
    n_used = jnp.maximum(pend[-1] // rows, 1)
    blk = jnp.minimum(jnp.arange(n_blocks, dtype=I32), n_used - 1)
    block_e = jnp.minimum(jnp.sum((pend[None, :] <= (blk * rows)[:, None]).astype(I32), axis=1), N_EXPERTS - 1)

    slots = _slots_call(pstart, code)
    xs = _scatter_call(slots, pstart, pend, h2_rows, n_blocks * rows, tm_r, rows)
    ys = _expert_call(block_e, n_used.reshape(1), xs,
                      w_exp_gate.astype(BF16), w_exp_up.astype(BF16), w_exp_down.astype(BF16), rows)

    tm_f = _tile(tm_r, 128)
    slots_f = slots.reshape(bsz * s // tm_r, SUBLANES, tm_r // tm_f, tm_f)
    slots_f = jnp.transpose(slots_f, (0, 2, 1, 3)).reshape(t // tm_f, SUBLANES, tm_f)
    return _final_call(slots_f, x1, h2, wtok, gate_f, row(g_post_ffn),
                       w_sh_gate.astype(BF16), w_sh_up.astype(BF16), w_sh_down.astype(BF16), ys, tm_f)


def kernel(x, c, w_ada, b_ada, g_pre_mix, g_post_mix, g_pre_ffn, g_post_ffn, w_in, b_forget, w_conv, b_conv, w_rg, b_rg, w_ig, b_ig, rglru_lambda, w_branch_attn, w_branch_rnn, w_out, w_router, router_bias, w_exp_gate, w_exp_up, w_exp_down, w_sh_gate, w_sh_up, w_sh_down):
    depth = w_ada.shape[0]
    for l in range(depth):
        x = _layer(x, c, w_ada[l], b_ada[l], g_pre_mix[l], g_post_mix[l], g_pre_ffn[l], g_post_ffn[l],
                   w_in[l], b_forget[l], w_conv[l], b_conv[l], w_rg[l], b_rg[l], w_ig[l], b_ig[l],
                   rglru_lambda[l], w_branch_attn[l], w_branch_rnn[l], w_out[l], w_router[l],
                   router_bias[l], w_exp_gate[l], w_exp_up[l], w_exp_down[l],
                   w_sh_gate[l], w_sh_up[l], w_sh_down[l])
    return x
```

```python
import functools

import jax
import jax.numpy as jnp
from jax import lax
from jax.experimental import pallas as pl
from jax.experimental.pallas import tpu as pltpu

F32 = jnp.float32
BF16 = jnp.bfloat16
I32 = jnp.int32

D_MODEL = 1024
ATT_HEADS = 8
ATT_HEAD_DIM = 64
ATT_WIDTH = ATT_HEADS * ATT_HEAD_DIM
RNN_WIDTH = D_MODEL
RNN_BLOCKS = 16
RNN_BLOCK_DIM = RNN_WIDTH // RNN_BLOCKS
CONV_WIDTH = 4
RGLRU_C = 8.0
N_EXPERTS = 64
TOP_K = 6
N_GROUPS = 8
GROUP_SIZE = N_EXPERTS // N_GROUPS
TOPK_GROUPS = 4
EXPERT_FF = D_MODEL // 4
ROUTE_SCALE = 2.5
NORM_EPS = 1e-6

LANES = 128
SUBLANES = 8
MXU_DIM = 256
VMEM_LIMIT_BYTES = 56 * 1024 * 1024

NEG_BIG = -1e30
RANK_BITS = 20
RANK_MASK = (1 << RANK_BITS) - 1


def _sigmoid(x):
    return 1.0 / (1.0 + jnp.exp(-x))


def _rms_scale(x):
    return x * lax.rsqrt(jnp.mean(x * x, axis=-1, keepdims=True) + NORM_EPS)


def _dot(a, b):
    return jnp.dot(a, b, preferred_element_type=F32)


def _dot_nt(a, b):
    return lax.dot_general(a, b, (((1,), (1,)), ((), ())), preferred_element_type=F32)


def _split3(a):
    hi = a.astype(BF16)
    r1 = a - hi.astype(F32)
    mid = r1.astype(BF16)
    lo = (r1 - mid.astype(F32)).astype(BF16)
    return hi, mid, lo


def _params(*sem):
    return pltpu.CompilerParams(dimension_semantics=sem, vmem_limit_bytes=VMEM_LIMIT_BYTES)


def _const_spec(shape):
    nd = len(shape)
    return pl.BlockSpec(shape, lambda *_: (0,) * nd)


def _ada_kernel(c_ref, w_ref, b_ref, o_ref):
    c = c_ref[...]
    sc = c * _sigmoid(c)
    s_hi, s_mid, _ = _split3(sc)
    w_hi, w_mid, _ = _split3(w_ref[...])
    acc = _dot(s_hi, w_hi) + _dot(s_hi, w_mid) + _dot(s_mid, w_hi)
    o_ref[...] = acc + b_ref[...]


def _ada_call(c, w_ada, b_ada):
    bsz, d = c.shape
    n = w_ada.shape[1]
    tn = d
    return pl.pallas_call(
        _ada_kernel,
        grid=(n // tn,),
        in_specs=[
            pl.BlockSpec((bsz, d), lambda j: (0, 0)),
            pl.BlockSpec((d, tn), lambda j: (0, j)),
            pl.BlockSpec((1, tn), lambda j: (0, j)),
        ],
        out_specs=pl.BlockSpec((bsz, tn), lambda j: (0, j)),
        out_shape=jax.ShapeDtypeStruct((bsz, n), F32),
        compiler_params=_params("arbitrary"),
        name="ada_mod",
    )(c, w_ada, b_ada.reshape(1, n))


def _inproj_kernel(x_ref, sh_ref, sc_ref, g_ref, wqkv_ref, wf_ref, wr_ref,
                   q_ref, k_ref, v_ref, f_ref, xr_ref, gr_ref, ga_ref, gb_ref):
    x = x_ref[0]
    h = (_rms_scale(x) * g_ref[...]) * (1.0 + sc_ref[0]) + sh_ref[0]
    hb = h.astype(BF16)
    for j, ref in enumerate((q_ref, k_ref, v_ref)):
        ref[0] = _dot(hb, wqkv_ref[:, j * ATT_WIDTH:(j + 1) * ATT_WIDTH]).astype(BF16)
    f_ref[0] = _dot(hb, wf_ref[...])
    half = D_MODEL // 2
    for j, ref in enumerate((xr_ref, gr_ref, ga_ref, gb_ref)):
        for c in range(2):
            lo = j * D_MODEL + c * half
            ref[0, :, c * half:(c + 1) * half] = _dot(hb, wr_ref[:, lo:lo + half]).astype(BF16)


def _inproj_call(x, shift, scale, g, wqkv, wf, wr, tm):
    bsz, s, d = x.shape
    nt = s // tm
    tok = lambda w: pl.BlockSpec((1, tm, w), lambda b, i: (b, i, 0))
    mod = pl.BlockSpec((1, 1, d), lambda b, i: (b, 0, 0))
    out_w = (ATT_WIDTH,) * 3 + (LANES,) + (d,) * 4
    out_dt = (BF16,) * 3 + (F32,) + (BF16,) * 4
    return pl.pallas_call(
        _inproj_kernel,
        grid=(bsz, nt),
        in_specs=[tok(d), mod, mod, _const_spec((1, d)),
                  _const_spec(wqkv.shape), _const_spec(wf.shape), _const_spec(wr.shape)],
        out_specs=[tok(w) for w in out_w],
        out_shape=[jax.ShapeDtypeStruct((bsz, s, w), dt) for w, dt in zip(out_w, out_dt)],
        compiler_params=_params("arbitrary", "arbitrary"),
        name="in_proj",
    )(x, shift, scale, g, wqkv, wf, wr)


N_BIAS_PIECES = 3


def _bias_placement():
    h = jnp.arange(LANES)[:, None]
    lane = jnp.arange(ATT_WIDTH)[None, :]
    mats = []
    for piece in range(N_BIAS_PIECES):
        target = (h // 2) * LANES + ATT_HEAD_DIM * (1 - h % 2) + piece
        mats.append(((lane == target) & (h < ATT_HEADS)).astype(BF16))
    return jnp.stack(mats)


def _cum_kernel(f_ref, b_ref, place_ref, kb_ref, *, blk):
    s = f_ref.shape[1]
    row = lax.broadcasted_iota(I32, (blk, blk), 0)
    col = lax.broadcasted_iota(I32, (blk, blk), 1)
    tri = (row >= col).astype(BF16)
    carry = jnp.zeros((1, LANES), F32)
    for i in range(s // blk):
        z = f_ref[0, i * blk:(i + 1) * blk, :] + b_ref[...]
        lf = jnp.minimum(z, 0.0) - jnp.log1p(jnp.exp(-jnp.abs(z)))
        hi, mid, lo = _split3(lf)
        c = _dot(tri, hi) + _dot(tri, mid) + _dot(tri, lo) + carry
        carry = c[blk - 1:blk, :]
        pieces = _split3(-c)
        kb = _dot(pieces[0], place_ref[0])
        for j in range(1, N_BIAS_PIECES):
            kb = kb + _dot(pieces[j], place_ref[j])
        kb_ref[0, i * blk:(i + 1) * blk, :] = kb.astype(BF16)


def _cum_call(f_logit, b_forget_pad, blk):
    bsz, s, _ = f_logit.shape
    place = _bias_placement()
    return pl.pallas_call(
        functools.partial(_cum_kernel, blk=blk),
        grid=(bsz,),
        in_specs=[pl.BlockSpec((1, s, LANES), lambda b: (b, 0, 0)), _const_spec((1, LANES)),
                  _const_spec(place.shape)],
        out_specs=pl.BlockSpec((1, s, ATT_WIDTH), lambda b: (b, 0, 0)),
        out_shape=jax.ShapeDtypeStruct((bsz, s, ATT_WIDTH), BF16),
        compiler_params=_params("arbitrary"),
        name="forget_cumsum",
    )(f_logit, b_forget_pad, place)


ATT_VT_ROWS = ATT_HEAD_DIM + 16


def _attn_kernel(q_ref, k_ref, v_ref, kb_ref, o_ref, vt_sc, st_sc, m_sc, acc_sc, *, tq, tk):
    s = q_ref.shape[1]
    nq, nk, ratio = s // tq, s // tk, tq // tk
    lane = lax.broadcasted_iota(I32, (1, LANES), 1)
    head_lanes = (lane < ATT_HEAD_DIM, lane >= ATT_HEAD_DIM)
    one_lanes = ((lane >= ATT_HEAD_DIM) & (lane < ATT_HEAD_DIM + N_BIAS_PIECES), lane < N_BIAS_PIECES)
    key_minus_query = (lax.broadcasted_iota(I32, (tk, tq), 0) - lax.broadcasted_iota(I32, (tk, tq), 1))

    vt = v_ref[0].astype(F32).T
    ones = jnp.ones((ATT_VT_ROWS - ATT_HEAD_DIM, s), F32)
    for h in range(2):
        vth = jnp.concatenate([vt[h * ATT_HEAD_DIM:(h + 1) * ATT_HEAD_DIM, :], ones], axis=0).astype(BF16)
        for j in range(nk):
            vt_sc[h, j] = vth[:, j * tk:(j + 1) * tk]

    def q_body(qi, carry):
        q0 = pl.multiple_of(qi * tq, tq)
        qs = q_ref[0, pl.ds(q0, tq), :] * (ATT_HEAD_DIM ** -0.5)
        qa = [jnp.where(head_lanes[h], qs, one_lanes[h].astype(BF16)) for h in range(2)]
        m_sc[...] = jnp.full(m_sc.shape, NEG_BIG, F32)
        acc_sc[...] = jnp.zeros(acc_sc.shape, F32)

        def scores(kj):
            k0 = pl.multiple_of(kj * tk, tk)
            kb = k_ref[0, pl.ds(k0, tk), :]
            bias = kb_ref[0, pl.ds(k0, tk), :]
            return [_dot_nt(jnp.where(head_lanes[h], kb, bias), qa[h]) for h in range(2)]

        def accumulate(kj, masked):
            for h in range(2):
                st = st_sc[h]
                if masked:
                    st = jnp.where(key_minus_query <= q0 - kj * tk, st, NEG_BIG)
                m_old = m_sc[h]
                m_new = jnp.maximum(m_old, jnp.max(st, axis=0, keepdims=True))
                p = jnp.exp(st - m_new).astype(BF16)
                acc_sc[h] = jnp.exp(m_old - m_new) * acc_sc[h] + _dot(vt_sc[h, kj], p)
                m_sc[h] = m_new

        def put(sts):
            for h in range(2):
                st_sc[h] = sts[h]

        n_full = qi * ratio
        put(scores(0))

        def kv_body(kj, c2):
            nxt = scores(kj + 1)
            accumulate(kj, False)
            put(nxt)
            return c2

        lax.fori_loop(0, n_full, kv_body, 0)
        for d in range(ratio):
            nxt = scores(n_full + d + 1) if d + 1 < ratio else None
            accumulate(n_full + d, True)
            if nxt is not None:
                put(nxt)

        outs = []
        for h in range(2):
            acc = acc_sc[h]
            outs.append(acc[:ATT_HEAD_DIM, :] * (1.0 / acc[ATT_HEAD_DIM:ATT_HEAD_DIM + 1, :]))
        o_ref[0, pl.ds(q0, tq), :] = jnp.concatenate(outs, axis=0).T.astype(BF16)
        return carry

    lax.fori_loop(0, nq, q_body, 0)


def _attn_call(q, k, v, kbias, tq, tk):
    bsz, s, _ = q.shape
    qkv = pl.BlockSpec((1, s, LANES), lambda b, p: (b, 0, p))
    return pl.pallas_call(
        functools.partial(_attn_kernel, tq=tq, tk=tk),
        grid=(bsz, ATT_HEADS // 2),
        in_specs=[qkv, qkv, qkv, qkv],
        out_specs=qkv,
        out_shape=jax.ShapeDtypeStruct((bsz, s, ATT_WIDTH), BF16),
        scratch_shapes=[pltpu.VMEM((2, s // tk, ATT_VT_ROWS, tk), BF16), pltpu.VMEM((2, tk, tq), F32),
                        pltpu.VMEM((2, 1, tq), F32), pltpu.VMEM((2, ATT_VT_ROWS, tq), F32)],
        compiler_params=_params("arbitrary", "arbitrary"),
        name="fox_attention",
    )(q, k, v, kbias)


def _gelu_tanh(x):
    c = 0.7978845608028654
    return 0.5 * x * (1.0 + jnp.tanh(c * (x + 0.044715 * (x * x * x))))


def _rnn_kernel(xr_ref, gr_ref, wc_ref, bc_ref, wbd_ref, brg_ref, big_ref, lam_ref, o_ref,
                tail_sc, hc_sc, a_sc, b_sc, h_sc, *, ts):
    @pl.when(pl.program_id(1) == 0)
    def _():
        tail_sc[...] = jnp.zeros(tail_sc.shape, F32)
        hc_sc[...] = jnp.zeros(hc_sc.shape, F32)

    x = xr_ref[0].astype(F32)
    full = jnp.concatenate([tail_sc[...], x], axis=0)
    u = bc_ref[...] + wc_ref[3:4, :] * x
    for j in range(CONV_WIDTH - 1):
        off = SUBLANES - (CONV_WIDTH - 1) + j
        u = u + wc_ref[j:j + 1, :] * full[off:off + ts, :]
    tail_sc[...] = x[ts - SUBLANES:ts, :]

    lam = lam_ref[...]
    neg_sp = -(jnp.maximum(-lam, 0.0) + jnp.log1p(jnp.exp(-jnp.abs(lam))))
    for jt in range(RNN_WIDTH // MXU_DIM):
        cs = slice(jt * MXU_DIM, (jt + 1) * MXU_DIM)
        uc = u[:, cs]
        g = _dot(uc.astype(BF16), wbd_ref[jt])
        r = _sigmoid(g[:, :MXU_DIM] + brg_ref[:, cs])
        ig = _sigmoid(g[:, MXU_DIM:] + big_ref[:, cs])
        log_a = RGLRU_C * r * neg_sp[:, cs]
        a_sc[:, cs] = jnp.exp(log_a)
        b_sc[:, cs] = jnp.sqrt(1.0 - jnp.exp(2.0 * log_a)) * ig * uc

    srow = lax.broadcasted_iota(I32, (SUBLANES, RNN_WIDTH), 0)

    def scan_body(i, carry):
        r0 = pl.multiple_of(i * SUBLANES, SUBLANES)
        a = a_sc[pl.ds(r0, SUBLANES), :]
        b = b_sc[pl.ds(r0, SUBLANES), :]
        for sh in (1, 2, 4):
            keep = srow >= sh
            a_prev = pltpu.roll(a, sh, 0)
            b_prev = pltpu.roll(b, sh, 0)
            b = jnp.where(keep, b + a * b_prev, b)
            a = jnp.where(keep, a * a_prev, a)
        h = b + a * carry
        h_sc[pl.ds(r0, SUBLANES), :] = h
        return jnp.broadcast_to(h[SUBLANES - 1:SUBLANES, :], (SUBLANES, RNN_WIDTH))

    hc_sc[...] = lax.fori_loop(0, ts // SUBLANES, scan_body, hc_sc[...])
    o_ref[0] = (h_sc[...] * _gelu_tanh(gr_ref[0].astype(F32))).astype(BF16)


def _rnn_call(x_rnn, g_rnn, w_conv, b_conv, wbd, b_rg, b_ig, lam, ts):
    bsz, s, w = x_rnn.shape
    tok = pl.BlockSpec((1, ts, w), lambda b, i: (b, i, 0))
    return pl.pallas_call(
        functools.partial(_rnn_kernel, ts=ts),
        grid=(bsz, s // ts),
        in_specs=[tok, tok, _const_spec(w_conv.shape), _const_spec((1, w)), _const_spec(wbd.shape),
                  _const_spec((1, w)), _const_spec((1, w)), _const_spec((1, w))],
        out_specs=tok,
        out_shape=jax.ShapeDtypeStruct((bsz, s, w), BF16),
        scratch_shapes=[pltpu.VMEM((SUBLANES, w), F32), pltpu.VMEM((SUBLANES, w), F32),
                        pltpu.VMEM((ts, w), F32), pltpu.VMEM((ts, w), F32), pltpu.VMEM((ts, w), F32)],
        compiler_params=_params("arbitrary", "arbitrary"),
        name="rglru",
    )(x_rnn, g_rnn, w_conv, b_conv, wbd, b_rg, b_ig, lam)


def _route(h2, wr_hi_ref, wr_mid_ref, rbias_ref, cnt_ref, code_ref, wtok_ref, tm):
    h_hi, h_mid, _ = _split3(h2)
    logits = _dot_nt(wr_hi_ref[...], h_hi) + _dot_nt(wr_hi_ref[...], h_mid) + _dot_nt(wr_mid_ref[...], h_hi)
    scores = _sigmoid(logits)
    sel = scores + rbias_ref[...]

    giota = lax.broadcasted_iota(I32, (N_GROUPS, tm), 0)
    gs = jnp.zeros((N_GROUPS, tm), F32)
    for g in range(N_GROUPS):
        blk = sel[g * GROUP_SIZE:(g + 1) * GROUP_SIZE, :]
        m1 = jnp.max(blk, axis=0, keepdims=True)
        i1 = jnp.min(jnp.where(blk == m1, giota, N_EXPERTS), axis=0, keepdims=True)
        m2 = jnp.max(jnp.where(giota == i1, -jnp.inf, blk), axis=0, keepdims=True)
        gs = jnp.where(giota == g, m1 + m2, gs)

    gsel = jnp.zeros((N_GROUPS, tm), F32)
    for _ in range(TOPK_GROUPS):
        m = jnp.max(gs, axis=0, keepdims=True)
        idx = jnp.min(jnp.where(gs == m, giota, N_EXPERTS), axis=0, keepdims=True)
        hit = giota == idx
        gsel = jnp.where(hit, 1.0, gsel)
        gs = jnp.where(hit, -jnp.inf, gs)

    masked = jnp.concatenate(
        [jnp.where(gsel[g:g + 1, :] > 0.0, sel[g * GROUP_SIZE:(g + 1) * GROUP_SIZE, :], -jnp.inf)
         for g in range(N_GROUPS)], axis=0)

    eiota = lax.broadcasted_iota(I32, (N_EXPERTS, tm), 0)
    chosen = jnp.zeros((N_EXPERTS, tm), F32)
    idxs, wts = [], []
    for _ in range(TOP_K):
        m = jnp.max(masked, axis=0, keepdims=True)
        idx = jnp.min(jnp.where(masked == m, eiota, N_EXPERTS), axis=0, keepdims=True)
        hit = eiota == idx
        wts.append(jnp.sum(jnp.where(hit, scores, 0.0), axis=0, keepdims=True))
        idxs.append(idx)
        chosen = jnp.where(hit, 1.0, chosen)
        masked = jnp.where(hit, -jnp.inf, masked)
    wsum = wts[0]
    for w in wts[1:]:
        wsum = wsum + w

    srow = lax.broadcasted_iota(I32, (tm, tm), 0)
    scol = lax.broadcasted_iota(I32, (tm, tm), 1)
    earlier = (srow < scol).astype(BF16)
    rank = cnt_ref[...] + _dot(chosen.astype(BF16), earlier)
    cnt_ref[...] = cnt_ref[...] + jnp.sum(chosen, axis=1, keepdims=True)

    riota = lax.broadcasted_iota(I32, (SUBLANES, tm), 0)
    code = jnp.zeros((SUBLANES, tm), I32)
    wrow = jnp.zeros((SUBLANES, tm), F32)
    for r in range(TOP_K):
        rk = jnp.sum(jnp.where(eiota == idxs[r], rank, 0.0), axis=0, keepdims=True)
        code = jnp.where(riota == r, idxs[r] * (1 << RANK_BITS) + rk.astype(I32), code)
        wrow = jnp.where(riota == r, wts[r] / wsum * ROUTE_SCALE, wrow)
    code_ref[0] = code
    wpad = jnp.concatenate([wrow, jnp.zeros((LANES - SUBLANES, tm), F32)], axis=0)
    wtok_ref[...] = wpad.T


U32 = jnp.uint32
PACK_ROWS = D_MODEL // (2 * LANES)
HIGH_HALF = 0xFFFF0000


def _store_token_rows(rows_ref, val):
    half = D_MODEL // 2
    n = val.shape[0]
    bits = lax.bitcast_convert_type(val.astype(BF16).astype(F32), U32)
    words = (bits[:, half:] & jnp.uint32(HIGH_HALF)) | (bits[:, :half] >> 16)
    for c in range(PACK_ROWS):
        chunk = words[:, c * LANES:(c + 1) * LANES]
        if len(rows_ref.shape) == 2:
            rows_ref[pl.ds(c, n, stride=PACK_ROWS), :] = chunk
        else:
            rows_ref[:, c, :] = chunk


def _load_token_rows(rows_ref):
    if len(rows_ref.shape) == 2:
        n = rows_ref.shape[0] // PACK_ROWS
        chunks = [rows_ref[pl.ds(c, n, stride=PACK_ROWS), :] for c in range(PACK_ROWS)]
    else:
        chunks = [rows_ref[:, c, :] for c in range(PACK_ROWS)]
    words = jnp.concatenate(chunks, axis=-1)
    lo = lax.bitcast_convert_type(words << 16, F32)
    hi = lax.bitcast_convert_type(words & jnp.uint32(HIGH_HALF), F32)
    return jnp.concatenate([lo, hi], axis=-1)


def _merge_kernel(ya_ref, yr_ref, ga_ref, gb_ref, x_ref, gm_ref, shf_ref, scf_ref, gpost_ref, gpre_ref,
                  wba_ref, wbr_ref, wout_ref, wr_hi_ref, wr_mid_ref, rbias_ref,
                  x1_ref, h2_ref, h2rows_ref, code_ref, wtok_ref, cnt_ref, *, tm):
    @pl.when((pl.program_id(0) == 0) & (pl.program_id(1) == 0))
    def _():
        cnt_ref[...] = jnp.zeros(cnt_ref.shape, F32)

    pa = _dot(ya_ref[0], wba_ref[...])
    pr = _dot(yr_ref[0], wbr_ref[...])
    merged = _sigmoid(ga_ref[0].astype(F32)) * pa + _sigmoid(gb_ref[0].astype(F32)) * pr
    o = _dot(merged.astype(BF16), wout_ref[...])
    x1 = x_ref[0] + gm_ref[0] * (_rms_scale(o) * gpost_ref[...])
    x1_ref[0] = x1
    h2 = (_rms_scale(x1) * gpre_ref[...]) * (1.0 + scf_ref[0]) + shf_ref[0]
    h2_ref[0] = h2.astype(BF16)
    _store_token_rows(h2rows_ref, h2)
    _route(h2, wr_hi_ref, wr_mid_ref, rbias_ref, cnt_ref, code_ref, wtok_ref, tm)


def _merge_call(y_attn, y_rnn, gate_a, gate_r, x, gate_m, shift_f, scale_f, g_post, g_pre,
                wba, wbr, wout, wr_hi, wr_mid, rbias, tm):
    bsz, s, d = x.shape
    nt = s // tm
    tok = lambda w: pl.BlockSpec((1, tm, w), lambda b, i: (b, i, 0))
    mod = pl.BlockSpec((1, 1, d), lambda b, i: (b, 0, 0))
    return pl.pallas_call(
        functools.partial(_merge_kernel, tm=tm),
        grid=(bsz, nt),
        in_specs=[tok(ATT_WIDTH), tok(d), tok(d), tok(d), tok(d), mod, mod, mod,
                  _const_spec((1, d)), _const_spec((1, d)),
                  _const_spec(wba.shape), _const_spec(wbr.shape), _const_spec(wout.shape),
                  _const_spec(wr_hi.shape), _const_spec(wr_mid.shape), _const_spec(rbias.shape)],
        out_specs=[tok(d), tok(d),
                   pl.BlockSpec((tm * PACK_ROWS, LANES), lambda b, i: (b * nt + i, 0)),
                   pl.BlockSpec((1, SUBLANES, tm), lambda b, i: (b * nt + i, 0, 0)),
                   pl.BlockSpec((tm, LANES), lambda b, i: (b * nt + i, 0)),
                   pl.BlockSpec((N_EXPERTS, 1), lambda b, i: (0, 0))],
        out_shape=[jax.ShapeDtypeStruct((bsz, s, d), F32), jax.ShapeDtypeStruct((bsz, s, d), BF16),
                   jax.ShapeDtypeStruct((bsz * s * PACK_ROWS, LANES), U32),
                   jax.ShapeDtypeStruct((bsz * nt, SUBLANES, tm), I32),
                   jax.ShapeDtypeStruct((bsz * s, LANES), F32),
                   jax.ShapeDtypeStruct((N_EXPERTS, 1), F32)],
        compiler_params=_params("arbitrary", "arbitrary"),
        name="merge_route",
    )(y_attn, y_rnn, gate_a, gate_r, x, gate_m, shift_f, scale_f, g_post, g_pre,
      wba, wbr, wout, wr_hi, wr_mid, rbias)


ISSUE_UNROLL = 4
N_DMA_PRIORITIES = 2


def _slots_kernel(pstart_ref, code_ref, o_ref):
    code = code_ref[...]
    e = code >> RANK_BITS
    base = jnp.zeros_like(code)
    for j in range(N_EXPERTS):
        base = jnp.where(e == j, pstart_ref[j], base)
    o_ref[...] = base + (code & RANK_MASK)


def _slots_call(pstart, code):
    return pl.pallas_call(
        _slots_kernel,
        in_specs=[pl.BlockSpec(memory_space=pltpu.SMEM), pl.BlockSpec(memory_space=pltpu.VMEM)],
        out_specs=pl.BlockSpec(memory_space=pltpu.VMEM),
        out_shape=jax.ShapeDtypeStruct(code.shape, I32),
        compiler_params=pltpu.CompilerParams(vmem_limit_bytes=VMEM_LIMIT_BYTES),
        name="moe_slots",
    )(pstart, code)


def _issue_token_rows(n_tokens, make_copies, unroll=ISSUE_UNROLL):
    def body(i, c):
        for u in range(unroll):
            for k, cp in enumerate(make_copies(i * unroll + u)):
                cp.start(priority=k % N_DMA_PRIORITIES)
        return c

    if unroll == n_tokens:
        body(0, 0)
    else:
        lax.fori_loop(0, n_tokens // unroll, body, 0)


def _scatter_kernel(slot_ref, pstart_ref, pend_ref, h_ref, xs_hbm, zero_sc, sem, *, tm, rows, n_blocks):
    @pl.when(pl.program_id(0) == 0)
    def _():
        zero_sc[...] = jnp.zeros(zero_sc.shape, U32)
        n_used = pend_ref[N_EXPERTS - 1] // rows

        def zero_block(b):
            return pltpu.make_async_copy(zero_sc, xs_hbm.at[pl.ds(b * rows, rows)], sem)

        def zstart(e, c):
            @pl.when(pend_ref[e] > pstart_ref[e])
            def _():
                zero_block(pend_ref[e] // rows - 1).start()
            return c

        def zwait(e, c):
            @pl.when(pend_ref[e] > pstart_ref[e])
            def _():
                zero_block(0).wait()
            return c

        def tstart(b, c):
            zero_block(b).start()
            return c

        def twait(b, c):
            zero_block(0).wait()
            return c

        lax.fori_loop(0, N_EXPERTS, zstart, 0)
        lax.fori_loop(n_used, n_blocks, tstart, 0)
        lax.fori_loop(0, N_EXPERTS, zwait, 0)
        lax.fori_loop(n_used, n_blocks, twait, 0)

    def copies(t):
        return [pltpu.make_async_copy(h_ref.at[t], xs_hbm.at[slot_ref[0, k, t]], sem) for k in range(TOP_K)]

    _issue_token_rows(tm, copies)
    for k in range(TOP_K):
        pltpu.make_async_copy(h_ref, xs_hbm.at[pl.ds(0, tm)], sem).wait()


def _scatter_call(slots, pstart, pend, h2_rows, n_rows, tm, rows):
    t = h2_rows.shape[0]
    smem = functools.partial(pl.BlockSpec, memory_space=pltpu.SMEM)
    return pl.pallas_call(
        functools.partial(_scatter_kernel, tm=tm, rows=rows, n_blocks=n_rows // rows),
        grid=(t // tm,),
        in_specs=[smem((1, SUBLANES, tm), lambda i: (i, 0, 0)), smem(), smem(),
                  pl.BlockSpec((tm, PACK_ROWS, LANES), lambda i: (i, 0, 0))],
        out_specs=pl.BlockSpec(memory_space=pl.ANY),
        out_shape=jax.ShapeDtypeStruct((n_rows, PACK_ROWS, LANES), U32),
        scratch_shapes=[pltpu.VMEM((rows, PACK_ROWS, LANES), U32), pltpu.SemaphoreType.DMA],
        compiler_params=_params("arbitrary"),
        name="moe_dispatch",
    )(slots, pstart, pend, h2_rows)


def _expert_kernel(be_ref, nused_ref, xs_ref, wg_ref, wu_ref, wd_ref, ys_ref, *, rows):
    @pl.when(pl.program_id(0) < nused_ref[0])
    def _():
        xb = _load_token_rows(xs_ref).astype(BF16)
        g = _dot(xb, wg_ref[0])
        u = _dot(xb, wu_ref[0])
        act = (g * _sigmoid(g)) * u
        _store_token_rows(ys_ref, _dot(act.astype(BF16), wd_ref[0]))

    @pl.when(pl.program_id(0) >= nused_ref[0])
    def _():
        ys_ref[...] = jnp.zeros(ys_ref.shape, U32)


def _expert_call(block_e, n_used, xs, wg, wu, wd, rows):
    n_rows = xs.shape[0] // PACK_ROWS
    _, d, ff = wg.shape
    row_spec = pl.BlockSpec((rows * PACK_ROWS, LANES), lambda i, be, nu: (jnp.minimum(i, nu[0] - 1), 0))
    grid_spec = pltpu.PrefetchScalarGridSpec(
        num_scalar_prefetch=2,
        grid=(n_rows // rows,),
        in_specs=[row_spec,
                  pl.BlockSpec((1, d, ff), lambda i, be, nu: (be[i], 0, 0)),
                  pl.BlockSpec((1, d, ff), lambda i, be, nu: (be[i], 0, 0)),
                  pl.BlockSpec((1, ff, d), lambda i, be, nu: (be[i], 0, 0))],
        out_specs=pl.BlockSpec((rows * PACK_ROWS, LANES), lambda i, be, nu: (i, 0)),
    )
    return pl.pallas_call(
        functools.partial(_expert_kernel, rows=rows),
        grid_spec=grid_spec,
        out_shape=jax.ShapeDtypeStruct(xs.shape, U32),
        compiler_params=_params("arbitrary"),
        name="moe_experts",
    )(block_e, n_used, xs, wg, wu, wd)


def _final_kernel(slot0_ref, slotn_ref, x1_ref, h2_ref, wtok_ref, gf_ref, gpost_ref,
                  wsg_ref, wsu_ref, wsd_ref, ys_hbm, o_ref, gat_sc, sem, *, tm):
    nt = pl.num_programs(1)
    g = pl.program_id(0) * nt + pl.program_id(1)
    n_tiles = pl.num_programs(0) * nt
    cur = lax.rem(g, 2)

    def issue(slot_ref, buf, unroll):
        def copies(t):
            return [pltpu.make_async_copy(ys_hbm.at[slot_ref[0, k, t]], gat_sc.at[buf, k, t], sem.at[buf])
                    for k in range(TOP_K)]
        _issue_token_rows(tm, copies, unroll)

    def wait(buf):
        for k in range(TOP_K):
            pltpu.make_async_copy(ys_hbm.at[pl.ds(0, tm)], gat_sc.at[buf, k], sem.at[buf]).wait()

    @pl.when(g == 0)
    def _():
        issue(slot0_ref, 0, ISSUE_UNROLL)

    issue(slotn_ref, 1 - cur, tm)

    hb = h2_ref[0]
    a = _dot(hb, wsg_ref[...])
    u = _dot(hb, wsu_ref[...])
    tot = _dot(((a * _sigmoid(a)) * u).astype(BF16), wsd_ref[...])

    wait(cur)

    @pl.when(g == n_tiles - 1)
    def _():
        wait(1 - cur)

    w = wtok_ref[...]
    routed = w[:, 0:1] * _load_token_rows(gat_sc.at[cur, 0])
    for k in range(1, TOP_K):
        routed = routed + w[:, k:k + 1] * _load_token_rows(gat_sc.at[cur, k])
    tot = routed + tot
    o_ref[0] = x1_ref[0] + gf_ref[0] * (_rms_scale(tot) * gpost_ref[...])


def _final_call(slots, x1, h2, wtok, gate_f, g_post, wsg, wsu, wsd, ys, tm):
    bsz, s, d = x1.shape
    nt = s // tm
    n_tiles = bsz * nt
    smem = functools.partial(pl.BlockSpec, memory_space=pltpu.SMEM)
    tok = pl.BlockSpec((1, tm, d), lambda b, i: (b, i, 0))
    return pl.pallas_call(
        functools.partial(_final_kernel, tm=tm),
        grid=(bsz, nt),
        in_specs=[smem((1, SUBLANES, tm), lambda b, i: (0, 0, 0)),
                  smem((1, SUBLANES, tm), lambda b, i: (jnp.minimum(b * nt + i + 1, n_tiles - 1), 0, 0)),
                  tok, tok,
                  pl.BlockSpec((tm, LANES), lambda b, i: (b * nt + i, 0)),
                  pl.BlockSpec((1, 1, d), lambda b, i: (b, 0, 0)),
                  _const_spec((1, d)),
                  _const_spec(wsg.shape), _const_spec(wsu.shape), _const_spec(wsd.shape),
                  pl.BlockSpec(memory_space=pl.ANY)],
        out_specs=tok,
        out_shape=jax.ShapeDtypeStruct((bsz, s, d), F32),
        scratch_shapes=[pltpu.VMEM((2, TOP_K, tm, PACK_ROWS, LANES), U32), pltpu.SemaphoreType.DMA((2,))],
        compiler_params=_params("arbitrary", "arbitrary"),
        name="moe_combine",
    )(slots, slots, x1, h2, wtok, gate_f, g_post, wsg, wsu, wsd, ys)


def _block_diag_tiles(w):
    per = MXU_DIM // RNN_BLOCK_DIM
    nt = RNN_BLOCKS // per
    w4 = w.reshape(nt, per, RNN_BLOCK_DIM, RNN_BLOCK_DIM)
    eye = jnp.eye(per, dtype=w.dtype)
    return jnp.einsum("tnij,nm->tnimj", w4, eye).reshape(nt, MXU_DIM, MXU_DIM)


def _tile(n, pref):
    t = min(n, pref)
    assert n % t == 0, (n, t)
    return t


def _layer(x, c, w_ada, b_ada, g_pre_mix, g_post_mix, g_pre_ffn, g_post_ffn, w_in, b_forget,
           w_conv, b_conv, w_rg, b_rg, w_ig, b_ig, lam, w_branch_attn, w_branch_rnn, w_out,
           w_router, router_bias, w_exp_gate, w_exp_up, w_exp_down, w_sh_gate, w_sh_up, w_sh_down):
    bsz, s, d = x.shape
    t = bsz * s
    row = lambda v: v.reshape(1, -1)

    mod = _ada_call(c, w_ada, b_ada)
    shift_m, scale_m, gate_m, shift_f, scale_f, gate_f = [
        mod[:, j * d:(j + 1) * d].reshape(bsz, 1, d) for j in range(6)]

    o_f = 3 * ATT_WIDTH
    o_r = o_f + ATT_HEADS
    wqkv = w_in[:, :o_f].astype(BF16)
    wf = jnp.pad(w_in[:, o_f:o_r], ((0, 0), (0, LANES - ATT_HEADS))).astype(BF16)
    wr = w_in[:, o_r:].astype(BF16)
    q, k, v, f_logit, x_rnn, g_rnn, gate_a, gate_r = _inproj_call(
        x, shift_m, scale_m, row(g_pre_mix), wqkv, wf, wr, _tile(s, 512))

    bf_pad = jnp.pad(b_forget, (0, LANES - ATT_HEADS)).reshape(1, LANES)
    kbias = _cum_call(f_logit, bf_pad, _tile(s, 256))
    tq = _tile(s, 512)
    y_attn = _attn_call(q, k, v, kbias, tq, _tile(tq, 256))

    wbd = jnp.concatenate([_block_diag_tiles(w_rg), _block_diag_tiles(w_ig)], axis=-1).astype(BF16)
    y_rnn = _rnn_call(x_rnn, g_rnn, w_conv, row(b_conv), wbd, row(b_rg), row(b_ig), row(lam), _tile(s, 256))

    wr_t = w_router.T
    wr_hi = wr_t.astype(BF16)
    wr_mid = (wr_t - wr_hi.astype(F32)).astype(BF16)
    tm_r = _tile(s, 512)
    x1, h2, h2_rows, code, wtok, counts = _merge_call(
        y_attn, y_rnn, gate_a, gate_r, x, gate_m, shift_f, scale_f, row(g_post_mix), row(g_pre_ffn),
        w_branch_attn.astype(BF16), w_branch_rnn.astype(BF16), w_out.astype(BF16),
        wr_hi, wr_mid, router_bias.reshape(N_EXPERTS, 1), tm_r)

    rows = 512
    n_blocks = -(-(t * TOP_K) // rows) + N_EXPERTS
    cnt = counts[:, 0].astype(I32)
    padded = ((cnt + rows - 1) // rows) * rows
    pend = jnp.cumsum(padded).astype(I32)
    pstart = pend - padded
    n_used = jnp.maximum(pend[-1] // rows, 1)
    blk = jnp.minimum(jnp.arange(n_blocks, dtype=I32), n_used - 1)
    block_e = jnp.minimum(jnp.sum((pend[None, :] <= (blk * rows)[:, None]).astype(I32), axis=1), N_EXPERTS - 1)

    rows3 = lambda a: a.reshape(-1, PACK_ROWS, LANES)
    rows2 = lambda a: a.reshape(-1, LANES)
    slots = _slots_call(pstart, code)
    xs = _scatter_call(slots, pstart, pend, rows3(h2_rows), n_blocks * rows, tm_r, rows)
    ys = rows3(_expert_call(block_e, n_used.reshape(1), rows2(xs),
                            w_exp_gate.astype(BF16), w_exp_up.astype(BF16), w_exp_down.astype(BF16), rows))

    tm_f = _tile(tm_r, 128)
    slots_f = slots.reshape(bsz * s // tm_r, SUBLANES, tm_r // tm_f, tm_f)
    slots_f = jnp.transpose(slots_f, (0, 2, 1, 3)).reshape(t // tm_f, SUBLANES, tm_f)
    return _final_call(slots_f, x1, h2, wtok, gate_f, row(g_post_ffn),
                       w_sh_gate.astype(BF16), w_sh_up.astype(BF16), w_sh_down.astype(BF16), ys, tm_f)


def kernel(x, c, w_ada, b_ada, g_pre_mix, g_post_mix, g_pre_ffn, g_post_ffn, w_in, b_forget, w_conv, b_conv, w_rg, b_rg, w_ig, b_ig, rglru_lambda, w_branch_attn, w_branch_rnn, w_out, w_router, router_bias, w_exp_gate, w_exp_up, w_exp_down, w_sh_gate, w_sh_up, w_sh_down):
    depth = w_ada.shape[0]
    for l in range(depth):
        x = _layer(x, c, w_ada[l], b_ada[l], g_pre_mix[l], g_post_mix[l], g_pre_ffn[l], g_post_ffn[l],
                   w_in[l], b_forget[l], w_conv[l], b_conv[l], w_rg[l], b_rg[l], w_ig[l], b_ig[l],
                   rglru_lambda[l], w_branch_attn[l], w_branch_rnn[l], w_out[l], w_router[l],
                   router_bias[l], w_exp_gate[l], w_exp_up[l], w_exp_down[l],
                   w_sh_gate[l], w_sh_up[l], w_sh_down[l])
    return x
```

```python
import functools

import jax
import jax.numpy as jnp
from jax import lax
from jax.experimental import pallas as pl
from jax.experimental.pallas import tpu as pltpu

F32 = jnp.float32
BF16 = jnp.bfloat16
I32 = jnp.int32

D_MODEL = 1024
ATT_HEADS = 8
ATT_HEAD_DIM = 64
ATT_WIDTH = ATT_HEADS * ATT_HEAD_DIM
RNN_WIDTH = D_MODEL
RNN_BLOCKS = 16
RNN_BLOCK_DIM = RNN_WIDTH // RNN_BLOCKS
CONV_WIDTH = 4
RGLRU_C = 8.0
N_EXPERTS = 64
TOP_K = 6
N_GROUPS = 8
GROUP_SIZE = N_EXPERTS // N_GROUPS
TOPK_GROUPS = 4
EXPERT_FF = D_MODEL // 4
ROUTE_SCALE = 2.5
NORM_EPS = 1e-6

LANES = 128
SUBLANES = 8
MXU_DIM = 256
VMEM_LIMIT_BYTES = 56 * 1024 * 1024

NEG_BIG = -1e30
RANK_BITS = 20
RANK_MASK = (1 << RANK_BITS) - 1


def _sigmoid(x):
    return 0.5 * jnp.tanh(0.5 * x) + 0.5


def _rms_scale(x):
    return x * lax.rsqrt(jnp.mean(x * x, axis=-1, keepdims=True) + NORM_EPS)


def _dot(a, b):
    return jnp.dot(a, b, preferred_element_type=F32)


def _dot_nt(a, b):
    return lax.dot_general(a, b, (((1,), (1,)), ((), ())), preferred_element_type=F32)


def _split3(a):
    hi = a.astype(BF16)
    r1 = a - hi.astype(F32)
    mid = r1.astype(BF16)
    lo = (r1 - mid.astype(F32)).astype(BF16)
    return hi, mid, lo


def _params(*sem):
    return pltpu.CompilerParams(dimension_semantics=sem, vmem_limit_bytes=VMEM_LIMIT_BYTES)


def _const_spec(shape):
    nd = len(shape)
    return pl.BlockSpec(shape, lambda *_: (0,) * nd)


def _ada_kernel(c_ref, w_ref, b_ref, o_ref):
    c = c_ref[...]
    sc = c * _sigmoid(c)
    s_hi, s_mid, _ = _split3(sc)
    w_hi, w_mid, _ = _split3(w_ref[...])
    acc = _dot(s_hi, w_hi) + _dot(s_hi, w_mid) + _dot(s_mid, w_hi)
    o_ref[...] = acc + b_ref[...]


def _ada_call(c, w_ada, b_ada):
    bsz, d = c.shape
    n = w_ada.shape[1]
    tn = d
    return pl.pallas_call(
        _ada_kernel,
        grid=(n // tn,),
        in_specs=[
            pl.BlockSpec((bsz, d), lambda j: (0, 0)),
            pl.BlockSpec((d, tn), lambda j: (0, j)),
            pl.BlockSpec((1, tn), lambda j: (0, j)),
        ],
        out_specs=pl.BlockSpec((bsz, tn), lambda j: (0, j)),
        out_shape=jax.ShapeDtypeStruct((bsz, n), F32),
        compiler_params=_params("arbitrary"),
        name="ada_mod",
    )(c, w_ada, b_ada.reshape(1, n))


def _inproj_kernel(x_ref, sh_ref, sc_ref, g_ref, wqkv_ref, wf_ref, wr_ref,
                   q_ref, k_ref, v_ref, f_ref, xr_ref, gr_ref, ga_ref, gb_ref):
    x = x_ref[0]
    h = (_rms_scale(x) * g_ref[...]) * (1.0 + sc_ref[0]) + sh_ref[0]
    hb = h.astype(BF16)
    for j, ref in enumerate((q_ref, k_ref, v_ref)):
        ref[0] = _dot(hb, wqkv_ref[:, j * ATT_WIDTH:(j + 1) * ATT_WIDTH]).astype(BF16)
    f_ref[0] = _dot(hb, wf_ref[...])
    half = D_MODEL // 2
    for j, ref in enumerate((xr_ref, gr_ref, ga_ref, gb_ref)):
        for c in range(2):
            lo = j * D_MODEL + c * half
            ref[0, :, c * half:(c + 1) * half] = _dot(hb, wr_ref[:, lo:lo + half]).astype(BF16)


def _inproj_call(x, shift, scale, g, wqkv, wf, wr, tm):
    bsz, s, d = x.shape
    nt = s // tm
    tok = lambda w: pl.BlockSpec((1, tm, w), lambda b, i: (b, i, 0))
    mod = pl.BlockSpec((1, 1, d), lambda b, i: (b, 0, 0))
    out_w = (ATT_WIDTH,) * 3 + (LANES,) + (d,) * 4
    out_dt = (BF16,) * 3 + (F32,) + (BF16,) * 4
    return pl.pallas_call(
        _inproj_kernel,
        grid=(bsz, nt),
        in_specs=[tok(d), mod, mod, _const_spec((1, d)),
                  _const_spec(wqkv.shape), _const_spec(wf.shape), _const_spec(wr.shape)],
        out_specs=[tok(w) for w in out_w],
        out_shape=[jax.ShapeDtypeStruct((bsz, s, w), dt) for w, dt in zip(out_w, out_dt)],
        compiler_params=_params("arbitrary", "arbitrary"),
        name="in_proj",
    )(x, shift, scale, g, wqkv, wf, wr)


N_BIAS_PIECES = 3


def _bias_placement():
    h = jnp.arange(LANES)[:, None]
    lane = jnp.arange(ATT_WIDTH)[None, :]
    mats = []
    for piece in range(N_BIAS_PIECES):
        target = (h // 2) * LANES + ATT_HEAD_DIM * (1 - h % 2) + piece
        mats.append(((lane == target) & (h < ATT_HEADS)).astype(BF16))
    return jnp.stack(mats)


def _cum_kernel(f_ref, b_ref, place_ref, kb_ref, *, blk):
    s = f_ref.shape[1]
    row = lax.broadcasted_iota(I32, (blk, blk), 0)
    col = lax.broadcasted_iota(I32, (blk, blk), 1)
    tri = (row >= col).astype(BF16)
    carry = jnp.zeros((1, LANES), F32)
    for i in range(s // blk):
        z = f_ref[0, i * blk:(i + 1) * blk, :] + b_ref[...]
        lf = jnp.minimum(z, 0.0) - jnp.log1p(jnp.exp(-jnp.abs(z)))
        hi, mid, lo = _split3(lf)
        c = _dot(tri, hi) + _dot(tri, mid) + _dot(tri, lo) + carry
        carry = c[blk - 1:blk, :]
        pieces = _split3(-c)
        kb = _dot(pieces[0], place_ref[0])
        for j in range(1, N_BIAS_PIECES):
            kb = kb + _dot(pieces[j], place_ref[j])
        kb_ref[0, i * blk:(i + 1) * blk, :] = kb.astype(BF16)


def _cum_call(f_logit, b_forget_pad, blk):
    bsz, s, _ = f_logit.shape
    place = _bias_placement()
    return pl.pallas_call(
        functools.partial(_cum_kernel, blk=blk),
        grid=(bsz,),
        in_specs=[pl.BlockSpec((1, s, LANES), lambda b: (b, 0, 0)), _const_spec((1, LANES)),
                  _const_spec(place.shape)],
        out_specs=pl.BlockSpec((1, s, ATT_WIDTH), lambda b: (b, 0, 0)),
        out_shape=jax.ShapeDtypeStruct((bsz, s, ATT_WIDTH), BF16),
        compiler_params=_params("arbitrary"),
        name="forget_cumsum",
    )(f_logit, b_forget_pad, place)


ATT_VT_ROWS = ATT_HEAD_DIM + 16


def _attn_kernel(q_ref, k_ref, v_ref, kb_ref, o_ref, vt_sc, st_sc, m_sc, acc_sc, *, tq, tk):
    s = q_ref.shape[1]
    nq, nk, ratio = s // tq, s // tk, tq // tk
    lane = lax.broadcasted_iota(I32, (1, LANES), 1)
    head_lanes = (lane < ATT_HEAD_DIM, lane >= ATT_HEAD_DIM)
    one_lanes = ((lane >= ATT_HEAD_DIM) & (lane < ATT_HEAD_DIM + N_BIAS_PIECES), lane < N_BIAS_PIECES)
    key_minus_query = (lax.broadcasted_iota(I32, (tk, tq), 0) - lax.broadcasted_iota(I32, (tk, tq), 1))

    vt = v_ref[0].astype(F32).T
    ones = jnp.ones((ATT_VT_ROWS - ATT_HEAD_DIM, s), F32)
    for h in range(2):
        vth = jnp.concatenate([vt[h * ATT_HEAD_DIM:(h + 1) * ATT_HEAD_DIM, :], ones], axis=0).astype(BF16)
        for j in range(nk):
            vt_sc[h, j] = vth[:, j * tk:(j + 1) * tk]

    def q_body(qi, carry):
        q0 = pl.multiple_of(qi * tq, tq)
        qs = q_ref[0, pl.ds(q0, tq), :] * (ATT_HEAD_DIM ** -0.5)
        qa = [jnp.where(head_lanes[h], qs, one_lanes[h].astype(BF16)) for h in range(2)]
        m_sc[...] = jnp.full(m_sc.shape, NEG_BIG, F32)
        acc_sc[...] = jnp.zeros(acc_sc.shape, F32)

        def scores(kj):
            k0 = pl.multiple_of(kj * tk, tk)
            kb = k_ref[0, pl.ds(k0, tk), :]
            bias = kb_ref[0, pl.ds(k0, tk), :]
            return [_dot_nt(jnp.where(head_lanes[h], kb, bias), qa[h]) for h in range(2)]

        def accumulate(kj, masked):
            for h in range(2):
                st = st_sc[h]
                if masked:
                    st = jnp.where(key_minus_query <= q0 - kj * tk, st, NEG_BIG)
                m_old = m_sc[h]
                m_new = jnp.maximum(m_old, jnp.max(st, axis=0, keepdims=True))
                p = jnp.exp(st - m_new).astype(BF16)
                acc_sc[h] = jnp.exp(m_old - m_new) * acc_sc[h] + _dot(vt_sc[h, kj], p)
                m_sc[h] = m_new

        def put(sts):
            for h in range(2):
                st_sc[h] = sts[h]

        n_full = qi * ratio
        put(scores(0))

        def kv_body(kj, c2):
            nxt = scores(kj + 1)
            accumulate(kj, False)
            put(nxt)
            return c2

        lax.fori_loop(0, n_full, kv_body, 0)
        for d in range(ratio):
            nxt = scores(n_full + d + 1) if d + 1 < ratio else None
            accumulate(n_full + d, True)
            if nxt is not None:
                put(nxt)

        outs = []
        for h in range(2):
            acc = acc_sc[h]
            outs.append(acc[:ATT_HEAD_DIM, :] * (1.0 / acc[ATT_HEAD_DIM:ATT_HEAD_DIM + 1, :]))
        o_ref[0, pl.ds(q0, tq), :] = jnp.concatenate(outs, axis=0).T.astype(BF16)
        return carry

    lax.fori_loop(0, nq, q_body, 0)


def _attn_call(q, k, v, kbias, tq, tk):
    bsz, s, _ = q.shape
    qkv = pl.BlockSpec((1, s, LANES), lambda b, p: (b, 0, p))
    return pl.pallas_call(
        functools.partial(_attn_kernel, tq=tq, tk=tk),
        grid=(bsz, ATT_HEADS // 2),
        in_specs=[qkv, qkv, qkv, qkv],
        out_specs=qkv,
        out_shape=jax.ShapeDtypeStruct((bsz, s, ATT_WIDTH), BF16),
        scratch_shapes=[pltpu.VMEM((2, s // tk, ATT_VT_ROWS, tk), BF16), pltpu.VMEM((2, tk, tq), F32),
                        pltpu.VMEM((2, 1, tq), F32), pltpu.VMEM((2, ATT_VT_ROWS, tq), F32)],
        compiler_params=_params("arbitrary", "arbitrary"),
        name="fox_attention",
    )(q, k, v, kbias)


def _gelu_tanh(x):
    c = 0.7978845608028654
    return 0.5 * x * (1.0 + jnp.tanh(c * (x + 0.044715 * (x * x * x))))


def _rnn_kernel(xr_ref, gr_ref, wc_ref, bc_ref, wbd_ref, brg_ref, big_ref, lam_ref, o_ref,
                tail_sc, hc_sc, a_sc, b_sc, h_sc, *, ts):
    @pl.when(pl.program_id(1) == 0)
    def _():
        tail_sc[...] = jnp.zeros(tail_sc.shape, F32)
        hc_sc[...] = jnp.zeros(hc_sc.shape, F32)

    x = xr_ref[0].astype(F32)
    full = jnp.concatenate([tail_sc[...], x], axis=0)
    u = bc_ref[...] + wc_ref[3:4, :] * x
    for j in range(CONV_WIDTH - 1):
        off = SUBLANES - (CONV_WIDTH - 1) + j
        u = u + wc_ref[j:j + 1, :] * full[off:off + ts, :]
    tail_sc[...] = x[ts - SUBLANES:ts, :]

    lam = lam_ref[...]
    neg_sp = -(jnp.maximum(-lam, 0.0) + jnp.log1p(jnp.exp(-jnp.abs(lam))))
    for jt in range(RNN_WIDTH // MXU_DIM):
        cs = slice(jt * MXU_DIM, (jt + 1) * MXU_DIM)
        uc = u[:, cs]
        g = _dot(uc.astype(BF16), wbd_ref[jt])
        r = _sigmoid(g[:, :MXU_DIM] + brg_ref[:, cs])
        ig = _sigmoid(g[:, MXU_DIM:] + big_ref[:, cs])
        a = jnp.exp(r * (RGLRU_C * neg_sp[:, cs]))
        a_sc[:, cs] = a
        v = 1.0 - a * a
        b_sc[:, cs] = jnp.where(v > 0.0, v * lax.rsqrt(v), 0.0) * (ig * uc)

    srow = lax.broadcasted_iota(I32, (SUBLANES, RNN_WIDTH), 0)

    def scan_body(i, carry):
        r0 = pl.multiple_of(i * SUBLANES, SUBLANES)
        a = a_sc[pl.ds(r0, SUBLANES), :]
        b = b_sc[pl.ds(r0, SUBLANES), :]
        b = b + jnp.where(srow == 0, a * carry, 0.0)
        for sh in (1, 2, 4):
            keep = srow >= sh
            b = jnp.where(keep, b + a * pltpu.roll(b, sh, 0), b)
            if sh < SUBLANES // 2:
                a = jnp.where(keep, a * pltpu.roll(a, sh, 0), a)
        h_sc[pl.ds(r0, SUBLANES), :] = b
        return jnp.broadcast_to(b[SUBLANES - 1:SUBLANES, :], (SUBLANES, RNN_WIDTH))

    hc_sc[...] = lax.fori_loop(0, ts // SUBLANES, scan_body, hc_sc[...])
    o_ref[0] = (h_sc[...] * _gelu_tanh(gr_ref[0].astype(F32))).astype(BF16)


def _rnn_call(x_rnn, g_rnn, w_conv, b_conv, wbd, b_rg, b_ig, lam, ts):
    bsz, s, w = x_rnn.shape
    tok = pl.BlockSpec((1, ts, w), lambda b, i: (b, i, 0))
    return pl.pallas_call(
        functools.partial(_rnn_kernel, ts=ts),
        grid=(bsz, s // ts),
        in_specs=[tok, tok, _const_spec(w_conv.shape), _const_spec((1, w)), _const_spec(wbd.shape),
                  _const_spec((1, w)), _const_spec((1, w)), _const_spec((1, w))],
        out_specs=tok,
        out_shape=jax.ShapeDtypeStruct((bsz, s, w), BF16),
        scratch_shapes=[pltpu.VMEM((SUBLANES, w), F32), pltpu.VMEM((SUBLANES, w), F32),
                        pltpu.VMEM((ts, w), F32), pltpu.VMEM((ts, w), F32), pltpu.VMEM((ts, w), F32)],
        compiler_params=_params("arbitrary", "arbitrary"),
        name="rglru",
    )(x_rnn, g_rnn, w_conv, b_conv, wbd, b_rg, b_ig, lam)


def _route(h2, wr_hi_ref, wr_mid_ref, rbias_ref, cnt_ref, code_ref, wtok_ref, tm):
    h_hi, h_mid, _ = _split3(h2)
    logits = _dot_nt(wr_hi_ref[...], h_hi) + _dot_nt(wr_hi_ref[...], h_mid) + _dot_nt(wr_mid_ref[...], h_hi)
    scores = _sigmoid(logits)
    sel = scores + rbias_ref[...]

    giota = lax.broadcasted_iota(I32, (N_GROUPS, tm), 0)
    gs = jnp.zeros((N_GROUPS, tm), F32)
    for g in range(N_GROUPS):
        blk = sel[g * GROUP_SIZE:(g + 1) * GROUP_SIZE, :]
        m1 = jnp.max(blk, axis=0, keepdims=True)
        i1 = jnp.min(jnp.where(blk == m1, giota, N_EXPERTS), axis=0, keepdims=True)
        m2 = jnp.max(jnp.where(giota == i1, -jnp.inf, blk), axis=0, keepdims=True)
        gs = jnp.where(giota == g, m1 + m2, gs)

    gsel = jnp.zeros((N_GROUPS, tm), F32)
    for _ in range(TOPK_GROUPS):
        m = jnp.max(gs, axis=0, keepdims=True)
        idx = jnp.min(jnp.where(gs == m, giota, N_EXPERTS), axis=0, keepdims=True)
        hit = giota == idx
        gsel = jnp.where(hit, 1.0, gsel)
        gs = jnp.where(hit, -jnp.inf, gs)

    masked = jnp.concatenate(
        [jnp.where(gsel[g:g + 1, :] > 0.0, sel[g * GROUP_SIZE:(g + 1) * GROUP_SIZE, :], -jnp.inf)
         for g in range(N_GROUPS)], axis=0)

    eiota = lax.broadcasted_iota(I32, (N_EXPERTS, tm), 0)
    chosen = jnp.zeros((N_EXPERTS, tm), F32)
    idxs, wts = [], []
    for _ in range(TOP_K):
        m = jnp.max(masked, axis=0, keepdims=True)
        idx = jnp.min(jnp.where(masked == m, eiota, N_EXPERTS), axis=0, keepdims=True)
        hit = eiota == idx
        wts.append(jnp.sum(jnp.where(hit, scores, 0.0), axis=0, keepdims=True))
        idxs.append(idx)
        chosen = jnp.where(hit, 1.0, chosen)
        masked = jnp.where(hit, -jnp.inf, masked)
    wsum = wts[0]
    for w in wts[1:]:
        wsum = wsum + w

    srow = lax.broadcasted_iota(I32, (tm, tm), 0)
    scol = lax.broadcasted_iota(I32, (tm, tm), 1)
    earlier = (srow < scol).astype(BF16)
    rank = cnt_ref[...] + _dot(chosen.astype(BF16), earlier)
    cnt_ref[...] = cnt_ref[...] + jnp.sum(chosen, axis=1, keepdims=True)

    riota = lax.broadcasted_iota(I32, (SUBLANES, tm), 0)
    code = jnp.zeros((SUBLANES, tm), I32)
    wrow = jnp.zeros((SUBLANES, tm), F32)
    for r in range(TOP_K):
        rk = jnp.sum(jnp.where(eiota == idxs[r], rank, 0.0), axis=0, keepdims=True)
        code = jnp.where(riota == r, idxs[r] * (1 << RANK_BITS) + rk.astype(I32), code)
        wrow = jnp.where(riota == r, wts[r] / wsum * ROUTE_SCALE, wrow)
    code_ref[0] = code
    wpad = jnp.concatenate([wrow, jnp.zeros((LANES - SUBLANES, tm), F32)], axis=0)
    wtok_ref[...] = wpad.T


U32 = jnp.uint32
PACK_ROWS = D_MODEL // (2 * LANES)
HIGH_HALF = 0xFFFF0000


def _store_token_rows(rows_ref, val):
    half = D_MODEL // 2
    n = val.shape[0]
    bits = lax.bitcast_convert_type(val.astype(BF16).astype(F32), U32)
    words = (bits[:, half:] & jnp.uint32(HIGH_HALF)) | (bits[:, :half] >> 16)
    for c in range(PACK_ROWS):
        chunk = words[:, c * LANES:(c + 1) * LANES]
        if len(rows_ref.shape) == 2:
            rows_ref[pl.ds(c, n, stride=PACK_ROWS), :] = chunk
        else:
            rows_ref[:, c, :] = chunk


def _load_token_rows(rows_ref):
    if len(rows_ref.shape) == 2:
        n = rows_ref.shape[0] // PACK_ROWS
        chunks = [rows_ref[pl.ds(c, n, stride=PACK_ROWS), :] for c in range(PACK_ROWS)]
    else:
        chunks = [rows_ref[:, c, :] for c in range(PACK_ROWS)]
    words = jnp.concatenate(chunks, axis=-1)
    lo = lax.bitcast_convert_type(words << 16, F32)
    hi = lax.bitcast_convert_type(words & jnp.uint32(HIGH_HALF), F32)
    return jnp.concatenate([lo, hi], axis=-1)


def _merge_kernel(ya_ref, yr_ref, ga_ref, gb_ref, x_ref, gm_ref, shf_ref, scf_ref, gpost_ref, gpre_ref,
                  wba_ref, wbr_ref, wout_ref, wr_hi_ref, wr_mid_ref, rbias_ref,
                  x1_ref, h2_ref, h2rows_ref, code_ref, wtok_ref, cnt_ref, *, tm):
    @pl.when((pl.program_id(0) == 0) & (pl.program_id(1) == 0))
    def _():
        cnt_ref[...] = jnp.zeros(cnt_ref.shape, F32)

    pa = _dot(ya_ref[0], wba_ref[...])
    pr = _dot(yr_ref[0], wbr_ref[...])
    merged = _sigmoid(ga_ref[0].astype(F32)) * pa + _sigmoid(gb_ref[0].astype(F32)) * pr
    o = _dot(merged.astype(BF16), wout_ref[...])
    x1 = x_ref[0] + gm_ref[0] * (_rms_scale(o) * gpost_ref[...])
    x1_ref[0] = x1
    h2 = (_rms_scale(x1) * gpre_ref[...]) * (1.0 + scf_ref[0]) + shf_ref[0]
    h2_ref[0] = h2.astype(BF16)
    _store_token_rows(h2rows_ref, h2)
    _route(h2, wr_hi_ref, wr_mid_ref, rbias_ref, cnt_ref, code_ref, wtok_ref, tm)


def _merge_call(y_attn, y_rnn, gate_a, gate_r, x, gate_m, shift_f, scale_f, g_post, g_pre,
                wba, wbr, wout, wr_hi, wr_mid, rbias, tm):
    bsz, s, d = x.shape
    nt = s // tm
    tok = lambda w: pl.BlockSpec((1, tm, w), lambda b, i: (b, i, 0))
    mod = pl.BlockSpec((1, 1, d), lambda b, i: (b, 0, 0))
    return pl.pallas_call(
        functools.partial(_merge_kernel, tm=tm),
        grid=(bsz, nt),
        in_specs=[tok(ATT_WIDTH), tok(d), tok(d), tok(d), tok(d), mod, mod, mod,
                  _const_spec((1, d)), _const_spec((1, d)),
                  _const_spec(wba.shape), _const_spec(wbr.shape), _const_spec(wout.shape),
                  _const_spec(wr_hi.shape), _const_spec(wr_mid.shape), _const_spec(rbias.shape)],
        out_specs=[tok(d), tok(d),
                   pl.BlockSpec((tm * PACK_ROWS, LANES), lambda b, i: (b * nt + i, 0)),
                   pl.BlockSpec((1, SUBLANES, tm), lambda b, i: (b * nt + i, 0, 0)),
                   pl.BlockSpec((tm, LANES), lambda b, i: (b * nt + i, 0)),
                   pl.BlockSpec((N_EXPERTS, 1), lambda b, i: (0, 0))],
        out_shape=[jax.ShapeDtypeStruct((bsz, s, d), F32), jax.ShapeDtypeStruct((bsz, s, d), BF16),
                   jax.ShapeDtypeStruct((bsz * s * PACK_ROWS, LANES), U32),
                   jax.ShapeDtypeStruct((bsz * nt, SUBLANES, tm), I32),
                   jax.ShapeDtypeStruct((bsz * s, LANES), F32),
                   jax.ShapeDtypeStruct((N_EXPERTS, 1), F32)],
        compiler_params=_params("arbitrary", "arbitrary"),
        name="merge_route",
    )(y_attn, y_rnn, gate_a, gate_r, x, gate_m, shift_f, scale_f, g_post, g_pre,
      wba, wbr, wout, wr_hi, wr_mid, rbias)


ISSUE_UNROLL = 4
N_DMA_PRIORITIES = 2


def _slots_kernel(pstart_ref, code_ref, o_ref):
    code = code_ref[...]
    e = code >> RANK_BITS
    base = jnp.zeros_like(code)
    for j in range(N_EXPERTS):
        base = jnp.where(e == j, pstart_ref[j], base)
    o_ref[...] = base + (code & RANK_MASK)


def _slots_call(pstart, code):
    return pl.pallas_call(
        _slots_kernel,
        in_specs=[pl.BlockSpec(memory_space=pltpu.SMEM), pl.BlockSpec(memory_space=pltpu.VMEM)],
        out_specs=pl.BlockSpec(memory_space=pltpu.VMEM),
        out_shape=jax.ShapeDtypeStruct(code.shape, I32),
        compiler_params=pltpu.CompilerParams(vmem_limit_bytes=VMEM_LIMIT_BYTES),
        name="moe_slots",
    )(pstart, code)


def _issue_token_rows(n_tokens, make_copies, unroll=ISSUE_UNROLL):
    def body(i, c):
        for u in range(unroll):
            for k, cp in enumerate(make_copies(i * unroll + u)):
                cp.start(priority=k % N_DMA_PRIORITIES)
        return c

    if unroll == n_tokens:
        body(0, 0)
    else:
        lax.fori_loop(0, n_tokens // unroll, body, 0)


def _scatter_kernel(slot_ref, pstart_ref, pend_ref, h_ref, xs_hbm, zero_sc, sem, *, tm, rows, n_blocks):
    @pl.when(pl.program_id(0) == 0)
    def _():
        zero_sc[...] = jnp.zeros(zero_sc.shape, U32)
        n_used = pend_ref[N_EXPERTS - 1] // rows

        def zero_block(b):
            return pltpu.make_async_copy(zero_sc, xs_hbm.at[pl.ds(b * rows, rows)], sem)

        def zstart(e, c):
            @pl.when(pend_ref[e] > pstart_ref[e])
            def _():
                zero_block(pend_ref[e] // rows - 1).start()
            return c

        def zwait(e, c):
            @pl.when(pend_ref[e] > pstart_ref[e])
            def _():
                zero_block(0).wait()
            return c

        def tstart(b, c):
            zero_block(b).start()
            return c

        def twait(b, c):
            zero_block(0).wait()
            return c

        lax.fori_loop(0, N_EXPERTS, zstart, 0)
        lax.fori_loop(n_used, n_blocks, tstart, 0)
        lax.fori_loop(0, N_EXPERTS, zwait, 0)
        lax.fori_loop(n_used, n_blocks, twait, 0)

    def copies(t):
        return [pltpu.make_async_copy(h_ref.at[t], xs_hbm.at[slot_ref[0, k, t]], sem) for k in range(TOP_K)]

    _issue_token_rows(tm, copies)
    for k in range(TOP_K):
        pltpu.make_async_copy(h_ref, xs_hbm.at[pl.ds(0, tm)], sem).wait()


def _scatter_call(slots, pstart, pend, h2_rows, n_rows, tm, rows):
    t = h2_rows.shape[0]
    smem = functools.partial(pl.BlockSpec, memory_space=pltpu.SMEM)
    return pl.pallas_call(
        functools.partial(_scatter_kernel, tm=tm, rows=rows, n_blocks=n_rows // rows),
        grid=(t // tm,),
        in_specs=[smem((1, SUBLANES, tm), lambda i: (i, 0, 0)), smem(), smem(),
                  pl.BlockSpec((tm, PACK_ROWS, LANES), lambda i: (i, 0, 0))],
        out_specs=pl.BlockSpec(memory_space=pl.ANY),
        out_shape=jax.ShapeDtypeStruct((n_rows, PACK_ROWS, LANES), U32),
        scratch_shapes=[pltpu.VMEM((rows, PACK_ROWS, LANES), U32), pltpu.SemaphoreType.DMA],
        compiler_params=_params("arbitrary"),
        name="moe_dispatch",
    )(slots, pstart, pend, h2_rows)


def _expert_kernel(be_ref, nused_ref, xs_ref, wg_ref, wu_ref, wd_ref, ys_ref, *, rows):
    @pl.when(pl.program_id(0) < nused_ref[0])
    def _():
        xb = _load_token_rows(xs_ref).astype(BF16)
        g = _dot(xb, wg_ref[0])
        u = _dot(xb, wu_ref[0])
        act = (g * _sigmoid(g)) * u
        _store_token_rows(ys_ref, _dot(act.astype(BF16), wd_ref[0]))

    @pl.when(pl.program_id(0) >= nused_ref[0])
    def _():
        ys_ref[...] = jnp.zeros(ys_ref.shape, U32)


def _expert_call(block_e, n_used, xs, wg, wu, wd, rows):
    n_rows = xs.shape[0] // PACK_ROWS
    _, d, ff = wg.shape
    row_spec = pl.BlockSpec((rows * PACK_ROWS, LANES), lambda i, be, nu: (jnp.minimum(i, nu[0] - 1), 0))
    grid_spec = pltpu.PrefetchScalarGridSpec(
        num_scalar_prefetch=2,
        grid=(n_rows // rows,),
        in_specs=[row_spec,
                  pl.BlockSpec((1, d, ff), lambda i, be, nu: (be[i], 0, 0)),
                  pl.BlockSpec((1, d, ff), lambda i, be, nu: (be[i], 0, 0)),
                  pl.BlockSpec((1, ff, d), lambda i, be, nu: (be[i], 0, 0))],
        out_specs=pl.BlockSpec((rows * PACK_ROWS, LANES), lambda i, be, nu: (i, 0)),
    )
    return pl.pallas_call(
        functools.partial(_expert_kernel, rows=rows),
        grid_spec=grid_spec,
        out_shape=jax.ShapeDtypeStruct(xs.shape, U32),
        compiler_params=_params("arbitrary"),
        name="moe_experts",
    )(block_e, n_used, xs, wg, wu, wd)


def _final_kernel(slot0_ref, slotn_ref, x1_ref, h2_ref, wtok_ref, gf_ref, gpost_ref,
                  wsg_ref, wsu_ref, wsd_ref, ys_hbm, o_ref, gat_sc, work_sc, sem, *, tm):
    nt = pl.num_programs(1)
    g = pl.program_id(0) * nt + pl.program_id(1)
    n_tiles = pl.num_programs(0) * nt
    cur = lax.rem(g, 2)

    def issue(slot_ref, buf, unroll):
        def copies(t):
            return [pltpu.make_async_copy(ys_hbm.at[slot_ref[0, k, t]], gat_sc.at[buf, k, t], sem.at[buf])
                    for k in range(TOP_K)]
        _issue_token_rows(tm, copies, unroll)

    def wait(buf):
        for k in range(TOP_K):
            pltpu.make_async_copy(ys_hbm.at[pl.ds(0, tm)], gat_sc.at[buf, k], sem.at[buf]).wait()

    @pl.when(g == 0)
    def _():
        issue(slot0_ref, 0, ISSUE_UNROLL)

    wait(cur)
    for k in range(TOP_K):
        work_sc[k] = gat_sc[cur, k].reshape(tm * PACK_ROWS, LANES)
    issue(slotn_ref, 1 - cur, tm)

    hb = h2_ref[0]
    a = _dot(hb, wsg_ref[...])
    u = _dot(hb, wsu_ref[...])
    tot = _dot(((a * _sigmoid(a)) * u).astype(BF16), wsd_ref[...])

    w = wtok_ref[...]
    routed = w[:, 0:1] * _load_token_rows(work_sc.at[0])
    for k in range(1, TOP_K):
        routed = routed + w[:, k:k + 1] * _load_token_rows(work_sc.at[k])
    tot = routed + tot
    o_ref[0] = x1_ref[0] + gf_ref[0] * (_rms_scale(tot) * gpost_ref[...])

    @pl.when(g == n_tiles - 1)
    def _():
        wait(1 - cur)


def _final_call(slots, x1, h2, wtok, gate_f, g_post, wsg, wsu, wsd, ys, tm):
    bsz, s, d = x1.shape
    nt = s // tm
    n_tiles = bsz * nt
    smem = functools.partial(pl.BlockSpec, memory_space=pltpu.SMEM)
    tok = pl.BlockSpec((1, tm, d), lambda b, i: (b, i, 0))
    return pl.pallas_call(
        functools.partial(_final_kernel, tm=tm),
        grid=(bsz, nt),
        in_specs=[smem((1, SUBLANES, tm), lambda b, i: (0, 0, 0)),
                  smem((1, SUBLANES, tm), lambda b, i: (jnp.minimum(b * nt + i + 1, n_tiles - 1), 0, 0)),
                  tok, tok,
                  pl.BlockSpec((tm, LANES), lambda b, i: (b * nt + i, 0)),
                  pl.BlockSpec((1, 1, d), lambda b, i: (b, 0, 0)),
                  _const_spec((1, d)),
                  _const_spec(wsg.shape), _const_spec(wsu.shape), _const_spec(wsd.shape),
                  pl.BlockSpec(memory_space=pl.ANY)],
        out_specs=tok,
        out_shape=jax.ShapeDtypeStruct((bsz, s, d), F32),
        scratch_shapes=[pltpu.VMEM((2, TOP_K, tm, PACK_ROWS, LANES), U32),
                        pltpu.VMEM((TOP_K, tm * PACK_ROWS, LANES), U32), pltpu.SemaphoreType.DMA((2,))],
        compiler_params=_params("arbitrary", "arbitrary"),
        name="moe_combine",
    )(slots, slots, x1, h2, wtok, gate_f, g_post, wsg, wsu, wsd, ys)


def _block_diag_tiles(w):
    per = MXU_DIM // RNN_BLOCK_DIM
    nt = RNN_BLOCKS // per
    w4 = w.reshape(nt, per, RNN_BLOCK_DIM, RNN_BLOCK_DIM)
    eye = jnp.eye(per, dtype=w.dtype)
    return jnp.einsum("tnij,nm->tnimj", w4, eye).reshape(nt, MXU_DIM, MXU_DIM)


def _tile(n, pref):
    t = min(n, pref)
    assert n % t == 0, (n, t)
    return t


def _layer(x, c, w_ada, b_ada, g_pre_mix, g_post_mix, g_pre_ffn, g_post_ffn, w_in, b_forget,
           w_conv, b_conv, w_rg, b_rg, w_ig, b_ig, lam, w_branch_attn, w_branch_rnn, w_out,
           w_router, router_bias, w_exp_gate, w_exp_up, w_exp_down, w_sh_gate, w_sh_up, w_sh_down):
    bsz, s, d = x.shape
    t = bsz * s
    row = lambda v: v.reshape(1, -1)

    mod = _ada_call(c, w_ada, b_ada)
    shift_m, scale_m, gate_m, shift_f, scale_f, gate_f = [
        mod[:, j * d:(j + 1) * d].reshape(bsz, 1, d) for j in range(6)]

    o_f = 3 * ATT_WIDTH
    o_r = o_f + ATT_HEADS
    wqkv = w_in[:, :o_f].astype(BF16)
    wf = jnp.pad(w_in[:, o_f:o_r], ((0, 0), (0, LANES - ATT_HEADS))).astype(BF16)
    wr = w_in[:, o_r:].astype(BF16)
    q, k, v, f_logit, x_rnn, g_rnn, gate_a, gate_r = _inproj_call(
        x, shift_m, scale_m, row(g_pre_mix), wqkv, wf, wr, _tile(s, 512))

    bf_pad = jnp.pad(b_forget, (0, LANES - ATT_HEADS)).reshape(1, LANES)
    kbias = _cum_call(f_logit, bf_pad, _tile(s, 256))
    tq = _tile(s, 512)
    y_attn = _attn_call(q, k, v, kbias, tq, _tile(tq, 256))

    wbd = jnp.concatenate([_block_diag_tiles(w_rg), _block_diag_tiles(w_ig)], axis=-1).astype(BF16)
    y_rnn = _rnn_call(x_rnn, g_rnn, w_conv, row(b_conv), wbd, row(b_rg), row(b_ig), row(lam), _tile(s, 256))

    wr_t = w_router.T
    wr_hi = wr_t.astype(BF16)
    wr_mid = (wr_t - wr_hi.astype(F32)).astype(BF16)
    tm_r = _tile(s, 512)
    x1, h2, h2_rows, code, wtok, counts = _merge_call(
        y_attn, y_rnn, gate_a, gate_r, x, gate_m, shift_f, scale_f, row(g_post_mix), row(g_pre_ffn),
        w_branch_attn.astype(BF16), w_branch_rnn.astype(BF16), w_out.astype(BF16),
        wr_hi, wr_mid, router_bias.reshape(N_EXPERTS, 1), tm_r)

    rows = 512
    n_blocks = -(-(t * TOP_K) // rows) + N_EXPERTS
    cnt = counts[:, 0].astype(I32)
    padded = ((cnt + rows - 1) // rows) * rows
    pend = jnp.cumsum(padded).astype(I32)
    pstart = pend - padded
    n_used = jnp.maximum(pend[-1] // rows, 1)
    blk = jnp.minimum(jnp.arange(n_blocks, dtype=I32), n_used - 1)
    block_e = jnp.minimum(jnp.sum((pend[None, :] <= (blk * rows)[:, None]).astype(I32), axis=1), N_EXPERTS - 1)

    rows3 = lambda a: a.reshape(-1, PACK_ROWS, LANES)
    rows2 = lambda a: a.reshape(-1, LANES)
    slots = _slots_call(pstart, code)
    xs = _scatter_call(slots, pstart, pend, rows3(h2_rows), n_blocks * rows, tm_r, rows)
    ys = rows3(_expert_call(block_e, n_used.reshape(1), rows2(xs),
                            w_exp_gate.astype(BF16), w_exp_up.astype(BF16), w_exp_down.astype(BF16), rows))

    tm_f = _tile(tm_r, 128)
    slots_f = slots.reshape(bsz * s // tm_r, SUBLANES, tm_r // tm_f, tm_f)
    slots_f = jnp.transpose(slots_f, (0, 2, 1, 3)).reshape(t // tm_f, SUBLANES, tm_f)
    return _final_call(slots_f, x1, h2, wtok, gate_f, row(g_post_ffn),
                       w_sh_gate.astype(BF16), w_sh_up.astype(BF16), w_sh_down.astype(BF16), ys, tm_f)


def kernel(x, c, w_ada, b_ada, g_pre_mix, g_post_mix, g_pre_ffn, g_post_ffn, w_in, b_forget, w_conv, b_conv, w_rg, b_rg, w_ig, b_ig, rglru_lambda, w_branch_attn, w_branch_rnn, w_out, w_router, router_bias, w_exp_gate, w_exp_up, w_exp_down, w_sh_gate, w_sh_up, w_sh_down):
    depth = w_ada.shape[0]
    for l in range(depth):
        x = _layer(x, c, w_ada[l], b_ada[l], g_pre_mix[l], g_post_mix[l], g_pre_ffn[l], g_post_ffn[l],
                   w_in[l], b_forget[l], w_conv[l], b_conv[l], w_rg[l], b_rg[l], w_ig[l], b_ig[l],
                   rglru_lambda[l], w_branch_attn[l], w_branch_rnn[l], w_out[l], w_router[l],
                   router_bias[l], w_exp_gate[l], w_exp_up[l], w_exp_down[l],
                   w_sh_gate[l], w_sh_up[l], w_sh_down[l])
    return x
```

```python
import functools

import jax
import jax.numpy as jnp
from jax import lax
from jax.experimental import pallas as pl
from jax.experimental.pallas import tpu as pltpu

F32 = jnp.float32
BF16 = jnp.bfloat16
I32 = jnp.int32

D_MODEL = 1024
ATT_HEADS = 8
ATT_HEAD_DIM = 64
ATT_WIDTH = ATT_HEADS * ATT_HEAD_DIM
RNN_WIDTH = D_MODEL
RNN_BLOCKS = 16
RNN_BLOCK_DIM = RNN_WIDTH // RNN_BLOCKS
CONV_WIDTH = 4
RGLRU_C = 8.0
N_EXPERTS = 64
TOP_K = 6
N_GROUPS = 8
GROUP_SIZE = N_EXPERTS // N_GROUPS
TOPK_GROUPS = 4
EXPERT_FF = D_MODEL // 4
ROUTE_SCALE = 2.5
NORM_EPS = 1e-6

LANES = 128
SUBLANES = 8
MXU_DIM = 256
VMEM_LIMIT_BYTES = 56 * 1024 * 1024

NEG_BIG = -1e30
RANK_BITS = 20
RANK_MASK = (1 << RANK_BITS) - 1


def _sigmoid(x):
    return 0.5 * jnp.tanh(0.5 * x) + 0.5


def _rms_scale(x):
    return x * lax.rsqrt(jnp.mean(x * x, axis=-1, keepdims=True) + NORM_EPS)


def _dot(a, b):
    return jnp.dot(a, b, preferred_element_type=F32)


def _dot_nt(a, b):
    return lax.dot_general(a, b, (((1,), (1,)), ((), ())), preferred_element_type=F32)


def _split3(a):
    hi = a.astype(BF16)
    r1 = a - hi.astype(F32)
    mid = r1.astype(BF16)
    lo = (r1 - mid.astype(F32)).astype(BF16)
    return hi, mid, lo


def _params(*sem):
    return pltpu.CompilerParams(dimension_semantics=sem, vmem_limit_bytes=VMEM_LIMIT_BYTES)


def _const_spec(shape):
    nd = len(shape)
    return pl.BlockSpec(shape, lambda *_: (0,) * nd)


def _ada_kernel(c_ref, w_ref, b_ref, o_ref):
    c = c_ref[...]
    sc = c * _sigmoid(c)
    s_hi, s_mid, _ = _split3(sc)
    w_hi, w_mid, _ = _split3(w_ref[...])
    acc = _dot(s_hi, w_hi) + _dot(s_hi, w_mid) + _dot(s_mid, w_hi)
    o_ref[...] = acc + b_ref[...]


def _ada_call(c, w_ada, b_ada):
    bsz, d = c.shape
    n = w_ada.shape[1]
    tn = d
    return pl.pallas_call(
        _ada_kernel,
        grid=(n // tn,),
        in_specs=[
            pl.BlockSpec((bsz, d), lambda j: (0, 0)),
            pl.BlockSpec((d, tn), lambda j: (0, j)),
            pl.BlockSpec((1, tn), lambda j: (0, j)),
        ],
        out_specs=pl.BlockSpec((bsz, tn), lambda j: (0, j)),
        out_shape=jax.ShapeDtypeStruct((bsz, n), F32),
        compiler_params=_params("arbitrary"),
        name="ada_mod",
    )(c, w_ada, b_ada.reshape(1, n))


def _inproj_kernel(x_ref, sh_ref, sc_ref, g_ref, wqkv_ref, wf_ref, wr_ref,
                   q_ref, k_ref, v_ref, f_ref, xr_ref, gr_ref, ga_ref, gb_ref):
    x = x_ref[0]
    h = (_rms_scale(x) * g_ref[...]) * (1.0 + sc_ref[0]) + sh_ref[0]
    hb = h.astype(BF16)
    for j, ref in enumerate((q_ref, k_ref, v_ref)):
        ref[0] = _dot(hb, wqkv_ref[:, j * ATT_WIDTH:(j + 1) * ATT_WIDTH]).astype(BF16)
    f_ref[0] = _dot(hb, wf_ref[...])
    half = D_MODEL // 2
    for j, ref in enumerate((xr_ref, gr_ref, ga_ref, gb_ref)):
        for c in range(2):
            lo = j * D_MODEL + c * half
            ref[0, :, c * half:(c + 1) * half] = _dot(hb, wr_ref[:, lo:lo + half]).astype(BF16)


def _inproj_call(x, shift, scale, g, wqkv, wf, wr, tm):
    bsz, s, d = x.shape
    nt = s // tm
    tok = lambda w: pl.BlockSpec((1, tm, w), lambda b, i: (b, i, 0))
    mod = pl.BlockSpec((1, 1, d), lambda b, i: (b, 0, 0))
    out_w = (ATT_WIDTH,) * 3 + (LANES,) + (d,) * 4
    out_dt = (BF16,) * 3 + (F32,) + (BF16,) * 4
    return pl.pallas_call(
        _inproj_kernel,
        grid=(bsz, nt),
        in_specs=[tok(d), mod, mod, _const_spec((1, d)),
                  _const_spec(wqkv.shape), _const_spec(wf.shape), _const_spec(wr.shape)],
        out_specs=[tok(w) for w in out_w],
        out_shape=[jax.ShapeDtypeStruct((bsz, s, w), dt) for w, dt in zip(out_w, out_dt)],
        compiler_params=_params("arbitrary", "arbitrary"),
        name="in_proj",
    )(x, shift, scale, g, wqkv, wf, wr)


N_BIAS_PIECES = 3


def _bias_placement():
    h = jnp.arange(LANES)[:, None]
    lane = jnp.arange(ATT_WIDTH)[None, :]
    mats = []
    for piece in range(N_BIAS_PIECES):
        target = (h // 2) * LANES + ATT_HEAD_DIM * (1 - h % 2) + piece
        mats.append(((lane == target) & (h < ATT_HEADS)).astype(BF16))
    return jnp.stack(mats)


def _cum_kernel(f_ref, b_ref, place_ref, kb_ref, *, blk):
    s = f_ref.shape[1]
    row = lax.broadcasted_iota(I32, (blk, blk), 0)
    col = lax.broadcasted_iota(I32, (blk, blk), 1)
    tri = (row >= col).astype(BF16)
    carry = jnp.zeros((1, LANES), F32)
    for i in range(s // blk):
        z = f_ref[0, i * blk:(i + 1) * blk, :] + b_ref[...]
        lf = jnp.minimum(z, 0.0) - jnp.log1p(jnp.exp(-jnp.abs(z)))
        hi, mid, lo = _split3(lf)
        c = _dot(tri, hi) + _dot(tri, mid) + _dot(tri, lo) + carry
        carry = c[blk - 1:blk, :]
        pieces = _split3(-c)
        kb = _dot(pieces[0], place_ref[0])
        for j in range(1, N_BIAS_PIECES):
            kb = kb + _dot(pieces[j], place_ref[j])
        kb_ref[0, i * blk:(i + 1) * blk, :] = kb.astype(BF16)


def _cum_call(f_logit, b_forget_pad, blk):
    bsz, s, _ = f_logit.shape
    place = _bias_placement()
    return pl.pallas_call(
        functools.partial(_cum_kernel, blk=blk),
        grid=(bsz,),
        in_specs=[pl.BlockSpec((1, s, LANES), lambda b: (b, 0, 0)), _const_spec((1, LANES)),
                  _const_spec(place.shape)],
        out_specs=pl.BlockSpec((1, s, ATT_WIDTH), lambda b: (b, 0, 0)),
        out_shape=jax.ShapeDtypeStruct((bsz, s, ATT_WIDTH), BF16),
        compiler_params=_params("arbitrary"),
        name="forget_cumsum",
    )(f_logit, b_forget_pad, place)


ATT_VT_ROWS = ATT_HEAD_DIM + 16


def _attn_kernel(q_ref, k_ref, v_ref, kb_ref, o_ref, vt_sc, st_sc, m_sc, acc_sc, *, tq, tk):
    s = q_ref.shape[1]
    nq, nk, ratio = s // tq, s // tk, tq // tk
    lane = lax.broadcasted_iota(I32, (1, LANES), 1)
    head_lanes = (lane < ATT_HEAD_DIM, lane >= ATT_HEAD_DIM)
    one_lanes = ((lane >= ATT_HEAD_DIM) & (lane < ATT_HEAD_DIM + N_BIAS_PIECES), lane < N_BIAS_PIECES)
    key_minus_query = (lax.broadcasted_iota(I32, (tk, tq), 0) - lax.broadcasted_iota(I32, (tk, tq), 1))

    vt = v_ref[0].astype(F32).T
    ones = jnp.ones((ATT_VT_ROWS - ATT_HEAD_DIM, s), F32)
    for h in range(2):
        vth = jnp.concatenate([vt[h * ATT_HEAD_DIM:(h + 1) * ATT_HEAD_DIM, :], ones], axis=0).astype(BF16)
        for j in range(nk):
            vt_sc[h, j] = vth[:, j * tk:(j + 1) * tk]

    def q_body(qi, carry):
        q0 = pl.multiple_of(qi * tq, tq)
        qs = q_ref[0, pl.ds(q0, tq), :] * (ATT_HEAD_DIM ** -0.5)
        qa = [jnp.where(head_lanes[h], qs, one_lanes[h].astype(BF16)) for h in range(2)]
        m_sc[...] = jnp.full(m_sc.shape, NEG_BIG, F32)
        acc_sc[...] = jnp.zeros(acc_sc.shape, F32)

        def scores(kj):
            k0 = pl.multiple_of(kj * tk, tk)
            kb = k_ref[0, pl.ds(k0, tk), :]
            bias = kb_ref[0, pl.ds(k0, tk), :]
            return [_dot_nt(jnp.where(head_lanes[h], kb, bias), qa[h]) for h in range(2)]

        def accumulate(kj, masked):
            for h in range(2):
                st = st_sc[h]
                if masked:
                    st = jnp.where(key_minus_query <= q0 - kj * tk, st, NEG_BIG)
                m_old = m_sc[h]
                m_new = jnp.maximum(m_old, jnp.max(st, axis=0, keepdims=True))
                p = jnp.exp(st - m_new).astype(BF16)
                acc_sc[h] = jnp.exp(m_old - m_new) * acc_sc[h] + _dot(vt_sc[h, kj], p)
                m_sc[h] = m_new

        def put(sts):
            for h in range(2):
                st_sc[h] = sts[h]

        n_full = qi * ratio
        put(scores(0))

        def kv_body(kj, c2):
            nxt = scores(kj + 1)
            accumulate(kj, False)
            put(nxt)
            return c2

        lax.fori_loop(0, n_full, kv_body, 0)
        for d in range(ratio):
            nxt = scores(n_full + d + 1) if d + 1 < ratio else None
            accumulate(n_full + d, True)
            if nxt is not None:
                put(nxt)

        outs = []
        for h in range(2):
            acc = acc_sc[h]
            outs.append(acc[:ATT_HEAD_DIM, :] * (1.0 / acc[ATT_HEAD_DIM:ATT_HEAD_DIM + 1, :]))
        o_ref[0, pl.ds(q0, tq), :] = jnp.concatenate(outs, axis=0).T.astype(BF16)
        return carry

    lax.fori_loop(0, nq, q_body, 0)


def _attn_call(q, k, v, kbias, tq, tk):
    bsz, s, _ = q.shape
    qkv = pl.BlockSpec((1, s, LANES), lambda b, p: (b, 0, p))
    return pl.pallas_call(
        functools.partial(_attn_kernel, tq=tq, tk=tk),
        grid=(bsz, ATT_HEADS // 2),
        in_specs=[qkv, qkv, qkv, qkv],
        out_specs=qkv,
        out_shape=jax.ShapeDtypeStruct((bsz, s, ATT_WIDTH), BF16),
        scratch_shapes=[pltpu.VMEM((2, s // tk, ATT_VT_ROWS, tk), BF16), pltpu.VMEM((2, tk, tq), F32),
                        pltpu.VMEM((2, 1, tq), F32), pltpu.VMEM((2, ATT_VT_ROWS, tq), F32)],
        compiler_params=_params("arbitrary", "arbitrary"),
        name="fox_attention",
    )(q, k, v, kbias)


def _gelu_tanh(x):
    c = 0.7978845608028654
    return 0.5 * x * (1.0 + jnp.tanh(c * (x + 0.044715 * (x * x * x))))


def _rnn_kernel(xr_ref, gr_ref, wc_ref, bc_ref, wbd_ref, brg_ref, big_ref, lam_ref, o_ref,
                tail_sc, hc_sc, a_sc, b_sc, h_sc, *, ts):
    @pl.when(pl.program_id(1) == 0)
    def _():
        tail_sc[...] = jnp.zeros(tail_sc.shape, F32)
        hc_sc[...] = jnp.zeros(hc_sc.shape, F32)

    x = xr_ref[0].astype(F32)
    full = jnp.concatenate([tail_sc[...], x], axis=0)
    u = bc_ref[...] + wc_ref[3:4, :] * x
    for j in range(CONV_WIDTH - 1):
        off = SUBLANES - (CONV_WIDTH - 1) + j
        u = u + wc_ref[j:j + 1, :] * full[off:off + ts, :]
    tail_sc[...] = x[ts - SUBLANES:ts, :]

    lam = lam_ref[...]
    neg_sp = -(jnp.maximum(-lam, 0.0) + jnp.log1p(jnp.exp(-jnp.abs(lam))))
    for jt in range(RNN_WIDTH // MXU_DIM):
        cs = slice(jt * MXU_DIM, (jt + 1) * MXU_DIM)
        uc = u[:, cs]
        g = _dot(uc.astype(BF16), wbd_ref[jt])
        r = _sigmoid(g[:, :MXU_DIM] + brg_ref[:, cs])
        ig = _sigmoid(g[:, MXU_DIM:] + big_ref[:, cs])
        a = jnp.exp(r * (RGLRU_C * neg_sp[:, cs]))
        a_sc[:, cs] = a
        v = 1.0 - a * a
        b_sc[:, cs] = jnp.where(v > 0.0, v * lax.rsqrt(v), 0.0) * (ig * uc)

    srow = lax.broadcasted_iota(I32, (SUBLANES, RNN_WIDTH), 0)

    def scan_body(i, carry):
        r0 = pl.multiple_of(i * SUBLANES, SUBLANES)
        a = a_sc[pl.ds(r0, SUBLANES), :]
        b = b_sc[pl.ds(r0, SUBLANES), :]
        b = b + jnp.where(srow == 0, a * carry, 0.0)
        for sh in (1, 2, 4):
            keep = srow >= sh
            b = jnp.where(keep, b + a * pltpu.roll(b, sh, 0), b)
            if sh < SUBLANES // 2:
                a = jnp.where(keep, a * pltpu.roll(a, sh, 0), a)
        h_sc[pl.ds(r0, SUBLANES), :] = b
        return jnp.broadcast_to(b[SUBLANES - 1:SUBLANES, :], (SUBLANES, RNN_WIDTH))

    hc_sc[...] = lax.fori_loop(0, ts // SUBLANES, scan_body, hc_sc[...])
    o_ref[0] = (h_sc[...] * _gelu_tanh(gr_ref[0].astype(F32))).astype(BF16)


def _rnn_call(x_rnn, g_rnn, w_conv, b_conv, wbd, b_rg, b_ig, lam, ts):
    bsz, s, w = x_rnn.shape
    tok = pl.BlockSpec((1, ts, w), lambda b, i: (b, i, 0))
    return pl.pallas_call(
        functools.partial(_rnn_kernel, ts=ts),
        grid=(bsz, s // ts),
        in_specs=[tok, tok, _const_spec(w_conv.shape), _const_spec((1, w)), _const_spec(wbd.shape),
                  _const_spec((1, w)), _const_spec((1, w)), _const_spec((1, w))],
        out_specs=tok,
        out_shape=jax.ShapeDtypeStruct((bsz, s, w), BF16),
        scratch_shapes=[pltpu.VMEM((SUBLANES, w), F32), pltpu.VMEM((SUBLANES, w), F32),
                        pltpu.VMEM((ts, w), F32), pltpu.VMEM((ts, w), F32), pltpu.VMEM((ts, w), F32)],
        compiler_params=_params("arbitrary", "arbitrary"),
        name="rglru",
    )(x_rnn, g_rnn, w_conv, b_conv, wbd, b_rg, b_ig, lam)


WINDOW_SHIFT = 5
WINDOW_ROWS = 1 << WINDOW_SHIFT


def _route(h2, wr_hi_ref, wr_mid_ref, rbias_ref, cnt_ref, lpos_ref, before_ref, ntile_ref, wtok_ref, j, tm):
    h_hi, h_mid, _ = _split3(h2)
    logits = _dot_nt(wr_hi_ref[...], h_hi) + _dot_nt(wr_hi_ref[...], h_mid) + _dot_nt(wr_mid_ref[...], h_hi)
    scores = _sigmoid(logits)
    sel = scores + rbias_ref[...]

    giota = lax.broadcasted_iota(I32, (N_GROUPS, tm), 0)
    gs = jnp.zeros((N_GROUPS, tm), F32)
    for g in range(N_GROUPS):
        blk = sel[g * GROUP_SIZE:(g + 1) * GROUP_SIZE, :]
        m1 = jnp.max(blk, axis=0, keepdims=True)
        i1 = jnp.min(jnp.where(blk == m1, giota, N_EXPERTS), axis=0, keepdims=True)
        m2 = jnp.max(jnp.where(giota == i1, -jnp.inf, blk), axis=0, keepdims=True)
        gs = jnp.where(giota == g, m1 + m2, gs)

    gsel = jnp.zeros((N_GROUPS, tm), F32)
    for _ in range(TOPK_GROUPS):
        m = jnp.max(gs, axis=0, keepdims=True)
        idx = jnp.min(jnp.where(gs == m, giota, N_EXPERTS), axis=0, keepdims=True)
        hit = giota == idx
        gsel = jnp.where(hit, 1.0, gsel)
        gs = jnp.where(hit, -jnp.inf, gs)

    masked = jnp.concatenate(
        [jnp.where(gsel[g:g + 1, :] > 0.0, sel[g * GROUP_SIZE:(g + 1) * GROUP_SIZE, :], -jnp.inf)
         for g in range(N_GROUPS)], axis=0)

    eiota = lax.broadcasted_iota(I32, (N_EXPERTS, tm), 0)
    chosen = jnp.zeros((N_EXPERTS, tm), F32)
    idxs, wts = [], []
    for _ in range(TOP_K):
        m = jnp.max(masked, axis=0, keepdims=True)
        idx = jnp.min(jnp.where(masked == m, eiota, N_EXPERTS), axis=0, keepdims=True)
        hit = eiota == idx
        wts.append(jnp.sum(jnp.where(hit, scores, 0.0), axis=0, keepdims=True))
        idxs.append(idx)
        chosen = jnp.where(hit, 1.0, chosen)
        masked = jnp.where(hit, -jnp.inf, masked)
    wsum = wts[0]
    for w in wts[1:]:
        wsum = wsum + w

    srow = lax.broadcasted_iota(I32, (tm, tm), 0)
    scol = lax.broadcasted_iota(I32, (tm, tm), 1)
    earlier = (srow < scol).astype(BF16)
    local_rank = _dot(chosen.astype(BF16), earlier)
    n_e = jnp.sum(chosen, axis=1, keepdims=True)
    before_ref[j] = cnt_ref[...].astype(I32)
    ntile_ref[j] = n_e.astype(I32)
    cnt_ref[...] = cnt_ref[...] + n_e
    n_pad = (((n_e.astype(I32) + (WINDOW_ROWS - 1)) >> WINDOW_SHIFT) << WINDOW_SHIFT).astype(F32)
    erow = lax.broadcasted_iota(I32, (N_EXPERTS, N_EXPERTS), 0)
    ecol = lax.broadcasted_iota(I32, (N_EXPERTS, N_EXPERTS), 1)
    off = _dot((ecol < erow).astype(BF16), jnp.broadcast_to(n_pad, (N_EXPERTS, LANES)).astype(BF16))[:, :1]
    local_pos = off + local_rank

    riota = lax.broadcasted_iota(I32, (SUBLANES, tm), 0)
    lpos = jnp.zeros((SUBLANES, tm), I32)
    wrow = jnp.zeros((SUBLANES, tm), F32)
    for r in range(TOP_K):
        lp = jnp.sum(jnp.where(eiota == idxs[r], local_pos, 0.0), axis=0, keepdims=True)
        lpos = jnp.where(riota == r, lp.astype(I32), lpos)
        wrow = jnp.where(riota == r, wts[r] / wsum * ROUTE_SCALE, wrow)
    lpos_ref[j] = lpos
    wpad = jnp.concatenate([wrow, jnp.zeros((LANES - SUBLANES, tm), F32)], axis=0)
    wtok_ref[j * tm:(j + 1) * tm, :] = wpad.T


U32 = jnp.uint32
PACK_ROWS = D_MODEL // (2 * LANES)
HIGH_HALF = 0xFFFF0000


def _store_token_rows(rows_ref, val):
    half = D_MODEL // 2
    n = val.shape[0]
    bits = lax.bitcast_convert_type(val.astype(BF16).astype(F32), U32)
    words = (bits[:, half:] & jnp.uint32(HIGH_HALF)) | (bits[:, :half] >> 16)
    for c in range(PACK_ROWS):
        chunk = words[:, c * LANES:(c + 1) * LANES]
        if len(rows_ref.shape) == 2:
            rows_ref[pl.ds(c, n, stride=PACK_ROWS), :] = chunk
        else:
            rows_ref[:, c, :] = chunk


def _load_token_rows(rows_ref):
    if len(rows_ref.shape) == 2:
        n = rows_ref.shape[0] // PACK_ROWS
        chunks = [rows_ref[pl.ds(c, n, stride=PACK_ROWS), :] for c in range(PACK_ROWS)]
    else:
        chunks = [rows_ref[:, c, :] for c in range(PACK_ROWS)]
    words = jnp.concatenate(chunks, axis=-1)
    lo = lax.bitcast_convert_type(words << 16, F32)
    hi = lax.bitcast_convert_type(words & jnp.uint32(HIGH_HALF), F32)
    return jnp.concatenate([lo, hi], axis=-1)


def _merge_kernel(ya_ref, yr_ref, ga_ref, gb_ref, x_ref, gm_ref, shf_ref, scf_ref, gpost_ref, gpre_ref,
                  wba_ref, wbr_ref, wout_ref, wr_hi_ref, wr_mid_ref, rbias_ref,
                  x1_ref, h2_ref, h2rows_ref, lpos_ref, before_ref, ntile_ref, wtok_ref, cnt_ref, *, tm, tr):
    @pl.when((pl.program_id(0) == 0) & (pl.program_id(1) == 0))
    def _():
        cnt_ref[...] = jnp.zeros(cnt_ref.shape, F32)

    pa = _dot(ya_ref[0], wba_ref[...])
    pr = _dot(yr_ref[0], wbr_ref[...])
    merged = _sigmoid(ga_ref[0].astype(F32)) * pa + _sigmoid(gb_ref[0].astype(F32)) * pr
    o = _dot(merged.astype(BF16), wout_ref[...])
    x1 = x_ref[0] + gm_ref[0] * (_rms_scale(o) * gpost_ref[...])
    x1_ref[0] = x1
    h2 = (_rms_scale(x1) * gpre_ref[...]) * (1.0 + scf_ref[0]) + shf_ref[0]
    h2_ref[0] = h2.astype(BF16)
    _store_token_rows(h2rows_ref, h2)
    for j in range(tm // tr):
        _route(h2[j * tr:(j + 1) * tr, :], wr_hi_ref, wr_mid_ref, rbias_ref, cnt_ref,
               lpos_ref, before_ref, ntile_ref, wtok_ref, j, tr)


def _merge_call(y_attn, y_rnn, gate_a, gate_r, x, gate_m, shift_f, scale_f, g_post, g_pre,
                wba, wbr, wout, wr_hi, wr_mid, rbias, tm, tr):
    bsz, s, d = x.shape
    nt = s // tm
    nr = tm // tr
    tok = lambda w: pl.BlockSpec((1, tm, w), lambda b, i: (b, i, 0))
    mod = pl.BlockSpec((1, 1, d), lambda b, i: (b, 0, 0))
    return pl.pallas_call(
        functools.partial(_merge_kernel, tm=tm, tr=tr),
        grid=(bsz, nt),
        in_specs=[tok(ATT_WIDTH), tok(d), tok(d), tok(d), tok(d), mod, mod, mod,
                  _const_spec((1, d)), _const_spec((1, d)),
                  _const_spec(wba.shape), _const_spec(wbr.shape), _const_spec(wout.shape),
                  _const_spec(wr_hi.shape), _const_spec(wr_mid.shape), _const_spec(rbias.shape)],
        out_specs=[tok(d), tok(d),
                   pl.BlockSpec((tm * PACK_ROWS, LANES), lambda b, i: (b * nt + i, 0)),
                   pl.BlockSpec((nr, SUBLANES, tr), lambda b, i: (b * nt + i, 0, 0)),
                   pl.BlockSpec((nr, N_EXPERTS, 1), lambda b, i: (b * nt + i, 0, 0)),
                   pl.BlockSpec((nr, N_EXPERTS, 1), lambda b, i: (b * nt + i, 0, 0)),
                   pl.BlockSpec((tm, LANES), lambda b, i: (b * nt + i, 0)),
                   pl.BlockSpec((N_EXPERTS, 1), lambda b, i: (0, 0))],
        out_shape=[jax.ShapeDtypeStruct((bsz, s, d), F32), jax.ShapeDtypeStruct((bsz, s, d), BF16),
                   jax.ShapeDtypeStruct((bsz * s * PACK_ROWS, LANES), U32),
                   jax.ShapeDtypeStruct((bsz * nt * nr, SUBLANES, tr), I32),
                   jax.ShapeDtypeStruct((bsz * nt * nr, N_EXPERTS, 1), I32),
                   jax.ShapeDtypeStruct((bsz * nt * nr, N_EXPERTS, 1), I32),
                   jax.ShapeDtypeStruct((bsz * s, LANES), F32),
                   jax.ShapeDtypeStruct((N_EXPERTS, 1), F32)],
        compiler_params=_params("arbitrary", "arbitrary"),
        name="merge_route",
    )(y_attn, y_rnn, gate_a, gate_r, x, gate_m, shift_f, scale_f, g_post, g_pre,
      wba, wbr, wout, wr_hi, wr_mid, rbias)


WIN_LIST = LANES
ROW_COPY_UNROLL = 8


def _local_rows(tm):
    rows = tm * TOP_K + N_EXPERTS * WINDOW_ROWS
    assert rows // WINDOW_ROWS < WIN_LIST
    return rows


def _for_each_window(wlist_ref, fn):
    def body(w, c):
        fn(pl.multiple_of(w * WINDOW_ROWS, WINDOW_ROWS), wlist_ref[0, 0, w])
        return c

    lax.fori_loop(0, wlist_ref[0, 0, WIN_LIST - 1], body, 0)


def _dispatch_kernel(lpos_ref, wlist_ref, wlistp_ref, pstart_ref, pend_ref, h_ref, xs_hbm, loc_sc, zero_sc, sem,
                     *, tm, rows, n_blocks):
    g = pl.program_id(0)
    cur = lax.rem(g, 2)

    @pl.when(g == 0)
    def _():
        loc_sc[...] = jnp.zeros(loc_sc.shape, U32)
        zero_sc[...] = jnp.zeros(zero_sc.shape, U32)
        n_used = pend_ref[N_EXPERTS - 1] // rows

        def zero_block(b):
            return pltpu.make_async_copy(zero_sc, xs_hbm.at[pl.ds(b * rows, rows)], sem)

        def region_tail(e, fn):
            last = pend_ref[e] // rows - 1
            for back in range(2):
                @pl.when(last - back >= pstart_ref[e] // rows)
                def _():
                    fn(zero_block(last - back))

        def zstart(e, c):
            region_tail(e, lambda cp: cp.start())
            return c

        def zwait(e, c):
            region_tail(e, lambda cp: cp.wait())
            return c

        def tstart(b, c):
            zero_block(b).start()
            return c

        def twait(b, c):
            zero_block(0).wait()
            return c

        lax.fori_loop(0, N_EXPERTS, zstart, 0)
        lax.fori_loop(n_used, n_blocks, tstart, 0)
        lax.fori_loop(0, N_EXPERTS, zwait, 0)
        lax.fori_loop(n_used, n_blocks, twait, 0)

    for t in range(tm):
        row = h_ref[t]
        for k in range(TOP_K):
            loc_sc[lpos_ref[0, k, t]] = row

    half = _local_rows(tm)

    def window(base):
        return lambda row0, slot0: pltpu.make_async_copy(
            loc_sc.at[pl.ds(pl.multiple_of(base + row0, WINDOW_ROWS), WINDOW_ROWS)],
            xs_hbm.at[pl.ds(slot0, WINDOW_ROWS)], sem)

    @pl.when(g > 0)
    def _():
        _for_each_window(wlistp_ref, lambda r, s: window((1 - cur) * half)(r, s).wait())

    _for_each_window(wlist_ref, lambda r, s: window(cur * half)(r, s).start())

    @pl.when(g == pl.num_programs(0) - 1)
    def _():
        _for_each_window(wlist_ref, lambda r, s: window(cur * half)(r, s).wait())


def _dispatch_call(lpos, wlist, pstart, pend, h2_rows, n_rows, tm, rows):
    t = h2_rows.shape[0]
    smem = functools.partial(pl.BlockSpec, memory_space=pltpu.SMEM)
    return pl.pallas_call(
        functools.partial(_dispatch_kernel, tm=tm, rows=rows, n_blocks=n_rows // rows),
        grid=(t // tm,),
        in_specs=[smem((1, SUBLANES, tm), lambda i: (i, 0, 0)),
                  smem((1, 1, WIN_LIST), lambda i: (i, 0, 0)),
                  smem((1, 1, WIN_LIST), lambda i: (jnp.maximum(i - 1, 0), 0, 0)),
                  smem(), smem(),
                  pl.BlockSpec((tm, PACK_ROWS, LANES), lambda i: (i, 0, 0))],
        out_specs=pl.BlockSpec(memory_space=pl.ANY),
        out_shape=jax.ShapeDtypeStruct((n_rows, PACK_ROWS, LANES), U32),
        scratch_shapes=[pltpu.VMEM((2 * _local_rows(tm), PACK_ROWS, LANES), U32),
                        pltpu.VMEM((rows, PACK_ROWS, LANES), U32), pltpu.SemaphoreType.DMA],
        compiler_params=_params("arbitrary"),
        name="moe_dispatch",
    )(lpos, wlist, wlist, pstart, pend, h2_rows)


def _expert_kernel(be_ref, nused_ref, xs_ref, wg_ref, wu_ref, wd_ref, ys_ref, *, rows):
    @pl.when(pl.program_id(0) < nused_ref[0])
    def _():
        xb = _load_token_rows(xs_ref).astype(BF16)
        g = _dot(xb, wg_ref[0])
        u = _dot(xb, wu_ref[0])
        act = (g * _sigmoid(g)) * u
        _store_token_rows(ys_ref, _dot(act.astype(BF16), wd_ref[0]))

    @pl.when(pl.program_id(0) >= nused_ref[0])
    def _():
        ys_ref[...] = jnp.zeros(ys_ref.shape, U32)


def _expert_call(block_e, n_used, xs, wg, wu, wd, rows):
    n_rows = xs.shape[0] // PACK_ROWS
    _, d, ff = wg.shape
    row_spec = pl.BlockSpec((rows * PACK_ROWS, LANES), lambda i, be, nu: (jnp.minimum(i, nu[0] - 1), 0))
    grid_spec = pltpu.PrefetchScalarGridSpec(
        num_scalar_prefetch=2,
        grid=(n_rows // rows,),
        in_specs=[row_spec,
                  pl.BlockSpec((1, d, ff), lambda i, be, nu: (be[i], 0, 0)),
                  pl.BlockSpec((1, d, ff), lambda i, be, nu: (be[i], 0, 0)),
                  pl.BlockSpec((1, ff, d), lambda i, be, nu: (be[i], 0, 0))],
        out_specs=pl.BlockSpec((rows * PACK_ROWS, LANES), lambda i, be, nu: (i, 0)),
    )
    return pl.pallas_call(
        functools.partial(_expert_kernel, rows=rows),
        grid_spec=grid_spec,
        out_shape=jax.ShapeDtypeStruct(xs.shape, U32),
        compiler_params=_params("arbitrary"),
        name="moe_experts",
    )(block_e, n_used, xs, wg, wu, wd)


def _final_kernel(lpos0_ref, lposn_ref, tab0_ref, tabw_ref, tabi_ref, x1_ref, h2_ref, wtok_ref, gf_ref, gpost_ref,
                  wsg_ref, wsu_ref, wsd_ref, ys_hbm, o_ref, win_sc, gnext_sc, gcur_sc, sem, *, tm):
    nt = pl.num_programs(1)
    g = pl.program_id(0) * nt + pl.program_id(1)
    n_tiles = pl.num_programs(0) * nt
    buf_next = lax.rem(g + 1, 2)
    buf_after = lax.rem(g, 2)
    half = _local_rows(tm)

    def window(buf):
        return lambda row0, slot0: pltpu.make_async_copy(
            ys_hbm.at[pl.ds(slot0, WINDOW_ROWS)],
            win_sc.at[pl.ds(pl.multiple_of(buf * half + row0, WINDOW_ROWS), WINDOW_ROWS)], sem.at[buf])

    def start_windows(tab_ref, buf):
        _for_each_window(tab_ref, lambda r, s: window(buf)(r, s).start())

    def wait_windows(tab_ref, buf):
        _for_each_window(tab_ref, lambda r, s: window(buf)(r, s).wait())

    def pick_rows(lpos_ref, t):
        for k in range(TOP_K):
            gnext_sc[k, t] = win_sc[lpos_ref[0, k, t]]

    def stage():
        for k in range(TOP_K):
            gcur_sc[k] = gnext_sc[k].reshape(tm * PACK_ROWS, LANES)

    @pl.when(g == 0)
    def _():
        start_windows(tab0_ref, 0)
        wait_windows(tab0_ref, 0)

        def body(i, c):
            for u in range(ROW_COPY_UNROLL):
                pick_rows(lpos0_ref, i * ROW_COPY_UNROLL + u)
            return c

        lax.fori_loop(0, tm // ROW_COPY_UNROLL, body, 0)
        stage()
        start_windows(tabw_ref, 1)

    wait_windows(tabw_ref, buf_next)
    for t in range(tm):
        pick_rows(lposn_ref, t)

    hb = h2_ref[0]
    a = _dot(hb, wsg_ref[...])
    u = _dot(hb, wsu_ref[...])
    tot = _dot(((a * _sigmoid(a)) * u).astype(BF16), wsd_ref[...])

    w = wtok_ref[...]
    routed = w[:, 0:1] * _load_token_rows(gcur_sc.at[0])
    for k in range(1, TOP_K):
        routed = routed + w[:, k:k + 1] * _load_token_rows(gcur_sc.at[k])
    tot = routed + tot
    o_ref[0] = x1_ref[0] + gf_ref[0] * (_rms_scale(tot) * gpost_ref[...])

    stage()
    start_windows(tabi_ref, buf_after)

    @pl.when(g == n_tiles - 1)
    def _():
        wait_windows(tabi_ref, buf_after)


def _final_call(lpos, tab, x1, h2, wtok, gate_f, g_post, wsg, wsu, wsd, ys, tm):
    bsz, s, d = x1.shape
    nt = s // tm
    n_tiles = bsz * nt
    smem = functools.partial(pl.BlockSpec, memory_space=pltpu.SMEM)
    tok = pl.BlockSpec((1, tm, d), lambda b, i: (b, i, 0))
    tab_shape = (1, 1, WIN_LIST)
    ahead = lambda k: (lambda b, i: (jnp.minimum(b * nt + i + k, n_tiles - 1), 0, 0))
    return pl.pallas_call(
        functools.partial(_final_kernel, tm=tm),
        grid=(bsz, nt),
        in_specs=[smem((1, SUBLANES, tm), lambda b, i: (0, 0, 0)),
                  smem((1, SUBLANES, tm), ahead(1)),
                  smem(tab_shape, lambda b, i: (0, 0, 0)),
                  smem(tab_shape, ahead(1)),
                  smem(tab_shape, ahead(2)),
                  tok, tok,
                  pl.BlockSpec((tm, LANES), lambda b, i: (b * nt + i, 0)),
                  pl.BlockSpec((1, 1, d), lambda b, i: (b, 0, 0)),
                  _const_spec((1, d)),
                  _const_spec(wsg.shape), _const_spec(wsu.shape), _const_spec(wsd.shape),
                  pl.BlockSpec(memory_space=pl.ANY)],
        out_specs=tok,
        out_shape=jax.ShapeDtypeStruct((bsz, s, d), F32),
        scratch_shapes=[pltpu.VMEM((2 * _local_rows(tm), PACK_ROWS, LANES), U32),
                        pltpu.VMEM((TOP_K, tm, PACK_ROWS, LANES), U32),
                        pltpu.VMEM((TOP_K, tm * PACK_ROWS, LANES), U32), pltpu.SemaphoreType.DMA((2,))],
        compiler_params=_params("arbitrary", "arbitrary"),
        name="moe_combine",
    )(lpos, lpos, tab, tab, tab, x1, h2, wtok, gate_f, g_post, wsg, wsu, wsd, ys)


def _block_diag_tiles(w):
    per = MXU_DIM // RNN_BLOCK_DIM
    nt = RNN_BLOCKS // per
    w4 = w.reshape(nt, per, RNN_BLOCK_DIM, RNN_BLOCK_DIM)
    eye = jnp.eye(per, dtype=w.dtype)
    return jnp.einsum("tnij,nm->tnimj", w4, eye).reshape(nt, MXU_DIM, MXU_DIM)


def _tile(n, pref):
    t = min(n, pref)
    assert n % t == 0, (n, t)
    return t


def _layer(x, c, w_ada, b_ada, g_pre_mix, g_post_mix, g_pre_ffn, g_post_ffn, w_in, b_forget,
           w_conv, b_conv, w_rg, b_rg, w_ig, b_ig, lam, w_branch_attn, w_branch_rnn, w_out,
           w_router, router_bias, w_exp_gate, w_exp_up, w_exp_down, w_sh_gate, w_sh_up, w_sh_down):
    bsz, s, d = x.shape
    t = bsz * s
    row = lambda v: v.reshape(1, -1)

    mod = _ada_call(c, w_ada, b_ada)
    shift_m, scale_m, gate_m, shift_f, scale_f, gate_f = [
        mod[:, j * d:(j + 1) * d].reshape(bsz, 1, d) for j in range(6)]

    o_f = 3 * ATT_WIDTH
    o_r = o_f + ATT_HEADS
    wqkv = w_in[:, :o_f].astype(BF16)
    wf = jnp.pad(w_in[:, o_f:o_r], ((0, 0), (0, LANES - ATT_HEADS))).astype(BF16)
    wr = w_in[:, o_r:].astype(BF16)
    q, k, v, f_logit, x_rnn, g_rnn, gate_a, gate_r = _inproj_call(
        x, shift_m, scale_m, row(g_pre_mix), wqkv, wf, wr, _tile(s, 512))

    bf_pad = jnp.pad(b_forget, (0, LANES - ATT_HEADS)).reshape(1, LANES)
    kbias = _cum_call(f_logit, bf_pad, _tile(s, 256))
    tq = _tile(s, 512)
    y_attn = _attn_call(q, k, v, kbias, tq, _tile(tq, 256))

    wbd = jnp.concatenate([_block_diag_tiles(w_rg), _block_diag_tiles(w_ig)], axis=-1).astype(BF16)
    y_rnn = _rnn_call(x_rnn, g_rnn, w_conv, row(b_conv), wbd, row(b_rg), row(b_ig), row(lam), _tile(s, 256))

    wr_t = w_router.T
    wr_hi = wr_t.astype(BF16)
    wr_mid = (wr_t - wr_hi.astype(F32)).astype(BF16)
    tm_m = _tile(s, 512)
    tm_r = _tile(tm_m, 256)
    x1, h2, h2_rows, lpos, before, n_tile, wtok, counts = _merge_call(
        y_attn, y_rnn, gate_a, gate_r, x, gate_m, shift_f, scale_f, row(g_post_mix), row(g_pre_ffn),
        w_branch_attn.astype(BF16), w_branch_rnn.astype(BF16), w_out.astype(BF16),
        wr_hi, wr_mid, router_bias.reshape(N_EXPERTS, 1), tm_m, tm_r)

    rows = 512
    n_blocks = -(-(t * TOP_K + N_EXPERTS * WINDOW_ROWS) // rows) + N_EXPERTS
    cnt = counts[:, 0].astype(I32)
    padded = ((cnt + WINDOW_ROWS + rows - 1) // rows) * rows
    pend = jnp.cumsum(padded).astype(I32)
    pstart = pend - padded
    n_used = jnp.maximum(pend[-1] // rows, 1)
    blk = jnp.minimum(jnp.arange(n_blocks, dtype=I32), n_used - 1)
    block_e = jnp.minimum(jnp.sum((pend[None, :] <= (blk * rows)[:, None]).astype(I32), axis=1), N_EXPERTS - 1)

    n_win = (n_tile[:, :, 0] + WINDOW_ROWS - 1) // WINDOW_ROWS
    win_end = jnp.cumsum(n_win, axis=1)
    w_id = jnp.arange(WIN_LIST, dtype=I32)[None, :, None]
    owner = (jnp.sum((win_end[:, None, :] <= w_id).astype(I32), axis=2, keepdims=True)
             == jnp.arange(N_EXPERTS, dtype=I32)[None, None, :]).astype(I32)
    first_slot = pstart[None, :] + before[:, :, 0] - (win_end - n_win) * WINDOW_ROWS
    wlist = jnp.sum(owner * first_slot[:, None, :], axis=2) + w_id[:, :, 0] * WINDOW_ROWS
    wlist = jnp.where(w_id[:, :, 0] == WIN_LIST - 1, win_end[:, -1:], wlist).astype(I32)[:, None, :]
    n_rt = lpos.shape[0]
    lpos = lpos + (jnp.arange(n_rt, dtype=I32) % 2)[:, None, None] * _local_rows(tm_r)

    rows3 = lambda a: a.reshape(-1, PACK_ROWS, LANES)
    rows2 = lambda a: a.reshape(-1, LANES)
    xs = _dispatch_call(lpos, wlist, pstart, pend, rows3(h2_rows), n_blocks * rows, tm_r, rows)
    ys = rows3(_expert_call(block_e, n_used.reshape(1), rows2(xs),
                            w_exp_gate.astype(BF16), w_exp_up.astype(BF16), w_exp_down.astype(BF16), rows))
    return _final_call(lpos, wlist, x1, h2, wtok, gate_f, row(g_post_ffn),
                       w_sh_gate.astype(BF16), w_sh_up.astype(BF16), w_sh_down.astype(BF16), ys, tm_r)


def kernel(x, c, w_ada, b_ada, g_pre_mix, g_post_mix, g_pre_ffn, g_post_ffn, w_in, b_forget, w_conv, b_conv, w_rg, b_rg, w_ig, b_ig, rglru_lambda, w_branch_attn, w_branch_rnn, w_out, w_router, router_bias, w_exp_gate, w_exp_up, w_exp_down, w_sh_gate, w_sh_up, w_sh_down):
    depth = w_ada.shape[0]
    for l in range(depth):
        x = _layer(x, c, w_ada[l], b_ada[l], g_pre_mix[l], g_post_mix[l], g_pre_ffn[l], g_post_ffn[l],
                   w_in[l], b_forget[l], w_conv[l], b_conv[l], w_rg[l], b_rg[l], w_ig[l], b_ig[l],
                   rglru_lambda[l], w_branch_attn[l], w_branch_rnn[l], w_out[l], w_router[l],
                   router_bias[l], w_exp_gate[l], w_exp_up[l], w_exp_down[l],
                   w_sh_gate[l], w_sh_up[l], w_sh_down[l])
    return x
```

```python
import functools

import jax
import jax.numpy as jnp
from jax import lax
from jax.experimental import pallas as pl
from jax.experimental.pallas import tpu as pltpu

F32 = jnp.float32
BF16 = jnp.bfloat16
I32 = jnp.int32

D_MODEL = 1024
ATT_HEADS = 8
ATT_HEAD_DIM = 64
ATT_WIDTH = ATT_HEADS * ATT_HEAD_DIM
RNN_WIDTH = D_MODEL
RNN_BLOCKS = 16
RNN_BLOCK_DIM = RNN_WIDTH // RNN_BLOCKS
CONV_WIDTH = 4
RGLRU_C = 8.0
N_EXPERTS = 64
TOP_K = 6
N_GROUPS = 8
GROUP_SIZE = N_EXPERTS // N_GROUPS
TOPK_GROUPS = 4
EXPERT_FF = D_MODEL // 4
ROUTE_SCALE = 2.5
NORM_EPS = 1e-6

LANES = 128
SUBLANES = 8
MXU_DIM = 256
VMEM_LIMIT_BYTES = 56 * 1024 * 1024

NEG_BIG = -1e30
RANK_BITS = 20
RANK_MASK = (1 << RANK_BITS) - 1


def _sigmoid(x):
    return 0.5 * jnp.tanh(0.5 * x) + 0.5


def _rms_scale(x):
    return x * lax.rsqrt(jnp.mean(x * x, axis=-1, keepdims=True) + NORM_EPS)


def _dot(a, b):
    return jnp.dot(a, b, preferred_element_type=F32)


def _dot_nt(a, b):
    return lax.dot_general(a, b, (((1,), (1,)), ((), ())), preferred_element_type=F32)


def _split3(a):
    hi = a.astype(BF16)
    r1 = a - hi.astype(F32)
    mid = r1.astype(BF16)
    lo = (r1 - mid.astype(F32)).astype(BF16)
    return hi, mid, lo


def _params(*sem):
    return pltpu.CompilerParams(dimension_semantics=sem, vmem_limit_bytes=VMEM_LIMIT_BYTES)


def _const_spec(shape):
    nd = len(shape)
    return pl.BlockSpec(shape, lambda *_: (0,) * nd)


def _ada_kernel(c_ref, w_ref, b_ref, o_ref):
    c = c_ref[...]
    sc = c * _sigmoid(c)
    s_hi, s_mid, _ = _split3(sc)
    w_hi, w_mid, _ = _split3(w_ref[...])
    acc = _dot(s_hi, w_hi) + _dot(s_hi, w_mid) + _dot(s_mid, w_hi)
    o_ref[...] = acc + b_ref[...]


def _ada_call(c, w_ada, b_ada):
    bsz, d = c.shape
    n = w_ada.shape[1]
    tn = d
    return pl.pallas_call(
        _ada_kernel,
        grid=(n // tn,),
        in_specs=[
            pl.BlockSpec((bsz, d), lambda j: (0, 0)),
            pl.BlockSpec((d, tn), lambda j: (0, j)),
            pl.BlockSpec((1, tn), lambda j: (0, j)),
        ],
        out_specs=pl.BlockSpec((bsz, tn), lambda j: (0, j)),
        out_shape=jax.ShapeDtypeStruct((bsz, n), F32),
        compiler_params=_params("arbitrary"),
        name="ada_mod",
    )(c, w_ada, b_ada.reshape(1, n))


def _inproj_kernel(x_ref, sh_ref, sc_ref, g_ref, wqkv_ref, wf_ref, wr_ref,
                   q_ref, k_ref, v_ref, f_ref, xr_ref, gr_ref, ga_ref, gb_ref):
    x = x_ref[0]
    h = (_rms_scale(x) * g_ref[...]) * (1.0 + sc_ref[0]) + sh_ref[0]
    hb = h.astype(BF16)
    for j, ref in enumerate((q_ref, k_ref, v_ref)):
        ref[0] = _dot(hb, wqkv_ref[:, j * ATT_WIDTH:(j + 1) * ATT_WIDTH]).astype(BF16)
    f_ref[0] = _dot(hb, wf_ref[...])
    half = D_MODEL // 2
    for j, ref in enumerate((xr_ref, gr_ref, ga_ref, gb_ref)):
        for c in range(2):
            lo = j * D_MODEL + c * half
            ref[0, :, c * half:(c + 1) * half] = _dot(hb, wr_ref[:, lo:lo + half]).astype(BF16)


def _inproj_call(x, shift, scale, g, wqkv, wf, wr, tm):
    bsz, s, d = x.shape
    nt = s // tm
    tok = lambda w: pl.BlockSpec((1, tm, w), lambda b, i: (b, i, 0))
    mod = pl.BlockSpec((1, 1, d), lambda b, i: (b, 0, 0))
    out_w = (ATT_WIDTH,) * 3 + (LANES,) + (d,) * 4
    out_dt = (BF16,) * 3 + (F32,) + (BF16,) * 4
    return pl.pallas_call(
        _inproj_kernel,
        grid=(bsz, nt),
        in_specs=[tok(d), mod, mod, _const_spec((1, d)),
                  _const_spec(wqkv.shape), _const_spec(wf.shape), _const_spec(wr.shape)],
        out_specs=[tok(w) for w in out_w],
        out_shape=[jax.ShapeDtypeStruct((bsz, s, w), dt) for w, dt in zip(out_w, out_dt)],
        compiler_params=_params("arbitrary", "arbitrary"),
        name="in_proj",
    )(x, shift, scale, g, wqkv, wf, wr)


N_BIAS_PIECES = 3


def _bias_placement():
    h = jnp.arange(LANES)[:, None]
    lane = jnp.arange(ATT_WIDTH)[None, :]
    mats = []
    for piece in range(N_BIAS_PIECES):
        target = (h // 2) * LANES + ATT_HEAD_DIM * (1 - h % 2) + piece
        mats.append(((lane == target) & (h < ATT_HEADS)).astype(BF16))
    return jnp.stack(mats)


def _cum_kernel(f_ref, b_ref, place_ref, kb_ref, *, blk):
    s = f_ref.shape[1]
    row = lax.broadcasted_iota(I32, (blk, blk), 0)
    col = lax.broadcasted_iota(I32, (blk, blk), 1)
    tri = (row >= col).astype(BF16)
    carry = jnp.zeros((1, LANES), F32)
    for i in range(s // blk):
        z = f_ref[0, i * blk:(i + 1) * blk, :] + b_ref[...]
        lf = jnp.minimum(z, 0.0) - jnp.log1p(jnp.exp(-jnp.abs(z)))
        hi, mid, lo = _split3(lf)
        c = _dot(tri, hi) + _dot(tri, mid) + _dot(tri, lo) + carry
        carry = c[blk - 1:blk, :]
        pieces = _split3(-c)
        kb = _dot(pieces[0], place_ref[0])
        for j in range(1, N_BIAS_PIECES):
            kb = kb + _dot(pieces[j], place_ref[j])
        kb_ref[0, i * blk:(i + 1) * blk, :] = kb.astype(BF16)


def _cum_call(f_logit, b_forget_pad, blk):
    bsz, s, _ = f_logit.shape
    place = _bias_placement()
    return pl.pallas_call(
        functools.partial(_cum_kernel, blk=blk),
        grid=(bsz,),
        in_specs=[pl.BlockSpec((1, s, LANES), lambda b: (b, 0, 0)), _const_spec((1, LANES)),
                  _const_spec(place.shape)],
        out_specs=pl.BlockSpec((1, s, ATT_WIDTH), lambda b: (b, 0, 0)),
        out_shape=jax.ShapeDtypeStruct((bsz, s, ATT_WIDTH), BF16),
        compiler_params=_params("arbitrary"),
        name="forget_cumsum",
    )(f_logit, b_forget_pad, place)


ATT_VT_ROWS = ATT_HEAD_DIM + 16


def _attn_kernel(q_ref, k_ref, v_ref, kb_ref, o_ref, vt_sc, st_sc, m_sc, acc_sc, *, tq, tk):
    s = q_ref.shape[1]
    nq, nk, ratio = s // tq, s // tk, tq // tk
    lane = lax.broadcasted_iota(I32, (1, LANES), 1)
    head_lanes = (lane < ATT_HEAD_DIM, lane >= ATT_HEAD_DIM)
    one_lanes = ((lane >= ATT_HEAD_DIM) & (lane < ATT_HEAD_DIM + N_BIAS_PIECES), lane < N_BIAS_PIECES)
    key_minus_query = (lax.broadcasted_iota(I32, (tk, tq), 0) - lax.broadcasted_iota(I32, (tk, tq), 1))

    vt = v_ref[0].astype(F32).T
    ones = jnp.ones((ATT_VT_ROWS - ATT_HEAD_DIM, s), F32)
    for h in range(2):
        vth = jnp.concatenate([vt[h * ATT_HEAD_DIM:(h + 1) * ATT_HEAD_DIM, :], ones], axis=0).astype(BF16)
        for j in range(nk):
            vt_sc[h, j] = vth[:, j * tk:(j + 1) * tk]

    def q_body(qi, carry):
        q0 = pl.multiple_of(qi * tq, tq)
        qs = q_ref[0, pl.ds(q0, tq), :] * (ATT_HEAD_DIM ** -0.5)
        qa = [jnp.where(head_lanes[h], qs, one_lanes[h].astype(BF16)) for h in range(2)]
        m_sc[...] = jnp.full(m_sc.shape, NEG_BIG, F32)
        acc_sc[...] = jnp.zeros(acc_sc.shape, F32)

        def scores(kj):
            k0 = pl.multiple_of(kj * tk, tk)
            kb = k_ref[0, pl.ds(k0, tk), :]
            bias = kb_ref[0, pl.ds(k0, tk), :]
            return [_dot_nt(jnp.where(head_lanes[h], kb, bias), qa[h]) for h in range(2)]

        def accumulate(kj, masked):
            for h in range(2):
                st = st_sc[h]
                if masked:
                    st = jnp.where(key_minus_query <= q0 - kj * tk, st, NEG_BIG)
                m_old = m_sc[h]
                m_new = jnp.maximum(m_old, jnp.max(st, axis=0, keepdims=True))
                p = jnp.exp(st - m_new).astype(BF16)
                acc_sc[h] = jnp.exp(m_old - m_new) * acc_sc[h] + _dot(vt_sc[h, kj], p)
                m_sc[h] = m_new

        def put(sts):
            for h in range(2):
                st_sc[h] = sts[h]

        n_full = qi * ratio
        put(scores(0))

        def kv_body(kj, c2):
            nxt = scores(kj + 1)
            accumulate(kj, False)
            put(nxt)
            return c2

        lax.fori_loop(0, n_full, kv_body, 0)
        for d in range(ratio):
            nxt = scores(n_full + d + 1) if d + 1 < ratio else None
            accumulate(n_full + d, True)
            if nxt is not None:
                put(nxt)

        outs = []
        for h in range(2):
            acc = acc_sc[h]
            outs.append(acc[:ATT_HEAD_DIM, :] * (1.0 / acc[ATT_HEAD_DIM:ATT_HEAD_DIM + 1, :]))
        o_ref[0, pl.ds(q0, tq), :] = jnp.concatenate(outs, axis=0).T.astype(BF16)
        return carry

    lax.fori_loop(0, nq, q_body, 0)


def _attn_call(q, k, v, kbias, tq, tk):
    bsz, s, _ = q.shape
    qkv = pl.BlockSpec((1, s, LANES), lambda b, p: (b, 0, p))
    return pl.pallas_call(
        functools.partial(_attn_kernel, tq=tq, tk=tk),
        grid=(bsz, ATT_HEADS // 2),
        in_specs=[qkv, qkv, qkv, qkv],
        out_specs=qkv,
        out_shape=jax.ShapeDtypeStruct((bsz, s, ATT_WIDTH), BF16),
        scratch_shapes=[pltpu.VMEM((2, s // tk, ATT_VT_ROWS, tk), BF16), pltpu.VMEM((2, tk, tq), F32),
                        pltpu.VMEM((2, 1, tq), F32), pltpu.VMEM((2, ATT_VT_ROWS, tq), F32)],
        compiler_params=_params("arbitrary", "arbitrary"),
        name="fox_attention",
    )(q, k, v, kbias)


def _gelu_tanh(x):
    c = 0.7978845608028654
    return 0.5 * x * (1.0 + jnp.tanh(c * (x + 0.044715 * (x * x * x))))


def _rnn_kernel(xr_ref, gr_ref, wc_ref, bc_ref, wbd_ref, brg_ref, big_ref, lam_ref, o_ref,
                tail_sc, hc_sc, a_sc, b_sc, h_sc, *, ts):
    @pl.when(pl.program_id(1) == 0)
    def _():
        tail_sc[...] = jnp.zeros(tail_sc.shape, F32)
        hc_sc[...] = jnp.zeros(hc_sc.shape, F32)

    x = xr_ref[0].astype(F32)
    full = jnp.concatenate([tail_sc[...], x], axis=0)
    u = bc_ref[...] + wc_ref[3:4, :] * x
    for j in range(CONV_WIDTH - 1):
        off = SUBLANES - (CONV_WIDTH - 1) + j
        u = u + wc_ref[j:j + 1, :] * full[off:off + ts, :]
    tail_sc[...] = x[ts - SUBLANES:ts, :]

    lam = lam_ref[...]
    neg_sp = -(jnp.maximum(-lam, 0.0) + jnp.log1p(jnp.exp(-jnp.abs(lam))))
    for jt in range(RNN_WIDTH // MXU_DIM):
        cs = slice(jt * MXU_DIM, (jt + 1) * MXU_DIM)
        uc = u[:, cs]
        g = _dot(uc.astype(BF16), wbd_ref[jt])
        r = _sigmoid(g[:, :MXU_DIM] + brg_ref[:, cs])
        ig = _sigmoid(g[:, MXU_DIM:] + big_ref[:, cs])
        a = jnp.exp(r * (RGLRU_C * neg_sp[:, cs]))
        a_sc[:, cs] = a
        v = 1.0 - a * a
        b_sc[:, cs] = jnp.where(v > 0.0, v * lax.rsqrt(v), 0.0) * (ig * uc)

    srow = lax.broadcasted_iota(I32, (SUBLANES, RNN_WIDTH), 0)

    def scan_body(i, carry):
        r0 = pl.multiple_of(i * SUBLANES, SUBLANES)
        a = a_sc[pl.ds(r0, SUBLANES), :]
        b = b_sc[pl.ds(r0, SUBLANES), :]
        b = b + jnp.where(srow == 0, a * carry, 0.0)
        for sh in (1, 2, 4):
            keep = srow >= sh
            b = jnp.where(keep, b + a * pltpu.roll(b, sh, 0), b)
            if sh < SUBLANES // 2:
                a = jnp.where(keep, a * pltpu.roll(a, sh, 0), a)
        h_sc[pl.ds(r0, SUBLANES), :] = b
        return jnp.broadcast_to(b[SUBLANES - 1:SUBLANES, :], (SUBLANES, RNN_WIDTH))

    hc_sc[...] = lax.fori_loop(0, ts // SUBLANES, scan_body, hc_sc[...])
    o_ref[0] = (h_sc[...] * _gelu_tanh(gr_ref[0].astype(F32))).astype(BF16)


def _rnn_call(x_rnn, g_rnn, w_conv, b_conv, wbd, b_rg, b_ig, lam, ts):
    bsz, s, w = x_rnn.shape
    tok = pl.BlockSpec((1, ts, w), lambda b, i: (b, i, 0))
    return pl.pallas_call(
        functools.partial(_rnn_kernel, ts=ts),
        grid=(bsz, s // ts),
        in_specs=[tok, tok, _const_spec(w_conv.shape), _const_spec((1, w)), _const_spec(wbd.shape),
                  _const_spec((1, w)), _const_spec((1, w)), _const_spec((1, w))],
        out_specs=tok,
        out_shape=jax.ShapeDtypeStruct((bsz, s, w), BF16),
        scratch_shapes=[pltpu.VMEM((SUBLANES, w), F32), pltpu.VMEM((SUBLANES, w), F32),
                        pltpu.VMEM((ts, w), F32), pltpu.VMEM((ts, w), F32), pltpu.VMEM((ts, w), F32)],
        compiler_params=_params("arbitrary", "arbitrary"),
        name="rglru",
    )(x_rnn, g_rnn, w_conv, b_conv, wbd, b_rg, b_ig, lam)


WINDOW_SHIFT = 5
WINDOW_ROWS = 1 << WINDOW_SHIFT


def _route(h2, wr_hi_ref, wr_mid_ref, rbias_ref, cnt_ref, lpos_ref, before_ref, ntile_ref, wtok_ref, j, tm):
    h_hi, h_mid, _ = _split3(h2)
    logits = _dot_nt(wr_hi_ref[...], h_hi) + _dot_nt(wr_hi_ref[...], h_mid) + _dot_nt(wr_mid_ref[...], h_hi)
    scores = _sigmoid(logits)
    sel = scores + rbias_ref[...]

    giota = lax.broadcasted_iota(I32, (N_GROUPS, tm), 0)
    gs = jnp.zeros((N_GROUPS, tm), F32)
    for g in range(N_GROUPS):
        blk = sel[g * GROUP_SIZE:(g + 1) * GROUP_SIZE, :]
        m1 = jnp.max(blk, axis=0, keepdims=True)
        i1 = jnp.min(jnp.where(blk == m1, giota, N_EXPERTS), axis=0, keepdims=True)
        m2 = jnp.max(jnp.where(giota == i1, -jnp.inf, blk), axis=0, keepdims=True)
        gs = jnp.where(giota == g, m1 + m2, gs)

    gsel = jnp.zeros((N_GROUPS, tm), F32)
    for _ in range(TOPK_GROUPS):
        m = jnp.max(gs, axis=0, keepdims=True)
        idx = jnp.min(jnp.where(gs == m, giota, N_EXPERTS), axis=0, keepdims=True)
        hit = giota == idx
        gsel = jnp.where(hit, 1.0, gsel)
        gs = jnp.where(hit, -jnp.inf, gs)

    masked = jnp.concatenate(
        [jnp.where(gsel[g:g + 1, :] > 0.0, sel[g * GROUP_SIZE:(g + 1) * GROUP_SIZE, :], -jnp.inf)
         for g in range(N_GROUPS)], axis=0)

    eiota = lax.broadcasted_iota(I32, (N_EXPERTS, tm), 0)
    chosen = jnp.zeros((N_EXPERTS, tm), F32)
    idxs, wts = [], []
    for _ in range(TOP_K):
        m = jnp.max(masked, axis=0, keepdims=True)
        idx = jnp.min(jnp.where(masked == m, eiota, N_EXPERTS), axis=0, keepdims=True)
        hit = eiota == idx
        wts.append(jnp.sum(jnp.where(hit, scores, 0.0), axis=0, keepdims=True))
        idxs.append(idx)
        chosen = jnp.where(hit, 1.0, chosen)
        masked = jnp.where(hit, -jnp.inf, masked)
    wsum = wts[0]
    for w in wts[1:]:
        wsum = wsum + w

    srow = lax.broadcasted_iota(I32, (tm, tm), 0)
    scol = lax.broadcasted_iota(I32, (tm, tm), 1)
    earlier = (srow < scol).astype(BF16)
    local_rank = _dot(chosen.astype(BF16), earlier)
    n_e = jnp.sum(chosen, axis=1, keepdims=True)
    before_ref[j] = cnt_ref[...].astype(I32)
    ntile_ref[j] = n_e.astype(I32)
    cnt_ref[...] = cnt_ref[...] + n_e
    n_pad = (((n_e.astype(I32) + (WINDOW_ROWS - 1)) >> WINDOW_SHIFT) << WINDOW_SHIFT).astype(F32)
    erow = lax.broadcasted_iota(I32, (N_EXPERTS, N_EXPERTS), 0)
    ecol = lax.broadcasted_iota(I32, (N_EXPERTS, N_EXPERTS), 1)
    off = _dot((ecol < erow).astype(BF16), jnp.broadcast_to(n_pad, (N_EXPERTS, LANES)).astype(BF16))[:, :1]
    local_pos = off + local_rank

    riota = lax.broadcasted_iota(I32, (SUBLANES, tm), 0)
    lpos = jnp.zeros((SUBLANES, tm), I32)
    wrow = jnp.zeros((SUBLANES, tm), F32)
    for r in range(TOP_K):
        lp = jnp.sum(jnp.where(eiota == idxs[r], local_pos, 0.0), axis=0, keepdims=True)
        lpos = jnp.where(riota == r, lp.astype(I32), lpos)
        wrow = jnp.where(riota == r, wts[r] / wsum * ROUTE_SCALE, wrow)
    lpos_ref[j] = lpos
    wpad = jnp.concatenate([wrow, jnp.zeros((LANES - SUBLANES, tm), F32)], axis=0)
    wtok_ref[j * tm:(j + 1) * tm, :] = wpad.T


U32 = jnp.uint32
PACK_ROWS = D_MODEL // (2 * LANES)
HIGH_HALF = 0xFFFF0000


def _store_token_rows(rows_ref, val):
    half = D_MODEL // 2
    n = val.shape[0]
    bits = lax.bitcast_convert_type(val.astype(BF16).astype(F32), U32)
    words = (bits[:, half:] & jnp.uint32(HIGH_HALF)) | (bits[:, :half] >> 16)
    for c in range(PACK_ROWS):
        chunk = words[:, c * LANES:(c + 1) * LANES]
        if len(rows_ref.shape) == 2:
            rows_ref[pl.ds(c, n, stride=PACK_ROWS), :] = chunk
        else:
            rows_ref[:, c, :] = chunk


def _load_token_rows(rows_ref):
    if len(rows_ref.shape) == 2:
        n = rows_ref.shape[0] // PACK_ROWS
        chunks = [rows_ref[pl.ds(c, n, stride=PACK_ROWS), :] for c in range(PACK_ROWS)]
    else:
        chunks = [rows_ref[:, c, :] for c in range(PACK_ROWS)]
    words = jnp.concatenate(chunks, axis=-1)
    lo = lax.bitcast_convert_type(words << 16, F32)
    hi = lax.bitcast_convert_type(words & jnp.uint32(HIGH_HALF), F32)
    return jnp.concatenate([lo, hi], axis=-1)


def _merge_kernel(ya_ref, yr_ref, ga_ref, gb_ref, x_ref, gm_ref, shf_ref, scf_ref, gpost_ref, gpre_ref,
                  wba_ref, wbr_ref, wout_ref, wr_hi_ref, wr_mid_ref, rbias_ref,
                  x1_ref, h2_ref, h2rows_ref, lpos_ref, before_ref, ntile_ref, wtok_ref, cnt_ref, *, tm, tr):
    @pl.when((pl.program_id(0) == 0) & (pl.program_id(1) == 0))
    def _():
        cnt_ref[...] = jnp.zeros(cnt_ref.shape, F32)

    pa = _dot(ya_ref[0], wba_ref[...])
    pr = _dot(yr_ref[0], wbr_ref[...])
    merged = _sigmoid(ga_ref[0].astype(F32)) * pa + _sigmoid(gb_ref[0].astype(F32)) * pr
    o = _dot(merged.astype(BF16), wout_ref[...])
    x1 = x_ref[0] + gm_ref[0] * (_rms_scale(o) * gpost_ref[...])
    x1_ref[0] = x1
    h2 = (_rms_scale(x1) * gpre_ref[...]) * (1.0 + scf_ref[0]) + shf_ref[0]
    h2_ref[0] = h2.astype(BF16)
    _store_token_rows(h2rows_ref, h2)
    for j in range(tm // tr):
        _route(h2[j * tr:(j + 1) * tr, :], wr_hi_ref, wr_mid_ref, rbias_ref, cnt_ref,
               lpos_ref, before_ref, ntile_ref, wtok_ref, j, tr)


def _merge_call(y_attn, y_rnn, gate_a, gate_r, x, gate_m, shift_f, scale_f, g_post, g_pre,
                wba, wbr, wout, wr_hi, wr_mid, rbias, tm, tr):
    bsz, s, d = x.shape
    nt = s // tm
    nr = tm // tr
    tok = lambda w: pl.BlockSpec((1, tm, w), lambda b, i: (b, i, 0))
    mod = pl.BlockSpec((1, 1, d), lambda b, i: (b, 0, 0))
    return pl.pallas_call(
        functools.partial(_merge_kernel, tm=tm, tr=tr),
        grid=(bsz, nt),
        in_specs=[tok(ATT_WIDTH), tok(d), tok(d), tok(d), tok(d), mod, mod, mod,
                  _const_spec((1, d)), _const_spec((1, d)),
                  _const_spec(wba.shape), _const_spec(wbr.shape), _const_spec(wout.shape),
                  _const_spec(wr_hi.shape), _const_spec(wr_mid.shape), _const_spec(rbias.shape)],
        out_specs=[tok(d), tok(d),
                   pl.BlockSpec((tm * PACK_ROWS, LANES), lambda b, i: (b * nt + i, 0)),
                   pl.BlockSpec((nr, SUBLANES, tr), lambda b, i: (b * nt + i, 0, 0)),
                   pl.BlockSpec((nr, N_EXPERTS, 1), lambda b, i: (b * nt + i, 0, 0)),
                   pl.BlockSpec((nr, N_EXPERTS, 1), lambda b, i: (b * nt + i, 0, 0)),
                   pl.BlockSpec((tm, LANES), lambda b, i: (b * nt + i, 0)),
                   pl.BlockSpec((N_EXPERTS, 1), lambda b, i: (0, 0))],
        out_shape=[jax.ShapeDtypeStruct((bsz, s, d), F32), jax.ShapeDtypeStruct((bsz, s, d), BF16),
                   jax.ShapeDtypeStruct((bsz * s * PACK_ROWS, LANES), U32),
                   jax.ShapeDtypeStruct((bsz * nt * nr, SUBLANES, tr), I32),
                   jax.ShapeDtypeStruct((bsz * nt * nr, N_EXPERTS, 1), I32),
                   jax.ShapeDtypeStruct((bsz * nt * nr, N_EXPERTS, 1), I32),
                   jax.ShapeDtypeStruct((bsz * s, LANES), F32),
                   jax.ShapeDtypeStruct((N_EXPERTS, 1), F32)],
        compiler_params=_params("arbitrary", "arbitrary"),
        name="merge_route",
    )(y_attn, y_rnn, gate_a, gate_r, x, gate_m, shift_f, scale_f, g_post, g_pre,
      wba, wbr, wout, wr_hi, wr_mid, rbias)


WIN_LIST = LANES
ROW_COPY_UNROLL = 8


def _local_rows(tm):
    rows = tm * TOP_K + N_EXPERTS * WINDOW_ROWS
    assert rows // WINDOW_ROWS < WIN_LIST
    return rows


def _for_each_window(wlist_ref, fn):
    def body(w, c):
        fn(pl.multiple_of(w * WINDOW_ROWS, WINDOW_ROWS), wlist_ref[0, 0, w])
        return c

    lax.fori_loop(0, wlist_ref[0, 0, WIN_LIST - 1], body, 0)


def _dispatch_kernel(lpos_ref, wlist_ref, wlistp_ref, pstart_ref, pend_ref, h_ref, xs_hbm, loc_sc, zero_sc, sem,
                     *, tm, rows, n_blocks):
    g = pl.program_id(0)
    cur = lax.rem(g, 2)

    @pl.when(g == 0)
    def _():
        loc_sc[...] = jnp.zeros(loc_sc.shape, U32)
        zero_sc[...] = jnp.zeros(zero_sc.shape, U32)
        n_used = pend_ref[N_EXPERTS - 1] // rows

        def zero_block(b):
            return pltpu.make_async_copy(zero_sc, xs_hbm.at[pl.ds(b * rows, rows)], sem)

        def region_tail(e, fn):
            last = pend_ref[e] // rows - 1
            for back in range(2):
                @pl.when(last - back >= pstart_ref[e] // rows)
                def _():
                    fn(zero_block(last - back))

        def zstart(e, c):
            region_tail(e, lambda cp: cp.start())
            return c

        def zwait(e, c):
            region_tail(e, lambda cp: cp.wait())
            return c

        def tstart(b, c):
            zero_block(b).start()
            return c

        def twait(b, c):
            zero_block(0).wait()
            return c

        lax.fori_loop(0, N_EXPERTS, zstart, 0)
        lax.fori_loop(n_used, n_blocks, tstart, 0)
        lax.fori_loop(0, N_EXPERTS, zwait, 0)
        lax.fori_loop(n_used, n_blocks, twait, 0)

    for t in range(tm):
        row = h_ref[t]
        for k in range(TOP_K):
            loc_sc[lpos_ref[0, k, t]] = row

    half = _local_rows(tm)

    def window(base):
        return lambda row0, slot0: pltpu.make_async_copy(
            loc_sc.at[pl.ds(pl.multiple_of(base + row0, WINDOW_ROWS), WINDOW_ROWS)],
            xs_hbm.at[pl.ds(slot0, WINDOW_ROWS)], sem)

    @pl.when(g > 0)
    def _():
        _for_each_window(wlistp_ref, lambda r, s: window((1 - cur) * half)(r, s).wait())

    _for_each_window(wlist_ref, lambda r, s: window(cur * half)(r, s).start())

    @pl.when(g == pl.num_programs(0) - 1)
    def _():
        _for_each_window(wlist_ref, lambda r, s: window(cur * half)(r, s).wait())


def _dispatch_call(lpos, wlist, pstart, pend, h2_rows, n_rows, tm, rows):
    t = h2_rows.shape[0]
    smem = functools.partial(pl.BlockSpec, memory_space=pltpu.SMEM)
    return pl.pallas_call(
        functools.partial(_dispatch_kernel, tm=tm, rows=rows, n_blocks=n_rows // rows),
        grid=(t // tm,),
        in_specs=[smem((1, SUBLANES, tm), lambda i: (i, 0, 0)),
                  smem((1, 1, WIN_LIST), lambda i: (i, 0, 0)),
                  smem((1, 1, WIN_LIST), lambda i: (jnp.maximum(i - 1, 0), 0, 0)),
                  smem(), smem(),
                  pl.BlockSpec((tm, PACK_ROWS, LANES), lambda i: (i, 0, 0))],
        out_specs=pl.BlockSpec(memory_space=pl.ANY),
        out_shape=jax.ShapeDtypeStruct((n_rows, PACK_ROWS, LANES), U32),
        scratch_shapes=[pltpu.VMEM((2 * _local_rows(tm), PACK_ROWS, LANES), U32),
                        pltpu.VMEM((rows, PACK_ROWS, LANES), U32), pltpu.SemaphoreType.DMA],
        compiler_params=_params("arbitrary"),
        name="moe_dispatch",
    )(lpos, wlist, wlist, pstart, pend, h2_rows)


def _expert_kernel(be_ref, nused_ref, xs_ref, wg_ref, wu_ref, wd_ref, ys_ref, *, rows):
    @pl.when(pl.program_id(0) < nused_ref[0])
    def _():
        xb = _load_token_rows(xs_ref).astype(BF16)
        g = _dot(xb, wg_ref[0])
        u = _dot(xb, wu_ref[0])
        act = (g * _sigmoid(g)) * u
        _store_token_rows(ys_ref, _dot(act.astype(BF16), wd_ref[0]))

    @pl.when(pl.program_id(0) >= nused_ref[0])
    def _():
        ys_ref[...] = jnp.zeros(ys_ref.shape, U32)


def _expert_call(block_e, n_used, xs, wg, wu, wd, rows):
    n_rows = xs.shape[0] // PACK_ROWS
    _, d, ff = wg.shape
    row_spec = pl.BlockSpec((rows * PACK_ROWS, LANES), lambda i, be, nu: (jnp.minimum(i, nu[0] - 1), 0))
    grid_spec = pltpu.PrefetchScalarGridSpec(
        num_scalar_prefetch=2,
        grid=(n_rows // rows,),
        in_specs=[row_spec,
                  pl.BlockSpec((1, d, ff), lambda i, be, nu: (be[i], 0, 0)),
                  pl.BlockSpec((1, d, ff), lambda i, be, nu: (be[i], 0, 0)),
                  pl.BlockSpec((1, ff, d), lambda i, be, nu: (be[i], 0, 0))],
        out_specs=pl.BlockSpec((rows * PACK_ROWS, LANES), lambda i, be, nu: (i, 0)),
    )
    return pl.pallas_call(
        functools.partial(_expert_kernel, rows=rows),
        grid_spec=grid_spec,
        out_shape=jax.ShapeDtypeStruct(xs.shape, U32),
        compiler_params=_params("arbitrary"),
        name="moe_experts",
    )(block_e, n_used, xs, wg, wu, wd)


def _final_kernel(lpos0_ref, lposn_ref, tab0_ref, tabw_ref, tabi_ref, x1_ref, h2_ref, wtok_ref, gf_ref, gpost_ref,
                  wsg_ref, wsu_ref, wsd_ref, ys_hbm, o_ref, win_sc, gnext_sc, gcur_sc, sem, *, tm):
    nt = pl.num_programs(1)
    g = pl.program_id(0) * nt + pl.program_id(1)
    n_tiles = pl.num_programs(0) * nt
    buf_next = lax.rem(g + 1, 2)
    buf_after = lax.rem(g, 2)
    half = _local_rows(tm)

    def window(buf):
        return lambda row0, slot0: pltpu.make_async_copy(
            ys_hbm.at[pl.ds(slot0, WINDOW_ROWS)],
            win_sc.at[pl.ds(pl.multiple_of(buf * half + row0, WINDOW_ROWS), WINDOW_ROWS)], sem.at[buf])

    def start_windows(tab_ref, buf):
        _for_each_window(tab_ref, lambda r, s: window(buf)(r, s).start())

    def wait_windows(tab_ref, buf):
        _for_each_window(tab_ref, lambda r, s: window(buf)(r, s).wait())

    def pick_rows(lpos_ref, t):
        for k in range(TOP_K):
            gnext_sc[k, t] = win_sc[lpos_ref[0, k, t]]

    def stage():
        for k in range(TOP_K):
            gcur_sc[k] = gnext_sc[k].reshape(tm * PACK_ROWS, LANES)

    @pl.when(g == 0)
    def _():
        start_windows(tab0_ref, 0)
        wait_windows(tab0_ref, 0)

        def body(i, c):
            for u in range(ROW_COPY_UNROLL):
                pick_rows(lpos0_ref, i * ROW_COPY_UNROLL + u)
            return c

        lax.fori_loop(0, tm // ROW_COPY_UNROLL, body, 0)
        stage()

        @pl.when(n_tiles > 1)
        def _():
            start_windows(tabw_ref, 1)

    @pl.when(g + 1 < n_tiles)
    def _():
        wait_windows(tabw_ref, buf_next)

    @pl.when(g + 2 < n_tiles)
    def _():
        start_windows(tabi_ref, buf_after)

    for t in range(tm):
        pick_rows(lposn_ref, t)

    hb = h2_ref[0]
    a = _dot(hb, wsg_ref[...])
    u = _dot(hb, wsu_ref[...])
    tot = _dot(((a * _sigmoid(a)) * u).astype(BF16), wsd_ref[...])

    w = wtok_ref[...]
    routed = w[:, 0:1] * _load_token_rows(gcur_sc.at[0])
    for k in range(1, TOP_K):
        routed = routed + w[:, k:k + 1] * _load_token_rows(gcur_sc.at[k])
    tot = routed + tot
    o_ref[0] = x1_ref[0] + gf_ref[0] * (_rms_scale(tot) * gpost_ref[...])

    stage()


def _final_call(lpos, tab, x1, h2, wtok, gate_f, g_post, wsg, wsu, wsd, ys, tm):
    bsz, s, d = x1.shape
    nt = s // tm
    n_tiles = bsz * nt
    smem = functools.partial(pl.BlockSpec, memory_space=pltpu.SMEM)
    tok = pl.BlockSpec((1, tm, d), lambda b, i: (b, i, 0))
    tab_shape = (1, 1, WIN_LIST)
    ahead = lambda k: (lambda b, i: (jnp.minimum(b * nt + i + k, n_tiles - 1), 0, 0))
    return pl.pallas_call(
        functools.partial(_final_kernel, tm=tm),
        grid=(bsz, nt),
        in_specs=[smem((1, SUBLANES, tm), lambda b, i: (0, 0, 0)),
                  smem((1, SUBLANES, tm), ahead(1)),
                  smem(tab_shape, lambda b, i: (0, 0, 0)),
                  smem(tab_shape, ahead(1)),
                  smem(tab_shape, ahead(2)),
                  tok, tok,
                  pl.BlockSpec((tm, LANES), lambda b, i: (b * nt + i, 0)),
                  pl.BlockSpec((1, 1, d), lambda b, i: (b, 0, 0)),
                  _const_spec((1, d)),
                  _const_spec(wsg.shape), _const_spec(wsu.shape), _const_spec(wsd.shape),
                  pl.BlockSpec(memory_space=pl.ANY)],
        out_specs=tok,
        out_shape=jax.ShapeDtypeStruct((bsz, s, d), F32),
        scratch_shapes=[pltpu.VMEM((2 * _local_rows(tm), PACK_ROWS, LANES), U32),
                        pltpu.VMEM((TOP_K, tm, PACK_ROWS, LANES), U32),
                        pltpu.VMEM((TOP_K, tm * PACK_ROWS, LANES), U32), pltpu.SemaphoreType.DMA((2,))],
        compiler_params=_params("arbitrary", "arbitrary"),
        name="moe_combine",
    )(lpos, lpos, tab, tab, tab, x1, h2, wtok, gate_f, g_post, wsg, wsu, wsd, ys)


def _block_diag_tiles(w):
    per = MXU_DIM // RNN_BLOCK_DIM
    nt = RNN_BLOCKS // per
    w4 = w.reshape(nt, per, RNN_BLOCK_DIM, RNN_BLOCK_DIM)
    eye = jnp.eye(per, dtype=w.dtype)
    return jnp.einsum("tnij,nm->tnimj", w4, eye).reshape(nt, MXU_DIM, MXU_DIM)


def _tile(n, pref):
    t = min(n, pref)
    assert n % t == 0, (n, t)
    return t


def _layer(x, c, w_ada, b_ada, g_pre_mix, g_post_mix, g_pre_ffn, g_post_ffn, w_in, b_forget,
           w_conv, b_conv, w_rg, b_rg, w_ig, b_ig, lam, w_branch_attn, w_branch_rnn, w_out,
           w_router, router_bias, w_exp_gate, w_exp_up, w_exp_down, w_sh_gate, w_sh_up, w_sh_down):
    bsz, s, d = x.shape
    t = bsz * s
    row = lambda v: v.reshape(1, -1)

    mod = _ada_call(c, w_ada, b_ada)
    shift_m, scale_m, gate_m, shift_f, scale_f, gate_f = [
        mod[:, j * d:(j + 1) * d].reshape(bsz, 1, d) for j in range(6)]

    o_f = 3 * ATT_WIDTH
    o_r = o_f + ATT_HEADS
    wqkv = w_in[:, :o_f].astype(BF16)
    wf = jnp.pad(w_in[:, o_f:o_r], ((0, 0), (0, LANES - ATT_HEADS))).astype(BF16)
    wr = w_in[:, o_r:].astype(BF16)
    q, k, v, f_logit, x_rnn, g_rnn, gate_a, gate_r = _inproj_call(
        x, shift_m, scale_m, row(g_pre_mix), wqkv, wf, wr, _tile(s, 512))

    bf_pad = jnp.pad(b_forget, (0, LANES - ATT_HEADS)).reshape(1, LANES)
    kbias = _cum_call(f_logit, bf_pad, _tile(s, 256))
    tq = _tile(s, 512)
    y_attn = _attn_call(q, k, v, kbias, tq, _tile(tq, 256))

    wbd = jnp.concatenate([_block_diag_tiles(w_rg), _block_diag_tiles(w_ig)], axis=-1).astype(BF16)
    y_rnn = _rnn_call(x_rnn, g_rnn, w_conv, row(b_conv), wbd, row(b_rg), row(b_ig), row(lam), _tile(s, 256))

    wr_t = w_router.T
    wr_hi = wr_t.astype(BF16)
    wr_mid = (wr_t - wr_hi.astype(F32)).astype(BF16)
    tm_m = _tile(s, 512)
    tm_r = _tile(tm_m, 256)
    x1, h2, h2_rows, lpos, before, n_tile, wtok, counts = _merge_call(
        y_attn, y_rnn, gate_a, gate_r, x, gate_m, shift_f, scale_f, row(g_post_mix), row(g_pre_ffn),
        w_branch_attn.astype(BF16), w_branch_rnn.astype(BF16), w_out.astype(BF16),
        wr_hi, wr_mid, router_bias.reshape(N_EXPERTS, 1), tm_m, tm_r)

    rows = 512
    n_blocks = -(-(t * TOP_K + N_EXPERTS * WINDOW_ROWS) // rows) + N_EXPERTS
    cnt = counts[:, 0].astype(I32)
    padded = ((cnt + WINDOW_ROWS + rows - 1) // rows) * rows
    pend = jnp.cumsum(padded).astype(I32)
    pstart = pend - padded
    n_used = jnp.maximum(pend[-1] // rows, 1)
    blk = jnp.minimum(jnp.arange(n_blocks, dtype=I32), n_used - 1)
    block_e = jnp.minimum(jnp.sum((pend[None, :] <= (blk * rows)[:, None]).astype(I32), axis=1), N_EXPERTS - 1)

    n_win = (n_tile[:, :, 0] + WINDOW_ROWS - 1) // WINDOW_ROWS
    win_end = jnp.cumsum(n_win, axis=1)
    w_id = jnp.arange(WIN_LIST, dtype=I32)[None, :, None]
    owner = (jnp.sum((win_end[:, None, :] <= w_id).astype(I32), axis=2, keepdims=True)
             == jnp.arange(N_EXPERTS, dtype=I32)[None, None, :]).astype(I32)
    first_slot = pstart[None, :] + before[:, :, 0] - (win_end - n_win) * WINDOW_ROWS
    wlist = jnp.sum(owner * first_slot[:, None, :], axis=2) + w_id[:, :, 0] * WINDOW_ROWS
    wlist = jnp.where(w_id[:, :, 0] == WIN_LIST - 1, win_end[:, -1:], wlist).astype(I32)[:, None, :]
    n_rt = lpos.shape[0]
    lpos = lpos + (jnp.arange(n_rt, dtype=I32) % 2)[:, None, None] * _local_rows(tm_r)

    rows3 = lambda a: a.reshape(-1, PACK_ROWS, LANES)
    rows2 = lambda a: a.reshape(-1, LANES)
    xs = _dispatch_call(lpos, wlist, pstart, pend, rows3(h2_rows), n_blocks * rows, tm_r, rows)
    ys = rows3(_expert_call(block_e, n_used.reshape(1), rows2(xs),
                            w_exp_gate.astype(BF16), w_exp_up.astype(BF16), w_exp_down.astype(BF16), rows))
    return _final_call(lpos, wlist, x1, h2, wtok, gate_f, row(g_post_ffn),
                       w_sh_gate.astype(BF16), w_sh_up.astype(BF16), w_sh_down.astype(BF16), ys, tm_r)


def kernel(x, c, w_ada, b_ada, g_pre_mix, g_post_mix, g_pre_ffn, g_post_ffn, w_in, b_forget, w_conv, b_conv, w_rg, b_rg, w_ig, b_ig, rglru_lambda, w_branch_attn, w_branch_rnn, w_out, w_router, router_bias, w_exp_gate, w_exp_up, w_exp_down, w_sh_gate, w_sh_up, w_sh_down):
    depth = w_ada.shape[0]
    for l in range(depth):
        x = _layer(x, c, w_ada[l], b_ada[l], g_pre_mix[l], g_post_mix[l], g_pre_ffn[l], g_post_ffn[l],
                   w_in[l], b_forget[l], w_conv[l], b_conv[l], w_rg[l], b_rg[l], w_ig[l], b_ig[l],
                   rglru_lambda[l], w_branch_attn[l], w_branch_rnn[l], w_out[l], w_router[l],
                   router_bias[l], w_exp_gate[l], w_exp_up[l], w_exp_down[l],
                   w_sh_gate[l], w_sh_up[l], w_sh_down[l])
    return x
```

```python
import functools

import jax
import jax.numpy as jnp
from jax import lax
from jax.experimental import pallas as pl
from jax.experimental.pallas import tpu as pltpu

F32 = jnp.float32
BF16 = jnp.bfloat16
I32 = jnp.int32

D_MODEL = 1024
ATT_HEADS = 8
ATT_HEAD_DIM = 64
ATT_WIDTH = ATT_HEADS * ATT_HEAD_DIM
RNN_WIDTH = D_MODEL
RNN_BLOCKS = 16
RNN_BLOCK_DIM = RNN_WIDTH // RNN_BLOCKS
CONV_WIDTH = 4
RGLRU_C = 8.0
N_EXPERTS = 64
TOP_K = 6
N_GROUPS = 8
GROUP_SIZE = N_EXPERTS // N_GROUPS
TOPK_GROUPS = 4
EXPERT_FF = D_MODEL // 4
ROUTE_SCALE = 2.5
NORM_EPS = 1e-6

LANES = 128
SUBLANES = 8
MXU_DIM = 256
VMEM_LIMIT_BYTES = 56 * 1024 * 1024

NEG_BIG = -1e30
RANK_BITS = 20
RANK_MASK = (1 << RANK_BITS) - 1


def _sigmoid(x):
    return 0.5 * jnp.tanh(0.5 * x) + 0.5


def _rms_scale(x):
    return x * lax.rsqrt(jnp.mean(x * x, axis=-1, keepdims=True) + NORM_EPS)


def _dot(a, b):
    return jnp.dot(a, b, preferred_element_type=F32)


def _dot_nt(a, b):
    return lax.dot_general(a, b, (((1,), (1,)), ((), ())), preferred_element_type=F32)


def _split3(a):
    hi = a.astype(BF16)
    r1 = a - hi.astype(F32)
    mid = r1.astype(BF16)
    lo = (r1 - mid.astype(F32)).astype(BF16)
    return hi, mid, lo


def _params(*sem):
    return pltpu.CompilerParams(dimension_semantics=sem, vmem_limit_bytes=VMEM_LIMIT_BYTES)


def _const_spec(shape):
    nd = len(shape)
    return pl.BlockSpec(shape, lambda *_: (0,) * nd)


def _ada_kernel(c_ref, w_ref, b_ref, o_ref):
    c = c_ref[...]
    sc = c * _sigmoid(c)
    s_hi, s_mid, _ = _split3(sc)
    w_hi, w_mid, _ = _split3(w_ref[...])
    acc = _dot(s_hi, w_hi) + _dot(s_hi, w_mid) + _dot(s_mid, w_hi)
    o_ref[...] = acc + b_ref[...]


def _ada_call(c, w_ada, b_ada):
    bsz, d = c.shape
    n = w_ada.shape[1]
    tn = d
    return pl.pallas_call(
        _ada_kernel,
        grid=(n // tn,),
        in_specs=[
            pl.BlockSpec((bsz, d), lambda j: (0, 0)),
            pl.BlockSpec((d, tn), lambda j: (0, j)),
            pl.BlockSpec((1, tn), lambda j: (0, j)),
        ],
        out_specs=pl.BlockSpec((bsz, tn), lambda j: (0, j)),
        out_shape=jax.ShapeDtypeStruct((bsz, n), F32),
        compiler_params=_params("arbitrary"),
        name="ada_mod",
    )(c, w_ada, b_ada.reshape(1, n))


N_BIAS_PIECES = 3
CUM_BLOCK = 256


def _bias_placement():
    h = jnp.arange(LANES)[:, None]
    lane = jnp.arange(ATT_WIDTH)[None, :]
    mats = []
    for piece in range(N_BIAS_PIECES):
        target = (h // 2) * LANES + ATT_HEAD_DIM * (1 - h % 2) + piece
        mats.append(((lane == target) & (h < ATT_HEADS)).astype(BF16))
    return jnp.concatenate(mats, axis=0)


def _key_bias(f_logit, bf_ref, place_ref, kb_ref, cum_sc):
    tm = f_logit.shape[0]
    blk = min(tm, CUM_BLOCK)
    row = lax.broadcasted_iota(I32, (blk, blk), 0)
    col = lax.broadcasted_iota(I32, (blk, blk), 1)
    tri = (row >= col).astype(BF16)
    carry = cum_sc[...]
    for i in range(tm // blk):
        z = f_logit[i * blk:(i + 1) * blk, :] + bf_ref[...]
        lf = jnp.minimum(z, 0.0) - jnp.log1p(jnp.exp(-jnp.abs(z)))
        c3 = _dot(tri, jnp.concatenate(_split3(lf), axis=1))
        c = c3[:, :LANES] + c3[:, LANES:2 * LANES] + c3[:, 2 * LANES:] + carry
        carry = c[blk - 1:blk, :]
        kb = _dot(jnp.concatenate(_split3(-c), axis=1), place_ref[...])
        kb_ref[0, i * blk:(i + 1) * blk, :] = kb.astype(BF16)
    cum_sc[...] = carry


def _inproj_kernel(x_ref, sh_ref, sc_ref, g_ref, wqkv_ref, wf_ref, wr_ref, bf_ref, place_ref,
                   q_ref, k_ref, v_ref, kb_ref, xr_ref, gr_ref, ga_ref, gb_ref, cum_sc):
    @pl.when(pl.program_id(1) == 0)
    def _():
        cum_sc[...] = jnp.zeros(cum_sc.shape, F32)

    x = x_ref[0]
    h = (_rms_scale(x) * g_ref[...]) * (1.0 + sc_ref[0]) + sh_ref[0]
    hb = h.astype(BF16)
    for j, ref in enumerate((q_ref, k_ref, v_ref)):
        ref[0] = _dot(hb, wqkv_ref[:, j * ATT_WIDTH:(j + 1) * ATT_WIDTH]).astype(BF16)
    _key_bias(_dot(hb, wf_ref[...]), bf_ref, place_ref, kb_ref, cum_sc)
    half = D_MODEL // 2
    for j, ref in enumerate((xr_ref, gr_ref, ga_ref, gb_ref)):
        for c in range(2):
            lo = j * D_MODEL + c * half
            ref[0, :, c * half:(c + 1) * half] = _dot(hb, wr_ref[:, lo:lo + half]).astype(BF16)


def _inproj_call(x, shift, scale, g, wqkv, wf, wr, b_forget_pad, tm):
    bsz, s, d = x.shape
    nt = s // tm
    tok = lambda w: pl.BlockSpec((1, tm, w), lambda b, i: (b, i, 0))
    mod = pl.BlockSpec((1, 1, d), lambda b, i: (b, 0, 0))
    place = _bias_placement()
    out_w = (ATT_WIDTH,) * 4 + (d,) * 4
    return pl.pallas_call(
        _inproj_kernel,
        grid=(bsz, nt),
        in_specs=[tok(d), mod, mod, _const_spec((1, d)),
                  _const_spec(wqkv.shape), _const_spec(wf.shape), _const_spec(wr.shape),
                  _const_spec((1, LANES)), _const_spec(place.shape)],
        out_specs=[tok(w) for w in out_w],
        out_shape=[jax.ShapeDtypeStruct((bsz, s, w), BF16) for w in out_w],
        scratch_shapes=[pltpu.VMEM((1, LANES), F32)],
        compiler_params=_params("arbitrary", "arbitrary"),
        name="in_proj",
    )(x, shift, scale, g, wqkv, wf, wr, b_forget_pad, place)


ATT_VT_ROWS = ATT_HEAD_DIM + 16


def _attn_kernel(q_ref, k_ref, v_ref, kb_ref, o_ref, vt_sc, st_sc, m_sc, acc_sc, *, tq, tk):
    s = q_ref.shape[1]
    nq, nk, ratio = s // tq, s // tk, tq // tk
    lane = lax.broadcasted_iota(I32, (1, LANES), 1)
    head_lanes = (lane < ATT_HEAD_DIM, lane >= ATT_HEAD_DIM)
    one_lanes = ((lane >= ATT_HEAD_DIM) & (lane < ATT_HEAD_DIM + N_BIAS_PIECES), lane < N_BIAS_PIECES)
    key_minus_query = (lax.broadcasted_iota(I32, (tk, tq), 0) - lax.broadcasted_iota(I32, (tk, tq), 1))

    vt = v_ref[0].astype(F32).T
    ones = jnp.ones((ATT_VT_ROWS - ATT_HEAD_DIM, s), F32)
    for h in range(2):
        vth = jnp.concatenate([vt[h * ATT_HEAD_DIM:(h + 1) * ATT_HEAD_DIM, :], ones], axis=0).astype(BF16)
        for j in range(nk):
            vt_sc[h, j] = vth[:, j * tk:(j + 1) * tk]

    def q_body(qi, carry):
        q0 = pl.multiple_of(qi * tq, tq)
        qs = q_ref[0, pl.ds(q0, tq), :] * (ATT_HEAD_DIM ** -0.5)
        qa = [jnp.where(head_lanes[h], qs, one_lanes[h].astype(BF16)) for h in range(2)]
        m_sc[...] = jnp.full(m_sc.shape, NEG_BIG, F32)
        acc_sc[...] = jnp.zeros(acc_sc.shape, F32)

        def scores(kj):
            k0 = pl.multiple_of(kj * tk, tk)
            kb = k_ref[0, pl.ds(k0, tk), :]
            bias = kb_ref[0, pl.ds(k0, tk), :]
            return [_dot_nt(jnp.where(head_lanes[h], kb, bias), qa[h]) for h in range(2)]

        def accumulate(kj, masked):
            for h in range(2):
                st = st_sc[h]
                if masked:
                    st = jnp.where(key_minus_query <= q0 - kj * tk, st, NEG_BIG)
                m_old = m_sc[h]
                m_new = jnp.maximum(m_old, jnp.max(st, axis=0, keepdims=True))
                p = jnp.exp(st - m_new).astype(BF16)
                acc_sc[h] = jnp.exp(m_old - m_new) * acc_sc[h] + _dot(vt_sc[h, kj], p)
                m_sc[h] = m_new

        def put(sts):
            for h in range(2):
                st_sc[h] = sts[h]

        n_full = qi * ratio
        put(scores(0))

        def kv_body(kj, c2):
            nxt = scores(kj + 1)
            accumulate(kj, False)
            put(nxt)
            return c2

        lax.fori_loop(0, n_full, kv_body, 0)
        for d in range(ratio):
            nxt = scores(n_full + d + 1) if d + 1 < ratio else None
            accumulate(n_full + d, True)
            if nxt is not None:
                put(nxt)

        outs = []
        for h in range(2):
            acc = acc_sc[h]
            outs.append(acc[:ATT_HEAD_DIM, :] * (1.0 / acc[ATT_HEAD_DIM:ATT_HEAD_DIM + 1, :]))
        o_ref[0, pl.ds(q0, tq), :] = jnp.concatenate(outs, axis=0).T.astype(BF16)
        return carry

    lax.fori_loop(0, nq, q_body, 0)


def _attn_call(q, k, v, kbias, tq, tk):
    bsz, s, _ = q.shape
    qkv = pl.BlockSpec((1, s, LANES), lambda b, p: (b, 0, p))
    return pl.pallas_call(
        functools.partial(_attn_kernel, tq=tq, tk=tk),
        grid=(bsz, ATT_HEADS // 2),
        in_specs=[qkv, qkv, qkv, qkv],
        out_specs=qkv,
        out_shape=jax.ShapeDtypeStruct((bsz, s, ATT_WIDTH), BF16),
        scratch_shapes=[pltpu.VMEM((2, s // tk, ATT_VT_ROWS, tk), BF16), pltpu.VMEM((2, tk, tq), F32),
                        pltpu.VMEM((2, 1, tq), F32), pltpu.VMEM((2, ATT_VT_ROWS, tq), F32)],
        compiler_params=_params("arbitrary", "arbitrary"),
        name="fox_attention",
    )(q, k, v, kbias)


def _gelu_tanh(x):
    c = 0.7978845608028654
    return 0.5 * x * (1.0 + jnp.tanh(c * (x + 0.044715 * (x * x * x))))


def _rnn_kernel(xr_ref, gr_ref, wc_ref, bc_ref, wbd_ref, brg_ref, big_ref, lam_ref, o_ref,
                xpad_sc, hc_sc, a_sc, b_sc, h_sc, *, ts):
    @pl.when(pl.program_id(1) == 0)
    def _():
        xpad_sc[:SUBLANES, :] = jnp.zeros((SUBLANES, RNN_WIDTH), F32)
        hc_sc[...] = jnp.zeros(hc_sc.shape, F32)

    x = xr_ref[0].astype(F32)
    xpad_sc[SUBLANES:, :] = x
    u = bc_ref[...] + wc_ref[3:4, :] * x
    for j in range(CONV_WIDTH - 1):
        off = SUBLANES - (CONV_WIDTH - 1) + j
        u = u + wc_ref[j:j + 1, :] * xpad_sc[off:off + ts, :]
    xpad_sc[:SUBLANES, :] = x[ts - SUBLANES:ts, :]

    lam = lam_ref[...]
    neg_sp = -(jnp.maximum(-lam, 0.0) + jnp.log1p(jnp.exp(-jnp.abs(lam))))
    for jt in range(RNN_WIDTH // MXU_DIM):
        cs = slice(jt * MXU_DIM, (jt + 1) * MXU_DIM)
        uc = u[:, cs]
        g = _dot(uc.astype(BF16), wbd_ref[jt])
        r = _sigmoid(g[:, :MXU_DIM] + brg_ref[:, cs])
        ig = _sigmoid(g[:, MXU_DIM:] + big_ref[:, cs])
        a = jnp.exp(r * (RGLRU_C * neg_sp[:, cs]))
        a_sc[:, cs] = a
        v = 1.0 - a * a
        b_sc[:, cs] = jnp.where(v > 0.0, v * lax.rsqrt(v), 0.0) * (ig * uc)

    srow = lax.broadcasted_iota(I32, (SUBLANES, RNN_WIDTH), 0)

    def scan_body(i, carry):
        r0 = pl.multiple_of(i * SUBLANES, SUBLANES)
        a = a_sc[pl.ds(r0, SUBLANES), :]
        b = b_sc[pl.ds(r0, SUBLANES), :]
        b = b + jnp.where(srow == 0, a * carry, 0.0)
        for sh in (1, 2, 4):
            keep = srow >= sh
            b = jnp.where(keep, b + a * pltpu.roll(b, sh, 0), b)
            if sh < SUBLANES // 2:
                a = jnp.where(keep, a * pltpu.roll(a, sh, 0), a)
        h_sc[pl.ds(r0, SUBLANES), :] = b
        return jnp.broadcast_to(b[SUBLANES - 1:SUBLANES, :], (SUBLANES, RNN_WIDTH))

    hc_sc[...] = lax.fori_loop(0, ts // SUBLANES, scan_body, hc_sc[...])
    o_ref[0] = (h_sc[...] * _gelu_tanh(gr_ref[0].astype(F32))).astype(BF16)


def _rnn_call(x_rnn, g_rnn, w_conv, b_conv, wbd, b_rg, b_ig, lam, ts):
    bsz, s, w = x_rnn.shape
    tok = pl.BlockSpec((1, ts, w), lambda b, i: (b, i, 0))
    return pl.pallas_call(
        functools.partial(_rnn_kernel, ts=ts),
        grid=(bsz, s // ts),
        in_specs=[tok, tok, _const_spec(w_conv.shape), _const_spec((1, w)), _const_spec(wbd.shape),
                  _const_spec((1, w)), _const_spec((1, w)), _const_spec((1, w))],
        out_specs=tok,
        out_shape=jax.ShapeDtypeStruct((bsz, s, w), BF16),
        scratch_shapes=[pltpu.VMEM((ts + SUBLANES, w), F32), pltpu.VMEM((SUBLANES, w), F32),
                        pltpu.VMEM((ts, w), F32), pltpu.VMEM((ts, w), F32), pltpu.VMEM((ts, w), F32)],
        compiler_params=_params("arbitrary", "arbitrary"),
        name="rglru",
    )(x_rnn, g_rnn, w_conv, b_conv, wbd, b_rg, b_ig, lam)


WINDOW_SHIFT = 5
WINDOW_ROWS = 1 << WINDOW_SHIFT


def _route(h2, wr_hi_ref, wr_mid_ref, rbias_ref, cnt_ref, lpos_ref, before_ref, ntile_ref, wtok_ref, j, tm):
    h_hi, h_mid, _ = _split3(h2)
    logits = _dot_nt(wr_hi_ref[...], h_hi) + _dot_nt(wr_hi_ref[...], h_mid) + _dot_nt(wr_mid_ref[...], h_hi)
    scores = _sigmoid(logits)
    sel = scores + rbias_ref[...]

    giota = lax.broadcasted_iota(I32, (N_GROUPS, tm), 0)
    gs = jnp.zeros((N_GROUPS, tm), F32)
    for g in range(N_GROUPS):
        blk = sel[g * GROUP_SIZE:(g + 1) * GROUP_SIZE, :]
        m1 = jnp.max(blk, axis=0, keepdims=True)
        i1 = jnp.min(jnp.where(blk == m1, giota, N_EXPERTS), axis=0, keepdims=True)
        m2 = jnp.max(jnp.where(giota == i1, -jnp.inf, blk), axis=0, keepdims=True)
        gs = jnp.where(giota == g, m1 + m2, gs)

    gsel = jnp.zeros((N_GROUPS, tm), F32)
    for _ in range(TOPK_GROUPS):
        m = jnp.max(gs, axis=0, keepdims=True)
        idx = jnp.min(jnp.where(gs == m, giota, N_EXPERTS), axis=0, keepdims=True)
        hit = giota == idx
        gsel = jnp.where(hit, 1.0, gsel)
        gs = jnp.where(hit, -jnp.inf, gs)

    masked = jnp.concatenate(
        [jnp.where(gsel[g:g + 1, :] > 0.0, sel[g * GROUP_SIZE:(g + 1) * GROUP_SIZE, :], -jnp.inf)
         for g in range(N_GROUPS)], axis=0)

    eiota = lax.broadcasted_iota(I32, (N_EXPERTS, tm), 0)
    chosen = jnp.zeros((N_EXPERTS, tm), F32)
    idxs, wts = [], []
    for _ in range(TOP_K):
        m = jnp.max(masked, axis=0, keepdims=True)
        idx = jnp.min(jnp.where(masked == m, eiota, N_EXPERTS), axis=0, keepdims=True)
        hit = eiota == idx
        wts.append(jnp.sum(jnp.where(hit, scores, 0.0), axis=0, keepdims=True))
        idxs.append(idx)
        chosen = jnp.where(hit, 1.0, chosen)
        masked = jnp.where(hit, -jnp.inf, masked)
    wsum = wts[0]
    for w in wts[1:]:
        wsum = wsum + w

    srow = lax.broadcasted_iota(I32, (tm, tm), 0)
    scol = lax.broadcasted_iota(I32, (tm, tm), 1)
    earlier = (srow < scol).astype(BF16)
    local_rank = _dot(chosen.astype(BF16), earlier)
    n_e = jnp.sum(chosen, axis=1, keepdims=True)
    before_ref[j] = cnt_ref[...].astype(I32)
    ntile_ref[j] = n_e.astype(I32)
    cnt_ref[...] = cnt_ref[...] + n_e
    n_pad = (((n_e.astype(I32) + (WINDOW_ROWS - 1)) >> WINDOW_SHIFT) << WINDOW_SHIFT).astype(F32)
    erow = lax.broadcasted_iota(I32, (N_EXPERTS, N_EXPERTS), 0)
    ecol = lax.broadcasted_iota(I32, (N_EXPERTS, N_EXPERTS), 1)
    off = _dot((ecol < erow).astype(BF16), jnp.broadcast_to(n_pad, (N_EXPERTS, LANES)).astype(BF16))[:, :1]
    local_pos = off + local_rank

    riota = lax.broadcasted_iota(I32, (SUBLANES, tm), 0)
    lpos = jnp.zeros((SUBLANES, tm), I32)
    wrow = jnp.zeros((SUBLANES, tm), F32)
    for r in range(TOP_K):
        lp = jnp.sum(jnp.where(eiota == idxs[r], local_pos, 0.0), axis=0, keepdims=True)
        lpos = jnp.where(riota == r, lp.astype(I32), lpos)
        wrow = jnp.where(riota == r, wts[r] / wsum * ROUTE_SCALE, wrow)
    lpos_ref[j] = lpos
    wpad = jnp.concatenate([wrow, jnp.zeros((LANES - SUBLANES, tm), F32)], axis=0)
    wtok_ref[j * tm:(j + 1) * tm, :] = wpad.T


U32 = jnp.uint32
PACK_ROWS = D_MODEL // (2 * LANES)
HIGH_HALF = 0xFFFF0000


def _store_token_rows(rows_ref, val):
    half = D_MODEL // 2
    n = val.shape[0]
    bits = lax.bitcast_convert_type(val.astype(BF16).astype(F32), U32)
    words = (bits[:, half:] & jnp.uint32(HIGH_HALF)) | (bits[:, :half] >> 16)
    for c in range(PACK_ROWS):
        chunk = words[:, c * LANES:(c + 1) * LANES]
        if len(rows_ref.shape) == 2:
            rows_ref[pl.ds(c, n, stride=PACK_ROWS), :] = chunk
        else:
            rows_ref[:, c, :] = chunk


def _load_token_rows(rows_ref):
    if len(rows_ref.shape) == 2:
        n = rows_ref.shape[0] // PACK_ROWS
        chunks = [rows_ref[pl.ds(c, n, stride=PACK_ROWS), :] for c in range(PACK_ROWS)]
    else:
        chunks = [rows_ref[:, c, :] for c in range(PACK_ROWS)]
    words = jnp.concatenate(chunks, axis=-1)
    lo = lax.bitcast_convert_type(words << 16, F32)
    hi = lax.bitcast_convert_type(words & jnp.uint32(HIGH_HALF), F32)
    return jnp.concatenate([lo, hi], axis=-1)


def _merge_kernel(ya_ref, yr_ref, ga_ref, gb_ref, x_ref, gm_ref, shf_ref, scf_ref, gpost_ref, gpre_ref,
                  wba_ref, wbr_ref, wout_ref, wr_hi_ref, wr_mid_ref, rbias_ref,
                  x1_ref, h2_ref, h2rows_ref, lpos_ref, before_ref, ntile_ref, wtok_ref, cnt_ref, *, tm, tr):
    @pl.when((pl.program_id(0) == 0) & (pl.program_id(1) == 0))
    def _():
        cnt_ref[...] = jnp.zeros(cnt_ref.shape, F32)

    pa = _dot(ya_ref[0], wba_ref[...])
    pr = _dot(yr_ref[0], wbr_ref[...])
    merged = _sigmoid(ga_ref[0].astype(F32)) * pa + _sigmoid(gb_ref[0].astype(F32)) * pr
    o = _dot(merged.astype(BF16), wout_ref[...])
    x1 = x_ref[0] + gm_ref[0] * (_rms_scale(o) * gpost_ref[...])
    x1_ref[0] = x1
    h2 = (_rms_scale(x1) * gpre_ref[...]) * (1.0 + scf_ref[0]) + shf_ref[0]
    h2_ref[0] = h2.astype(BF16)
    _store_token_rows(h2rows_ref, h2)
    for j in range(tm // tr):
        _route(h2[j * tr:(j + 1) * tr, :], wr_hi_ref, wr_mid_ref, rbias_ref, cnt_ref,
               lpos_ref, before_ref, ntile_ref, wtok_ref, j, tr)


def _merge_call(y_attn, y_rnn, gate_a, gate_r, x, gate_m, shift_f, scale_f, g_post, g_pre,
                wba, wbr, wout, wr_hi, wr_mid, rbias, tm, tr):
    bsz, s, d = x.shape
    nt = s // tm
    nr = tm // tr
    tok = lambda w: pl.BlockSpec((1, tm, w), lambda b, i: (b, i, 0))
    mod = pl.BlockSpec((1, 1, d), lambda b, i: (b, 0, 0))
    return pl.pallas_call(
        functools.partial(_merge_kernel, tm=tm, tr=tr),
        grid=(bsz, nt),
        in_specs=[tok(ATT_WIDTH), tok(d), tok(d), tok(d), tok(d), mod, mod, mod,
                  _const_spec((1, d)), _const_spec((1, d)),
                  _const_spec(wba.shape), _const_spec(wbr.shape), _const_spec(wout.shape),
                  _const_spec(wr_hi.shape), _const_spec(wr_mid.shape), _const_spec(rbias.shape)],
        out_specs=[tok(d), tok(d),
                   pl.BlockSpec((tm * PACK_ROWS, LANES), lambda b, i: (b * nt + i, 0)),
                   pl.BlockSpec((nr, SUBLANES, tr), lambda b, i: (b * nt + i, 0, 0)),
                   pl.BlockSpec((nr, N_EXPERTS, 1), lambda b, i: (b * nt + i, 0, 0)),
                   pl.BlockSpec((nr, N_EXPERTS, 1), lambda b, i: (b * nt + i, 0, 0)),
                   pl.BlockSpec((tm, LANES), lambda b, i: (b * nt + i, 0)),
                   pl.BlockSpec((N_EXPERTS, 1), lambda b, i: (0, 0))],
        out_shape=[jax.ShapeDtypeStruct((bsz, s, d), F32), jax.ShapeDtypeStruct((bsz, s, d), BF16),
                   jax.ShapeDtypeStruct((bsz * s * PACK_ROWS, LANES), U32),
                   jax.ShapeDtypeStruct((bsz * nt * nr, SUBLANES, tr), I32),
                   jax.ShapeDtypeStruct((bsz * nt * nr, N_EXPERTS, 1), I32),
                   jax.ShapeDtypeStruct((bsz * nt * nr, N_EXPERTS, 1), I32),
                   jax.ShapeDtypeStruct((bsz * s, LANES), F32),
                   jax.ShapeDtypeStruct((N_EXPERTS, 1), F32)],
        compiler_params=_params("arbitrary", "arbitrary"),
        name="merge_route",
    )(y_attn, y_rnn, gate_a, gate_r, x, gate_m, shift_f, scale_f, g_post, g_pre,
      wba, wbr, wout, wr_hi, wr_mid, rbias)


WIN_LIST = LANES
ROW_COPY_UNROLL = 8


def _local_rows(tm):
    rows = tm * TOP_K + N_EXPERTS * WINDOW_ROWS
    assert rows // WINDOW_ROWS < WIN_LIST
    return rows


def _for_each_window(wlist_ref, fn):
    def body(w, c):
        fn(pl.multiple_of(w * WINDOW_ROWS, WINDOW_ROWS), wlist_ref[0, 0, w])
        return c

    lax.fori_loop(0, wlist_ref[0, 0, WIN_LIST - 1], body, 0)


def _dispatch_kernel(lpos_ref, wlist_ref, wlistp_ref, pstart_ref, pend_ref, h_ref, xs_hbm, loc_sc, zero_sc, sem,
                     *, tm, rows, n_blocks):
    g = pl.program_id(0)
    cur = lax.rem(g, 2)

    @pl.when(g == 0)
    def _():
        loc_sc[...] = jnp.zeros(loc_sc.shape, U32)
        zero_sc[...] = jnp.zeros(zero_sc.shape, U32)
        n_used = pend_ref[N_EXPERTS - 1] // rows

        def zero_block(b):
            return pltpu.make_async_copy(zero_sc, xs_hbm.at[pl.ds(b * rows, rows)], sem)

        def region_tail(e, fn):
            last = pend_ref[e] // rows - 1
            for back in range(2):
                @pl.when(last - back >= pstart_ref[e] // rows)
                def _():
                    fn(zero_block(last - back))

        def zstart(e, c):
            region_tail(e, lambda cp: cp.start())
            return c

        def zwait(e, c):
            region_tail(e, lambda cp: cp.wait())
            return c

        def tstart(b, c):
            zero_block(b).start()
            return c

        def twait(b, c):
            zero_block(0).wait()
            return c

        lax.fori_loop(0, N_EXPERTS, zstart, 0)
        lax.fori_loop(n_used, n_blocks, tstart, 0)
        lax.fori_loop(0, N_EXPERTS, zwait, 0)
        lax.fori_loop(n_used, n_blocks, twait, 0)

    for t in range(tm):
        row = h_ref[t]
        for k in range(TOP_K):
            loc_sc[lpos_ref[0, k, t]] = row

    half = _local_rows(tm)

    def window(base):
        return lambda row0, slot0: pltpu.make_async_copy(
            loc_sc.at[pl.ds(pl.multiple_of(base + row0, WINDOW_ROWS), WINDOW_ROWS)],
            xs_hbm.at[pl.ds(slot0, WINDOW_ROWS)], sem)

    @pl.when(g > 0)
    def _():
        _for_each_window(wlistp_ref, lambda r, s: window((1 - cur) * half)(r, s).wait())

    _for_each_window(wlist_ref, lambda r, s: window(cur * half)(r, s).start())

    @pl.when(g == pl.num_programs(0) - 1)
    def _():
        _for_each_window(wlist_ref, lambda r, s: window(cur * half)(r, s).wait())


def _dispatch_call(lpos, wlist, pstart, pend, h2_rows, n_rows, tm, rows):
    t = h2_rows.shape[0]
    smem = functools.partial(pl.BlockSpec, memory_space=pltpu.SMEM)
    return pl.pallas_call(
        functools.partial(_dispatch_kernel, tm=tm, rows=rows, n_blocks=n_rows // rows),
        grid=(t // tm,),
        in_specs=[smem((1, SUBLANES, tm), lambda i: (i, 0, 0)),
                  smem((1, 1, WIN_LIST), lambda i: (i, 0, 0)),
                  smem((1, 1, WIN_LIST), lambda i: (jnp.maximum(i - 1, 0), 0, 0)),
                  smem(), smem(),
                  pl.BlockSpec((tm, PACK_ROWS, LANES), lambda i: (i, 0, 0))],
        out_specs=pl.BlockSpec(memory_space=pl.ANY),
        out_shape=jax.ShapeDtypeStruct((n_rows, PACK_ROWS, LANES), U32),
        scratch_shapes=[pltpu.VMEM((2 * _local_rows(tm), PACK_ROWS, LANES), U32),
                        pltpu.VMEM((rows, PACK_ROWS, LANES), U32), pltpu.SemaphoreType.DMA],
        compiler_params=_params("arbitrary"),
        name="moe_dispatch",
    )(lpos, wlist, wlist, pstart, pend, h2_rows)


def _expert_kernel(be_ref, nused_ref, xs_ref, wg_ref, wu_ref, wd_ref, ys_ref, *, rows):
    @pl.when(pl.program_id(0) < nused_ref[0])
    def _():
        xb = _load_token_rows(xs_ref).astype(BF16)
        g = _dot(xb, wg_ref[0])
        u = _dot(xb, wu_ref[0])
        act = (g * _sigmoid(g)) * u
        _store_token_rows(ys_ref, _dot(act.astype(BF16), wd_ref[0]))

    @pl.when(pl.program_id(0) >= nused_ref[0])
    def _():
        ys_ref[...] = jnp.zeros(ys_ref.shape, U32)


def _expert_call(block_e, n_used, xs, wg, wu, wd, rows):
    n_rows = xs.shape[0] // PACK_ROWS
    _, d, ff = wg.shape
    row_spec = pl.BlockSpec((rows * PACK_ROWS, LANES), lambda i, be, nu: (jnp.minimum(i, nu[0] - 1), 0))
    grid_spec = pltpu.PrefetchScalarGridSpec(
        num_scalar_prefetch=2,
        grid=(n_rows // rows,),
        in_specs=[row_spec,
                  pl.BlockSpec((1, d, ff), lambda i, be, nu: (be[i], 0, 0)),
                  pl.BlockSpec((1, d, ff), lambda i, be, nu: (be[i], 0, 0)),
                  pl.BlockSpec((1, ff, d), lambda i, be, nu: (be[i], 0, 0))],
        out_specs=pl.BlockSpec((rows * PACK_ROWS, LANES), lambda i, be, nu: (i, 0)),
    )
    return pl.pallas_call(
        functools.partial(_expert_kernel, rows=rows),
        grid_spec=grid_spec,
        out_shape=jax.ShapeDtypeStruct(xs.shape, U32),
        compiler_params=_params("arbitrary"),
        name="moe_experts",
    )(block_e, n_used, xs, wg, wu, wd)


def _final_kernel(lpos0_ref, lposn_ref, tab0_ref, tabw_ref, tabi_ref, x1_ref, h2_ref, wtok_ref, gf_ref, gpost_ref,
                  wsg_ref, wsu_ref, wsd_ref, ys_hbm, o_ref, win_sc, gnext_sc, gcur_sc, sem, *, tm):
    nt = pl.num_programs(1)
    g = pl.program_id(0) * nt + pl.program_id(1)
    n_tiles = pl.num_programs(0) * nt
    buf_next = lax.rem(g + 1, 2)
    buf_after = lax.rem(g, 2)
    half = _local_rows(tm)

    def window(buf):
        return lambda row0, slot0: pltpu.make_async_copy(
            ys_hbm.at[pl.ds(slot0, WINDOW_ROWS)],
            win_sc.at[pl.ds(pl.multiple_of(buf * half + row0, WINDOW_ROWS), WINDOW_ROWS)], sem.at[buf])

    def start_windows(tab_ref, buf):
        _for_each_window(tab_ref, lambda r, s: window(buf)(r, s).start())

    def wait_windows(tab_ref, buf):
        _for_each_window(tab_ref, lambda r, s: window(buf)(r, s).wait())

    def pick_rows(lpos_ref, t):
        for k in range(TOP_K):
            gnext_sc[k, t] = win_sc[lpos_ref[0, k, t]]

    def stage():
        for k in range(TOP_K):
            gcur_sc[k] = gnext_sc[k].reshape(tm * PACK_ROWS, LANES)

    @pl.when(g == 0)
    def _():
        start_windows(tab0_ref, 0)
        wait_windows(tab0_ref, 0)

        def body(i, c):
            for u in range(ROW_COPY_UNROLL):
                pick_rows(lpos0_ref, i * ROW_COPY_UNROLL + u)
            return c

        lax.fori_loop(0, tm // ROW_COPY_UNROLL, body, 0)
        stage()

        @pl.when(n_tiles > 1)
        def _():
            start_windows(tabw_ref, 1)

    @pl.when(g + 1 < n_tiles)
    def _():
        wait_windows(tabw_ref, buf_next)

    @pl.when(g + 2 < n_tiles)
    def _():
        start_windows(tabi_ref, buf_after)

    for t in range(tm):
        pick_rows(lposn_ref, t)

    hb = h2_ref[0]
    a = _dot(hb, wsg_ref[...])
    u = _dot(hb, wsu_ref[...])
    tot = _dot(((a * _sigmoid(a)) * u).astype(BF16), wsd_ref[...])

    w = wtok_ref[...]
    routed = w[:, 0:1] * _load_token_rows(gcur_sc.at[0])
    for k in range(1, TOP_K):
        routed = routed + w[:, k:k + 1] * _load_token_rows(gcur_sc.at[k])
    tot = routed + tot
    o_ref[0] = x1_ref[0] + gf_ref[0] * (_rms_scale(tot) * gpost_ref[...])

    stage()


def _final_call(lpos, tab, x1, h2, wtok, gate_f, g_post, wsg, wsu, wsd, ys, tm):
    bsz, s, d = x1.shape
    nt = s // tm
    n_tiles = bsz * nt
    smem = functools.partial(pl.BlockSpec, memory_space=pltpu.SMEM)
    tok = pl.BlockSpec((1, tm, d), lambda b, i: (b, i, 0))
    tab_shape = (1, 1, WIN_LIST)
    ahead = lambda k: (lambda b, i: (jnp.minimum(b * nt + i + k, n_tiles - 1), 0, 0))
    return pl.pallas_call(
        functools.partial(_final_kernel, tm=tm),
        grid=(bsz, nt),
        in_specs=[smem((1, SUBLANES, tm), lambda b, i: (0, 0, 0)),
                  smem((1, SUBLANES, tm), ahead(1)),
                  smem(tab_shape, lambda b, i: (0, 0, 0)),
                  smem(tab_shape, ahead(1)),
                  smem(tab_shape, ahead(2)),
                  tok, tok,
                  pl.BlockSpec((tm, LANES), lambda b, i: (b * nt + i, 0)),
                  pl.BlockSpec((1, 1, d), lambda b, i: (b, 0, 0)),
                  _const_spec((1, d)),
                  _const_spec(wsg.shape), _const_spec(wsu.shape), _const_spec(wsd.shape),
                  pl.BlockSpec(memory_space=pl.ANY)],
        out_specs=tok,
        out_shape=jax.ShapeDtypeStruct((bsz, s, d), F32),
        scratch_shapes=[pltpu.VMEM((2 * _local_rows(tm), PACK_ROWS, LANES), U32),
                        pltpu.VMEM((TOP_K, tm, PACK_ROWS, LANES), U32),
                        pltpu.VMEM((TOP_K, tm * PACK_ROWS, LANES), U32), pltpu.SemaphoreType.DMA((2,))],
        compiler_params=_params("arbitrary", "arbitrary"),
        name="moe_combine",
    )(lpos, lpos, tab, tab, tab, x1, h2, wtok, gate_f, g_post, wsg, wsu, wsd, ys)


def _block_diag_tiles(w):
    per = MXU_DIM // RNN_BLOCK_DIM
    nt = RNN_BLOCKS // per
    w4 = w.reshape(nt, per, RNN_BLOCK_DIM, RNN_BLOCK_DIM)
    eye = jnp.eye(per, dtype=w.dtype)
    return jnp.einsum("tnij,nm->tnimj", w4, eye).reshape(nt, MXU_DIM, MXU_DIM)


def _tile(n, pref):
    t = min(n, pref)
    assert n % t == 0, (n, t)
    return t


def _layer(x, c, w_ada, b_ada, g_pre_mix, g_post_mix, g_pre_ffn, g_post_ffn, w_in, b_forget,
           w_conv, b_conv, w_rg, b_rg, w_ig, b_ig, lam, w_branch_attn, w_branch_rnn, w_out,
           w_router, router_bias, w_exp_gate, w_exp_up, w_exp_down, w_sh_gate, w_sh_up, w_sh_down):
    bsz, s, d = x.shape
    t = bsz * s
    row = lambda v: v.reshape(1, -1)

    mod = _ada_call(c, w_ada, b_ada)
    shift_m, scale_m, gate_m, shift_f, scale_f, gate_f = [
        mod[:, j * d:(j + 1) * d].reshape(bsz, 1, d) for j in range(6)]

    o_f = 3 * ATT_WIDTH
    o_r = o_f + ATT_HEADS
    wqkv = w_in[:, :o_f].astype(BF16)
    wf = jnp.pad(w_in[:, o_f:o_r], ((0, 0), (0, LANES - ATT_HEADS))).astype(BF16)
    wr = w_in[:, o_r:].astype(BF16)
    bf_pad = jnp.pad(b_forget, (0, LANES - ATT_HEADS)).reshape(1, LANES)
    q, k, v, kbias, x_rnn, g_rnn, gate_a, gate_r = _inproj_call(
        x, shift_m, scale_m, row(g_pre_mix), wqkv, wf, wr, bf_pad, _tile(s, 512))
    tq = _tile(s, 512)
    y_attn = _attn_call(q, k, v, kbias, tq, _tile(tq, 256))

    wbd = jnp.concatenate([_block_diag_tiles(w_rg), _block_diag_tiles(w_ig)], axis=-1).astype(BF16)
    y_rnn = _rnn_call(x_rnn, g_rnn, w_conv, row(b_conv), wbd, row(b_rg), row(b_ig), row(lam), _tile(s, 256))

    wr_t = w_router.T
    wr_hi = wr_t.astype(BF16)
    wr_mid = (wr_t - wr_hi.astype(F32)).astype(BF16)
    tm_m = _tile(s, 512)
    tm_r = _tile(tm_m, 256)
    x1, h2, h2_rows, lpos, before, n_tile, wtok, counts = _merge_call(
        y_attn, y_rnn, gate_a, gate_r, x, gate_m, shift_f, scale_f, row(g_post_mix), row(g_pre_ffn),
        w_branch_attn.astype(BF16), w_branch_rnn.astype(BF16), w_out.astype(BF16),
        wr_hi, wr_mid, router_bias.reshape(N_EXPERTS, 1), tm_m, tm_r)

    rows = 1024
    n_blocks =-(-(t * TOP_K + N_EXPERTS * WINDOW_ROWS) // rows) + N_EXPERTS
    cnt = counts[:, 0].astype(I32)
    padded = ((cnt + WINDOW_ROWS + rows - 1) // rows) * rows
    pend = jnp.cumsum(padded).astype(I32)
    pstart = pend - padded
    n_used = jnp.maximum(pend[-1] // rows, 1)
    blk = jnp.minimum(jnp.arange(n_blocks, dtype=I32), n_used - 1)
    block_e = jnp.minimum(jnp.sum((pend[None, :] <= (blk * rows)[:, None]).astype(I32), axis=1), N_EXPERTS - 1)

    n_win = (n_tile[:, :, 0] + WINDOW_ROWS - 1) // WINDOW_ROWS
    win_end = jnp.cumsum(n_win, axis=1)
    w_id = jnp.arange(WIN_LIST, dtype=I32)[None, :, None]
    owner = (jnp.sum((win_end[:, None, :] <= w_id).astype(I32), axis=2, keepdims=True)
             == jnp.arange(N_EXPERTS, dtype=I32)[None, None, :]).astype(I32)
    first_slot = pstart[None, :] + before[:, :, 0] - (win_end - n_win) * WINDOW_ROWS
    wlist = jnp.sum(owner * first_slot[:, None, :], axis=2) + w_id[:, :, 0] * WINDOW_ROWS
    wlist = jnp.where(w_id[:, :, 0] == WIN_LIST - 1, win_end[:, -1:], wlist).astype(I32)[:, None, :]
    n_rt = lpos.shape[0]
    lpos = lpos + (jnp.arange(n_rt, dtype=I32) % 2)[:, None, None] * _local_rows(tm_r)

    rows3 = lambda a: a.reshape(-1, PACK_ROWS, LANES)
    rows2 = lambda a: a.reshape(-1, LANES)
    xs = _dispatch_call(lpos, wlist, pstart, pend, rows3(h2_rows), n_blocks * rows, tm_r, rows)
    ys = rows3(_expert_call(block_e, n_used.reshape(1), rows2(xs),
                            w_exp_gate.astype(BF16), w_exp_up.astype(BF16), w_exp_down.astype(BF16), rows))
    return _final_call(lpos, wlist, x1, h2, wtok, gate_f, row(g_post_ffn),
                       w_sh_gate.astype(BF16), w_sh_up.astype(BF16), w_sh_down.astype(BF16), ys, tm_r)


def kernel(x, c, w_ada, b_ada, g_pre_mix, g_post_mix, g_pre_ffn, g_post_ffn, w_in, b_forget, w_conv, b_conv, w_rg, b_rg, w_ig, b_ig, rglru_lambda, w_branch_attn, w_branch_rnn, w_out, w_router, router_bias, w_exp_gate, w_exp_up, w_exp_down, w_sh_gate, w_sh_up, w_sh_down):
    depth = w_ada.shape[0]
    for l in range(depth):
        x = _layer(x, c, w_ada[l], b_ada[l], g_pre_mix[l], g_post_mix[l], g_pre_ffn[l], g_post_ffn[l],
                   w_in[l], b_forget[l], w_conv[l], b_conv[l], w_rg[l], b_rg[l], w_ig[l], b_ig[l],
                   rglru_lambda[l], w_branch_attn[l], w_branch_rnn[l], w_out[l], w_router[l],
                   router_bias[l], w_exp_gate[l], w_exp_up[l], w_exp_down[l],
                   w_sh_gate[l], w_sh_up[l], w_sh_down[l])
    return x
```

```python
import functools

import jax
import jax.numpy as jnp
from jax import lax
from jax.experimental import pallas as pl
from jax.experimental.pallas import tpu as pltpu

F32 = jnp.float32
BF16 = jnp.bfloat16
I32 = jnp.int32

D_MODEL = 1024
ATT_HEADS = 8
ATT_HEAD_DIM = 64
ATT_WIDTH = ATT_HEADS * ATT_HEAD_DIM
RNN_WIDTH = D_MODEL
RNN_BLOCKS = 16
RNN_BLOCK_DIM = RNN_WIDTH // RNN_BLOCKS
CONV_WIDTH = 4
RGLRU_C = 8.0
N_EXPERTS = 64
TOP_K = 6
N_GROUPS = 8
GROUP_SIZE = N_EXPERTS // N_GROUPS
TOPK_GROUPS = 4
EXPERT_FF = D_MODEL // 4
ROUTE_SCALE = 2.5
NORM_EPS = 1e-6

LANES = 128
SUBLANES = 8
MXU_DIM = 256
VMEM_LIMIT_BYTES = 56 * 1024 * 1024

NEG_BIG = -1e30
RANK_BITS = 20
RANK_MASK = (1 << RANK_BITS) - 1


def _sigmoid(x):
    return 0.5 * jnp.tanh(0.5 * x) + 0.5


def _rms_scale(x):
    return x * lax.rsqrt(jnp.mean(x * x, axis=-1, keepdims=True) + NORM_EPS)


def _dot(a, b):
    return jnp.dot(a, b, preferred_element_type=F32)


def _dot_nt(a, b):
    return lax.dot_general(a, b, (((1,), (1,)), ((), ())), preferred_element_type=F32)


def _split3(a):
    hi = a.astype(BF16)
    r1 = a - hi.astype(F32)
    mid = r1.astype(BF16)
    lo = (r1 - mid.astype(F32)).astype(BF16)
    return hi, mid, lo


def _params(*sem):
    return pltpu.CompilerParams(dimension_semantics=sem, vmem_limit_bytes=VMEM_LIMIT_BYTES)


def _const_spec(shape):
    nd = len(shape)
    return pl.BlockSpec(shape, lambda *_: (0,) * nd)


def _ada_kernel(c_ref, w_ref, b_ref, o_ref):
    c = c_ref[...]
    sc = c * _sigmoid(c)
    s_hi, s_mid, _ = _split3(sc)
    w_hi, w_mid, _ = _split3(w_ref[...])
    acc = _dot(s_hi, w_hi) + _dot(s_hi, w_mid) + _dot(s_mid, w_hi)
    o_ref[...] = acc + b_ref[...]


def _ada_call(c, w_ada, b_ada):
    bsz, d = c.shape
    n = w_ada.shape[1]
    tn = d
    return pl.pallas_call(
        _ada_kernel,
        grid=(n // tn,),
        in_specs=[
            pl.BlockSpec((bsz, d), lambda j: (0, 0)),
            pl.BlockSpec((d, tn), lambda j: (0, j)),
            pl.BlockSpec((1, tn), lambda j: (0, j)),
        ],
        out_specs=pl.BlockSpec((bsz, tn), lambda j: (0, j)),
        out_shape=jax.ShapeDtypeStruct((bsz, n), F32),
        compiler_params=_params("arbitrary"),
        name="ada_mod",
    )(c, w_ada, b_ada.reshape(1, n))


N_BIAS_PIECES = 3
CUM_BLOCK = 256


def _bias_placement():
    h = jnp.arange(LANES)[:, None]
    lane = jnp.arange(ATT_WIDTH)[None, :]
    mats = []
    for piece in range(N_BIAS_PIECES):
        target = (h // 2) * LANES + ATT_HEAD_DIM * (1 - h % 2) + piece
        mats.append(((lane == target) & (h < ATT_HEADS)).astype(BF16))
    return jnp.concatenate(mats, axis=0)


def _key_bias(f_logit, bf_ref, place_ref, kb_ref, cum_sc):
    tm = f_logit.shape[0]
    blk = min(tm, CUM_BLOCK)
    row = lax.broadcasted_iota(I32, (blk, blk), 0)
    col = lax.broadcasted_iota(I32, (blk, blk), 1)
    tri = (row >= col).astype(BF16)
    carry = cum_sc[...]
    for i in range(tm // blk):
        z = f_logit[i * blk:(i + 1) * blk, :] + bf_ref[...]
        lf = jnp.minimum(z, 0.0) - jnp.log1p(jnp.exp(-jnp.abs(z)))
        c3 = _dot(tri, jnp.concatenate(_split3(lf), axis=1))
        c = c3[:, :LANES] + c3[:, LANES:2 * LANES] + c3[:, 2 * LANES:] + carry
        carry = c[blk - 1:blk, :]
        kb = _dot(jnp.concatenate(_split3(-c), axis=1), place_ref[...])
        kb_ref[0, i * blk:(i + 1) * blk, :] = kb.astype(BF16)
    cum_sc[...] = carry


def _inproj_kernel(x_ref, sh_ref, sc_ref, g_ref, wqkv_ref, wf_ref, wr_ref, bf_ref, place_ref,
                   q_ref, k_ref, v_ref, kb_ref, xr_ref, gr_ref, ga_ref, gb_ref, cum_sc):
    @pl.when(pl.program_id(1) == 0)
    def _():
        cum_sc[...] = jnp.zeros(cum_sc.shape, F32)

    x = x_ref[0]
    h = (_rms_scale(x) * g_ref[...]) * (1.0 + sc_ref[0]) + sh_ref[0]
    hb = h.astype(BF16)
    for j, ref in enumerate((q_ref, k_ref, v_ref)):
        ref[0] = _dot(hb, wqkv_ref[:, j * ATT_WIDTH:(j + 1) * ATT_WIDTH]).astype(BF16)
    _key_bias(_dot(hb, wf_ref[...]), bf_ref, place_ref, kb_ref, cum_sc)
    half = D_MODEL // 2
    for j, ref in enumerate((xr_ref, gr_ref, ga_ref, gb_ref)):
        for c in range(2):
            lo = j * D_MODEL + c * half
            ref[0, :, c * half:(c + 1) * half] = _dot(hb, wr_ref[:, lo:lo + half]).astype(BF16)


def _inproj_call(x, shift, scale, g, wqkv, wf, wr, b_forget_pad, tm):
    bsz, s, d = x.shape
    nt = s // tm
    tok = lambda w: pl.BlockSpec((1, tm, w), lambda b, i: (b, i, 0))
    mod = pl.BlockSpec((1, 1, d), lambda b, i: (b, 0, 0))
    place = _bias_placement()
    out_w = (ATT_WIDTH,) * 4 + (d,) * 4
    return pl.pallas_call(
        _inproj_kernel,
        grid=(bsz, nt),
        in_specs=[tok(d), mod, mod, _const_spec((1, d)),
                  _const_spec(wqkv.shape), _const_spec(wf.shape), _const_spec(wr.shape),
                  _const_spec((1, LANES)), _const_spec(place.shape)],
        out_specs=[tok(w) for w in out_w],
        out_shape=[jax.ShapeDtypeStruct((bsz, s, w), BF16) for w in out_w],
        scratch_shapes=[pltpu.VMEM((1, LANES), F32)],
        compiler_params=_params("arbitrary", "arbitrary"),
        name="in_proj",
    )(x, shift, scale, g, wqkv, wf, wr, b_forget_pad, place)


ATT_VT_ROWS = ATT_HEAD_DIM + 16


def _attn_kernel(q_ref, k_ref, v_ref, kb_ref, o_ref, vt_sc, st_sc, m_sc, acc_sc, *, tq, tk):
    s = q_ref.shape[1]
    nq, nk, ratio = s // tq, s // tk, tq // tk
    lane = lax.broadcasted_iota(I32, (1, LANES), 1)
    head_lanes = (lane < ATT_HEAD_DIM, lane >= ATT_HEAD_DIM)
    one_lanes = ((lane >= ATT_HEAD_DIM) & (lane < ATT_HEAD_DIM + N_BIAS_PIECES), lane < N_BIAS_PIECES)
    key_minus_query = (lax.broadcasted_iota(I32, (tk, tq), 0) - lax.broadcasted_iota(I32, (tk, tq), 1))

    vt = v_ref[0].astype(F32).T
    ones = jnp.ones((ATT_VT_ROWS - ATT_HEAD_DIM, s), F32)
    for h in range(2):
        vth = jnp.concatenate([vt[h * ATT_HEAD_DIM:(h + 1) * ATT_HEAD_DIM, :], ones], axis=0).astype(BF16)
        for j in range(nk):
            vt_sc[h, j] = vth[:, j * tk:(j + 1) * tk]

    def q_body(qi, carry):
        q0 = pl.multiple_of(qi * tq, tq)
        qs = q_ref[0, pl.ds(q0, tq), :] * (ATT_HEAD_DIM ** -0.5)
        qa = [jnp.where(head_lanes[h], qs, one_lanes[h].astype(BF16)) for h in range(2)]
        m_sc[...] = jnp.full(m_sc.shape, NEG_BIG, F32)
        acc_sc[...] = jnp.zeros(acc_sc.shape, F32)

        def scores(kj):
            k0 = pl.multiple_of(kj * tk, tk)
            kb = k_ref[0, pl.ds(k0, tk), :]
            bias = kb_ref[0, pl.ds(k0, tk), :]
            return [_dot_nt(jnp.where(head_lanes[h], kb, bias), qa[h]) for h in range(2)]

        def accumulate(kj, masked):
            for h in range(2):
                st = st_sc[h]
                if masked:
                    st = jnp.where(key_minus_query <= q0 - kj * tk, st, NEG_BIG)
                m_old = m_sc[h]
                m_new = jnp.maximum(m_old, jnp.max(st, axis=0, keepdims=True))
                p = jnp.exp(st - m_new).astype(BF16)
                acc_sc[h] = jnp.exp(m_old - m_new) * acc_sc[h] + _dot(vt_sc[h, kj], p)
                m_sc[h] = m_new

        def put(sts):
            for h in range(2):
                st_sc[h] = sts[h]

        n_full = qi * ratio
        put(scores(0))

        def kv_body(kj, c2):
            nxt = scores(kj + 1)
            accumulate(kj, False)
            put(nxt)
            return c2

        lax.fori_loop(0, n_full, kv_body, 0)
        for d in range(ratio):
            nxt = scores(n_full + d + 1) if d + 1 < ratio else None
            accumulate(n_full + d, True)
            if nxt is not None:
                put(nxt)

        outs = []
        for h in range(2):
            acc = acc_sc[h]
            outs.append(acc[:ATT_HEAD_DIM, :] * (1.0 / acc[ATT_HEAD_DIM:ATT_HEAD_DIM + 1, :]))
        o_ref[0, pl.ds(q0, tq), :] = jnp.concatenate(outs, axis=0).T.astype(BF16)
        return carry

    lax.fori_loop(0, nq, q_body, 0)


def _attn_call(q, k, v, kbias, tq, tk):
    bsz, s, _ = q.shape
    qkv = pl.BlockSpec((1, s, LANES), lambda b, p: (b, 0, p))
    return pl.pallas_call(
        functools.partial(_attn_kernel, tq=tq, tk=tk),
        grid=(bsz, ATT_HEADS // 2),
        in_specs=[qkv, qkv, qkv, qkv],
        out_specs=qkv,
        out_shape=jax.ShapeDtypeStruct((bsz, s, ATT_WIDTH), BF16),
        scratch_shapes=[pltpu.VMEM((2, s // tk, ATT_VT_ROWS, tk), BF16), pltpu.VMEM((2, tk, tq), F32),
                        pltpu.VMEM((2, 1, tq), F32), pltpu.VMEM((2, ATT_VT_ROWS, tq), F32)],
        compiler_params=_params("arbitrary", "arbitrary"),
        name="fox_attention",
    )(q, k, v, kbias)


def _gelu_tanh(x):
    c = 0.7978845608028654
    return 0.5 * x * (1.0 + jnp.tanh(c * (x + 0.044715 * (x * x * x))))


def _rnn_kernel(xr_ref, gr_ref, wc_ref, bc_ref, wbd_ref, brg_ref, big_ref, lam_ref, o_ref,
                xpad_sc, hc_sc, a_sc, b_sc, h_sc, *, ts):
    @pl.when(pl.program_id(1) == 0)
    def _():
        xpad_sc[:SUBLANES, :] = jnp.zeros((SUBLANES, RNN_WIDTH), F32)
        hc_sc[...] = jnp.zeros(hc_sc.shape, F32)

    x = xr_ref[0].astype(F32)
    xpad_sc[SUBLANES:, :] = x
    u = bc_ref[...] + wc_ref[3:4, :] * x
    for j in range(CONV_WIDTH - 1):
        off = SUBLANES - (CONV_WIDTH - 1) + j
        u = u + wc_ref[j:j + 1, :] * xpad_sc[off:off + ts, :]
    xpad_sc[:SUBLANES, :] = x[ts - SUBLANES:ts, :]

    lam = lam_ref[...]
    neg_sp = -(jnp.maximum(-lam, 0.0) + jnp.log1p(jnp.exp(-jnp.abs(lam))))
    for jt in range(RNN_WIDTH // MXU_DIM):
        cs = slice(jt * MXU_DIM, (jt + 1) * MXU_DIM)
        uc = u[:, cs]
        g = _dot(uc.astype(BF16), wbd_ref[jt])
        r = _sigmoid(g[:, :MXU_DIM] + brg_ref[:, cs])
        ig = _sigmoid(g[:, MXU_DIM:] + big_ref[:, cs])
        a = jnp.exp(r * (RGLRU_C * neg_sp[:, cs]))
        a_sc[:, cs] = a
        v = 1.0 - a * a
        b_sc[:, cs] = jnp.where(v > 0.0, v * lax.rsqrt(v), 0.0) * (ig * uc)

    srow = lax.broadcasted_iota(I32, (SUBLANES, RNN_WIDTH), 0)

    def scan_body(i, carry):
        r0 = pl.multiple_of(i * SUBLANES, SUBLANES)
        a = a_sc[pl.ds(r0, SUBLANES), :]
        b = b_sc[pl.ds(r0, SUBLANES), :]
        b = b + jnp.where(srow == 0, a * carry, 0.0)
        for sh in (1, 2, 4):
            keep = srow >= sh
            b = jnp.where(keep, b + a * pltpu.roll(b, sh, 0), b)
            if sh < SUBLANES // 2:
                a = jnp.where(keep, a * pltpu.roll(a, sh, 0), a)
        h_sc[pl.ds(r0, SUBLANES), :] = b
        return jnp.broadcast_to(b[SUBLANES - 1:SUBLANES, :], (SUBLANES, RNN_WIDTH))

    hc_sc[...] = lax.fori_loop(0, ts // SUBLANES, scan_body, hc_sc[...])
    o_ref[0] = (h_sc[...] * _gelu_tanh(gr_ref[0].astype(F32))).astype(BF16)


def _rnn_call(x_rnn, g_rnn, w_conv, b_conv, wbd, b_rg, b_ig, lam, ts):
    bsz, s, w = x_rnn.shape
    tok = pl.BlockSpec((1, ts, w), lambda b, i: (b, i, 0))
    return pl.pallas_call(
        functools.partial(_rnn_kernel, ts=ts),
        grid=(bsz, s // ts),
        in_specs=[tok, tok, _const_spec(w_conv.shape), _const_spec((1, w)), _const_spec(wbd.shape),
                  _const_spec((1, w)), _const_spec((1, w)), _const_spec((1, w))],
        out_specs=tok,
        out_shape=jax.ShapeDtypeStruct((bsz, s, w), BF16),
        scratch_shapes=[pltpu.VMEM((ts + SUBLANES, w), F32), pltpu.VMEM((SUBLANES, w), F32),
                        pltpu.VMEM((ts, w), F32), pltpu.VMEM((ts, w), F32), pltpu.VMEM((ts, w), F32)],
        compiler_params=_params("arbitrary", "arbitrary"),
        name="rglru",
    )(x_rnn, g_rnn, w_conv, b_conv, wbd, b_rg, b_ig, lam)


WINDOW_SHIFT = 5
WINDOW_ROWS = 1 << WINDOW_SHIFT


def _route(h2, wr_hi_ref, wr_mid_ref, rbias_ref, cnt_ref, lpos_ref, before_ref, ntile_ref, wtok_ref, j, tm):
    h_hi, h_mid, _ = _split3(h2)
    logits = _dot_nt(wr_hi_ref[...], h_hi) + _dot_nt(wr_hi_ref[...], h_mid) + _dot_nt(wr_mid_ref[...], h_hi)
    scores = _sigmoid(logits)
    sel = scores + rbias_ref[...]

    giota = lax.broadcasted_iota(I32, (N_GROUPS, tm), 0)
    gs = jnp.zeros((N_GROUPS, tm), F32)
    for g in range(N_GROUPS):
        blk = sel[g * GROUP_SIZE:(g + 1) * GROUP_SIZE, :]
        m1 = jnp.max(blk, axis=0, keepdims=True)
        i1 = jnp.min(jnp.where(blk == m1, giota, N_EXPERTS), axis=0, keepdims=True)
        m2 = jnp.max(jnp.where(giota == i1, -jnp.inf, blk), axis=0, keepdims=True)
        gs = jnp.where(giota == g, m1 + m2, gs)

    gsel = jnp.zeros((N_GROUPS, tm), F32)
    for _ in range(TOPK_GROUPS):
        m = jnp.max(gs, axis=0, keepdims=True)
        idx = jnp.min(jnp.where(gs == m, giota, N_EXPERTS), axis=0, keepdims=True)
        hit = giota == idx
        gsel = jnp.where(hit, 1.0, gsel)
        gs = jnp.where(hit, -jnp.inf, gs)

    masked = jnp.concatenate(
        [jnp.where(gsel[g:g + 1, :] > 0.0, sel[g * GROUP_SIZE:(g + 1) * GROUP_SIZE, :], -jnp.inf)
         for g in range(N_GROUPS)], axis=0)

    eiota = lax.broadcasted_iota(I32, (N_EXPERTS, tm), 0)
    chosen = jnp.zeros((N_EXPERTS, tm), F32)
    idxs, wts = [], []
    for _ in range(TOP_K):
        m = jnp.max(masked, axis=0, keepdims=True)
        idx = jnp.min(jnp.where(masked == m, eiota, N_EXPERTS), axis=0, keepdims=True)
        hit = eiota == idx
        wts.append(jnp.sum(jnp.where(hit, scores, 0.0), axis=0, keepdims=True))
        idxs.append(idx)
        chosen = jnp.where(hit, 1.0, chosen)
        masked = jnp.where(hit, -jnp.inf, masked)
    wsum = wts[0]
    for w in wts[1:]:
        wsum = wsum + w

    srow = lax.broadcasted_iota(I32, (tm, tm), 0)
    scol = lax.broadcasted_iota(I32, (tm, tm), 1)
    earlier = (srow < scol).astype(BF16)
    local_rank = _dot(chosen.astype(BF16), earlier)
    n_e = jnp.sum(chosen, axis=1, keepdims=True)
    before_ref[j] = cnt_ref[...].astype(I32)
    ntile_ref[j] = n_e.astype(I32)
    cnt_ref[...] = cnt_ref[...] + n_e
    n_pad = (((n_e.astype(I32) + (WINDOW_ROWS - 1)) >> WINDOW_SHIFT) << WINDOW_SHIFT).astype(F32)
    erow = lax.broadcasted_iota(I32, (N_EXPERTS, N_EXPERTS), 0)
    ecol = lax.broadcasted_iota(I32, (N_EXPERTS, N_EXPERTS), 1)
    off = _dot((ecol < erow).astype(BF16), jnp.broadcast_to(n_pad, (N_EXPERTS, LANES)).astype(BF16))[:, :1]
    local_pos = off + local_rank

    riota = lax.broadcasted_iota(I32, (SUBLANES, tm), 0)
    lpos = jnp.zeros((SUBLANES, tm), I32)
    wrow = jnp.zeros((SUBLANES, tm), F32)
    for r in range(TOP_K):
        lp = jnp.sum(jnp.where(eiota == idxs[r], local_pos, 0.0), axis=0, keepdims=True)
        lpos = jnp.where(riota == r, lp.astype(I32), lpos)
        wrow = jnp.where(riota == r, wts[r] / wsum * ROUTE_SCALE, wrow)
    lpos_ref[j] = lpos
    wpad = jnp.concatenate([wrow, jnp.zeros((LANES - SUBLANES, tm), F32)], axis=0)
    wtok_ref[j * tm:(j + 1) * tm, :] = wpad.T


U32 = jnp.uint32
PACK_ROWS = D_MODEL // (2 * LANES)
HIGH_HALF = 0xFFFF0000


def _store_token_rows(rows_ref, val):
    half = D_MODEL // 2
    n = val.shape[0]
    bits = lax.bitcast_convert_type(val.astype(BF16).astype(F32), U32)
    words = (bits[:, half:] & jnp.uint32(HIGH_HALF)) | (bits[:, :half] >> 16)
    for c in range(PACK_ROWS):
        chunk = words[:, c * LANES:(c + 1) * LANES]
        if len(rows_ref.shape) == 2:
            rows_ref[pl.ds(c, n, stride=PACK_ROWS), :] = chunk
        else:
            rows_ref[:, c, :] = chunk


def _load_token_rows(rows_ref):
    if len(rows_ref.shape) == 2:
        n = rows_ref.shape[0] // PACK_ROWS
        chunks = [rows_ref[pl.ds(c, n, stride=PACK_ROWS), :] for c in range(PACK_ROWS)]
    else:
        chunks = [rows_ref[:, c, :] for c in range(PACK_ROWS)]
    words = jnp.concatenate(chunks, axis=-1)
    lo = lax.bitcast_convert_type(words << 16, F32)
    hi = lax.bitcast_convert_type(words & jnp.uint32(HIGH_HALF), F32)
    return jnp.concatenate([lo, hi], axis=-1)


def _merge_kernel(ya_ref, yr_ref, ga_ref, gb_ref, x_ref, gm_ref, shf_ref, scf_ref, gpost_ref, gpre_ref,
                  wba_ref, wbr_ref, wout_ref, wr_hi_ref, wr_mid_ref, rbias_ref,
                  x1_ref, h2_ref, h2rows_ref, lpos_ref, before_ref, ntile_ref, wtok_ref, cnt_ref, *, tm, tr):
    @pl.when((pl.program_id(0) == 0) & (pl.program_id(1) == 0))
    def _():
        cnt_ref[...] = jnp.zeros(cnt_ref.shape, F32)

    pa = _dot(ya_ref[0], wba_ref[...])
    pr = _dot(yr_ref[0], wbr_ref[...])
    merged = _sigmoid(ga_ref[0].astype(F32)) * pa + _sigmoid(gb_ref[0].astype(F32)) * pr
    o = _dot(merged.astype(BF16), wout_ref[...])
    x1 = x_ref[0] + gm_ref[0] * (_rms_scale(o) * gpost_ref[...])
    x1_ref[0] = x1
    h2 = (_rms_scale(x1) * gpre_ref[...]) * (1.0 + scf_ref[0]) + shf_ref[0]
    h2_ref[0] = h2.astype(BF16)
    _store_token_rows(h2rows_ref, h2)
    for j in range(tm // tr):
        _route(h2[j * tr:(j + 1) * tr, :], wr_hi_ref, wr_mid_ref, rbias_ref, cnt_ref,
               lpos_ref, before_ref, ntile_ref, wtok_ref, j, tr)


def _merge_call(y_attn, y_rnn, gate_a, gate_r, x, gate_m, shift_f, scale_f, g_post, g_pre,
                wba, wbr, wout, wr_hi, wr_mid, rbias, tm, tr):
    bsz, s, d = x.shape
    nt = s // tm
    nr = tm // tr
    tok = lambda w: pl.BlockSpec((1, tm, w), lambda b, i: (b, i, 0))
    mod = pl.BlockSpec((1, 1, d), lambda b, i: (b, 0, 0))
    return pl.pallas_call(
        functools.partial(_merge_kernel, tm=tm, tr=tr),
        grid=(bsz, nt),
        in_specs=[tok(ATT_WIDTH), tok(d), tok(d), tok(d), tok(d), mod, mod, mod,
                  _const_spec((1, d)), _const_spec((1, d)),
                  _const_spec(wba.shape), _const_spec(wbr.shape), _const_spec(wout.shape),
                  _const_spec(wr_hi.shape), _const_spec(wr_mid.shape), _const_spec(rbias.shape)],
        out_specs=[tok(d), tok(d),
                   pl.BlockSpec((tm * PACK_ROWS, LANES), lambda b, i: (b * nt + i, 0)),
                   pl.BlockSpec((nr, SUBLANES, tr), lambda b, i: (b * nt + i, 0, 0)),
                   pl.BlockSpec((nr, N_EXPERTS, 1), lambda b, i: (b * nt + i, 0, 0)),
                   pl.BlockSpec((nr, N_EXPERTS, 1), lambda b, i: (b * nt + i, 0, 0)),
                   pl.BlockSpec((tm, LANES), lambda b, i: (b * nt + i, 0)),
                   pl.BlockSpec((N_EXPERTS, 1), lambda b, i: (0, 0))],
        out_shape=[jax.ShapeDtypeStruct((bsz, s, d), F32), jax.ShapeDtypeStruct((bsz, s, d), BF16),
                   jax.ShapeDtypeStruct((bsz * s * PACK_ROWS, LANES), U32),
                   jax.ShapeDtypeStruct((bsz * nt * nr, SUBLANES, tr), I32),
                   jax.ShapeDtypeStruct((bsz * nt * nr, N_EXPERTS, 1), I32),
                   jax.ShapeDtypeStruct((bsz * nt * nr, N_EXPERTS, 1), I32),
                   jax.ShapeDtypeStruct((bsz * s, LANES), F32),
                   jax.ShapeDtypeStruct((N_EXPERTS, 1), F32)],
        compiler_params=_params("arbitrary", "arbitrary"),
        name="merge_route",
    )(y_attn, y_rnn, gate_a, gate_r, x, gate_m, shift_f, scale_f, g_post, g_pre,
      wba, wbr, wout, wr_hi, wr_mid, rbias)


WIN_LIST = LANES
ROW_COPY_UNROLL = 8


def _local_rows(tm):
    rows = tm * TOP_K + N_EXPERTS * WINDOW_ROWS
    assert rows // WINDOW_ROWS < WIN_LIST
    return rows


def _for_each_window(wlist_ref, fn):
    def body(w, c):
        fn(pl.multiple_of(w * WINDOW_ROWS, WINDOW_ROWS), wlist_ref[0, 0, w])
        return c

    lax.fori_loop(0, wlist_ref[0, 0, WIN_LIST - 1], body, 0)


def _dispatch_kernel(lpos_ref, wlist_ref, wlistp_ref, zfrom_ref, zto_ref, h_ref, xs_hbm, loc_sc, zero_sc, sem,
                     *, tm):
    g = pl.program_id(0)
    cur = lax.rem(g, 2)

    @pl.when(g == 0)
    def _():
        loc_sc[...] = jnp.zeros(loc_sc.shape, U32)
        zero_sc[...] = jnp.zeros(zero_sc.shape, U32)

        def for_each_zero_window(fn):
            def per_range(j, c):
                def per_window(w, c2):
                    r0 = pl.multiple_of(zfrom_ref[j] + w * WINDOW_ROWS, WINDOW_ROWS)
                    fn(pltpu.make_async_copy(zero_sc, xs_hbm.at[pl.ds(r0, WINDOW_ROWS)], sem))
                    return c2
                lax.fori_loop(0, (zto_ref[j] - zfrom_ref[j]) >> WINDOW_SHIFT, per_window, 0)
                return c
            lax.fori_loop(0, zfrom_ref.shape[0], per_range, 0)

        for_each_zero_window(lambda cp: cp.start())
        for_each_zero_window(lambda cp: cp.wait())

    for t in range(tm):
        row = h_ref[t]
        for k in range(TOP_K):
            loc_sc[lpos_ref[0, k, t]] = row

    half = _local_rows(tm)

    def window(base):
        return lambda row0, slot0: pltpu.make_async_copy(
            loc_sc.at[pl.ds(pl.multiple_of(base + row0, WINDOW_ROWS), WINDOW_ROWS)],
            xs_hbm.at[pl.ds(slot0, WINDOW_ROWS)], sem)

    @pl.when(g > 0)
    def _():
        _for_each_window(wlistp_ref, lambda r, s: window((1 - cur) * half)(r, s).wait())

    _for_each_window(wlist_ref, lambda r, s: window(cur * half)(r, s).start())

    @pl.when(g == pl.num_programs(0) - 1)
    def _():
        _for_each_window(wlist_ref, lambda r, s: window(cur * half)(r, s).wait())


def _dispatch_call(lpos, wlist, zero_from, zero_to, h2_rows, n_rows, tm):
    t = h2_rows.shape[0]
    smem = functools.partial(pl.BlockSpec, memory_space=pltpu.SMEM)
    return pl.pallas_call(
        functools.partial(_dispatch_kernel, tm=tm),
        grid=(t // tm,),
        in_specs=[smem((1, SUBLANES, tm), lambda i: (i, 0, 0)),
                  smem((1, 1, WIN_LIST), lambda i: (i, 0, 0)),
                  smem((1, 1, WIN_LIST), lambda i: (jnp.maximum(i - 1, 0), 0, 0)),
                  smem(), smem(),
                  pl.BlockSpec((tm, PACK_ROWS, LANES), lambda i: (i, 0, 0))],
        out_specs=pl.BlockSpec(memory_space=pl.ANY),
        out_shape=jax.ShapeDtypeStruct((n_rows, PACK_ROWS, LANES), U32),
        scratch_shapes=[pltpu.VMEM((2 * _local_rows(tm), PACK_ROWS, LANES), U32),
                        pltpu.VMEM((WINDOW_ROWS, PACK_ROWS, LANES), U32), pltpu.SemaphoreType.DMA],
        compiler_params=_params("arbitrary"),
        name="moe_dispatch",
    )(lpos, wlist, wlist, zero_from, zero_to, h2_rows)


def _expert_kernel(be_ref, nused_ref, xs_ref, wg_ref, wu_ref, wd_ref, ys_ref, wgb_sc, wub_sc, wdb_sc, *, rows):
    i = pl.program_id(0)

    @pl.when((i == 0) | (be_ref[i] != be_ref[jnp.maximum(i - 1, 0)]))
    def _():
        wgb_sc[...] = wg_ref[0].astype(BF16)
        wub_sc[...] = wu_ref[0].astype(BF16)
        wdb_sc[...] = wd_ref[0].astype(BF16)

    @pl.when(i < nused_ref[0])
    def _():
        xb = _load_token_rows(xs_ref).astype(BF16)
        g = _dot(xb, wgb_sc[...])
        u = _dot(xb, wub_sc[...])
        act = (g * _sigmoid(g)) * u
        _store_token_rows(ys_ref, _dot(act.astype(BF16), wdb_sc[...]))

    @pl.when(pl.program_id(0) >= nused_ref[0])
    def _():
        ys_ref[...] = jnp.zeros(ys_ref.shape, U32)


def _expert_call(block_e, n_used, xs, wg, wu, wd, rows):
    n_rows = xs.shape[0] // PACK_ROWS
    _, d, ff = wg.shape
    row_spec = pl.BlockSpec((rows * PACK_ROWS, LANES), lambda i, be, nu: (jnp.minimum(i, nu[0] - 1), 0))
    grid_spec = pltpu.PrefetchScalarGridSpec(
        num_scalar_prefetch=2,
        grid=(n_rows // rows,),
        in_specs=[row_spec,
                  pl.BlockSpec((1, d, ff), lambda i, be, nu: (be[i], 0, 0)),
                  pl.BlockSpec((1, d, ff), lambda i, be, nu: (be[i], 0, 0)),
                  pl.BlockSpec((1, ff, d), lambda i, be, nu: (be[i], 0, 0))],
        out_specs=pl.BlockSpec((rows * PACK_ROWS, LANES), lambda i, be, nu: (i, 0)),
        scratch_shapes=[pltpu.VMEM((d, ff), BF16), pltpu.VMEM((d, ff), BF16), pltpu.VMEM((ff, d), BF16)],
    )
    return pl.pallas_call(
        functools.partial(_expert_kernel, rows=rows),
        grid_spec=grid_spec,
        out_shape=jax.ShapeDtypeStruct(xs.shape, U32),
        compiler_params=_params("arbitrary"),
        name="moe_experts",
    )(block_e, n_used, xs, wg, wu, wd)


def _final_kernel(lpos0_ref, lposn_ref, tab0_ref, tabw_ref, tabi_ref, x1_ref, h2_ref, wtok_ref, gf_ref, gpost_ref,
                  wsg_ref, wsu_ref, wsd_ref, ys_hbm, o_ref, win_sc, gnext_sc, gcur_sc, sem, *, tm):
    nt = pl.num_programs(1)
    g = pl.program_id(0) * nt + pl.program_id(1)
    n_tiles = pl.num_programs(0) * nt
    buf_next = lax.rem(g + 1, 2)
    buf_after = lax.rem(g, 2)
    half = _local_rows(tm)

    def window(buf):
        return lambda row0, slot0: pltpu.make_async_copy(
            ys_hbm.at[pl.ds(slot0, WINDOW_ROWS)],
            win_sc.at[pl.ds(pl.multiple_of(buf * half + row0, WINDOW_ROWS), WINDOW_ROWS)], sem.at[buf])

    def start_windows(tab_ref, buf):
        _for_each_window(tab_ref, lambda r, s: window(buf)(r, s).start())

    def wait_windows(tab_ref, buf):
        _for_each_window(tab_ref, lambda r, s: window(buf)(r, s).wait())

    def pick_rows(lpos_ref, t):
        for k in range(TOP_K):
            gnext_sc[k, t] = win_sc[lpos_ref[0, k, t]]

    def stage():
        for k in range(TOP_K):
            gcur_sc[k] = gnext_sc[k].reshape(tm * PACK_ROWS, LANES)

    @pl.when(g == 0)
    def _():
        start_windows(tab0_ref, 0)
        wait_windows(tab0_ref, 0)

        def body(i, c):
            for u in range(ROW_COPY_UNROLL):
                pick_rows(lpos0_ref, i * ROW_COPY_UNROLL + u)
            return c

        lax.fori_loop(0, tm // ROW_COPY_UNROLL, body, 0)
        stage()

        @pl.when(n_tiles > 1)
        def _():
            start_windows(tabw_ref, 1)

    @pl.when(g + 1 < n_tiles)
    def _():
        wait_windows(tabw_ref, buf_next)

    @pl.when(g + 2 < n_tiles)
    def _():
        start_windows(tabi_ref, buf_after)

    for t in range(tm):
        pick_rows(lposn_ref, t)

    hb = h2_ref[0]
    a = _dot(hb, wsg_ref[...])
    u = _dot(hb, wsu_ref[...])
    tot = _dot(((a * _sigmoid(a)) * u).astype(BF16), wsd_ref[...])

    w = wtok_ref[...]
    routed = w[:, 0:1] * _load_token_rows(gcur_sc.at[0])
    for k in range(1, TOP_K):
        routed = routed + w[:, k:k + 1] * _load_token_rows(gcur_sc.at[k])
    tot = routed + tot
    o_ref[0] = x1_ref[0] + gf_ref[0] * (_rms_scale(tot) * gpost_ref[...])

    stage()


def _final_call(lpos, tab, x1, h2, wtok, gate_f, g_post, wsg, wsu, wsd, ys, tm):
    bsz, s, d = x1.shape
    nt = s // tm
    n_tiles = bsz * nt
    smem = functools.partial(pl.BlockSpec, memory_space=pltpu.SMEM)
    tok = pl.BlockSpec((1, tm, d), lambda b, i: (b, i, 0))
    tab_shape = (1, 1, WIN_LIST)
    ahead = lambda k: (lambda b, i: (jnp.minimum(b * nt + i + k, n_tiles - 1), 0, 0))
    return pl.pallas_call(
        functools.partial(_final_kernel, tm=tm),
        grid=(bsz, nt),
        in_specs=[smem((1, SUBLANES, tm), lambda b, i: (0, 0, 0)),
                  smem((1, SUBLANES, tm), ahead(1)),
                  smem(tab_shape, lambda b, i: (0, 0, 0)),
                  smem(tab_shape, ahead(1)),
                  smem(tab_shape, ahead(2)),
                  tok, tok,
                  pl.BlockSpec((tm, LANES), lambda b, i: (b * nt + i, 0)),
                  pl.BlockSpec((1, 1, d), lambda b, i: (b, 0, 0)),
                  _const_spec((1, d)),
                  _const_spec(wsg.shape), _const_spec(wsu.shape), _const_spec(wsd.shape),
                  pl.BlockSpec(memory_space=pl.ANY)],
        out_specs=tok,
        out_shape=jax.ShapeDtypeStruct((bsz, s, d), F32),
        scratch_shapes=[pltpu.VMEM((2 * _local_rows(tm), PACK_ROWS, LANES), U32),
                        pltpu.VMEM((TOP_K, tm, PACK_ROWS, LANES), U32),
                        pltpu.VMEM((TOP_K, tm * PACK_ROWS, LANES), U32), pltpu.SemaphoreType.DMA((2,))],
        compiler_params=_params("arbitrary", "arbitrary"),
        name="moe_combine",
    )(lpos, lpos, tab, tab, tab, x1, h2, wtok, gate_f, g_post, wsg, wsu, wsd, ys)


def _block_diag_tiles(w):
    per = MXU_DIM // RNN_BLOCK_DIM
    nt = RNN_BLOCKS // per
    w4 = w.reshape(nt, per, RNN_BLOCK_DIM, RNN_BLOCK_DIM)
    eye = jnp.eye(per, dtype=w.dtype)
    return jnp.einsum("tnij,nm->tnimj", w4, eye).reshape(nt, MXU_DIM, MXU_DIM)


def _tile(n, pref):
    t = min(n, pref)
    assert n % t == 0, (n, t)
    return t


def _layer(x, c, w_ada, b_ada, g_pre_mix, g_post_mix, g_pre_ffn, g_post_ffn, w_in, b_forget,
           w_conv, b_conv, w_rg, b_rg, w_ig, b_ig, lam, w_branch_attn, w_branch_rnn, w_out,
           w_router, router_bias, w_exp_gate, w_exp_up, w_exp_down, w_sh_gate, w_sh_up, w_sh_down):
    bsz, s, d = x.shape
    t = bsz * s
    row = lambda v: v.reshape(1, -1)

    mod = _ada_call(c, w_ada, b_ada)
    shift_m, scale_m, gate_m, shift_f, scale_f, gate_f = [
        mod[:, j * d:(j + 1) * d].reshape(bsz, 1, d) for j in range(6)]

    o_f = 3 * ATT_WIDTH
    o_r = o_f + ATT_HEADS
    wqkv = w_in[:, :o_f].astype(BF16)
    wf = jnp.pad(w_in[:, o_f:o_r], ((0, 0), (0, LANES - ATT_HEADS))).astype(BF16)
    wr = w_in[:, o_r:].astype(BF16)
    bf_pad = jnp.pad(b_forget, (0, LANES - ATT_HEADS)).reshape(1, LANES)
    q, k, v, kbias, x_rnn, g_rnn, gate_a, gate_r = _inproj_call(
        x, shift_m, scale_m, row(g_pre_mix), wqkv, wf, wr, bf_pad, _tile(s, 512))
    tq = _tile(s, 512)
    y_attn = _attn_call(q, k, v, kbias, tq, _tile(tq, 512))

    wbd = jnp.concatenate([_block_diag_tiles(w_rg), _block_diag_tiles(w_ig)], axis=-1).astype(BF16)
    y_rnn = _rnn_call(x_rnn, g_rnn, w_conv, row(b_conv), wbd, row(b_rg), row(b_ig), row(lam), _tile(s, 256))

    wr_t = w_router.T
    wr_hi = wr_t.astype(BF16)
    wr_mid = (wr_t - wr_hi.astype(F32)).astype(BF16)
    tm_m = _tile(s, 512)
    tm_r = _tile(tm_m, 256)
    x1, h2, h2_rows, lpos, before, n_tile, wtok, counts = _merge_call(
        y_attn, y_rnn, gate_a, gate_r, x, gate_m, shift_f, scale_f, row(g_post_mix), row(g_pre_ffn),
        w_branch_attn.astype(BF16), w_branch_rnn.astype(BF16), w_out.astype(BF16),
        wr_hi, wr_mid, router_bias.reshape(N_EXPERTS, 1), tm_m, tm_r)

    rows = 1024
    n_blocks =-(-(t * TOP_K + N_EXPERTS * WINDOW_ROWS) // rows) + N_EXPERTS
    cnt = counts[:, 0].astype(I32)
    padded = ((cnt + WINDOW_ROWS + rows - 1) // rows) * rows
    pend = jnp.cumsum(padded).astype(I32)
    pstart = pend - padded
    n_used = jnp.maximum(pend[-1] // rows, 1)
    blk = jnp.minimum(jnp.arange(n_blocks, dtype=I32), n_used - 1)
    block_e = jnp.minimum(jnp.sum((pend[None, :] <= (blk * rows)[:, None]).astype(I32), axis=1), N_EXPERTS - 1)

    n_win = (n_tile[:, :, 0] + WINDOW_ROWS - 1) // WINDOW_ROWS
    win_end = jnp.cumsum(n_win, axis=1)
    w_id = jnp.arange(WIN_LIST, dtype=I32)[None, :, None]
    owner = (jnp.sum((win_end[:, None, :] <= w_id).astype(I32), axis=2, keepdims=True)
             == jnp.arange(N_EXPERTS, dtype=I32)[None, None, :]).astype(I32)
    first_slot = pstart[None, :] + before[:, :, 0] - (win_end - n_win) * WINDOW_ROWS
    wlist = jnp.sum(owner * first_slot[:, None, :], axis=2) + w_id[:, :, 0] * WINDOW_ROWS
    wlist = jnp.where(w_id[:, :, 0] == WIN_LIST - 1, win_end[:, -1:], wlist).astype(I32)[:, None, :]
    n_rt = lpos.shape[0]
    lpos = lpos + (jnp.arange(n_rt, dtype=I32) % 2)[:, None, None] * _local_rows(tm_r)

    rows3 = lambda a: a.reshape(-1, PACK_ROWS, LANES)
    rows2 = lambda a: a.reshape(-1, LANES)
    zero_from = jnp.concatenate([pstart + cnt // WINDOW_ROWS * WINDOW_ROWS, pend[-1:]]).astype(I32)
    zero_to = jnp.concatenate([pend, jnp.full((1,), n_blocks * rows, I32)]).astype(I32)
    xs = _dispatch_call(lpos, wlist, zero_from, zero_to, rows3(h2_rows), n_blocks * rows, tm_r)
    ys = rows3(_expert_call(block_e, n_used.reshape(1), rows2(xs),
                            w_exp_gate, w_exp_up, w_exp_down, rows))
    return _final_call(lpos, wlist, x1, h2, wtok, gate_f, row(g_post_ffn),
                       w_sh_gate.astype(BF16), w_sh_up.astype(BF16), w_sh_down.astype(BF16), ys, tm_r)


def kernel(x, c, w_ada, b_ada, g_pre_mix, g_post_mix, g_pre_ffn, g_post_ffn, w_in, b_forget, w_conv, b_conv, w_rg, b_rg, w_ig, b_ig, rglru_lambda, w_branch_attn, w_branch_rnn, w_out, w_router, router_bias, w_exp_gate, w_exp_up, w_exp_down, w_sh_gate, w_sh_up, w_sh_down):
    depth = w_ada.shape[0]
    for l in range(depth):
        x = _layer(x, c, w_ada[l], b_ada[l], g_pre_mix[l], g_post_mix[l], g_pre_ffn[l], g_post_ffn[l],
                   w_in[l], b_forget[l], w_conv[l], b_conv[l], w_rg[l], b_rg[l], w_ig[l], b_ig[l],
                   rglru_lambda[l], w_branch_attn[l], w_branch_rnn[l], w_out[l], w_router[l],
                   router_bias[l], w_exp_gate[l], w_exp_up[l], w_exp_down[l],
                   w_sh_gate[l], w_sh_up[l], w_sh_down[l])
    return x
```

```python
import functools

import jax
import jax.numpy as jnp
from jax import lax
from jax.experimental import pallas as pl
from jax.experimental.pallas import tpu as pltpu

F32 = jnp.float32
BF16 = jnp.bfloat16
I32 = jnp.int32

D_MODEL = 1024
ATT_HEADS = 8
ATT_HEAD_DIM = 64
ATT_WIDTH = ATT_HEADS * ATT_HEAD_DIM
RNN_WIDTH = D_MODEL
RNN_BLOCKS = 16
RNN_BLOCK_DIM = RNN_WIDTH // RNN_BLOCKS
CONV_WIDTH = 4
RGLRU_C = 8.0
N_EXPERTS = 64
TOP_K = 6
N_GROUPS = 8
GROUP_SIZE = N_EXPERTS // N_GROUPS
TOPK_GROUPS = 4
EXPERT_FF = D_MODEL // 4
ROUTE_SCALE = 2.5
NORM_EPS = 1e-6

LANES = 128
SUBLANES = 8
MXU_DIM = 256
VMEM_LIMIT_BYTES = 56 * 1024 * 1024

NEG_BIG = -1e30
RANK_BITS = 20
RANK_MASK = (1 << RANK_BITS) - 1


def _sigmoid(x):
    return 0.5 * jnp.tanh(0.5 * x) + 0.5


def _rms_scale(x):
    return x * lax.rsqrt(jnp.mean(x * x, axis=-1, keepdims=True) + NORM_EPS)


def _dot(a, b):
    return jnp.dot(a, b, preferred_element_type=F32)


def _dot_nt(a, b):
    return lax.dot_general(a, b, (((1,), (1,)), ((), ())), preferred_element_type=F32)


def _split3(a):
    hi = a.astype(BF16)
    r1 = a - hi.astype(F32)
    mid = r1.astype(BF16)
    lo = (r1 - mid.astype(F32)).astype(BF16)
    return hi, mid, lo


def _params(*sem):
    return pltpu.CompilerParams(dimension_semantics=sem, vmem_limit_bytes=VMEM_LIMIT_BYTES)


def _const_spec(shape):
    nd = len(shape)
    return pl.BlockSpec(shape, lambda *_: (0,) * nd)


def _ada_kernel(c_ref, w_ref, b_ref, o_ref):
    c = c_ref[...]
    sc = c * _sigmoid(c)
    s_hi, s_mid, _ = _split3(sc)
    w_hi, w_mid, _ = _split3(w_ref[...])
    acc = _dot(s_hi, w_hi) + _dot(s_hi, w_mid) + _dot(s_mid, w_hi)
    o_ref[...] = acc + b_ref[...]


def _ada_call(c, w_ada, b_ada):
    bsz, d = c.shape
    n = w_ada.shape[1]
    tn = d
    return pl.pallas_call(
        _ada_kernel,
        grid=(n // tn,),
        in_specs=[
            pl.BlockSpec((bsz, d), lambda j: (0, 0)),
            pl.BlockSpec((d, tn), lambda j: (0, j)),
            pl.BlockSpec((1, tn), lambda j: (0, j)),
        ],
        out_specs=pl.BlockSpec((bsz, tn), lambda j: (0, j)),
        out_shape=jax.ShapeDtypeStruct((bsz, n), F32),
        compiler_params=_params("arbitrary"),
        name="ada_mod",
    )(c, w_ada, b_ada.reshape(1, n))


N_BIAS_PIECES = 3
CUM_BLOCK = 256


def _bias_placement():
    h = jnp.arange(LANES)[:, None]
    lane = jnp.arange(ATT_WIDTH)[None, :]
    mats = []
    for piece in range(N_BIAS_PIECES):
        target = (h // 2) * LANES + ATT_HEAD_DIM * (1 - h % 2) + piece
        mats.append(((lane == target) & (h < ATT_HEADS)).astype(BF16))
    return jnp.concatenate(mats, axis=0)


def _key_bias(f_logit, bf_ref, place_ref, kb_ref, cum_sc):
    tm = f_logit.shape[0]
    blk = min(tm, CUM_BLOCK)
    row = lax.broadcasted_iota(I32, (blk, blk), 0)
    col = lax.broadcasted_iota(I32, (blk, blk), 1)
    tri = (row >= col).astype(BF16)
    carry = cum_sc[...]
    for i in range(tm // blk):
        z = f_logit[i * blk:(i + 1) * blk, :] + bf_ref[...]
        lf = jnp.minimum(z, 0.0) - jnp.log1p(jnp.exp(-jnp.abs(z)))
        c3 = _dot(tri, jnp.concatenate(_split3(lf), axis=1))
        c = c3[:, :LANES] + c3[:, LANES:2 * LANES] + c3[:, 2 * LANES:] + carry
        carry = c[blk - 1:blk, :]
        kb = _dot(jnp.concatenate(_split3(-c), axis=1), place_ref[...])
        kb_ref[0, i * blk:(i + 1) * blk, :] = kb.astype(BF16)
    cum_sc[...] = carry


def _inproj_kernel(x_ref, sh_ref, sc_ref, g_ref, wqkv_ref, wf_ref, wr_ref, bf_ref, place_ref,
                   q_ref, k_ref, v_ref, kb_ref, xr_ref, gr_ref, ga_ref, gb_ref, cum_sc):
    @pl.when(pl.program_id(1) == 0)
    def _():
        cum_sc[...] = jnp.zeros(cum_sc.shape, F32)

    x = x_ref[0]
    h = (_rms_scale(x) * g_ref[...]) * (1.0 + sc_ref[0]) + sh_ref[0]
    hb = h.astype(BF16)
    for j, ref in enumerate((q_ref, k_ref, v_ref)):
        ref[0] = _dot(hb, wqkv_ref[:, j * ATT_WIDTH:(j + 1) * ATT_WIDTH]).astype(BF16)
    _key_bias(_dot(hb, wf_ref[...]), bf_ref, place_ref, kb_ref, cum_sc)
    half = D_MODEL // 2
    for j, ref in enumerate((xr_ref, gr_ref, ga_ref, gb_ref)):
        for c in range(2):
            lo = j * D_MODEL + c * half
            ref[0, :, c * half:(c + 1) * half] = _dot(hb, wr_ref[:, lo:lo + half]).astype(BF16)


def _inproj_call(x, shift, scale, g, wqkv, wf, wr, b_forget_pad, tm):
    bsz, s, d = x.shape
    nt = s // tm
    tok = lambda w: pl.BlockSpec((1, tm, w), lambda b, i: (b, i, 0))
    mod = pl.BlockSpec((1, 1, d), lambda b, i: (b, 0, 0))
    place = _bias_placement()
    out_w = (ATT_WIDTH,) * 4 + (d,) * 4
    return pl.pallas_call(
        _inproj_kernel,
        grid=(bsz, nt),
        in_specs=[tok(d), mod, mod, _const_spec((1, d)),
                  _const_spec(wqkv.shape), _const_spec(wf.shape), _const_spec(wr.shape),
                  _const_spec((1, LANES)), _const_spec(place.shape)],
        out_specs=[tok(w) for w in out_w],
        out_shape=[jax.ShapeDtypeStruct((bsz, s, w), BF16) for w in out_w],
        scratch_shapes=[pltpu.VMEM((1, LANES), F32)],
        compiler_params=_params("arbitrary", "arbitrary"),
        name="in_proj",
    )(x, shift, scale, g, wqkv, wf, wr, b_forget_pad, place)


ATT_VT_ROWS = ATT_HEAD_DIM + 16


def _attn_kernel(q_ref, k_ref, v_ref, kb_ref, o_ref, vt_sc, st_sc, m_sc, acc_sc, *, tq, tk):
    s = q_ref.shape[1]
    nq, nk, ratio = s // tq, s // tk, tq // tk
    lane = lax.broadcasted_iota(I32, (1, LANES), 1)
    head_lanes = (lane < ATT_HEAD_DIM, lane >= ATT_HEAD_DIM)
    one_lanes = ((lane >= ATT_HEAD_DIM) & (lane < ATT_HEAD_DIM + N_BIAS_PIECES), lane < N_BIAS_PIECES)
    key_minus_query = (lax.broadcasted_iota(I32, (tk, tq), 0) - lax.broadcasted_iota(I32, (tk, tq), 1))

    vt = v_ref[0].astype(F32).T
    ones = jnp.ones((ATT_VT_ROWS - ATT_HEAD_DIM, s), F32)
    for h in range(2):
        vth = jnp.concatenate([vt[h * ATT_HEAD_DIM:(h + 1) * ATT_HEAD_DIM, :], ones], axis=0).astype(BF16)
        for j in range(nk):
            vt_sc[h, j] = vth[:, j * tk:(j + 1) * tk]

    def q_body(qi, carry):
        q0 = pl.multiple_of(qi * tq, tq)
        qs = q_ref[0, pl.ds(q0, tq), :] * (ATT_HEAD_DIM ** -0.5)
        qa = [jnp.where(head_lanes[h], qs, one_lanes[h].astype(BF16)) for h in range(2)]
        m_sc[...] = jnp.full(m_sc.shape, NEG_BIG, F32)
        acc_sc[...] = jnp.zeros(acc_sc.shape, F32)

        def scores(kj):
            k0 = pl.multiple_of(kj * tk, tk)
            kb = k_ref[0, pl.ds(k0, tk), :]
            bias = kb_ref[0, pl.ds(k0, tk), :]
            return [_dot_nt(jnp.where(head_lanes[h], kb, bias), qa[h]) for h in range(2)]

        def accumulate(kj, masked):
            for h in range(2):
                st = st_sc[h]
                if masked:
                    st = jnp.where(key_minus_query <= q0 - kj * tk, st, NEG_BIG)
                m_old = m_sc[h]
                m_new = jnp.maximum(m_old, jnp.max(st, axis=0, keepdims=True))
                p = jnp.exp(st - m_new).astype(BF16)
                acc_sc[h] = jnp.exp(m_old - m_new) * acc_sc[h] + _dot(vt_sc[h, kj], p)
                m_sc[h] = m_new

        def put(sts):
            for h in range(2):
                st_sc[h] = sts[h]

        n_full = qi * ratio
        put(scores(0))

        def kv_body(kj, c2):
            nxt = scores(kj + 1)
            accumulate(kj, False)
            put(nxt)
            return c2

        lax.fori_loop(0, n_full, kv_body, 0)
        for d in range(ratio):
            nxt = scores(n_full + d + 1) if d + 1 < ratio else None
            accumulate(n_full + d, True)
            if nxt is not None:
                put(nxt)

        outs = []
        for h in range(2):
            acc = acc_sc[h]
            outs.append(acc[:ATT_HEAD_DIM, :] * (1.0 / acc[ATT_HEAD_DIM:ATT_HEAD_DIM + 1, :]))
        o_ref[0, pl.ds(q0, tq), :] = jnp.concatenate(outs, axis=0).T.astype(BF16)
        return carry

    lax.fori_loop(0, nq, q_body, 0)


def _attn_call(q, k, v, kbias, tq, tk):
    bsz, s, _ = q.shape
    qkv = pl.BlockSpec((1, s, LANES), lambda b, p: (b, 0, p))
    return pl.pallas_call(
        functools.partial(_attn_kernel, tq=tq, tk=tk),
        grid=(bsz, ATT_HEADS // 2),
        in_specs=[qkv, qkv, qkv, qkv],
        out_specs=qkv,
        out_shape=jax.ShapeDtypeStruct((bsz, s, ATT_WIDTH), BF16),
        scratch_shapes=[pltpu.VMEM((2, s // tk, ATT_VT_ROWS, tk), BF16), pltpu.VMEM((2, tk, tq), F32),
                        pltpu.VMEM((2, 1, tq), F32), pltpu.VMEM((2, ATT_VT_ROWS, tq), F32)],
        compiler_params=_params("arbitrary", "arbitrary"),
        name="fox_attention",
    )(q, k, v, kbias)


def _gelu_tanh(x):
    c = 0.7978845608028654
    return 0.5 * x * (1.0 + jnp.tanh(c * (x + 0.044715 * (x * x * x))))


def _rnn_kernel(xr_ref, gr_ref, wc_ref, bc_ref, wbd_ref, brg_ref, big_ref, lam_ref, o_ref,
                xpad_sc, hc_sc, a_sc, b_sc, h_sc, *, ts):
    @pl.when(pl.program_id(1) == 0)
    def _():
        xpad_sc[:SUBLANES, :] = jnp.zeros((SUBLANES, RNN_WIDTH), F32)
        hc_sc[...] = jnp.zeros(hc_sc.shape, F32)

    x = xr_ref[0].astype(F32)
    xpad_sc[SUBLANES:, :] = x
    u = bc_ref[...] + wc_ref[3:4, :] * x
    for j in range(CONV_WIDTH - 1):
        off = SUBLANES - (CONV_WIDTH - 1) + j
        u = u + wc_ref[j:j + 1, :] * xpad_sc[off:off + ts, :]
    xpad_sc[:SUBLANES, :] = x[ts - SUBLANES:ts, :]

    lam = lam_ref[...]
    neg_sp = -(jnp.maximum(-lam, 0.0) + jnp.log1p(jnp.exp(-jnp.abs(lam))))
    for jt in range(RNN_WIDTH // MXU_DIM):
        cs = slice(jt * MXU_DIM, (jt + 1) * MXU_DIM)
        uc = u[:, cs]
        g = _dot(uc.astype(BF16), wbd_ref[jt])
        r = _sigmoid(g[:, :MXU_DIM] + brg_ref[:, cs])
        ig = _sigmoid(g[:, MXU_DIM:] + big_ref[:, cs])
        a = jnp.exp(r * (RGLRU_C * neg_sp[:, cs]))
        a_sc[:, cs] = a
        v = 1.0 - a * a
        b_sc[:, cs] = jnp.where(v > 0.0, v * lax.rsqrt(v), 0.0) * (ig * uc)

    srow = lax.broadcasted_iota(I32, (SUBLANES, RNN_WIDTH), 0)

    def scan_body(i, carry):
        r0 = pl.multiple_of(i * SUBLANES, SUBLANES)
        a = a_sc[pl.ds(r0, SUBLANES), :]
        b = b_sc[pl.ds(r0, SUBLANES), :]
        b = b + jnp.where(srow == 0, a * carry, 0.0)
        for sh in (1, 2, 4):
            keep = srow >= sh
            b = jnp.where(keep, b + a * pltpu.roll(b, sh, 0), b)
            if sh < SUBLANES // 2:
                a = jnp.where(keep, a * pltpu.roll(a, sh, 0), a)
        h_sc[pl.ds(r0, SUBLANES), :] = b
        return jnp.broadcast_to(b[SUBLANES - 1:SUBLANES, :], (SUBLANES, RNN_WIDTH))

    hc_sc[...] = lax.fori_loop(0, ts // SUBLANES, scan_body, hc_sc[...])
    o_ref[0] = (h_sc[...] * _gelu_tanh(gr_ref[0].astype(F32))).astype(BF16)


def _rnn_call(x_rnn, g_rnn, w_conv, b_conv, wbd, b_rg, b_ig, lam, ts):
    bsz, s, w = x_rnn.shape
    tok = pl.BlockSpec((1, ts, w), lambda b, i: (b, i, 0))
    return pl.pallas_call(
        functools.partial(_rnn_kernel, ts=ts),
        grid=(bsz, s // ts),
        in_specs=[tok, tok, _const_spec(w_conv.shape), _const_spec((1, w)), _const_spec(wbd.shape),
                  _const_spec((1, w)), _const_spec((1, w)), _const_spec((1, w))],
        out_specs=tok,
        out_shape=jax.ShapeDtypeStruct((bsz, s, w), BF16),
        scratch_shapes=[pltpu.VMEM((ts + SUBLANES, w), F32), pltpu.VMEM((SUBLANES, w), F32),
                        pltpu.VMEM((ts, w), F32), pltpu.VMEM((ts, w), F32), pltpu.VMEM((ts, w), F32)],
        compiler_params=_params("arbitrary", "arbitrary"),
        name="rglru",
    )(x_rnn, g_rnn, w_conv, b_conv, wbd, b_rg, b_ig, lam)


WINDOW_SHIFT = 5
WINDOW_ROWS = 1 << WINDOW_SHIFT


def _route(h2, wr_hi_ref, wr_mid_ref, rbias_ref, cnt_ref, lpos_ref, before_ref, ntile_ref, wtok_ref, j, tm):
    h_hi, h_mid, _ = _split3(h2)
    logits = _dot_nt(wr_hi_ref[...], h_hi) + _dot_nt(wr_hi_ref[...], h_mid) + _dot_nt(wr_mid_ref[...], h_hi)
    scores = _sigmoid(logits)
    sel = scores + rbias_ref[...]

    giota = lax.broadcasted_iota(I32, (N_GROUPS, tm), 0)
    gs = jnp.zeros((N_GROUPS, tm), F32)
    for g in range(N_GROUPS):
        blk = sel[g * GROUP_SIZE:(g + 1) * GROUP_SIZE, :]
        m1 = jnp.max(blk, axis=0, keepdims=True)
        i1 = jnp.min(jnp.where(blk == m1, giota, N_EXPERTS), axis=0, keepdims=True)
        m2 = jnp.max(jnp.where(giota == i1, -jnp.inf, blk), axis=0, keepdims=True)
        gs = jnp.where(giota == g, m1 + m2, gs)

    gsel = jnp.zeros((N_GROUPS, tm), F32)
    for _ in range(TOPK_GROUPS):
        m = jnp.max(gs, axis=0, keepdims=True)
        idx = jnp.min(jnp.where(gs == m, giota, N_EXPERTS), axis=0, keepdims=True)
        hit = giota == idx
        gsel = jnp.where(hit, 1.0, gsel)
        gs = jnp.where(hit, -jnp.inf, gs)

    masked = jnp.concatenate(
        [jnp.where(gsel[g:g + 1, :] > 0.0, sel[g * GROUP_SIZE:(g + 1) * GROUP_SIZE, :], -jnp.inf)
         for g in range(N_GROUPS)], axis=0)

    eiota = lax.broadcasted_iota(I32, (N_EXPERTS, tm), 0)
    chosen = jnp.zeros((N_EXPERTS, tm), F32)
    idxs, wts = [], []
    for _ in range(TOP_K):
        m = jnp.max(masked, axis=0, keepdims=True)
        idx = jnp.min(jnp.where(masked == m, eiota, N_EXPERTS), axis=0, keepdims=True)
        hit = eiota == idx
        wts.append(jnp.sum(jnp.where(hit, scores, 0.0), axis=0, keepdims=True))
        idxs.append(idx)
        chosen = jnp.where(hit, 1.0, chosen)
        masked = jnp.where(hit, -jnp.inf, masked)
    wsum = wts[0]
    for w in wts[1:]:
        wsum = wsum + w

    srow = lax.broadcasted_iota(I32, (tm, tm), 0)
    scol = lax.broadcasted_iota(I32, (tm, tm), 1)
    earlier = (srow < scol).astype(BF16)
    local_rank = _dot(chosen.astype(BF16), earlier)
    n_e = jnp.sum(chosen, axis=1, keepdims=True)
    before_ref[j] = cnt_ref[...].astype(I32)
    ntile_ref[j] = n_e.astype(I32)
    cnt_ref[...] = cnt_ref[...] + n_e
    n_pad = (((n_e.astype(I32) + (WINDOW_ROWS - 1)) >> WINDOW_SHIFT) << WINDOW_SHIFT).astype(F32)
    erow = lax.broadcasted_iota(I32, (N_EXPERTS, N_EXPERTS), 0)
    ecol = lax.broadcasted_iota(I32, (N_EXPERTS, N_EXPERTS), 1)
    off = _dot((ecol < erow).astype(BF16), jnp.broadcast_to(n_pad, (N_EXPERTS, LANES)).astype(BF16))[:, :1]
    local_pos = off + local_rank

    riota = lax.broadcasted_iota(I32, (SUBLANES, tm), 0)
    lpos = jnp.zeros((SUBLANES, tm), I32)
    wrow = jnp.zeros((SUBLANES, tm), F32)
    for r in range(TOP_K):
        lp = jnp.sum(jnp.where(eiota == idxs[r], local_pos, 0.0), axis=0, keepdims=True)
        lpos = jnp.where(riota == r, lp.astype(I32), lpos)
        wrow = jnp.where(riota == r, wts[r] / wsum * ROUTE_SCALE, wrow)
    lpos_ref[j] = lpos
    wpad = jnp.concatenate([wrow, jnp.zeros((LANES - SUBLANES, tm), F32)], axis=0)
    wtok_ref[j * tm:(j + 1) * tm, :] = wpad.T


U32 = jnp.uint32
PACK_ROWS = D_MODEL // (2 * LANES)
HIGH_HALF = 0xFFFF0000


def _store_token_rows(rows_ref, val):
    half = D_MODEL // 2
    n = val.shape[0]
    bits = lax.bitcast_convert_type(val.astype(BF16).astype(F32), U32)
    words = (bits[:, half:] & jnp.uint32(HIGH_HALF)) | (bits[:, :half] >> 16)
    for c in range(PACK_ROWS):
        chunk = words[:, c * LANES:(c + 1) * LANES]
        if len(rows_ref.shape) == 2:
            rows_ref[pl.ds(c, n, stride=PACK_ROWS), :] = chunk
        else:
            rows_ref[:, c, :] = chunk


def _load_token_rows(rows_ref):
    if len(rows_ref.shape) == 2:
        n = rows_ref.shape[0] // PACK_ROWS
        chunks = [rows_ref[pl.ds(c, n, stride=PACK_ROWS), :] for c in range(PACK_ROWS)]
    else:
        chunks = [rows_ref[:, c, :] for c in range(PACK_ROWS)]
    words = jnp.concatenate(chunks, axis=-1)
    lo = lax.bitcast_convert_type(words << 16, F32)
    hi = lax.bitcast_convert_type(words & jnp.uint32(HIGH_HALF), F32)
    return jnp.concatenate([lo, hi], axis=-1)


def _merge_kernel(ya_ref, yr_ref, ga_ref, gb_ref, x_ref, gm_ref, shf_ref, scf_ref, gpost_ref, gpre_ref,
                  wba_ref, wbr_ref, wout_ref, wr_hi_ref, wr_mid_ref, rbias_ref, wsg_ref, wsu_ref, wsd_ref,
                  x1_ref, shared_ref, h2rows_ref, lpos_ref, before_ref, ntile_ref, wtok_ref, cnt_ref, *, tm, tr):
    @pl.when((pl.program_id(0) == 0) & (pl.program_id(1) == 0))
    def _():
        cnt_ref[...] = jnp.zeros(cnt_ref.shape, F32)

    pa = _dot(ya_ref[0], wba_ref[...])
    pr = _dot(yr_ref[0], wbr_ref[...])
    merged = _sigmoid(ga_ref[0].astype(F32)) * pa + _sigmoid(gb_ref[0].astype(F32)) * pr
    o = _dot(merged.astype(BF16), wout_ref[...])
    x1 = x_ref[0] + gm_ref[0] * (_rms_scale(o) * gpost_ref[...])
    x1_ref[0] = x1
    h2 = (_rms_scale(x1) * gpre_ref[...]) * (1.0 + scf_ref[0]) + shf_ref[0]
    hb = h2.astype(BF16)
    a = _dot(hb, wsg_ref[...])
    u = _dot(hb, wsu_ref[...])
    shared_ref[0] = _dot(((a * _sigmoid(a)) * u).astype(BF16), wsd_ref[...]).astype(BF16)
    _store_token_rows(h2rows_ref, h2)
    for j in range(tm // tr):
        _route(h2[j * tr:(j + 1) * tr, :], wr_hi_ref, wr_mid_ref, rbias_ref, cnt_ref,
               lpos_ref, before_ref, ntile_ref, wtok_ref, j, tr)


def _merge_call(y_attn, y_rnn, gate_a, gate_r, x, gate_m, shift_f, scale_f, g_post, g_pre,
                wba, wbr, wout, wr_hi, wr_mid, rbias, wsg, wsu, wsd, tm, tr):
    bsz, s, d = x.shape
    nt = s // tm
    nr = tm // tr
    tok = lambda w: pl.BlockSpec((1, tm, w), lambda b, i: (b, i, 0))
    mod = pl.BlockSpec((1, 1, d), lambda b, i: (b, 0, 0))
    return pl.pallas_call(
        functools.partial(_merge_kernel, tm=tm, tr=tr),
        grid=(bsz, nt),
        in_specs=[tok(ATT_WIDTH), tok(d), tok(d), tok(d), tok(d), mod, mod, mod,
                  _const_spec((1, d)), _const_spec((1, d)),
                  _const_spec(wba.shape), _const_spec(wbr.shape), _const_spec(wout.shape),
                  _const_spec(wr_hi.shape), _const_spec(wr_mid.shape), _const_spec(rbias.shape),
                  _const_spec(wsg.shape), _const_spec(wsu.shape), _const_spec(wsd.shape)],
        out_specs=[tok(d), tok(d),
                   pl.BlockSpec((tm * PACK_ROWS, LANES), lambda b, i: (b * nt + i, 0)),
                   pl.BlockSpec((nr, SUBLANES, tr), lambda b, i: (b * nt + i, 0, 0)),
                   pl.BlockSpec((nr, N_EXPERTS, 1), lambda b, i: (b * nt + i, 0, 0)),
                   pl.BlockSpec((nr, N_EXPERTS, 1), lambda b, i: (b * nt + i, 0, 0)),
                   pl.BlockSpec((tm, LANES), lambda b, i: (b * nt + i, 0)),
                   pl.BlockSpec((N_EXPERTS, 1), lambda b, i: (0, 0))],
        out_shape=[jax.ShapeDtypeStruct((bsz, s, d), F32), jax.ShapeDtypeStruct((bsz, s, d), BF16),
                   jax.ShapeDtypeStruct((bsz * s * PACK_ROWS, LANES), U32),
                   jax.ShapeDtypeStruct((bsz * nt * nr, SUBLANES, tr), I32),
                   jax.ShapeDtypeStruct((bsz * nt * nr, N_EXPERTS, 1), I32),
                   jax.ShapeDtypeStruct((bsz * nt * nr, N_EXPERTS, 1), I32),
                   jax.ShapeDtypeStruct((bsz * s, LANES), F32),
                   jax.ShapeDtypeStruct((N_EXPERTS, 1), F32)],
        compiler_params=_params("arbitrary", "arbitrary"),
        name="merge_route",
    )(y_attn, y_rnn, gate_a, gate_r, x, gate_m, shift_f, scale_f, g_post, g_pre,
      wba, wbr, wout, wr_hi, wr_mid, rbias, wsg, wsu, wsd)


WIN_LIST = LANES
ROW_COPY_UNROLL = 8


def _local_rows(tm):
    rows = tm * TOP_K + N_EXPERTS * WINDOW_ROWS
    assert rows // WINDOW_ROWS < WIN_LIST
    return rows


def _for_each_window(wlist_ref, fn):
    def body(w, c):
        fn(pl.multiple_of(w * WINDOW_ROWS, WINDOW_ROWS), wlist_ref[0, 0, w])
        return c

    lax.fori_loop(0, wlist_ref[0, 0, WIN_LIST - 1], body, 0)


def _dispatch_kernel(lpos_ref, wlist_ref, wlistp_ref, zfrom_ref, zto_ref, h_ref, xs_hbm, loc_sc, zero_sc, sem,
                     *, tm):
    g = pl.program_id(0)
    cur = lax.rem(g, 2)

    @pl.when(g == 0)
    def _():
        loc_sc[...] = jnp.zeros(loc_sc.shape, U32)
        zero_sc[...] = jnp.zeros(zero_sc.shape, U32)

        def for_each_zero_window(fn):
            def per_range(j, c):
                def per_window(w, c2):
                    r0 = pl.multiple_of(zfrom_ref[j] + w * WINDOW_ROWS, WINDOW_ROWS)
                    fn(pltpu.make_async_copy(zero_sc, xs_hbm.at[pl.ds(r0, WINDOW_ROWS)], sem))
                    return c2
                lax.fori_loop(0, (zto_ref[j] - zfrom_ref[j]) >> WINDOW_SHIFT, per_window, 0)
                return c
            lax.fori_loop(0, zfrom_ref.shape[0], per_range, 0)

        for_each_zero_window(lambda cp: cp.start())
        for_each_zero_window(lambda cp: cp.wait())

    for t in range(tm):
        row = h_ref[t]
        for k in range(TOP_K):
            loc_sc[lpos_ref[0, k, t]] = row

    half = _local_rows(tm)

    def window(base):
        return lambda row0, slot0: pltpu.make_async_copy(
            loc_sc.at[pl.ds(pl.multiple_of(base + row0, WINDOW_ROWS), WINDOW_ROWS)],
            xs_hbm.at[pl.ds(slot0, WINDOW_ROWS)], sem)

    @pl.when(g > 0)
    def _():
        _for_each_window(wlistp_ref, lambda r, s: window((1 - cur) * half)(r, s).wait())

    _for_each_window(wlist_ref, lambda r, s: window(cur * half)(r, s).start())

    @pl.when(g == pl.num_programs(0) - 1)
    def _():
        _for_each_window(wlist_ref, lambda r, s: window(cur * half)(r, s).wait())


def _dispatch_call(lpos, wlist, zero_from, zero_to, h2_rows, n_rows, tm):
    t = h2_rows.shape[0]
    smem = functools.partial(pl.BlockSpec, memory_space=pltpu.SMEM)
    return pl.pallas_call(
        functools.partial(_dispatch_kernel, tm=tm),
        grid=(t // tm,),
        in_specs=[smem((1, SUBLANES, tm), lambda i: (i, 0, 0)),
                  smem((1, 1, WIN_LIST), lambda i: (i, 0, 0)),
                  smem((1, 1, WIN_LIST), lambda i: (jnp.maximum(i - 1, 0), 0, 0)),
                  smem(), smem(),
                  pl.BlockSpec((tm, PACK_ROWS, LANES), lambda i: (i, 0, 0))],
        out_specs=pl.BlockSpec(memory_space=pl.ANY),
        out_shape=jax.ShapeDtypeStruct((n_rows, PACK_ROWS, LANES), U32),
        scratch_shapes=[pltpu.VMEM((2 * _local_rows(tm), PACK_ROWS, LANES), U32),
                        pltpu.VMEM((WINDOW_ROWS, PACK_ROWS, LANES), U32), pltpu.SemaphoreType.DMA],
        compiler_params=_params("arbitrary"),
        name="moe_dispatch",
    )(lpos, wlist, wlist, zero_from, zero_to, h2_rows)


def _expert_kernel(be_ref, nused_ref, xs_ref, wg_ref, wu_ref, wd_ref, ys_ref, wgb_sc, wub_sc, wdb_sc, *, rows):
    i = pl.program_id(0)

    @pl.when((i == 0) | (be_ref[i] != be_ref[jnp.maximum(i - 1, 0)]))
    def _():
        wgb_sc[...] = wg_ref[0].astype(BF16)
        wub_sc[...] = wu_ref[0].astype(BF16)
        wdb_sc[...] = wd_ref[0].astype(BF16)

    @pl.when(i < nused_ref[0])
    def _():
        xb = _load_token_rows(xs_ref).astype(BF16)
        g = _dot(xb, wgb_sc[...])
        u = _dot(xb, wub_sc[...])
        act = (g * _sigmoid(g)) * u
        _store_token_rows(ys_ref, _dot(act.astype(BF16), wdb_sc[...]))

    @pl.when(pl.program_id(0) >= nused_ref[0])
    def _():
        ys_ref[...] = jnp.zeros(ys_ref.shape, U32)


def _expert_call(block_e, n_used, xs, wg, wu, wd, rows):
    n_rows = xs.shape[0] // PACK_ROWS
    _, d, ff = wg.shape
    row_spec = pl.BlockSpec((rows * PACK_ROWS, LANES), lambda i, be, nu: (jnp.minimum(i, nu[0] - 1), 0))
    grid_spec = pltpu.PrefetchScalarGridSpec(
        num_scalar_prefetch=2,
        grid=(n_rows // rows,),
        in_specs=[row_spec,
                  pl.BlockSpec((1, d, ff), lambda i, be, nu: (be[i], 0, 0)),
                  pl.BlockSpec((1, d, ff), lambda i, be, nu: (be[i], 0, 0)),
                  pl.BlockSpec((1, ff, d), lambda i, be, nu: (be[i], 0, 0))],
        out_specs=pl.BlockSpec((rows * PACK_ROWS, LANES), lambda i, be, nu: (i, 0)),
        scratch_shapes=[pltpu.VMEM((d, ff), BF16), pltpu.VMEM((d, ff), BF16), pltpu.VMEM((ff, d), BF16)],
    )
    return pl.pallas_call(
        functools.partial(_expert_kernel, rows=rows),
        grid_spec=grid_spec,
        out_shape=jax.ShapeDtypeStruct(xs.shape, U32),
        compiler_params=_params("arbitrary"),
        name="moe_experts",
    )(block_e, n_used, xs, wg, wu, wd)


def _final_kernel(lpos0_ref, lposn_ref, tab0_ref, tabw_ref, tabi_ref, x1_ref, shared_ref, wtok_ref, gf_ref,
                  gpost_ref, ys_hbm, o_ref, win_sc, gnext_sc, gcur_sc, sem, *, tm):
    nt = pl.num_programs(1)
    g = pl.program_id(0) * nt + pl.program_id(1)
    n_tiles = pl.num_programs(0) * nt
    buf_next = lax.rem(g + 1, 2)
    buf_after = lax.rem(g, 2)
    half = _local_rows(tm)

    def window(buf):
        return lambda row0, slot0: pltpu.make_async_copy(
            ys_hbm.at[pl.ds(slot0, WINDOW_ROWS)],
            win_sc.at[pl.ds(pl.multiple_of(buf * half + row0, WINDOW_ROWS), WINDOW_ROWS)], sem.at[buf])

    def start_windows(tab_ref, buf):
        _for_each_window(tab_ref, lambda r, s: window(buf)(r, s).start())

    def wait_windows(tab_ref, buf):
        _for_each_window(tab_ref, lambda r, s: window(buf)(r, s).wait())

    def pick_rows(lpos_ref, t):
        for k in range(TOP_K):
            gnext_sc[k, t] = win_sc[lpos_ref[0, k, t]]

    def stage():
        for k in range(TOP_K):
            gcur_sc[k] = gnext_sc[k].reshape(tm * PACK_ROWS, LANES)

    @pl.when(g == 0)
    def _():
        start_windows(tab0_ref, 0)
        wait_windows(tab0_ref, 0)

        def body(i, c):
            for u in range(ROW_COPY_UNROLL):
                pick_rows(lpos0_ref, i * ROW_COPY_UNROLL + u)
            return c

        lax.fori_loop(0, tm // ROW_COPY_UNROLL, body, 0)
        stage()

        @pl.when(n_tiles > 1)
        def _():
            start_windows(tabw_ref, 1)

    @pl.when(g + 1 < n_tiles)
    def _():
        wait_windows(tabw_ref, buf_next)

    @pl.when(g + 2 < n_tiles)
    def _():
        start_windows(tabi_ref, buf_after)

    for t in range(tm):
        pick_rows(lposn_ref, t)

    w = wtok_ref[...]
    routed = w[:, 0:1] * _load_token_rows(gcur_sc.at[0])
    for k in range(1, TOP_K):
        routed = routed + w[:, k:k + 1] * _load_token_rows(gcur_sc.at[k])
    tot = routed + shared_ref[0].astype(F32)
    o_ref[0] = x1_ref[0] + gf_ref[0] * (_rms_scale(tot) * gpost_ref[...])

    stage()


def _final_call(lpos, tab, x1, shared, wtok, gate_f, g_post, ys, tm):
    bsz, s, d = x1.shape
    nt = s // tm
    n_tiles = bsz * nt
    smem = functools.partial(pl.BlockSpec, memory_space=pltpu.SMEM)
    tok = pl.BlockSpec((1, tm, d), lambda b, i: (b, i, 0))
    tab_shape = (1, 1, WIN_LIST)
    ahead = lambda k: (lambda b, i: (jnp.minimum(b * nt + i + k, n_tiles - 1), 0, 0))
    return pl.pallas_call(
        functools.partial(_final_kernel, tm=tm),
        grid=(bsz, nt),
        in_specs=[smem((1, SUBLANES, tm), lambda b, i: (0, 0, 0)),
                  smem((1, SUBLANES, tm), ahead(1)),
                  smem(tab_shape, lambda b, i: (0, 0, 0)),
                  smem(tab_shape, ahead(1)),
                  smem(tab_shape, ahead(2)),
                  tok, tok,
                  pl.BlockSpec((tm, LANES), lambda b, i: (b * nt + i, 0)),
                  pl.BlockSpec((1, 1, d), lambda b, i: (b, 0, 0)),
                  _const_spec((1, d)),
                  pl.BlockSpec(memory_space=pl.ANY)],
        out_specs=tok,
        out_shape=jax.ShapeDtypeStruct((bsz, s, d), F32),
        scratch_shapes=[pltpu.VMEM((2 * _local_rows(tm), PACK_ROWS, LANES), U32),
                        pltpu.VMEM((TOP_K, tm, PACK_ROWS, LANES), U32),
                        pltpu.VMEM((TOP_K, tm * PACK_ROWS, LANES), U32), pltpu.SemaphoreType.DMA((2,))],
        compiler_params=_params("arbitrary", "arbitrary"),
        name="moe_combine",
    )(lpos, lpos, tab, tab, tab, x1, shared, wtok, gate_f, g_post, ys)


def _block_diag_tiles(w):
    per = MXU_DIM // RNN_BLOCK_DIM
    nt = RNN_BLOCKS // per
    w4 = w.reshape(nt, per, RNN_BLOCK_DIM, RNN_BLOCK_DIM)
    eye = jnp.eye(per, dtype=w.dtype)
    return jnp.einsum("tnij,nm->tnimj", w4, eye).reshape(nt, MXU_DIM, MXU_DIM)


def _tile(n, pref):
    t = min(n, pref)
    assert n % t == 0, (n, t)
    return t


def _layer(x, c, w_ada, b_ada, g_pre_mix, g_post_mix, g_pre_ffn, g_post_ffn, w_in, b_forget,
           w_conv, b_conv, w_rg, b_rg, w_ig, b_ig, lam, w_branch_attn, w_branch_rnn, w_out,
           w_router, router_bias, w_exp_gate, w_exp_up, w_exp_down, w_sh_gate, w_sh_up, w_sh_down):
    bsz, s, d = x.shape
    t = bsz * s
    row = lambda v: v.reshape(1, -1)

    mod = _ada_call(c, w_ada, b_ada)
    shift_m, scale_m, gate_m, shift_f, scale_f, gate_f = [
        mod[:, j * d:(j + 1) * d].reshape(bsz, 1, d) for j in range(6)]

    o_f = 3 * ATT_WIDTH
    o_r = o_f + ATT_HEADS
    wqkv = w_in[:, :o_f].astype(BF16)
    wf = jnp.pad(w_in[:, o_f:o_r], ((0, 0), (0, LANES - ATT_HEADS))).astype(BF16)
    wr = w_in[:, o_r:].astype(BF16)
    bf_pad = jnp.pad(b_forget, (0, LANES - ATT_HEADS)).reshape(1, LANES)
    q, k, v, kbias, x_rnn, g_rnn, gate_a, gate_r = _inproj_call(
        x, shift_m, scale_m, row(g_pre_mix), wqkv, wf, wr, bf_pad, _tile(s, 512))
    tq = _tile(s, 512)
    y_attn = _attn_call(q, k, v, kbias, tq, _tile(tq, 512))

    wbd = jnp.concatenate([_block_diag_tiles(w_rg), _block_diag_tiles(w_ig)], axis=-1).astype(BF16)
    y_rnn = _rnn_call(x_rnn, g_rnn, w_conv, row(b_conv), wbd, row(b_rg), row(b_ig), row(lam), _tile(s, 256))

    wr_t = w_router.T
    wr_hi = wr_t.astype(BF16)
    wr_mid = (wr_t - wr_hi.astype(F32)).astype(BF16)
    tm_m = _tile(s, 512)
    tm_r = _tile(tm_m, 256)
    x1, shared, h2_rows, lpos, before, n_tile, wtok, counts = _merge_call(
        y_attn, y_rnn, gate_a, gate_r, x, gate_m, shift_f, scale_f, row(g_post_mix), row(g_pre_ffn),
        w_branch_attn.astype(BF16), w_branch_rnn.astype(BF16), w_out.astype(BF16),
        wr_hi, wr_mid, router_bias.reshape(N_EXPERTS, 1),
        w_sh_gate.astype(BF16), w_sh_up.astype(BF16), w_sh_down.astype(BF16), tm_m, tm_r)

    rows = 1024
    n_blocks =-(-(t * TOP_K + N_EXPERTS * WINDOW_ROWS) // rows) + N_EXPERTS
    cnt = counts[:, 0].astype(I32)
    padded = ((cnt + WINDOW_ROWS + rows - 1) // rows) * rows
    pend = jnp.cumsum(padded).astype(I32)
    pstart = pend - padded
    n_used = jnp.maximum(pend[-1] // rows, 1)
    blk = jnp.minimum(jnp.arange(n_blocks, dtype=I32), n_used - 1)
    block_e = jnp.minimum(jnp.sum((pend[None, :] <= (blk * rows)[:, None]).astype(I32), axis=1), N_EXPERTS - 1)

    n_win = (n_tile[:, :, 0] + WINDOW_ROWS - 1) // WINDOW_ROWS
    win_end = jnp.cumsum(n_win, axis=1)
    w_id = jnp.arange(WIN_LIST, dtype=I32)[None, :, None]
    owner = (jnp.sum((win_end[:, None, :] <= w_id).astype(I32), axis=2, keepdims=True)
             == jnp.arange(N_EXPERTS, dtype=I32)[None, None, :]).astype(I32)
    first_slot = pstart[None, :] + before[:, :, 0] - (win_end - n_win) * WINDOW_ROWS
    wlist = jnp.sum(owner * first_slot[:, None, :], axis=2) + w_id[:, :, 0] * WINDOW_ROWS
    wlist = jnp.where(w_id[:, :, 0] == WIN_LIST - 1, win_end[:, -1:], wlist).astype(I32)[:, None, :]
    n_rt = lpos.shape[0]
    lpos = lpos + (jnp.arange(n_rt, dtype=I32) % 2)[:, None, None] * _local_rows(tm_r)

    rows3 = lambda a: a.reshape(-1, PACK_ROWS, LANES)
    rows2 = lambda a: a.reshape(-1, LANES)
    zero_from = jnp.concatenate([pstart + cnt // WINDOW_ROWS * WINDOW_ROWS, pend[-1:]]).astype(I32)
    zero_to = jnp.concatenate([pend, jnp.full((1,), n_blocks * rows, I32)]).astype(I32)
    xs = _dispatch_call(lpos, wlist, zero_from, zero_to, rows3(h2_rows), n_blocks * rows, tm_r)
    ys = rows3(_expert_call(block_e, n_used.reshape(1), rows2(xs),
                            w_exp_gate, w_exp_up, w_exp_down, rows))
    return _final_call(lpos, wlist, x1, shared, wtok, gate_f, row(g_post_ffn), ys, tm_r)


def kernel(x, c, w_ada, b_ada, g_pre_mix, g_post_mix, g_pre_ffn, g_post_ffn, w_in, b_forget, w_conv, b_conv, w_rg, b_rg, w_ig, b_ig, rglru_lambda, w_branch_attn, w_branch_rnn, w_out, w_router, router_bias, w_exp_gate, w_exp_up, w_exp_down, w_sh_gate, w_sh_up, w_sh_down):
    depth = w_ada.shape[0]
    for l in range(depth):
        x = _layer(x, c, w_ada[l], b_ada[l], g_pre_mix[l], g_post_mix[l], g_pre_ffn[l], g_post_ffn[l],
                   w_in[l], b_forget[l], w_conv[l], b_conv[l], w_rg[l], b_rg[l], w_ig[l], b_ig[l],
                   rglru_lambda[l], w_branch_attn[l], w_branch_rnn[l], w_out[l], w_router[l],
                   router_bias[l], w_exp_gate[l], w_exp_up[l], w_exp_down[l],
                   w_sh_gate[l], w_sh_up[l], w_sh_down[l])
    return x
```

```python
import functools

import jax
import jax.numpy as jnp
from jax import lax
from jax.experimental import pallas as pl
from jax.experimental.pallas import tpu as pltpu

F32 = jnp.float32
BF16 = jnp.bfloat16
I32 = jnp.int32

D_MODEL = 1024
ATT_HEADS = 8
ATT_HEAD_DIM = 64
ATT_WIDTH = ATT_HEADS * ATT_HEAD_DIM
RNN_WIDTH = D_MODEL
RNN_BLOCKS = 16
RNN_BLOCK_DIM = RNN_WIDTH // RNN_BLOCKS
CONV_WIDTH = 4
RGLRU_C = 8.0
N_EXPERTS = 64
TOP_K = 6
N_GROUPS = 8
GROUP_SIZE = N_EXPERTS // N_GROUPS
TOPK_GROUPS = 4
EXPERT_FF = D_MODEL // 4
ROUTE_SCALE = 2.5
NORM_EPS = 1e-6

LANES = 128
SUBLANES = 8
MXU_DIM = 256
VMEM_LIMIT_BYTES = 56 * 1024 * 1024

NEG_BIG = -1e30
RANK_BITS = 20
RANK_MASK = (1 << RANK_BITS) - 1


def _sigmoid(x):
    return 0.5 * jnp.tanh(0.5 * x) + 0.5


def _rms_scale(x):
    return x * lax.rsqrt(jnp.mean(x * x, axis=-1, keepdims=True) + NORM_EPS)


def _dot(a, b):
    return jnp.dot(a, b, preferred_element_type=F32)


def _dot_nt(a, b):
    return lax.dot_general(a, b, (((1,), (1,)), ((), ())), preferred_element_type=F32)


def _split3(a):
    hi = a.astype(BF16)
    r1 = a - hi.astype(F32)
    mid = r1.astype(BF16)
    lo = (r1 - mid.astype(F32)).astype(BF16)
    return hi, mid, lo


def _params(*sem):
    return pltpu.CompilerParams(dimension_semantics=sem, vmem_limit_bytes=VMEM_LIMIT_BYTES)


def _const_spec(shape):
    nd = len(shape)
    return pl.BlockSpec(shape, lambda *_: (0,) * nd)


def _ada_kernel(c_ref, w_ref, b_ref, o_ref):
    c = c_ref[...]
    sc = c * _sigmoid(c)
    s_hi, s_mid, _ = _split3(sc)
    w_hi, w_mid, _ = _split3(w_ref[...])
    acc = _dot(s_hi, w_hi) + _dot(s_hi, w_mid) + _dot(s_mid, w_hi)
    o_ref[...] = acc + b_ref[...]


def _ada_call(c, w_ada, b_ada):
    bsz, d = c.shape
    n = w_ada.shape[1]
    tn = d
    return pl.pallas_call(
        _ada_kernel,
        grid=(n // tn,),
        in_specs=[
            pl.BlockSpec((bsz, d), lambda j: (0, 0)),
            pl.BlockSpec((d, tn), lambda j: (0, j)),
            pl.BlockSpec((1, tn), lambda j: (0, j)),
        ],
        out_specs=pl.BlockSpec((bsz, tn), lambda j: (0, j)),
        out_shape=jax.ShapeDtypeStruct((bsz, n), F32),
        compiler_params=_params("arbitrary"),
        name="ada_mod",
    )(c, w_ada, b_ada.reshape(1, n))


N_BIAS_PIECES = 3
CUM_BLOCK = 256


def _bias_placement():
    h = jnp.arange(LANES)[:, None]
    lane = jnp.arange(ATT_WIDTH)[None, :]
    mats = []
    for piece in range(N_BIAS_PIECES):
        target = (h // 2) * LANES + ATT_HEAD_DIM * (1 - h % 2) + piece
        mats.append(((lane == target) & (h < ATT_HEADS)).astype(BF16))
    return jnp.concatenate(mats, axis=0)


def _key_bias(f_logit, bf_ref, place_ref, kb_ref, cum_sc):
    tm = f_logit.shape[0]
    blk = min(tm, CUM_BLOCK)
    row = lax.broadcasted_iota(I32, (blk, blk), 0)
    col = lax.broadcasted_iota(I32, (blk, blk), 1)
    tri = (row >= col).astype(BF16)
    carry = cum_sc[...]
    for i in range(tm // blk):
        z = f_logit[i * blk:(i + 1) * blk, :] + bf_ref[...]
        lf = jnp.minimum(z, 0.0) - jnp.log1p(jnp.exp(-jnp.abs(z)))
        c3 = _dot(tri, jnp.concatenate(_split3(lf), axis=1))
        c = c3[:, :LANES] + c3[:, LANES:2 * LANES] + c3[:, 2 * LANES:] + carry
        carry = c[blk - 1:blk, :]
        kb = _dot(jnp.concatenate(_split3(-c), axis=1), place_ref[...])
        kb_ref[0, i * blk:(i + 1) * blk, :] = kb.astype(BF16)
    cum_sc[...] = carry


def _inproj_kernel(x_ref, sh_ref, sc_ref, g_ref, wqkv_ref, wf_ref, wr_ref, bf_ref, place_ref,
                   q_ref, k_ref, v_ref, kb_ref, xr_ref, gr_ref, ga_ref, gb_ref, cum_sc):
    @pl.when(pl.program_id(1) == 0)
    def _():
        cum_sc[...] = jnp.zeros(cum_sc.shape, F32)

    x = x_ref[0]
    h = (_rms_scale(x) * g_ref[...]) * (1.0 + sc_ref[0]) + sh_ref[0]
    hb = h.astype(BF16)
    for j, ref in enumerate((q_ref, k_ref, v_ref)):
        ref[0] = _dot(hb, wqkv_ref[:, j * ATT_WIDTH:(j + 1) * ATT_WIDTH]).astype(BF16)
    _key_bias(_dot(hb, wf_ref[...]), bf_ref, place_ref, kb_ref, cum_sc)
    half = D_MODEL // 2
    for j, ref in enumerate((xr_ref, gr_ref, ga_ref, gb_ref)):
        for c in range(2):
            lo = j * D_MODEL + c * half
            ref[0, :, c * half:(c + 1) * half] = _dot(hb, wr_ref[:, lo:lo + half]).astype(BF16)


def _inproj_call(x, shift, scale, g, wqkv, wf, wr, b_forget_pad, tm):
    bsz, s, d = x.shape
    nt = s // tm
    tok = lambda w: pl.BlockSpec((1, tm, w), lambda b, i: (b, i, 0))
    mod = pl.BlockSpec((1, 1, d), lambda b, i: (b, 0, 0))
    place = _bias_placement()
    out_w = (ATT_WIDTH,) * 4 + (d,) * 4
    return pl.pallas_call(
        _inproj_kernel,
        grid=(bsz, nt),
        in_specs=[tok(d), mod, mod, _const_spec((1, d)),
                  _const_spec(wqkv.shape), _const_spec(wf.shape), _const_spec(wr.shape),
                  _const_spec((1, LANES)), _const_spec(place.shape)],
        out_specs=[tok(w) for w in out_w],
        out_shape=[jax.ShapeDtypeStruct((bsz, s, w), BF16) for w in out_w],
        scratch_shapes=[pltpu.VMEM((1, LANES), F32)],
        compiler_params=_params("arbitrary", "arbitrary"),
        name="in_proj",
    )(x, shift, scale, g, wqkv, wf, wr, b_forget_pad, place)


ATT_VT_ROWS = ATT_HEAD_DIM + 16


def _attn_kernel(q_ref, k_ref, v_ref, kb_ref, o_ref, vt_sc, st_sc, m_sc, acc_sc, *, tq, tk):
    s = q_ref.shape[1]
    nq, nk, ratio = s // tq, s // tk, tq // tk
    lane = lax.broadcasted_iota(I32, (1, LANES), 1)
    head_lanes = (lane < ATT_HEAD_DIM, lane >= ATT_HEAD_DIM)
    one_lanes = ((lane >= ATT_HEAD_DIM) & (lane < ATT_HEAD_DIM + N_BIAS_PIECES), lane < N_BIAS_PIECES)
    key_minus_query = (lax.broadcasted_iota(I32, (tk, tq), 0) - lax.broadcasted_iota(I32, (tk, tq), 1))

    vt = v_ref[0].astype(F32).T
    ones = jnp.ones((ATT_VT_ROWS - ATT_HEAD_DIM, s), F32)
    for h in range(2):
        vth = jnp.concatenate([vt[h * ATT_HEAD_DIM:(h + 1) * ATT_HEAD_DIM, :], ones], axis=0).astype(BF16)
        for j in range(nk):
            vt_sc[h, j] = vth[:, j * tk:(j + 1) * tk]

    def q_body(qi, carry):
        q0 = pl.multiple_of(qi * tq, tq)
        qs = q_ref[0, pl.ds(q0, tq), :] * (ATT_HEAD_DIM ** -0.5)
        qa = [jnp.where(head_lanes[h], qs, one_lanes[h].astype(BF16)) for h in range(2)]
        m_sc[...] = jnp.full(m_sc.shape, NEG_BIG, F32)
        acc_sc[...] = jnp.zeros(acc_sc.shape, F32)

        def scores(kj):
            k0 = pl.multiple_of(kj * tk, tk)
            kb = k_ref[0, pl.ds(k0, tk), :]
            bias = kb_ref[0, pl.ds(k0, tk), :]
            return [_dot_nt(jnp.where(head_lanes[h], kb, bias), qa[h]) for h in range(2)]

        def accumulate(kj, masked):
            for h in range(2):
                st = st_sc[h]
                if masked:
                    st = jnp.where(key_minus_query <= q0 - kj * tk, st, NEG_BIG)
                m_old = m_sc[h]
                m_new = jnp.maximum(m_old, jnp.max(st, axis=0, keepdims=True))
                p = jnp.exp(st - m_new).astype(BF16)
                acc_sc[h] = jnp.exp(m_old - m_new) * acc_sc[h] + _dot(vt_sc[h, kj], p)
                m_sc[h] = m_new

        def put(sts):
            for h in range(2):
                st_sc[h] = sts[h]

        n_full = qi * ratio
        put(scores(0))

        def kv_body(kj, c2):
            nxt = scores(kj + 1)
            accumulate(kj, False)
            put(nxt)
            return c2

        lax.fori_loop(0, n_full, kv_body, 0)
        for d in range(ratio):
            nxt = scores(n_full + d + 1) if d + 1 < ratio else None
            accumulate(n_full + d, True)
            if nxt is not None:
                put(nxt)

        outs = []
        for h in range(2):
            acc = acc_sc[h]
            outs.append(acc[:ATT_HEAD_DIM, :] * (1.0 / acc[ATT_HEAD_DIM:ATT_HEAD_DIM + 1, :]))
        o_ref[0, pl.ds(q0, tq), :] = jnp.concatenate(outs, axis=0).T.astype(BF16)
        return carry

    lax.fori_loop(0, nq, q_body, 0)


def _attn_call(q, k, v, kbias, tq, tk):
    bsz, s, _ = q.shape
    qkv = pl.BlockSpec((1, s, LANES), lambda b, p: (b, 0, p))
    return pl.pallas_call(
        functools.partial(_attn_kernel, tq=tq, tk=tk),
        grid=(bsz, ATT_HEADS // 2),
        in_specs=[qkv, qkv, qkv, qkv],
        out_specs=qkv,
        out_shape=jax.ShapeDtypeStruct((bsz, s, ATT_WIDTH), BF16),
        scratch_shapes=[pltpu.VMEM((2, s // tk, ATT_VT_ROWS, tk), BF16), pltpu.VMEM((2, tk, tq), F32),
                        pltpu.VMEM((2, 1, tq), F32), pltpu.VMEM((2, ATT_VT_ROWS, tq), F32)],
        compiler_params=_params("arbitrary", "arbitrary"),
        name="fox_attention",
    )(q, k, v, kbias)


def _gelu_tanh(x):
    c = 0.7978845608028654
    return 0.5 * x * (1.0 + jnp.tanh(c * (x + 0.044715 * (x * x * x))))


def _rnn_kernel(xr_ref, gr_ref, wc_ref, bc_ref, wbd_ref, brg_ref, big_ref, lam_ref, o_ref,
                xpad_sc, hc_sc, a_sc, b_sc, h_sc, *, ts):
    @pl.when(pl.program_id(1) == 0)
    def _():
        xpad_sc[:SUBLANES, :] = jnp.zeros((SUBLANES, RNN_WIDTH), F32)
        hc_sc[...] = jnp.zeros(hc_sc.shape, F32)

    x = xr_ref[0].astype(F32)
    xpad_sc[SUBLANES:, :] = x
    u = bc_ref[...] + wc_ref[3:4, :] * x
    for j in range(CONV_WIDTH - 1):
        off = SUBLANES - (CONV_WIDTH - 1) + j
        u = u + wc_ref[j:j + 1, :] * xpad_sc[off:off + ts, :]
    xpad_sc[:SUBLANES, :] = x[ts - SUBLANES:ts, :]

    lam = lam_ref[...]
    neg_sp = -(jnp.maximum(-lam, 0.0) + jnp.log1p(jnp.exp(-jnp.abs(lam))))
    for jt in range(RNN_WIDTH // MXU_DIM):
        cs = slice(jt * MXU_DIM, (jt + 1) * MXU_DIM)
        uc = u[:, cs]
        g = _dot(uc.astype(BF16), wbd_ref[jt])
        r = _sigmoid(g[:, :MXU_DIM] + brg_ref[:, cs])
        ig = _sigmoid(g[:, MXU_DIM:] + big_ref[:, cs])
        a = jnp.exp(r * (RGLRU_C * neg_sp[:, cs]))
        a_sc[:, cs] = a
        v = 1.0 - a * a
        b_sc[:, cs] = jnp.where(v > 0.0, v * lax.rsqrt(v), 0.0) * (ig * uc)

    srow = lax.broadcasted_iota(I32, (SUBLANES, RNN_WIDTH), 0)

    def scan_body(i, carry):
        r0 = pl.multiple_of(i * SUBLANES, SUBLANES)
        a = a_sc[pl.ds(r0, SUBLANES), :]
        b = b_sc[pl.ds(r0, SUBLANES), :]
        b = b + jnp.where(srow == 0, a * carry, 0.0)
        for sh in (1, 2, 4):
            keep = srow >= sh
            b = jnp.where(keep, b + a * pltpu.roll(b, sh, 0), b)
            if sh < SUBLANES // 2:
                a = jnp.where(keep, a * pltpu.roll(a, sh, 0), a)
        h_sc[pl.ds(r0, SUBLANES), :] = b
        return jnp.broadcast_to(b[SUBLANES - 1:SUBLANES, :], (SUBLANES, RNN_WIDTH))

    hc_sc[...] = lax.fori_loop(0, ts // SUBLANES, scan_body, hc_sc[...])
    o_ref[0] = (h_sc[...] * _gelu_tanh(gr_ref[0].astype(F32))).astype(BF16)


def _rnn_call(x_rnn, g_rnn, w_conv, b_conv, wbd, b_rg, b_ig, lam, ts):
    bsz, s, w = x_rnn.shape
    tok = pl.BlockSpec((1, ts, w), lambda b, i: (b, i, 0))
    return pl.pallas_call(
        functools.partial(_rnn_kernel, ts=ts),
        grid=(bsz, s // ts),
        in_specs=[tok, tok, _const_spec(w_conv.shape), _const_spec((1, w)), _const_spec(wbd.shape),
                  _const_spec((1, w)), _const_spec((1, w)), _const_spec((1, w))],
        out_specs=tok,
        out_shape=jax.ShapeDtypeStruct((bsz, s, w), BF16),
        scratch_shapes=[pltpu.VMEM((ts + SUBLANES, w), F32), pltpu.VMEM((SUBLANES, w), F32),
                        pltpu.VMEM((ts, w), F32), pltpu.VMEM((ts, w), F32), pltpu.VMEM((ts, w), F32)],
        compiler_params=_params("arbitrary", "arbitrary"),
        name="rglru",
    )(x_rnn, g_rnn, w_conv, b_conv, wbd, b_rg, b_ig, lam)


WINDOW_SHIFT = 5
WINDOW_ROWS = 1 << WINDOW_SHIFT


def _route(h2, wr_hi_ref, wr_mid_ref, rbias_ref, cnt_ref, lpos_ref, before_ref, ntile_ref, wtok_ref, j, tm):
    h_hi, h_mid, _ = _split3(h2)
    logits = _dot_nt(wr_hi_ref[...], h_hi) + _dot_nt(wr_hi_ref[...], h_mid) + _dot_nt(wr_mid_ref[...], h_hi)
    scores = _sigmoid(logits)
    sel = scores + rbias_ref[...]

    giota = lax.broadcasted_iota(I32, (N_GROUPS, tm), 0)
    gs = jnp.zeros((N_GROUPS, tm), F32)
    for g in range(N_GROUPS):
        blk = sel[g * GROUP_SIZE:(g + 1) * GROUP_SIZE, :]
        m1 = jnp.max(blk, axis=0, keepdims=True)
        i1 = jnp.min(jnp.where(blk == m1, giota, N_EXPERTS), axis=0, keepdims=True)
        m2 = jnp.max(jnp.where(giota == i1, -jnp.inf, blk), axis=0, keepdims=True)
        gs = jnp.where(giota == g, m1 + m2, gs)

    gsel = jnp.zeros((N_GROUPS, tm), F32)
    for _ in range(TOPK_GROUPS):
        m = jnp.max(gs, axis=0, keepdims=True)
        idx = jnp.min(jnp.where(gs == m, giota, N_EXPERTS), axis=0, keepdims=True)
        hit = giota == idx
        gsel = jnp.where(hit, 1.0, gsel)
        gs = jnp.where(hit, -jnp.inf, gs)

    masked = jnp.concatenate(
        [jnp.where(gsel[g:g + 1, :] > 0.0, sel[g * GROUP_SIZE:(g + 1) * GROUP_SIZE, :], -jnp.inf)
         for g in range(N_GROUPS)], axis=0)

    eiota = lax.broadcasted_iota(I32, (N_EXPERTS, tm), 0)
    chosen = jnp.zeros((N_EXPERTS, tm), F32)
    idxs, wts = [], []
    for _ in range(TOP_K):
        m = jnp.max(masked, axis=0, keepdims=True)
        idx = jnp.min(jnp.where(masked == m, eiota, N_EXPERTS), axis=0, keepdims=True)
        hit = eiota == idx
        wts.append(jnp.sum(jnp.where(hit, scores, 0.0), axis=0, keepdims=True))
        idxs.append(idx)
        chosen = jnp.where(hit, 1.0, chosen)
        masked = jnp.where(hit, -jnp.inf, masked)
    wsum = wts[0]
    for w in wts[1:]:
        wsum = wsum + w

    srow = lax.broadcasted_iota(I32, (tm, tm), 0)
    scol = lax.broadcasted_iota(I32, (tm, tm), 1)
    earlier = (srow < scol).astype(BF16)
    local_rank = _dot(chosen.astype(BF16), earlier)
    n_e = jnp.sum(chosen, axis=1, keepdims=True)
    before_ref[j] = cnt_ref[...].astype(I32)
    ntile_ref[j] = n_e.astype(I32)
    cnt_ref[...] = cnt_ref[...] + n_e
    n_pad = (((n_e.astype(I32) + (WINDOW_ROWS - 1)) >> WINDOW_SHIFT) << WINDOW_SHIFT).astype(F32)
    erow = lax.broadcasted_iota(I32, (N_EXPERTS, N_EXPERTS), 0)
    ecol = lax.broadcasted_iota(I32, (N_EXPERTS, N_EXPERTS), 1)
    off = _dot((ecol < erow).astype(BF16), jnp.broadcast_to(n_pad, (N_EXPERTS, LANES)).astype(BF16))[:, :1]
    local_pos = off + local_rank

    riota = lax.broadcasted_iota(I32, (SUBLANES, tm), 0)
    lpos = jnp.zeros((SUBLANES, tm), I32)
    wrow = jnp.zeros((SUBLANES, tm), F32)
    for r in range(TOP_K):
        lp = jnp.sum(jnp.where(eiota == idxs[r], local_pos, 0.0), axis=0, keepdims=True)
        lpos = jnp.where(riota == r, lp.astype(I32), lpos)
        wrow = jnp.where(riota == r, wts[r] / wsum * ROUTE_SCALE, wrow)
    lpos_ref[j] = lpos
    wpad = jnp.concatenate([wrow, jnp.zeros((LANES - SUBLANES, tm), F32)], axis=0)
    wtok_ref[j * tm:(j + 1) * tm, :] = wpad.T


U32 = jnp.uint32
PACK_ROWS = D_MODEL // (2 * LANES)
HIGH_HALF = 0xFFFF0000


def _store_token_rows(rows_ref, val):
    half = D_MODEL // 2
    n = val.shape[0]
    bits = lax.bitcast_convert_type(val.astype(BF16).astype(F32), U32)
    words = (bits[:, half:] & jnp.uint32(HIGH_HALF)) | (bits[:, :half] >> 16)
    for c in range(PACK_ROWS):
        chunk = words[:, c * LANES:(c + 1) * LANES]
        if len(rows_ref.shape) == 2:
            rows_ref[pl.ds(c, n, stride=PACK_ROWS), :] = chunk
        else:
            rows_ref[:, c, :] = chunk


def _load_token_rows(rows_ref):
    if len(rows_ref.shape) == 2:
        n = rows_ref.shape[0] // PACK_ROWS
        chunks = [rows_ref[pl.ds(c, n, stride=PACK_ROWS), :] for c in range(PACK_ROWS)]
    else:
        chunks = [rows_ref[:, c, :] for c in range(PACK_ROWS)]
    words = jnp.concatenate(chunks, axis=-1)
    lo = lax.bitcast_convert_type(words << 16, F32)
    hi = lax.bitcast_convert_type(words & jnp.uint32(HIGH_HALF), F32)
    return jnp.concatenate([lo, hi], axis=-1)


def _merge_kernel(ya_ref, yr_ref, ga_ref, gb_ref, x_ref, gm_ref, shf_ref, scf_ref, gpost_ref, gpre_ref,
                  wba_ref, wbr_ref, wout_ref, wr_hi_ref, wr_mid_ref, rbias_ref,
                  x1_ref, h2rows_ref, lpos_ref, before_ref, ntile_ref, wtok_ref, cnt_ref, *, tm, tr):
    @pl.when((pl.program_id(0) == 0) & (pl.program_id(1) == 0))
    def _():
        cnt_ref[...] = jnp.zeros(cnt_ref.shape, F32)

    pa = _dot(ya_ref[0], wba_ref[...])
    pr = _dot(yr_ref[0], wbr_ref[...])
    merged = _sigmoid(ga_ref[0].astype(F32)) * pa + _sigmoid(gb_ref[0].astype(F32)) * pr
    o = _dot(merged.astype(BF16), wout_ref[...])
    x1 = x_ref[0] + gm_ref[0] * (_rms_scale(o) * gpost_ref[...])
    x1_ref[0] = x1
    h2 = (_rms_scale(x1) * gpre_ref[...]) * (1.0 + scf_ref[0]) + shf_ref[0]
    _store_token_rows(h2rows_ref, h2)
    for j in range(tm // tr):
        _route(h2[j * tr:(j + 1) * tr, :], wr_hi_ref, wr_mid_ref, rbias_ref, cnt_ref,
               lpos_ref, before_ref, ntile_ref, wtok_ref, j, tr)


def _merge_call(y_attn, y_rnn, gate_a, gate_r, x, gate_m, shift_f, scale_f, g_post, g_pre,
                wba, wbr, wout, wr_hi, wr_mid, rbias, tm, tr):
    bsz, s, d = x.shape
    nt = s // tm
    nr = tm // tr
    tok = lambda w: pl.BlockSpec((1, tm, w), lambda b, i: (b, i, 0))
    mod = pl.BlockSpec((1, 1, d), lambda b, i: (b, 0, 0))
    return pl.pallas_call(
        functools.partial(_merge_kernel, tm=tm, tr=tr),
        grid=(bsz, nt),
        in_specs=[tok(ATT_WIDTH), tok(d), tok(d), tok(d), tok(d), mod, mod, mod,
                  _const_spec((1, d)), _const_spec((1, d)),
                  _const_spec(wba.shape), _const_spec(wbr.shape), _const_spec(wout.shape),
                  _const_spec(wr_hi.shape), _const_spec(wr_mid.shape), _const_spec(rbias.shape)],
        out_specs=[tok(d),
                   pl.BlockSpec((tm * PACK_ROWS, LANES), lambda b, i: (b * nt + i, 0)),
                   pl.BlockSpec((nr, SUBLANES, tr), lambda b, i: (b * nt + i, 0, 0)),
                   pl.BlockSpec((nr, N_EXPERTS, 1), lambda b, i: (b * nt + i, 0, 0)),
                   pl.BlockSpec((nr, N_EXPERTS, 1), lambda b, i: (b * nt + i, 0, 0)),
                   pl.BlockSpec((tm, LANES), lambda b, i: (b * nt + i, 0)),
                   pl.BlockSpec((N_EXPERTS, 1), lambda b, i: (0, 0))],
        out_shape=[jax.ShapeDtypeStruct((bsz, s, d), F32),
                   jax.ShapeDtypeStruct((bsz * s * PACK_ROWS, LANES), U32),
                   jax.ShapeDtypeStruct((bsz * nt * nr, SUBLANES, tr), I32),
                   jax.ShapeDtypeStruct((bsz * nt * nr, N_EXPERTS, 1), I32),
                   jax.ShapeDtypeStruct((bsz * nt * nr, N_EXPERTS, 1), I32),
                   jax.ShapeDtypeStruct((bsz * s, LANES), F32),
                   jax.ShapeDtypeStruct((N_EXPERTS, 1), F32)],
        compiler_params=_params("arbitrary", "arbitrary"),
        name="merge_route",
    )(y_attn, y_rnn, gate_a, gate_r, x, gate_m, shift_f, scale_f, g_post, g_pre,
      wba, wbr, wout, wr_hi, wr_mid, rbias)


WIN_LIST = LANES
ROW_COPY_UNROLL = 8


def _local_rows(tm):
    rows = tm * TOP_K + N_EXPERTS * WINDOW_ROWS
    assert rows // WINDOW_ROWS < WIN_LIST
    return rows


def _for_each_window(wlist_ref, fn):
    def body(w, c):
        fn(pl.multiple_of(w * WINDOW_ROWS, WINDOW_ROWS), wlist_ref[0, 0, w])
        return c

    lax.fori_loop(0, wlist_ref[0, 0, WIN_LIST - 1], body, 0)


def _dispatch_kernel(lpos_ref, wlist_ref, wlistp_ref, zfrom_ref, zto_ref, h_ref, h2d_ref, wsg_ref, wsu_ref, wsd_ref,
                     xs_hbm, shared_ref, loc_sc, zero_sc, sem, *, tm):
    g = pl.program_id(0)
    cur = lax.rem(g, 2)

    @pl.when(g == 0)
    def _():
        loc_sc[...] = jnp.zeros(loc_sc.shape, U32)
        zero_sc[...] = jnp.zeros(zero_sc.shape, U32)

        def for_each_zero_window(fn):
            def per_range(j, c):
                def per_window(w, c2):
                    r0 = pl.multiple_of(zfrom_ref[j] + w * WINDOW_ROWS, WINDOW_ROWS)
                    fn(pltpu.make_async_copy(zero_sc, xs_hbm.at[pl.ds(r0, WINDOW_ROWS)], sem))
                    return c2
                lax.fori_loop(0, (zto_ref[j] - zfrom_ref[j]) >> WINDOW_SHIFT, per_window, 0)
                return c
            lax.fori_loop(0, zfrom_ref.shape[0], per_range, 0)

        for_each_zero_window(lambda cp: cp.start())
        for_each_zero_window(lambda cp: cp.wait())

    for t in range(tm):
        row = h_ref[t]
        for k in range(TOP_K):
            loc_sc[lpos_ref[0, k, t]] = row

    hb = _load_token_rows(h2d_ref).astype(BF16)
    a = _dot(hb, wsg_ref[...])
    u = _dot(hb, wsu_ref[...])
    shared_ref[...] = _dot(((a * _sigmoid(a)) * u).astype(BF16), wsd_ref[...]).astype(BF16)

    half = _local_rows(tm)

    def window(base):
        return lambda row0, slot0: pltpu.make_async_copy(
            loc_sc.at[pl.ds(pl.multiple_of(base + row0, WINDOW_ROWS), WINDOW_ROWS)],
            xs_hbm.at[pl.ds(slot0, WINDOW_ROWS)], sem)

    @pl.when(g > 0)
    def _():
        _for_each_window(wlistp_ref, lambda r, s: window((1 - cur) * half)(r, s).wait())

    _for_each_window(wlist_ref, lambda r, s: window(cur * half)(r, s).start())

    @pl.when(g == pl.num_programs(0) - 1)
    def _():
        _for_each_window(wlist_ref, lambda r, s: window(cur * half)(r, s).wait())


def _dispatch_call(lpos, wlist, zero_from, zero_to, h2_rows, wsg, wsu, wsd, n_rows, tm):
    t = h2_rows.shape[0] // PACK_ROWS
    smem = functools.partial(pl.BlockSpec, memory_space=pltpu.SMEM)
    return pl.pallas_call(
        functools.partial(_dispatch_kernel, tm=tm),
        grid=(t // tm,),
        in_specs=[smem((1, SUBLANES, tm), lambda i: (i, 0, 0)),
                  smem((1, 1, WIN_LIST), lambda i: (i, 0, 0)),
                  smem((1, 1, WIN_LIST), lambda i: (jnp.maximum(i - 1, 0), 0, 0)),
                  smem(), smem(),
                  pl.BlockSpec((tm, PACK_ROWS, LANES), lambda i: (i, 0, 0)),
                  pl.BlockSpec((tm * PACK_ROWS, LANES), lambda i: (i, 0)),
                  _const_spec(wsg.shape), _const_spec(wsu.shape), _const_spec(wsd.shape)],
        out_specs=[pl.BlockSpec(memory_space=pl.ANY), pl.BlockSpec((tm, D_MODEL), lambda i: (i, 0))],
        out_shape=[jax.ShapeDtypeStruct((n_rows, PACK_ROWS, LANES), U32),
                   jax.ShapeDtypeStruct((t, D_MODEL), BF16)],
        scratch_shapes=[pltpu.VMEM((2 * _local_rows(tm), PACK_ROWS, LANES), U32),
                        pltpu.VMEM((WINDOW_ROWS, PACK_ROWS, LANES), U32), pltpu.SemaphoreType.DMA],
        compiler_params=_params("arbitrary"),
        name="moe_dispatch",
    )(lpos, wlist, wlist, zero_from, zero_to, h2_rows.reshape(-1, PACK_ROWS, LANES), h2_rows, wsg, wsu, wsd)


def _expert_kernel(be_ref, nused_ref, xs_ref, wg_ref, wu_ref, wd_ref, ys_ref, wgb_sc, wub_sc, wdb_sc, *, rows):
    i = pl.program_id(0)

    @pl.when((i == 0) | (be_ref[i] != be_ref[jnp.maximum(i - 1, 0)]))
    def _():
        wgb_sc[...] = wg_ref[0].astype(BF16)
        wub_sc[...] = wu_ref[0].astype(BF16)
        wdb_sc[...] = wd_ref[0].astype(BF16)

    @pl.when(i < nused_ref[0])
    def _():
        xb = _load_token_rows(xs_ref).astype(BF16)
        g = _dot(xb, wgb_sc[...])
        u = _dot(xb, wub_sc[...])
        act = (g * _sigmoid(g)) * u
        _store_token_rows(ys_ref, _dot(act.astype(BF16), wdb_sc[...]))

    @pl.when(pl.program_id(0) >= nused_ref[0])
    def _():
        ys_ref[...] = jnp.zeros(ys_ref.shape, U32)


def _expert_call(block_e, n_used, xs, wg, wu, wd, rows):
    n_rows = xs.shape[0] // PACK_ROWS
    _, d, ff = wg.shape
    row_spec = pl.BlockSpec((rows * PACK_ROWS, LANES), lambda i, be, nu: (jnp.minimum(i, nu[0] - 1), 0))
    grid_spec = pltpu.PrefetchScalarGridSpec(
        num_scalar_prefetch=2,
        grid=(n_rows // rows,),
        in_specs=[row_spec,
                  pl.BlockSpec((1, d, ff), lambda i, be, nu: (be[i], 0, 0)),
                  pl.BlockSpec((1, d, ff), lambda i, be, nu: (be[i], 0, 0)),
                  pl.BlockSpec((1, ff, d), lambda i, be, nu: (be[i], 0, 0))],
        out_specs=pl.BlockSpec((rows * PACK_ROWS, LANES), lambda i, be, nu: (i, 0)),
        scratch_shapes=[pltpu.VMEM((d, ff), BF16), pltpu.VMEM((d, ff), BF16), pltpu.VMEM((ff, d), BF16)],
    )
    return pl.pallas_call(
        functools.partial(_expert_kernel, rows=rows),
        grid_spec=grid_spec,
        out_shape=jax.ShapeDtypeStruct(xs.shape, U32),
        compiler_params=_params("arbitrary"),
        name="moe_experts",
    )(block_e, n_used, xs, wg, wu, wd)


def _final_kernel(lpos0_ref, lposn_ref, tab0_ref, tabw_ref, tabi_ref, x1_ref, shared_ref, wtok_ref, gf_ref,
                  gpost_ref, ys_hbm, o_ref, win_sc, gnext_sc, gcur_sc, sem, *, tm):
    nt = pl.num_programs(1)
    g = pl.program_id(0) * nt + pl.program_id(1)
    n_tiles = pl.num_programs(0) * nt
    buf_next = lax.rem(g + 1, 2)
    buf_after = lax.rem(g, 2)
    half = _local_rows(tm)

    def window(buf):
        return lambda row0, slot0: pltpu.make_async_copy(
            ys_hbm.at[pl.ds(slot0, WINDOW_ROWS)],
            win_sc.at[pl.ds(pl.multiple_of(buf * half + row0, WINDOW_ROWS), WINDOW_ROWS)], sem.at[buf])

    def start_windows(tab_ref, buf):
        _for_each_window(tab_ref, lambda r, s: window(buf)(r, s).start())

    def wait_windows(tab_ref, buf):
        _for_each_window(tab_ref, lambda r, s: window(buf)(r, s).wait())

    def pick_rows(lpos_ref, t):
        for k in range(TOP_K):
            gnext_sc[k, t] = win_sc[lpos_ref[0, k, t]]

    def stage():
        for k in range(TOP_K):
            gcur_sc[k] = gnext_sc[k].reshape(tm * PACK_ROWS, LANES)

    @pl.when(g == 0)
    def _():
        start_windows(tab0_ref, 0)
        wait_windows(tab0_ref, 0)

        def body(i, c):
            for u in range(ROW_COPY_UNROLL):
                pick_rows(lpos0_ref, i * ROW_COPY_UNROLL + u)
            return c

        lax.fori_loop(0, tm // ROW_COPY_UNROLL, body, 0)
        stage()

        @pl.when(n_tiles > 1)
        def _():
            start_windows(tabw_ref, 1)

    @pl.when(g + 1 < n_tiles)
    def _():
        wait_windows(tabw_ref, buf_next)

    @pl.when(g + 2 < n_tiles)
    def _():
        start_windows(tabi_ref, buf_after)

    for t in range(tm):
        pick_rows(lposn_ref, t)

    w = wtok_ref[...]
    routed = w[:, 0:1] * _load_token_rows(gcur_sc.at[0])
    for k in range(1, TOP_K):
        routed = routed + w[:, k:k + 1] * _load_token_rows(gcur_sc.at[k])
    tot = routed + shared_ref[0].astype(F32)
    o_ref[0] = x1_ref[0] + gf_ref[0] * (_rms_scale(tot) * gpost_ref[...])

    stage()


def _final_call(lpos, tab, x1, shared, wtok, gate_f, g_post, ys, tm):
    bsz, s, d = x1.shape
    nt = s // tm
    n_tiles = bsz * nt
    smem = functools.partial(pl.BlockSpec, memory_space=pltpu.SMEM)
    tok = pl.BlockSpec((1, tm, d), lambda b, i: (b, i, 0))
    tab_shape = (1, 1, WIN_LIST)
    ahead = lambda k: (lambda b, i: (jnp.minimum(b * nt + i + k, n_tiles - 1), 0, 0))
    return pl.pallas_call(
        functools.partial(_final_kernel, tm=tm),
        grid=(bsz, nt),
        in_specs=[smem((1, SUBLANES, tm), lambda b, i: (0, 0, 0)),
                  smem((1, SUBLANES, tm), ahead(1)),
                  smem(tab_shape, lambda b, i: (0, 0, 0)),
                  smem(tab_shape, ahead(1)),
                  smem(tab_shape, ahead(2)),
                  tok, tok,
                  pl.BlockSpec((tm, LANES), lambda b, i: (b * nt + i, 0)),
                  pl.BlockSpec((1, 1, d), lambda b, i: (b, 0, 0)),
                  _const_spec((1, d)),
                  pl.BlockSpec(memory_space=pl.ANY)],
        out_specs=tok,
        out_shape=jax.ShapeDtypeStruct((bsz, s, d), F32),
        scratch_shapes=[pltpu.VMEM((2 * _local_rows(tm), PACK_ROWS, LANES), U32),
                        pltpu.VMEM((TOP_K, tm, PACK_ROWS, LANES), U32),
                        pltpu.VMEM((TOP_K, tm * PACK_ROWS, LANES), U32), pltpu.SemaphoreType.DMA((2,))],
        compiler_params=_params("arbitrary", "arbitrary"),
        name="moe_combine",
    )(lpos, lpos, tab, tab, tab, x1, shared, wtok, gate_f, g_post, ys)


def _block_diag_tiles(w):
    per = MXU_DIM // RNN_BLOCK_DIM
    nt = RNN_BLOCKS // per
    w4 = w.reshape(nt, per, RNN_BLOCK_DIM, RNN_BLOCK_DIM)
    eye = jnp.eye(per, dtype=w.dtype)
    return jnp.einsum("tnij,nm->tnimj", w4, eye).reshape(nt, MXU_DIM, MXU_DIM)


def _tile(n, pref):
    t = min(n, pref)
    assert n % t == 0, (n, t)
    return t


def _layer(x, c, w_ada, b_ada, g_pre_mix, g_post_mix, g_pre_ffn, g_post_ffn, w_in, b_forget,
           w_conv, b_conv, w_rg, b_rg, w_ig, b_ig, lam, w_branch_attn, w_branch_rnn, w_out,
           w_router, router_bias, w_exp_gate, w_exp_up, w_exp_down, w_sh_gate, w_sh_up, w_sh_down):
    bsz, s, d = x.shape
    t = bsz * s
    row = lambda v: v.reshape(1, -1)

    mod = _ada_call(c, w_ada, b_ada)
    shift_m, scale_m, gate_m, shift_f, scale_f, gate_f = [
        mod[:, j * d:(j + 1) * d].reshape(bsz, 1, d) for j in range(6)]

    o_f = 3 * ATT_WIDTH
    o_r = o_f + ATT_HEADS
    wqkv = w_in[:, :o_f].astype(BF16)
    wf = jnp.pad(w_in[:, o_f:o_r], ((0, 0), (0, LANES - ATT_HEADS))).astype(BF16)
    wr = w_in[:, o_r:].astype(BF16)
    bf_pad = jnp.pad(b_forget, (0, LANES - ATT_HEADS)).reshape(1, LANES)
    q, k, v, kbias, x_rnn, g_rnn, gate_a, gate_r = _inproj_call(
        x, shift_m, scale_m, row(g_pre_mix), wqkv, wf, wr, bf_pad, _tile(s, 512))
    tq = _tile(s, 512)
    y_attn = _attn_call(q, k, v, kbias, tq, _tile(tq, 512))

    wbd = jnp.concatenate([_block_diag_tiles(w_rg), _block_diag_tiles(w_ig)], axis=-1).astype(BF16)
    y_rnn = _rnn_call(x_rnn, g_rnn, w_conv, row(b_conv), wbd, row(b_rg), row(b_ig), row(lam), _tile(s, 256))

    wr_t = w_router.T
    wr_hi = wr_t.astype(BF16)
    wr_mid = (wr_t - wr_hi.astype(F32)).astype(BF16)
    tm_m = _tile(s, 512)
    tm_r = _tile(tm_m, 256)
    x1, h2_rows, lpos, before, n_tile, wtok, counts = _merge_call(
        y_attn, y_rnn, gate_a, gate_r, x, gate_m, shift_f, scale_f, row(g_post_mix), row(g_pre_ffn),
        w_branch_attn.astype(BF16), w_branch_rnn.astype(BF16), w_out.astype(BF16),
        wr_hi, wr_mid, router_bias.reshape(N_EXPERTS, 1), tm_m, tm_r)

    rows = 1024
    n_blocks =-(-(t * TOP_K + N_EXPERTS * WINDOW_ROWS) // rows) + N_EXPERTS
    cnt = counts[:, 0].astype(I32)
    padded = ((cnt + WINDOW_ROWS + rows - 1) // rows) * rows
    pend = jnp.cumsum(padded).astype(I32)
    pstart = pend - padded
    n_used = jnp.maximum(pend[-1] // rows, 1)
    blk = jnp.minimum(jnp.arange(n_blocks, dtype=I32), n_used - 1)
    block_e = jnp.minimum(jnp.sum((pend[None, :] <= (blk * rows)[:, None]).astype(I32), axis=1), N_EXPERTS - 1)

    n_win = (n_tile[:, :, 0] + WINDOW_ROWS - 1) // WINDOW_ROWS
    win_end = jnp.cumsum(n_win, axis=1)
    w_id = jnp.arange(WIN_LIST, dtype=I32)[None, :, None]
    owner = (jnp.sum((win_end[:, None, :] <= w_id).astype(I32), axis=2, keepdims=True)
             == jnp.arange(N_EXPERTS, dtype=I32)[None, None, :]).astype(I32)
    first_slot = pstart[None, :] + before[:, :, 0] - (win_end - n_win) * WINDOW_ROWS
    wlist = jnp.sum(owner * first_slot[:, None, :], axis=2) + w_id[:, :, 0] * WINDOW_ROWS
    wlist = jnp.where(w_id[:, :, 0] == WIN_LIST - 1, win_end[:, -1:], wlist).astype(I32)[:, None, :]
    n_rt = lpos.shape[0]
    lpos = lpos + (jnp.arange(n_rt, dtype=I32) % 2)[:, None, None] * _local_rows(tm_r)

    rows3 = lambda a: a.reshape(-1, PACK_ROWS, LANES)
    rows2 = lambda a: a.reshape(-1, LANES)
    zero_from = jnp.concatenate([pstart + cnt // WINDOW_ROWS * WINDOW_ROWS, pend[-1:]]).astype(I32)
    zero_to = jnp.concatenate([pend, jnp.full((1,), n_blocks * rows, I32)]).astype(I32)
    xs, shared = _dispatch_call(lpos, wlist, zero_from, zero_to, h2_rows,
                                w_sh_gate.astype(BF16), w_sh_up.astype(BF16), w_sh_down.astype(BF16),
                                n_blocks * rows, tm_r)
    ys = rows3(_expert_call(block_e, n_used.reshape(1), rows2(xs),
                            w_exp_gate, w_exp_up, w_exp_down, rows))
    return _final_call(lpos, wlist, x1, shared.reshape(bsz, s, d), wtok, gate_f, row(g_post_ffn), ys, tm_r)


def kernel(x, c, w_ada, b_ada, g_pre_mix, g_post_mix, g_pre_ffn, g_post_ffn, w_in, b_forget, w_conv, b_conv, w_rg, b_rg, w_ig, b_ig, rglru_lambda, w_branch_attn, w_branch_rnn, w_out, w_router, router_bias, w_exp_gate, w_exp_up, w_exp_down, w_sh_gate, w_sh_up, w_sh_down):
    depth = w_ada.shape[0]
    for l in range(depth):
        x = _layer(x, c, w_ada[l], b_ada[l], g_pre_mix[l], g_post_mix[l], g_pre_ffn[l], g_post_ffn[l],
                   w_in[l], b_forget[l], w_conv[l], b_conv[l], w_rg[l], b_rg[l], w_ig[l], b_ig[l],
                   rglru_lambda[l], w_branch_attn[l], w_branch_rnn[l], w_out[l], w_router[l],
                   router_bias[l], w_exp_gate[l], w_exp_up[l], w_exp_down[l],
                   w_sh_gate[l], w_sh_up[l], w_sh_down[l])
    return x
```

```python
import functools

import jax
import jax.numpy as jnp
from jax import lax
from jax.experimental import pallas as pl
from jax.experimental.pallas import tpu as pltpu

F32 = jnp.float32
BF16 = jnp.bfloat16
I32 = jnp.int32

D_MODEL = 1024
ATT_HEADS = 8
ATT_HEAD_DIM = 64
ATT_WIDTH = ATT_HEADS * ATT_HEAD_DIM
RNN_WIDTH = D_MODEL
RNN_BLOCKS = 16
RNN_BLOCK_DIM = RNN_WIDTH // RNN_BLOCKS
CONV_WIDTH = 4
RGLRU_C = 8.0
N_EXPERTS = 64
TOP_K = 6
N_GROUPS = 8
GROUP_SIZE = N_EXPERTS // N_GROUPS
TOPK_GROUPS = 4
EXPERT_FF = D_MODEL // 4
ROUTE_SCALE = 2.5
NORM_EPS = 1e-6

LANES = 128
SUBLANES = 8
MXU_DIM = 256
VMEM_LIMIT_BYTES = 56 * 1024 * 1024

NEG_BIG = -1e30
RANK_BITS = 20
RANK_MASK = (1 << RANK_BITS) - 1


def _sigmoid(x):
    return 0.5 * jnp.tanh(0.5 * x) + 0.5


def _rms_scale(x):
    return x * lax.rsqrt(jnp.mean(x * x, axis=-1, keepdims=True) + NORM_EPS)


def _dot(a, b):
    return jnp.dot(a, b, preferred_element_type=F32)


def _dot_nt(a, b):
    return lax.dot_general(a, b, (((1,), (1,)), ((), ())), preferred_element_type=F32)


def _split3(a):
    hi = a.astype(BF16)
    r1 = a - hi.astype(F32)
    mid = r1.astype(BF16)
    lo = (r1 - mid.astype(F32)).astype(BF16)
    return hi, mid, lo


def _params(*sem):
    return pltpu.CompilerParams(dimension_semantics=sem, vmem_limit_bytes=VMEM_LIMIT_BYTES)


def _const_spec(shape):
    nd = len(shape)
    return pl.BlockSpec(shape, lambda *_: (0,) * nd)


def _ada_kernel(c_ref, w_ref, b_ref, o_ref):
    c = c_ref[...]
    sc = c * _sigmoid(c)
    s_hi, s_mid, _ = _split3(sc)
    w_hi, w_mid, _ = _split3(w_ref[...])
    acc = _dot(s_hi, w_hi) + _dot(s_hi, w_mid) + _dot(s_mid, w_hi)
    o_ref[...] = acc + b_ref[...]


def _ada_call(c, w_ada, b_ada):
    bsz, d = c.shape
    n = w_ada.shape[1]
    tn = d
    return pl.pallas_call(
        _ada_kernel,
        grid=(n // tn,),
        in_specs=[
            pl.BlockSpec((bsz, d), lambda j: (0, 0)),
            pl.BlockSpec((d, tn), lambda j: (0, j)),
            pl.BlockSpec((1, tn), lambda j: (0, j)),
        ],
        out_specs=pl.BlockSpec((bsz, tn), lambda j: (0, j)),
        out_shape=jax.ShapeDtypeStruct((bsz, n), F32),
        compiler_params=_params("arbitrary"),
        name="ada_mod",
    )(c, w_ada, b_ada.reshape(1, n))


N_BIAS_PIECES = 3
CUM_BLOCK = 256


def _bias_placement():
    h = jnp.arange(LANES)[:, None]
    lane = jnp.arange(ATT_WIDTH)[None, :]
    mats = []
    for piece in range(N_BIAS_PIECES):
        target = (h // 2) * LANES + ATT_HEAD_DIM * (1 - h % 2) + piece
        mats.append(((lane == target) & (h < ATT_HEADS)).astype(BF16))
    return jnp.concatenate(mats, axis=0)


def _key_bias(f_logit, bf_ref, place_ref, kb_ref, cum_sc):
    tm = f_logit.shape[0]
    blk = min(tm, CUM_BLOCK)
    row = lax.broadcasted_iota(I32, (blk, blk), 0)
    col = lax.broadcasted_iota(I32, (blk, blk), 1)
    tri = (row >= col).astype(BF16)
    carry = cum_sc[...]
    for i in range(tm // blk):
        z = f_logit[i * blk:(i + 1) * blk, :] + bf_ref[...]
        lf = jnp.minimum(z, 0.0) - jnp.log1p(jnp.exp(-jnp.abs(z)))
        c3 = _dot(tri, jnp.concatenate(_split3(lf), axis=1))
        c = c3[:, :LANES] + c3[:, LANES:2 * LANES] + c3[:, 2 * LANES:] + carry
        carry = c[blk - 1:blk, :]
        kb = _dot(jnp.concatenate(_split3(-c), axis=1), place_ref[...])
        kb_ref[0, i * blk:(i + 1) * blk, :] = kb.astype(BF16)
    cum_sc[...] = carry


def _inproj_kernel(x_ref, sh_ref, sc_ref, g_ref, wqkv_ref, wf_ref, wr_ref, bf_ref, place_ref,
                   q_ref, k_ref, v_ref, kb_ref, xr_ref, gr_ref, ga_ref, gb_ref, cum_sc):
    @pl.when(pl.program_id(1) == 0)
    def _():
        cum_sc[...] = jnp.zeros(cum_sc.shape, F32)

    x = x_ref[0]
    h = (_rms_scale(x) * g_ref[...]) * (1.0 + sc_ref[0]) + sh_ref[0]
    hb = h.astype(BF16)
    for j, ref in enumerate((q_ref, k_ref, v_ref)):
        ref[0] = _dot(hb, wqkv_ref[:, j * ATT_WIDTH:(j + 1) * ATT_WIDTH]).astype(BF16)
    _key_bias(_dot(hb, wf_ref[...]), bf_ref, place_ref, kb_ref, cum_sc)
    half = D_MODEL // 2
    for j, ref in enumerate((xr_ref, gr_ref, ga_ref, gb_ref)):
        for c in range(2):
            lo = j * D_MODEL + c * half
            ref[0, :, c * half:(c + 1) * half] = _dot(hb, wr_ref[:, lo:lo + half]).astype(BF16)


def _inproj_call(x, shift, scale, g, wqkv, wf, wr, b_forget_pad, tm):
    bsz, s, d = x.shape
    nt = s // tm
    tok = lambda w: pl.BlockSpec((1, tm, w), lambda b, i: (b, i, 0))
    mod = pl.BlockSpec((1, 1, d), lambda b, i: (b, 0, 0))
    place = _bias_placement()
    out_w = (ATT_WIDTH,) * 4 + (d,) * 4
    return pl.pallas_call(
        _inproj_kernel,
        grid=(bsz, nt),
        in_specs=[tok(d), mod, mod, _const_spec((1, d)),
                  _const_spec(wqkv.shape), _const_spec(wf.shape), _const_spec(wr.shape),
                  _const_spec((1, LANES)), _const_spec(place.shape)],
        out_specs=[tok(w) for w in out_w],
        out_shape=[jax.ShapeDtypeStruct((bsz, s, w), BF16) for w in out_w],
        scratch_shapes=[pltpu.VMEM((1, LANES), F32)],
        compiler_params=_params("arbitrary", "arbitrary"),
        name="in_proj",
    )(x, shift, scale, g, wqkv, wf, wr, b_forget_pad, place)


ATT_VT_ROWS = ATT_HEAD_DIM + 16


def _attn_kernel(q_ref, k_ref, v_ref, kb_ref, o_ref, vt_sc, st_sc, m_sc, acc_sc, *, tq, tk):
    s = q_ref.shape[1]
    nq, nk, ratio = s // tq, s // tk, tq // tk
    lane = lax.broadcasted_iota(I32, (1, LANES), 1)
    head_lanes = (lane < ATT_HEAD_DIM, lane >= ATT_HEAD_DIM)
    one_lanes = ((lane >= ATT_HEAD_DIM) & (lane < ATT_HEAD_DIM + N_BIAS_PIECES), lane < N_BIAS_PIECES)
    key_minus_query = (lax.broadcasted_iota(I32, (tk, tq), 0) - lax.broadcasted_iota(I32, (tk, tq), 1))

    vt = v_ref[0].astype(F32).T
    ones = jnp.ones((ATT_VT_ROWS - ATT_HEAD_DIM, s), F32)
    for h in range(2):
        vth = jnp.concatenate([vt[h * ATT_HEAD_DIM:(h + 1) * ATT_HEAD_DIM, :], ones], axis=0).astype(BF16)
        for j in range(nk):
            vt_sc[h, j] = vth[:, j * tk:(j + 1) * tk]

    def q_body(qi, carry):
        q0 = pl.multiple_of(qi * tq, tq)
        qs = q_ref[0, pl.ds(q0, tq), :] * (ATT_HEAD_DIM ** -0.5)
        qa = [jnp.where(head_lanes[h], qs, one_lanes[h].astype(BF16)) for h in range(2)]
        m_sc[...] = jnp.full(m_sc.shape, NEG_BIG, F32)
        acc_sc[...] = jnp.zeros(acc_sc.shape, F32)

        def scores(kj):
            k0 = pl.multiple_of(kj * tk, tk)
            kb = k_ref[0, pl.ds(k0, tk), :]
            bias = kb_ref[0, pl.ds(k0, tk), :]
            return [_dot_nt(jnp.where(head_lanes[h], kb, bias), qa[h]) for h in range(2)]

        def online_update(st, m_old, acc, vt):
            m_new = jnp.maximum(m_old, jnp.max(st, axis=0, keepdims=True))
            p = jnp.exp(st - m_new).astype(BF16)
            return m_new, jnp.exp(m_old - m_new) * acc + _dot(vt, p)

        def accumulate(kj, masked):
            for h in range(2):
                if not masked:
                    m_sc[h], acc_sc[h] = online_update(st_sc[h], m_sc[h], acc_sc[h], vt_sc[h, kj])
                elif ratio == 1:
                    hk = tk // 2
                    tri = key_minus_query[:hk, :] <= 0
                    st_a = jnp.where(tri, st_sc[h, :hk, :], NEG_BIG)
                    m_a, acc_a = online_update(st_a, m_sc[h], acc_sc[h], vt_sc[h, kj, :, :hk])
                    st_b = jnp.where(tri[:, :hk], st_sc[h, hk:, hk:], NEG_BIG)
                    m_b, acc_b = online_update(st_b, m_a[:, hk:], acc_a[:, hk:], vt_sc[h, kj, :, hk:])
                    m_sc[h] = jnp.concatenate([m_a[:, :hk], m_b], axis=1)
                    acc_sc[h] = jnp.concatenate([acc_a[:, :hk], acc_b], axis=1)
                else:
                    st = jnp.where(key_minus_query <= q0 - kj * tk, st_sc[h], NEG_BIG)
                    m_sc[h], acc_sc[h] = online_update(st, m_sc[h], acc_sc[h], vt_sc[h, kj])

        def put(sts):
            for h in range(2):
                st_sc[h] = sts[h]

        n_full = qi * ratio
        put(scores(0))

        def kv_body(kj, c2):
            nxt = scores(kj + 1)
            accumulate(kj, False)
            put(nxt)
            return c2

        lax.fori_loop(0, n_full, kv_body, 0)
        for d in range(ratio):
            nxt = scores(n_full + d + 1) if d + 1 < ratio else None
            accumulate(n_full + d, True)
            if nxt is not None:
                put(nxt)

        outs = []
        for h in range(2):
            acc = acc_sc[h]
            outs.append(acc[:ATT_HEAD_DIM, :] * (1.0 / acc[ATT_HEAD_DIM:ATT_HEAD_DIM + 1, :]))
        o_ref[0, pl.ds(q0, tq), :] = jnp.concatenate(outs, axis=0).T.astype(BF16)
        return carry

    lax.fori_loop(0, nq, q_body, 0)


def _attn_call(q, k, v, kbias, tq, tk):
    bsz, s, _ = q.shape
    qkv = pl.BlockSpec((1, s, LANES), lambda b, p: (b, 0, p))
    return pl.pallas_call(
        functools.partial(_attn_kernel, tq=tq, tk=tk),
        grid=(bsz, ATT_HEADS // 2),
        in_specs=[qkv, qkv, qkv, qkv],
        out_specs=qkv,
        out_shape=jax.ShapeDtypeStruct((bsz, s, ATT_WIDTH), BF16),
        scratch_shapes=[pltpu.VMEM((2, s // tk, ATT_VT_ROWS, tk), BF16), pltpu.VMEM((2, tk, tq), F32),
                        pltpu.VMEM((2, 1, tq), F32), pltpu.VMEM((2, ATT_VT_ROWS, tq), F32)],
        compiler_params=_params("arbitrary", "arbitrary"),
        name="fox_attention",
    )(q, k, v, kbias)


def _gelu_tanh(x):
    c = 0.7978845608028654
    return 0.5 * x * (1.0 + jnp.tanh(c * (x + 0.044715 * (x * x * x))))


def _rnn_kernel(xr_ref, gr_ref, wc_ref, bc_ref, wbd_ref, brg_ref, big_ref, lam_ref, o_ref,
                xpad_sc, hc_sc, a_sc, b_sc, h_sc, *, ts):
    @pl.when(pl.program_id(1) == 0)
    def _():
        xpad_sc[:SUBLANES, :] = jnp.zeros((SUBLANES, RNN_WIDTH), F32)
        hc_sc[...] = jnp.zeros(hc_sc.shape, F32)

    x = xr_ref[0].astype(F32)
    xpad_sc[SUBLANES:, :] = x
    u = bc_ref[...] + wc_ref[3:4, :] * x
    for j in range(CONV_WIDTH - 1):
        off = SUBLANES - (CONV_WIDTH - 1) + j
        u = u + wc_ref[j:j + 1, :] * xpad_sc[off:off + ts, :]
    xpad_sc[:SUBLANES, :] = x[ts - SUBLANES:ts, :]

    lam = lam_ref[...]
    neg_sp = -(jnp.maximum(-lam, 0.0) + jnp.log1p(jnp.exp(-jnp.abs(lam))))
    for jt in range(RNN_WIDTH // MXU_DIM):
        cs = slice(jt * MXU_DIM, (jt + 1) * MXU_DIM)
        uc = u[:, cs]
        g = _dot(uc.astype(BF16), wbd_ref[jt])
        r = _sigmoid(g[:, :MXU_DIM] + brg_ref[:, cs])
        ig = _sigmoid(g[:, MXU_DIM:] + big_ref[:, cs])
        a = jnp.exp(r * (RGLRU_C * neg_sp[:, cs]))
        a_sc[:, cs] = a
        v = 1.0 - a * a
        b_sc[:, cs] = jnp.where(v > 0.0, v * lax.rsqrt(v), 0.0) * (ig * uc)

    srow = lax.broadcasted_iota(I32, (SUBLANES, RNN_WIDTH), 0)

    def scan_body(i, carry):
        r0 = pl.multiple_of(i * SUBLANES, SUBLANES)
        a = a_sc[pl.ds(r0, SUBLANES), :]
        b = b_sc[pl.ds(r0, SUBLANES), :]
        b = b + jnp.where(srow == 0, a * carry, 0.0)
        for sh in (1, 2, 4):
            keep = srow >= sh
            b = jnp.where(keep, b + a * pltpu.roll(b, sh, 0), b)
            if sh < SUBLANES // 2:
                a = jnp.where(keep, a * pltpu.roll(a, sh, 0), a)
        h_sc[pl.ds(r0, SUBLANES), :] = b
        return jnp.broadcast_to(b[SUBLANES - 1:SUBLANES, :], (SUBLANES, RNN_WIDTH))

    hc_sc[...] = lax.fori_loop(0, ts // SUBLANES, scan_body, hc_sc[...])
    o_ref[0] = (h_sc[...] * _gelu_tanh(gr_ref[0].astype(F32))).astype(BF16)


def _rnn_call(x_rnn, g_rnn, w_conv, b_conv, wbd, b_rg, b_ig, lam, ts):
    bsz, s, w = x_rnn.shape
    tok = pl.BlockSpec((1, ts, w), lambda b, i: (b, i, 0))
    return pl.pallas_call(
        functools.partial(_rnn_kernel, ts=ts),
        grid=(bsz, s // ts),
        in_specs=[tok, tok, _const_spec(w_conv.shape), _const_spec((1, w)), _const_spec(wbd.shape),
                  _const_spec((1, w)), _const_spec((1, w)), _const_spec((1, w))],
        out_specs=tok,
        out_shape=jax.ShapeDtypeStruct((bsz, s, w), BF16),
        scratch_shapes=[pltpu.VMEM((ts + SUBLANES, w), F32), pltpu.VMEM((SUBLANES, w), F32),
                        pltpu.VMEM((ts, w), F32), pltpu.VMEM((ts, w), F32), pltpu.VMEM((ts, w), F32)],
        compiler_params=_params("arbitrary", "arbitrary"),
        name="rglru",
    )(x_rnn, g_rnn, w_conv, b_conv, wbd, b_rg, b_ig, lam)


WINDOW_SHIFT = 5
WINDOW_ROWS = 1 << WINDOW_SHIFT


def _route(h2, wr_hi_ref, wr_mid_ref, rbias_ref, cnt_ref, lpos_ref, before_ref, ntile_ref, wtok_ref, j, tm):
    h_hi, h_mid, _ = _split3(h2)
    logits = _dot_nt(wr_hi_ref[...], h_hi) + _dot_nt(wr_hi_ref[...], h_mid) + _dot_nt(wr_mid_ref[...], h_hi)
    scores = _sigmoid(logits)
    sel = scores + rbias_ref[...]

    giota = lax.broadcasted_iota(I32, (N_GROUPS, tm), 0)
    gs = jnp.zeros((N_GROUPS, tm), F32)
    for g in range(N_GROUPS):
        blk = sel[g * GROUP_SIZE:(g + 1) * GROUP_SIZE, :]
        m1 = jnp.max(blk, axis=0, keepdims=True)
        i1 = jnp.min(jnp.where(blk == m1, giota, N_EXPERTS), axis=0, keepdims=True)
        m2 = jnp.max(jnp.where(giota == i1, -jnp.inf, blk), axis=0, keepdims=True)
        gs = jnp.where(giota == g, m1 + m2, gs)

    gsel = jnp.zeros((N_GROUPS, tm), F32)
    for _ in range(TOPK_GROUPS):
        m = jnp.max(gs, axis=0, keepdims=True)
        idx = jnp.min(jnp.where(gs == m, giota, N_EXPERTS), axis=0, keepdims=True)
        hit = giota == idx
        gsel = jnp.where(hit, 1.0, gsel)
        gs = jnp.where(hit, -jnp.inf, gs)

    masked = jnp.concatenate(
        [jnp.where(gsel[g:g + 1, :] > 0.0, sel[g * GROUP_SIZE:(g + 1) * GROUP_SIZE, :], -jnp.inf)
         for g in range(N_GROUPS)], axis=0)

    eiota = lax.broadcasted_iota(I32, (N_EXPERTS, tm), 0)
    chosen = jnp.zeros((N_EXPERTS, tm), F32)
    idxs, wts = [], []
    for _ in range(TOP_K):
        m = jnp.max(masked, axis=0, keepdims=True)
        idx = jnp.min(jnp.where(masked == m, eiota, N_EXPERTS), axis=0, keepdims=True)
        hit = eiota == idx
        wts.append(jnp.sum(jnp.where(hit, scores, 0.0), axis=0, keepdims=True))
        idxs.append(idx)
        chosen = jnp.where(hit, 1.0, chosen)
        masked = jnp.where(hit, -jnp.inf, masked)
    wsum = wts[0]
    for w in wts[1:]:
        wsum = wsum + w

    srow = lax.broadcasted_iota(I32, (tm, tm), 0)
    scol = lax.broadcasted_iota(I32, (tm, tm), 1)
    earlier = (srow < scol).astype(BF16)
    local_rank = _dot(chosen.astype(BF16), earlier)
    n_e = jnp.sum(chosen, axis=1, keepdims=True)
    before_ref[j] = cnt_ref[...].astype(I32)
    ntile_ref[j] = n_e.astype(I32)
    cnt_ref[...] = cnt_ref[...] + n_e
    n_pad = (((n_e.astype(I32) + (WINDOW_ROWS - 1)) >> WINDOW_SHIFT) << WINDOW_SHIFT).astype(F32)
    erow = lax.broadcasted_iota(I32, (N_EXPERTS, N_EXPERTS), 0)
    ecol = lax.broadcasted_iota(I32, (N_EXPERTS, N_EXPERTS), 1)
    off = _dot((ecol < erow).astype(BF16), jnp.broadcast_to(n_pad, (N_EXPERTS, LANES)).astype(BF16))[:, :1]
    local_pos = off + local_rank

    riota = lax.broadcasted_iota(I32, (SUBLANES, tm), 0)
    lpos = jnp.zeros((SUBLANES, tm), I32)
    wrow = jnp.zeros((SUBLANES, tm), F32)
    for r in range(TOP_K):
        lp = jnp.sum(jnp.where(eiota == idxs[r], local_pos, 0.0), axis=0, keepdims=True)
        lpos = jnp.where(riota == r, lp.astype(I32), lpos)
        wrow = jnp.where(riota == r, wts[r] / wsum * ROUTE_SCALE, wrow)
    lpos_ref[j] = lpos
    wpad = jnp.concatenate([wrow, jnp.zeros((LANES - SUBLANES, tm), F32)], axis=0)
    wtok_ref[j * tm:(j + 1) * tm, :] = wpad.T


U32 = jnp.uint32
PACK_ROWS = D_MODEL // (2 * LANES)
HIGH_HALF = 0xFFFF0000


def _store_token_rows(rows_ref, val):
    half = D_MODEL // 2
    n = val.shape[0]
    bits = lax.bitcast_convert_type(val.astype(BF16).astype(F32), U32)
    words = (bits[:, half:] & jnp.uint32(HIGH_HALF)) | (bits[:, :half] >> 16)
    for c in range(PACK_ROWS):
        chunk = words[:, c * LANES:(c + 1) * LANES]
        if len(rows_ref.shape) == 2:
            rows_ref[pl.ds(c, n, stride=PACK_ROWS), :] = chunk
        else:
            rows_ref[:, c, :] = chunk


def _load_token_rows(rows_ref):
    if len(rows_ref.shape) == 2:
        n = rows_ref.shape[0] // PACK_ROWS
        chunks = [rows_ref[pl.ds(c, n, stride=PACK_ROWS), :] for c in range(PACK_ROWS)]
    else:
        chunks = [rows_ref[:, c, :] for c in range(PACK_ROWS)]
    words = jnp.concatenate(chunks, axis=-1)
    lo = lax.bitcast_convert_type(words << 16, F32)
    hi = lax.bitcast_convert_type(words & jnp.uint32(HIGH_HALF), F32)
    return jnp.concatenate([lo, hi], axis=-1)


def _merge_kernel(ya_ref, yr_ref, ga_ref, gb_ref, x_ref, gm_ref, shf_ref, scf_ref, gpost_ref, gpre_ref,
                  wba_ref, wbr_ref, wout_ref, wr_hi_ref, wr_mid_ref, rbias_ref,
                  x1_ref, h2rows_ref, lpos_ref, before_ref, ntile_ref, wtok_ref, cnt_ref, *, tm, tr):
    @pl.when((pl.program_id(0) == 0) & (pl.program_id(1) == 0))
    def _():
        cnt_ref[...] = jnp.zeros(cnt_ref.shape, F32)

    pa = _dot(ya_ref[0], wba_ref[...])
    pr = _dot(yr_ref[0], wbr_ref[...])
    merged = _sigmoid(ga_ref[0].astype(F32)) * pa + _sigmoid(gb_ref[0].astype(F32)) * pr
    o = _dot(merged.astype(BF16), wout_ref[...])
    x1 = x_ref[0] + gm_ref[0] * (_rms_scale(o) * gpost_ref[...])
    x1_ref[0] = x1
    h2 = (_rms_scale(x1) * gpre_ref[...]) * (1.0 + scf_ref[0]) + shf_ref[0]
    _store_token_rows(h2rows_ref, h2)
    for j in range(tm // tr):
        _route(h2[j * tr:(j + 1) * tr, :], wr_hi_ref, wr_mid_ref, rbias_ref, cnt_ref,
               lpos_ref, before_ref, ntile_ref, wtok_ref, j, tr)


def _merge_call(y_attn, y_rnn, gate_a, gate_r, x, gate_m, shift_f, scale_f, g_post, g_pre,
                wba, wbr, wout, wr_hi, wr_mid, rbias, tm, tr):
    bsz, s, d = x.shape
    nt = s // tm
    nr = tm // tr
    tok = lambda w: pl.BlockSpec((1, tm, w), lambda b, i: (b, i, 0))
    mod = pl.BlockSpec((1, 1, d), lambda b, i: (b, 0, 0))
    return pl.pallas_call(
        functools.partial(_merge_kernel, tm=tm, tr=tr),
        grid=(bsz, nt),
        in_specs=[tok(ATT_WIDTH), tok(d), tok(d), tok(d), tok(d), mod, mod, mod,
                  _const_spec((1, d)), _const_spec((1, d)),
                  _const_spec(wba.shape), _const_spec(wbr.shape), _const_spec(wout.shape),
                  _const_spec(wr_hi.shape), _const_spec(wr_mid.shape), _const_spec(rbias.shape)],
        out_specs=[tok(d),
                   pl.BlockSpec((tm * PACK_ROWS, LANES), lambda b, i: (b * nt + i, 0)),
                   pl.BlockSpec((nr, SUBLANES, tr), lambda b, i: (b * nt + i, 0, 0)),
                   pl.BlockSpec((nr, N_EXPERTS, 1), lambda b, i: (b * nt + i, 0, 0)),
                   pl.BlockSpec((nr, N_EXPERTS, 1), lambda b, i: (b * nt + i, 0, 0)),
                   pl.BlockSpec((tm, LANES), lambda b, i: (b * nt + i, 0)),
                   pl.BlockSpec((N_EXPERTS, 1), lambda b, i: (0, 0))],
        out_shape=[jax.ShapeDtypeStruct((bsz, s, d), F32),
                   jax.ShapeDtypeStruct((bsz * s * PACK_ROWS, LANES), U32),
                   jax.ShapeDtypeStruct((bsz * nt * nr, SUBLANES, tr), I32),
                   jax.ShapeDtypeStruct((bsz * nt * nr, N_EXPERTS, 1), I32),
                   jax.ShapeDtypeStruct((bsz * nt * nr, N_EXPERTS, 1), I32),
                   jax.ShapeDtypeStruct((bsz * s, LANES), F32),
                   jax.ShapeDtypeStruct((N_EXPERTS, 1), F32)],
        compiler_params=_params("arbitrary", "arbitrary"),
        name="merge_route",
    )(y_attn, y_rnn, gate_a, gate_r, x, gate_m, shift_f, scale_f, g_post, g_pre,
      wba, wbr, wout, wr_hi, wr_mid, rbias)


WIN_LIST = LANES
ROW_COPY_UNROLL = 8


def _local_rows(tm):
    rows = tm * TOP_K + N_EXPERTS * WINDOW_ROWS
    assert rows // WINDOW_ROWS < WIN_LIST
    return rows


def _for_each_window(wlist_ref, fn):
    def body(w, c):
        fn(pl.multiple_of(w * WINDOW_ROWS, WINDOW_ROWS), wlist_ref[0, 0, w])
        return c

    lax.fori_loop(0, wlist_ref[0, 0, WIN_LIST - 1], body, 0)


def _dispatch_kernel(lpos_ref, wlist_ref, wlistp_ref, wlistpp_ref, zfrom_ref, zto_ref, h_ref, h2d_ref,
                     wsg_ref, wsu_ref, wsd_ref, xs_hbm, shared_ref, loc_sc, zero_sc, sem, *, tm):
    g = pl.program_id(0)
    cur = lax.rem(g, 2)

    @pl.when(g == 0)
    def _():
        loc_sc[...] = jnp.zeros(loc_sc.shape, U32)
        zero_sc[...] = jnp.zeros(zero_sc.shape, U32)

        def for_each_zero_window(fn):
            def per_range(j, c):
                def per_window(w, c2):
                    r0 = pl.multiple_of(zfrom_ref[j] + w * WINDOW_ROWS, WINDOW_ROWS)
                    fn(pltpu.make_async_copy(zero_sc, xs_hbm.at[pl.ds(r0, WINDOW_ROWS)], sem))
                    return c2
                lax.fori_loop(0, (zto_ref[j] - zfrom_ref[j]) >> WINDOW_SHIFT, per_window, 0)
                return c
            lax.fori_loop(0, zfrom_ref.shape[0], per_range, 0)

        for_each_zero_window(lambda cp: cp.start())
        for_each_zero_window(lambda cp: cp.wait())

    half = _local_rows(tm)

    def window(base):
        return lambda row0, slot0: pltpu.make_async_copy(
            loc_sc.at[pl.ds(pl.multiple_of(base + row0, WINDOW_ROWS), WINDOW_ROWS)],
            xs_hbm.at[pl.ds(slot0, WINDOW_ROWS)], sem)

    def start_windows(wl_ref, buf):
        _for_each_window(wl_ref, lambda r, s: window(buf * half)(r, s).start())

    def wait_windows(wl_ref, buf):
        _for_each_window(wl_ref, lambda r, s: window(buf * half)(r, s).wait())

    @pl.when(g > 1)
    def _():
        wait_windows(wlistpp_ref, cur)

    @pl.when(g > 0)
    def _():
        start_windows(wlistp_ref, 1 - cur)

    for t in range(tm):
        row = h_ref[t]
        for k in range(TOP_K):
            loc_sc[lpos_ref[0, k, t]] = row

    hb = _load_token_rows(h2d_ref).astype(BF16)
    a = _dot(hb, wsg_ref[...])
    u = _dot(hb, wsu_ref[...])
    shared_ref[...] = _dot(((a * _sigmoid(a)) * u).astype(BF16), wsd_ref[...]).astype(BF16)

    @pl.when(g == pl.num_programs(0) - 1)
    def _():
        @pl.when(g > 0)
        def _():
            wait_windows(wlistp_ref, 1 - cur)

        start_windows(wlist_ref, cur)
        wait_windows(wlist_ref, cur)


def _dispatch_call(lpos, wlist, zero_from, zero_to, h2_rows, wsg, wsu, wsd, n_rows, tm):
    t = h2_rows.shape[0] // PACK_ROWS
    smem = functools.partial(pl.BlockSpec, memory_space=pltpu.SMEM)
    return pl.pallas_call(
        functools.partial(_dispatch_kernel, tm=tm),
        grid=(t // tm,),
        in_specs=[smem((1, SUBLANES, tm), lambda i: (i, 0, 0)),
                  smem((1, 1, WIN_LIST), lambda i: (i, 0, 0)),
                  smem((1, 1, WIN_LIST), lambda i: (jnp.maximum(i - 1, 0), 0, 0)),
                  smem((1, 1, WIN_LIST), lambda i: (jnp.maximum(i - 2, 0), 0, 0)),
                  smem(), smem(),
                  pl.BlockSpec((tm, PACK_ROWS, LANES), lambda i: (i, 0, 0)),
                  pl.BlockSpec((tm * PACK_ROWS, LANES), lambda i: (i, 0)),
                  _const_spec(wsg.shape), _const_spec(wsu.shape), _const_spec(wsd.shape)],
        out_specs=[pl.BlockSpec(memory_space=pl.ANY), pl.BlockSpec((tm, D_MODEL), lambda i: (i, 0))],
        out_shape=[jax.ShapeDtypeStruct((n_rows, PACK_ROWS, LANES), U32),
                   jax.ShapeDtypeStruct((t, D_MODEL), BF16)],
        scratch_shapes=[pltpu.VMEM((2 * _local_rows(tm), PACK_ROWS, LANES), U32),
                        pltpu.VMEM((WINDOW_ROWS, PACK_ROWS, LANES), U32), pltpu.SemaphoreType.DMA],
        compiler_params=_params("arbitrary"),
        name="moe_dispatch",
    )(lpos, wlist, wlist, wlist, zero_from, zero_to, h2_rows.reshape(-1, PACK_ROWS, LANES), h2_rows, wsg, wsu, wsd)


def _expert_kernel(be_ref, nused_ref, xs_ref, wg_ref, wu_ref, wd_ref, ys_ref, wgb_sc, wub_sc, wdb_sc, *, rows):
    i = pl.program_id(0)

    @pl.when((i == 0) | (be_ref[i] != be_ref[jnp.maximum(i - 1, 0)]))
    def _():
        wgb_sc[...] = wg_ref[0].astype(BF16)
        wub_sc[...] = wu_ref[0].astype(BF16)
        wdb_sc[...] = wd_ref[0].astype(BF16)

    @pl.when(i < nused_ref[0])
    def _():
        xb = _load_token_rows(xs_ref).astype(BF16)
        g = _dot(xb, wgb_sc[...])
        u = _dot(xb, wub_sc[...])
        act = (g * _sigmoid(g)) * u
        _store_token_rows(ys_ref, _dot(act.astype(BF16), wdb_sc[...]))

    @pl.when(pl.program_id(0) >= nused_ref[0])
    def _():
        ys_ref[...] = jnp.zeros(ys_ref.shape, U32)


def _expert_call(block_e, n_used, xs, wg, wu, wd, rows):
    n_rows = xs.shape[0] // PACK_ROWS
    _, d, ff = wg.shape
    row_spec = pl.BlockSpec((rows * PACK_ROWS, LANES), lambda i, be, nu: (jnp.minimum(i, nu[0] - 1), 0))
    grid_spec = pltpu.PrefetchScalarGridSpec(
        num_scalar_prefetch=2,
        grid=(n_rows // rows,),
        in_specs=[row_spec,
                  pl.BlockSpec((1, d, ff), lambda i, be, nu: (be[i], 0, 0)),
                  pl.BlockSpec((1, d, ff), lambda i, be, nu: (be[i], 0, 0)),
                  pl.BlockSpec((1, ff, d), lambda i, be, nu: (be[i], 0, 0))],
        out_specs=pl.BlockSpec((rows * PACK_ROWS, LANES), lambda i, be, nu: (i, 0)),
        scratch_shapes=[pltpu.VMEM((d, ff), BF16), pltpu.VMEM((d, ff), BF16), pltpu.VMEM((ff, d), BF16)],
    )
    return pl.pallas_call(
        functools.partial(_expert_kernel, rows=rows),
        grid_spec=grid_spec,
        out_shape=jax.ShapeDtypeStruct(xs.shape, U32),
        compiler_params=_params("arbitrary"),
        name="moe_experts",
    )(block_e, n_used, xs, wg, wu, wd)


def _final_kernel(lpos0_ref, lposn_ref, tab0_ref, tabw_ref, tabi_ref, x1_ref, shared_ref, wtok_ref, gf_ref,
                  gpost_ref, ys_hbm, o_ref, win_sc, gnext_sc, gcur_sc, sem, *, tm):
    nt = pl.num_programs(1)
    g = pl.program_id(0) * nt + pl.program_id(1)
    n_tiles = pl.num_programs(0) * nt
    buf_next = lax.rem(g + 1, 2)
    buf_after = lax.rem(g, 2)
    half = _local_rows(tm)

    def window(buf):
        return lambda row0, slot0: pltpu.make_async_copy(
            ys_hbm.at[pl.ds(slot0, WINDOW_ROWS)],
            win_sc.at[pl.ds(pl.multiple_of(buf * half + row0, WINDOW_ROWS), WINDOW_ROWS)], sem.at[buf])

    def start_windows(tab_ref, buf):
        _for_each_window(tab_ref, lambda r, s: window(buf)(r, s).start())

    def wait_windows(tab_ref, buf):
        _for_each_window(tab_ref, lambda r, s: window(buf)(r, s).wait())

    def pick_rows(lpos_ref, t):
        for k in range(TOP_K):
            gnext_sc[k, t] = win_sc[lpos_ref[0, k, t]]

    def stage():
        for k in range(TOP_K):
            gcur_sc[k] = gnext_sc[k].reshape(tm * PACK_ROWS, LANES)

    @pl.when(g == 0)
    def _():
        start_windows(tab0_ref, 0)
        wait_windows(tab0_ref, 0)

        def body(i, c):
            for u in range(ROW_COPY_UNROLL):
                pick_rows(lpos0_ref, i * ROW_COPY_UNROLL + u)
            return c

        lax.fori_loop(0, tm // ROW_COPY_UNROLL, body, 0)
        stage()

        @pl.when(n_tiles > 1)
        def _():
            start_windows(tabw_ref, 1)

    @pl.when(g + 1 < n_tiles)
    def _():
        wait_windows(tabw_ref, buf_next)

    @pl.when(g + 2 < n_tiles)
    def _():
        start_windows(tabi_ref, buf_after)

    for t in range(tm):
        pick_rows(lposn_ref, t)

    w = wtok_ref[...]
    routed = w[:, 0:1] * _load_token_rows(gcur_sc.at[0])
    for k in range(1, TOP_K):
        routed = routed + w[:, k:k + 1] * _load_token_rows(gcur_sc.at[k])
    tot = routed + shared_ref[0].astype(F32)
    o_ref[0] = x1_ref[0] + gf_ref[0] * (_rms_scale(tot) * gpost_ref[...])

    stage()


def _final_call(lpos, tab, x1, shared, wtok, gate_f, g_post, ys, tm):
    bsz, s, d = x1.shape
    nt = s // tm
    n_tiles = bsz * nt
    smem = functools.partial(pl.BlockSpec, memory_space=pltpu.SMEM)
    tok = pl.BlockSpec((1, tm, d), lambda b, i: (b, i, 0))
    tab_shape = (1, 1, WIN_LIST)
    ahead = lambda k: (lambda b, i: (jnp.minimum(b * nt + i + k, n_tiles - 1), 0, 0))
    return pl.pallas_call(
        functools.partial(_final_kernel, tm=tm),
        grid=(bsz, nt),
        in_specs=[smem((1, SUBLANES, tm), lambda b, i: (0, 0, 0)),
                  smem((1, SUBLANES, tm), ahead(1)),
                  smem(tab_shape, lambda b, i: (0, 0, 0)),
                  smem(tab_shape, ahead(1)),
                  smem(tab_shape, ahead(2)),
                  tok, tok,
                  pl.BlockSpec((tm, LANES), lambda b, i: (b * nt + i, 0)),
                  pl.BlockSpec((1, 1, d), lambda b, i: (b, 0, 0)),
                  _const_spec((1, d)),
                  pl.BlockSpec(memory_space=pl.ANY)],
        out_specs=tok,
        out_shape=jax.ShapeDtypeStruct((bsz, s, d), F32),
        scratch_shapes=[pltpu.VMEM((2 * _local_rows(tm), PACK_ROWS, LANES), U32),
                        pltpu.VMEM((TOP_K, tm, PACK_ROWS, LANES), U32),
                        pltpu.VMEM((TOP_K, tm * PACK_ROWS, LANES), U32), pltpu.SemaphoreType.DMA((2,))],
        compiler_params=_params("arbitrary", "arbitrary"),
        name="moe_combine",
    )(lpos, lpos, tab, tab, tab, x1, shared, wtok, gate_f, g_post, ys)


def _block_diag_tiles(w):
    per = MXU_DIM // RNN_BLOCK_DIM
    nt = RNN_BLOCKS // per
    w4 = w.reshape(nt, per, RNN_BLOCK_DIM, RNN_BLOCK_DIM)
    eye = jnp.eye(per, dtype=w.dtype)
    return jnp.einsum("tnij,nm->tnimj", w4, eye).reshape(nt, MXU_DIM, MXU_DIM)


def _tile(n, pref):
    t = min(n, pref)
    assert n % t == 0, (n, t)
    return t


def _layer(x, c, w_ada, b_ada, g_pre_mix, g_post_mix, g_pre_ffn, g_post_ffn, w_in, b_forget,
           w_conv, b_conv, w_rg, b_rg, w_ig, b_ig, lam, w_branch_attn, w_branch_rnn, w_out,
           w_router, router_bias, w_exp_gate, w_exp_up, w_exp_down, w_sh_gate, w_sh_up, w_sh_down):
    bsz, s, d = x.shape
    t = bsz * s
    row = lambda v: v.reshape(1, -1)

    mod = _ada_call(c, w_ada, b_ada)
    shift_m, scale_m, gate_m, shift_f, scale_f, gate_f = [
        mod[:, j * d:(j + 1) * d].reshape(bsz, 1, d) for j in range(6)]

    o_f = 3 * ATT_WIDTH
    o_r = o_f + ATT_HEADS
    wqkv = w_in[:, :o_f].astype(BF16)
    wf = jnp.pad(w_in[:, o_f:o_r], ((0, 0), (0, LANES - ATT_HEADS))).astype(BF16)
    wr = w_in[:, o_r:].astype(BF16)
    bf_pad = jnp.pad(b_forget, (0, LANES - ATT_HEADS)).reshape(1, LANES)
    q, k, v, kbias, x_rnn, g_rnn, gate_a, gate_r = _inproj_call(
        x, shift_m, scale_m, row(g_pre_mix), wqkv, wf, wr, bf_pad, _tile(s, 512))
    tq = _tile(s, 512)
    y_attn = _attn_call(q, k, v, kbias, tq, _tile(tq, 512))

    wbd = jnp.concatenate([_block_diag_tiles(w_rg), _block_diag_tiles(w_ig)], axis=-1).astype(BF16)
    y_rnn = _rnn_call(x_rnn, g_rnn, w_conv, row(b_conv), wbd, row(b_rg), row(b_ig), row(lam), _tile(s, 256))

    wr_t = w_router.T
    wr_hi = wr_t.astype(BF16)
    wr_mid = (wr_t - wr_hi.astype(F32)).astype(BF16)
    tm_m = _tile(s, 512)
    tm_r = _tile(tm_m, 256)
    x1, h2_rows, lpos, before, n_tile, wtok, counts = _merge_call(
        y_attn, y_rnn, gate_a, gate_r, x, gate_m, shift_f, scale_f, row(g_post_mix), row(g_pre_ffn),
        w_branch_attn.astype(BF16), w_branch_rnn.astype(BF16), w_out.astype(BF16),
        wr_hi, wr_mid, router_bias.reshape(N_EXPERTS, 1), tm_m, tm_r)

    rows = 1024
    n_blocks =-(-(t * TOP_K + N_EXPERTS * WINDOW_ROWS) // rows) + N_EXPERTS
    cnt = counts[:, 0].astype(I32)
    padded = ((cnt + WINDOW_ROWS + rows - 1) // rows) * rows
    pend = jnp.cumsum(padded).astype(I32)
    pstart = pend - padded
    n_used = jnp.maximum(pend[-1] // rows, 1)
    blk = jnp.minimum(jnp.arange(n_blocks, dtype=I32), n_used - 1)
    block_e = jnp.minimum(jnp.sum((pend[None, :] <= (blk * rows)[:, None]).astype(I32), axis=1), N_EXPERTS - 1)

    n_win = (n_tile[:, :, 0] + WINDOW_ROWS - 1) // WINDOW_ROWS
    win_end = jnp.cumsum(n_win, axis=1)
    w_id = jnp.arange(WIN_LIST, dtype=I32)[None, :, None]
    owner = (jnp.sum((win_end[:, None, :] <= w_id).astype(I32), axis=2, keepdims=True)
             == jnp.arange(N_EXPERTS, dtype=I32)[None, None, :]).astype(I32)
    first_slot = pstart[None, :] + before[:, :, 0] - (win_end - n_win) * WINDOW_ROWS
    wlist = jnp.sum(owner * first_slot[:, None, :], axis=2) + w_id[:, :, 0] * WINDOW_ROWS
    wlist = jnp.where(w_id[:, :, 0] == WIN_LIST - 1, win_end[:, -1:], wlist).astype(I32)[:, None, :]
    n_rt = lpos.shape[0]
    lpos = lpos + (jnp.arange(n_rt, dtype=I32) % 2)[:, None, None] * _local_rows(tm_r)

    rows3 = lambda a: a.reshape(-1, PACK_ROWS, LANES)
    rows2 = lambda a: a.reshape(-1, LANES)
    zero_from = jnp.concatenate([pstart + cnt // WINDOW_ROWS * WINDOW_ROWS, pend[-1:]]).astype(I32)
    zero_to = jnp.concatenate([pend, jnp.full((1,), n_blocks * rows, I32)]).astype(I32)
    xs, shared = _dispatch_call(lpos, wlist, zero_from, zero_to, h2_rows,
                                w_sh_gate.astype(BF16), w_sh_up.astype(BF16), w_sh_down.astype(BF16),
                                n_blocks * rows, tm_r)
    ys = rows3(_expert_call(block_e, n_used.reshape(1), rows2(xs),
                            w_exp_gate, w_exp_up, w_exp_down, rows))
    return _final_call(lpos, wlist, x1, shared.reshape(bsz, s, d), wtok, gate_f, row(g_post_ffn), ys, tm_r)


def kernel(x, c, w_ada, b_ada, g_pre_mix, g_post_mix, g_pre_ffn, g_post_ffn, w_in, b_forget, w_conv, b_conv, w_rg, b_rg, w_ig, b_ig, rglru_lambda, w_branch_attn, w_branch_rnn, w_out, w_router, router_bias, w_exp_gate, w_exp_up, w_exp_down, w_sh_gate, w_sh_up, w_sh_down):
    depth = w_ada.shape[0]
    for l in range(depth):
        x = _layer(x, c, w_ada[l], b_ada[l], g_pre_mix[l], g_post_mix[l], g_pre_ffn[l], g_post_ffn[l],
                   w_in[l], b_forget[l], w_conv[l], b_conv[l], w_rg[l], b_rg[l], w_ig[l], b_ig[l],
                   rglru_lambda[l], w_branch_attn[l], w_branch_rnn[l], w_out[l], w_router[l],
                   router_bias[l], w_exp_gate[l], w_exp_up[l], w_exp_down[l],
                   w_sh_gate[l], w_sh_up[l], w_sh_down[l])
    return x
```

```python
import functools

import jax
import jax.numpy as jnp
from jax import lax
from jax.experimental import pallas as pl
from jax.experimental.pallas import tpu as pltpu

F32 = jnp.float32
BF16 = jnp.bfloat16
I32 = jnp.int32

D_MODEL = 1024
ATT_HEADS = 8
ATT_HEAD_DIM = 64
ATT_WIDTH = ATT_HEADS * ATT_HEAD_DIM
RNN_WIDTH = D_MODEL
RNN_BLOCKS = 16
RNN_BLOCK_DIM = RNN_WIDTH // RNN_BLOCKS
CONV_WIDTH = 4
RGLRU_C = 8.0
N_EXPERTS = 64
TOP_K = 6
N_GROUPS = 8
GROUP_SIZE = N_EXPERTS // N_GROUPS
TOPK_GROUPS = 4
EXPERT_FF = D_MODEL // 4
ROUTE_SCALE = 2.5
NORM_EPS = 1e-6

LANES = 128
SUBLANES = 8
MXU_DIM = 256
VMEM_LIMIT_BYTES = 56 * 1024 * 1024

NEG_BIG = -1e30


def _sigmoid(x):
    return 0.5 * jnp.tanh(0.5 * x) + 0.5


def _rms_scale(x):
    return x * lax.rsqrt(jnp.mean(x * x, axis=-1, keepdims=True) + NORM_EPS)


def _dot(a, b):
    return jnp.dot(a, b, preferred_element_type=F32)


def _dot_nt(a, b):
    return lax.dot_general(a, b, (((1,), (1,)), ((), ())), preferred_element_type=F32)


def _split3(a):
    hi = a.astype(BF16)
    r1 = a - hi.astype(F32)
    mid = r1.astype(BF16)
    lo = (r1 - mid.astype(F32)).astype(BF16)
    return hi, mid, lo


def _params(*sem):
    return pltpu.CompilerParams(dimension_semantics=sem, vmem_limit_bytes=VMEM_LIMIT_BYTES)


def _const_spec(shape):
    nd = len(shape)
    return pl.BlockSpec(shape, lambda *_: (0,) * nd)


def _ada_kernel(c_ref, w_ref, b_ref, o_ref):
    c = c_ref[...]
    sc = c * _sigmoid(c)
    s_hi, s_mid, _ = _split3(sc)
    w_hi, w_mid, _ = _split3(w_ref[...])
    acc = _dot(s_hi, w_hi) + _dot(s_hi, w_mid) + _dot(s_mid, w_hi)
    o_ref[...] = acc + b_ref[...]


def _ada_call(c, w_ada, b_ada):
    bsz, d = c.shape
    n = w_ada.shape[1]
    tn = d
    return pl.pallas_call(
        _ada_kernel,
        grid=(n // tn,),
        in_specs=[
            pl.BlockSpec((bsz, d), lambda j: (0, 0)),
            pl.BlockSpec((d, tn), lambda j: (0, j)),
            pl.BlockSpec((1, tn), lambda j: (0, j)),
        ],
        out_specs=pl.BlockSpec((bsz, tn), lambda j: (0, j)),
        out_shape=jax.ShapeDtypeStruct((bsz, n), F32),
        compiler_params=_params("arbitrary"),
        name="ada_mod",
    )(c, w_ada, b_ada.reshape(1, n))


N_BIAS_PIECES = 3
CUM_BLOCK = 256


def _bias_placement():
    h = jnp.arange(LANES)[:, None]
    lane = jnp.arange(ATT_WIDTH)[None, :]
    mats = []
    for piece in range(N_BIAS_PIECES):
        target = (h // 2) * LANES + ATT_HEAD_DIM * (1 - h % 2) + piece
        mats.append(((lane == target) & (h < ATT_HEADS)).astype(BF16))
    return jnp.concatenate(mats, axis=0)


def _key_bias(f_logit, bf_ref, place_ref, kb_ref, cum_sc):
    tm = f_logit.shape[0]
    blk = min(tm, CUM_BLOCK)
    row = lax.broadcasted_iota(I32, (blk, blk), 0)
    col = lax.broadcasted_iota(I32, (blk, blk), 1)
    tri = (row >= col).astype(BF16)
    carry = cum_sc[...]
    for i in range(tm // blk):
        z = f_logit[i * blk:(i + 1) * blk, :] + bf_ref[...]
        lf = jnp.minimum(z, 0.0) - jnp.log1p(jnp.exp(-jnp.abs(z)))
        c3 = _dot(tri, jnp.concatenate(_split3(lf), axis=1))
        c = c3[:, :LANES] + c3[:, LANES:2 * LANES] + c3[:, 2 * LANES:] + carry
        carry = c[blk - 1:blk, :]
        kb = _dot(jnp.concatenate(_split3(-c), axis=1), place_ref[...])
        kb_ref[0, i * blk:(i + 1) * blk, :] = kb.astype(BF16)
    cum_sc[...] = carry


def _inproj_kernel(x_ref, sh_ref, sc_ref, g_ref, wqkv_ref, wf_ref, wr_ref, bf_ref, place_ref,
                   q_ref, k_ref, v_ref, kb_ref, xr_ref, gr_ref, ga_ref, gb_ref, cum_sc):
    @pl.when(pl.program_id(1) == 0)
    def _():
        cum_sc[...] = jnp.zeros(cum_sc.shape, F32)

    x = x_ref[0]
    h = (_rms_scale(x) * g_ref[...]) * (1.0 + sc_ref[0]) + sh_ref[0]
    hb = h.astype(BF16)
    for j, ref in enumerate((q_ref, k_ref, v_ref)):
        ref[0] = _dot(hb, wqkv_ref[:, j * ATT_WIDTH:(j + 1) * ATT_WIDTH]).astype(BF16)
    _key_bias(_dot(hb, wf_ref[...]), bf_ref, place_ref, kb_ref, cum_sc)
    half = D_MODEL // 2
    for j, ref in enumerate((xr_ref, gr_ref, ga_ref, gb_ref)):
        for c in range(2):
            lo = j * D_MODEL + c * half
            ref[0, :, c * half:(c + 1) * half] = _dot(hb, wr_ref[:, lo:lo + half]).astype(BF16)


def _inproj_call(x, shift, scale, g, wqkv, wf, wr, b_forget_pad, tm):
    bsz, s, d = x.shape
    nt = s // tm
    tok = lambda w: pl.BlockSpec((1, tm, w), lambda b, i: (b, i, 0))
    mod = pl.BlockSpec((1, 1, d), lambda b, i: (b, 0, 0))
    place = _bias_placement()
    out_w = (ATT_WIDTH,) * 4 + (d,) * 4
    return pl.pallas_call(
        _inproj_kernel,
        grid=(bsz, nt),
        in_specs=[tok(d), mod, mod, _const_spec((1, d)),
                  _const_spec(wqkv.shape), _const_spec(wf.shape), _const_spec(wr.shape),
                  _const_spec((1, LANES)), _const_spec(place.shape)],
        out_specs=[tok(w) for w in out_w],
        out_shape=[jax.ShapeDtypeStruct((bsz, s, w), BF16) for w in out_w],
        scratch_shapes=[pltpu.VMEM((1, LANES), F32)],
        compiler_params=_params("arbitrary", "arbitrary"),
        name="in_proj",
    )(x, shift, scale, g, wqkv, wf, wr, b_forget_pad, place)


ATT_VT_ROWS = ATT_HEAD_DIM + 16


def _attn_kernel(q_ref, k_ref, v_ref, kb_ref, o_ref, vt_sc, st_sc, m_sc, acc_sc, *, tq, tk):
    s = q_ref.shape[1]
    nq, nk, ratio = s // tq, s // tk, tq // tk
    lane = lax.broadcasted_iota(I32, (1, LANES), 1)
    head_lanes = (lane < ATT_HEAD_DIM, lane >= ATT_HEAD_DIM)
    one_lanes = ((lane >= ATT_HEAD_DIM) & (lane < ATT_HEAD_DIM + N_BIAS_PIECES), lane < N_BIAS_PIECES)
    key_minus_query = (lax.broadcasted_iota(I32, (tk, tq), 0) - lax.broadcasted_iota(I32, (tk, tq), 1))

    vt = v_ref[0].astype(F32).T
    ones = jnp.ones((ATT_VT_ROWS - ATT_HEAD_DIM, s), F32)
    for h in range(2):
        vth = jnp.concatenate([vt[h * ATT_HEAD_DIM:(h + 1) * ATT_HEAD_DIM, :], ones], axis=0).astype(BF16)
        for j in range(nk):
            vt_sc[h, j] = vth[:, j * tk:(j + 1) * tk]

    def q_body(qi, carry):
        q0 = pl.multiple_of(qi * tq, tq)
        qs = q_ref[0, pl.ds(q0, tq), :] * (ATT_HEAD_DIM ** -0.5)
        qa = [jnp.where(head_lanes[h], qs, one_lanes[h].astype(BF16)) for h in range(2)]
        m_sc[...] = jnp.full(m_sc.shape, NEG_BIG, F32)
        acc_sc[...] = jnp.zeros(acc_sc.shape, F32)

        def scores(kj):
            k0 = pl.multiple_of(kj * tk, tk)
            kb = k_ref[0, pl.ds(k0, tk), :]
            bias = kb_ref[0, pl.ds(k0, tk), :]
            return [_dot_nt(jnp.where(head_lanes[h], kb, bias), qa[h]) for h in range(2)]

        def online_update(st, m_old, acc, vt):
            m_new = jnp.maximum(m_old, jnp.max(st, axis=0, keepdims=True))
            p = jnp.exp(st - m_new).astype(BF16)
            return m_new, jnp.exp(m_old - m_new) * acc + _dot(vt, p)

        def accumulate(kj, masked):
            for h in range(2):
                if not masked:
                    m_sc[h], acc_sc[h] = online_update(st_sc[h], m_sc[h], acc_sc[h], vt_sc[h, kj])
                elif ratio == 1:
                    hk = tk // 2
                    tri = key_minus_query[:hk, :] <= 0
                    st_a = jnp.where(tri, st_sc[h, :hk, :], NEG_BIG)
                    m_a, acc_a = online_update(st_a, m_sc[h], acc_sc[h], vt_sc[h, kj, :, :hk])
                    st_b = jnp.where(tri[:, :hk], st_sc[h, hk:, hk:], NEG_BIG)
                    m_b, acc_b = online_update(st_b, m_a[:, hk:], acc_a[:, hk:], vt_sc[h, kj, :, hk:])
                    m_sc[h] = jnp.concatenate([m_a[:, :hk], m_b], axis=1)
                    acc_sc[h] = jnp.concatenate([acc_a[:, :hk], acc_b], axis=1)
                else:
                    st = jnp.where(key_minus_query <= q0 - kj * tk, st_sc[h], NEG_BIG)
                    m_sc[h], acc_sc[h] = online_update(st, m_sc[h], acc_sc[h], vt_sc[h, kj])

        def put(sts):
            for h in range(2):
                st_sc[h] = sts[h]

        n_full = qi * ratio
        put(scores(0))

        def kv_step(kj):
            nxt = scores(kj + 1)
            accumulate(kj, False)
            put(nxt)

        def kv_pair(i, c2):
            kv_step(2 * i)
            kv_step(2 * i + 1)
            return c2

        lax.fori_loop(0, n_full // 2, kv_pair, 0)

        @pl.when(n_full % 2 == 1)
        def _():
            kv_step(n_full - 1)

        for d in range(ratio):
            nxt = scores(n_full + d + 1) if d + 1 < ratio else None
            accumulate(n_full + d, True)
            if nxt is not None:
                put(nxt)

        outs = []
        for h in range(2):
            acc = acc_sc[h]
            outs.append(acc[:ATT_HEAD_DIM, :] * (1.0 / acc[ATT_HEAD_DIM:ATT_HEAD_DIM + 1, :]))
        o_ref[0, pl.ds(q0, tq), :] = jnp.concatenate(outs, axis=0).T.astype(BF16)
        return carry

    lax.fori_loop(0, nq, q_body, 0)


def _attn_call(q, k, v, kbias, tq, tk):
    bsz, s, _ = q.shape
    qkv = pl.BlockSpec((1, s, LANES), lambda b, p: (b, 0, p))
    return pl.pallas_call(
        functools.partial(_attn_kernel, tq=tq, tk=tk),
        grid=(bsz, ATT_HEADS // 2),
        in_specs=[qkv, qkv, qkv, qkv],
        out_specs=qkv,
        out_shape=jax.ShapeDtypeStruct((bsz, s, ATT_WIDTH), BF16),
        scratch_shapes=[pltpu.VMEM((2, s // tk, ATT_VT_ROWS, tk), BF16), pltpu.VMEM((2, tk, tq), F32),
                        pltpu.VMEM((2, 1, tq), F32), pltpu.VMEM((2, ATT_VT_ROWS, tq), F32)],
        compiler_params=_params("arbitrary", "arbitrary"),
        name="fox_attention",
    )(q, k, v, kbias)


def _gelu_tanh(x):
    c = 0.7978845608028654
    return 0.5 * x * (1.0 + jnp.tanh(c * (x + 0.044715 * (x * x * x))))


def _rnn_kernel(xr_ref, gr_ref, wc_ref, bc_ref, wbd_ref, brg_ref, big_ref, lam_ref, o_ref,
                tail_sc, hc_sc, a_sc, b_sc, h_sc, *, ts):
    @pl.when(pl.program_id(1) == 0)
    def _():
        tail_sc[...] = jnp.zeros(tail_sc.shape, F32)
        hc_sc[...] = jnp.zeros(hc_sc.shape, F32)

    x = xr_ref[0].astype(F32)
    full = jnp.concatenate([tail_sc[...], x], axis=0)
    u = bc_ref[...] + wc_ref[CONV_WIDTH - 1:CONV_WIDTH, :] * x
    for j in range(CONV_WIDTH - 1):
        off = SUBLANES - (CONV_WIDTH - 1) + j
        u = u + wc_ref[j:j + 1, :] * full[off:off + ts, :]
    tail_sc[...] = x[ts - SUBLANES:ts, :]

    lam = lam_ref[...]
    neg_sp = -(jnp.maximum(-lam, 0.0) + jnp.log1p(jnp.exp(-jnp.abs(lam))))
    for jt in range(RNN_WIDTH // MXU_DIM):
        cs = slice(jt * MXU_DIM, (jt + 1) * MXU_DIM)
        uc = u[:, cs]
        g = _dot(uc.astype(BF16), wbd_ref[jt])
        r = _sigmoid(g[:, :MXU_DIM] + brg_ref[:, cs])
        ig = _sigmoid(g[:, MXU_DIM:] + big_ref[:, cs])
        a = jnp.exp(r * (RGLRU_C * neg_sp[:, cs]))
        a_sc[:, cs] = a
        v = 1.0 - a * a
        b_sc[:, cs] = jnp.where(v > 0.0, v * lax.rsqrt(v), 0.0) * (ig * uc)

    srow = lax.broadcasted_iota(I32, (SUBLANES, RNN_WIDTH), 0)

    def scan_body(i, carry):
        r0 = pl.multiple_of(i * SUBLANES, SUBLANES)
        a = a_sc[pl.ds(r0, SUBLANES), :]
        b = b_sc[pl.ds(r0, SUBLANES), :]
        b = b + jnp.where(srow == 0, a * carry, 0.0)
        for sh in (1, 2, 4):
            keep = srow >= sh
            b = jnp.where(keep, b + a * pltpu.roll(b, sh, 0), b)
            if sh < SUBLANES // 2:
                a = jnp.where(keep, a * pltpu.roll(a, sh, 0), a)
        h_sc[pl.ds(r0, SUBLANES), :] = b
        return jnp.broadcast_to(b[SUBLANES - 1:SUBLANES, :], (SUBLANES, RNN_WIDTH))

    hc_sc[...] = lax.fori_loop(0, ts // SUBLANES, scan_body, hc_sc[...])
    o_ref[0] = (h_sc[...] * _gelu_tanh(gr_ref[0].astype(F32))).astype(BF16)


def _rnn_call(x_rnn, g_rnn, w_conv, b_conv, wbd, b_rg, b_ig, lam, ts):
    bsz, s, w = x_rnn.shape
    tok = pl.BlockSpec((1, ts, w), lambda b, i: (b, i, 0))
    return pl.pallas_call(
        functools.partial(_rnn_kernel, ts=ts),
        grid=(bsz, s // ts),
        in_specs=[tok, tok, _const_spec(w_conv.shape), _const_spec((1, w)), _const_spec(wbd.shape),
                  _const_spec((1, w)), _const_spec((1, w)), _const_spec((1, w))],
        out_specs=tok,
        out_shape=jax.ShapeDtypeStruct((bsz, s, w), BF16),
        scratch_shapes=[pltpu.VMEM((SUBLANES, w), F32), pltpu.VMEM((SUBLANES, w), F32),
                        pltpu.VMEM((ts, w), F32), pltpu.VMEM((ts, w), F32), pltpu.VMEM((ts, w), F32)],
        compiler_params=_params("arbitrary", "arbitrary"),
        name="rglru",
    )(x_rnn, g_rnn, w_conv, b_conv, wbd, b_rg, b_ig, lam)


WINDOW_SHIFT = 5
WINDOW_ROWS = 1 << WINDOW_SHIFT


def _route(h2, wr_hi_ref, wr_mid_ref, rbias_ref, cnt_ref, lpos_ref, before_ref, ntile_ref, wtok_ref, j, tm):
    h_hi, h_mid, _ = _split3(h2)
    logits = _dot_nt(wr_hi_ref[...], h_hi) + _dot_nt(wr_hi_ref[...], h_mid) + _dot_nt(wr_mid_ref[...], h_hi)
    scores = _sigmoid(logits)
    sel = scores + rbias_ref[...]

    giota = lax.broadcasted_iota(I32, (N_GROUPS, tm), 0)
    gs = jnp.zeros((N_GROUPS, tm), F32)
    for g in range(N_GROUPS):
        blk = sel[g * GROUP_SIZE:(g + 1) * GROUP_SIZE, :]
        m1 = jnp.max(blk, axis=0, keepdims=True)
        i1 = jnp.min(jnp.where(blk == m1, giota, N_EXPERTS), axis=0, keepdims=True)
        m2 = jnp.max(jnp.where(giota == i1, -jnp.inf, blk), axis=0, keepdims=True)
        gs = jnp.where(giota == g, m1 + m2, gs)

    gsel = jnp.zeros((N_GROUPS, tm), F32)
    for _ in range(TOPK_GROUPS):
        m = jnp.max(gs, axis=0, keepdims=True)
        idx = jnp.min(jnp.where(gs == m, giota, N_EXPERTS), axis=0, keepdims=True)
        hit = giota == idx
        gsel = jnp.where(hit, 1.0, gsel)
        gs = jnp.where(hit, -jnp.inf, gs)

    masked = jnp.concatenate(
        [jnp.where(gsel[g:g + 1, :] > 0.0, sel[g * GROUP_SIZE:(g + 1) * GROUP_SIZE, :], -jnp.inf)
         for g in range(N_GROUPS)], axis=0)

    eiota = lax.broadcasted_iota(I32, (N_EXPERTS, tm), 0)
    chosen = jnp.zeros((N_EXPERTS, tm), F32)
    idxs, wts = [], []
    for _ in range(TOP_K):
        m = jnp.max(masked, axis=0, keepdims=True)
        idx = jnp.min(jnp.where(masked == m, eiota, N_EXPERTS), axis=0, keepdims=True)
        hit = eiota == idx
        wts.append(jnp.sum(jnp.where(hit, scores, 0.0), axis=0, keepdims=True))
        idxs.append(idx)
        chosen = jnp.where(hit, 1.0, chosen)
        masked = jnp.where(hit, -jnp.inf, masked)
    wsum = wts[0]
    for w in wts[1:]:
        wsum = wsum + w

    srow = lax.broadcasted_iota(I32, (tm, tm), 0)
    scol = lax.broadcasted_iota(I32, (tm, tm), 1)
    earlier = (srow < scol).astype(BF16)
    local_rank = _dot(chosen.astype(BF16), earlier)
    n_e = jnp.sum(chosen, axis=1, keepdims=True)
    before_ref[j] = cnt_ref[...].astype(I32)
    ntile_ref[j] = n_e.astype(I32)
    cnt_ref[...] = cnt_ref[...] + n_e
    n_pad = (((n_e.astype(I32) + (WINDOW_ROWS - 1)) >> WINDOW_SHIFT) << WINDOW_SHIFT).astype(F32)
    erow = lax.broadcasted_iota(I32, (N_EXPERTS, N_EXPERTS), 0)
    ecol = lax.broadcasted_iota(I32, (N_EXPERTS, N_EXPERTS), 1)
    off = _dot((ecol < erow).astype(BF16), jnp.broadcast_to(n_pad, (N_EXPERTS, LANES)).astype(BF16))[:, :1]
    local_pos = off + local_rank

    riota = lax.broadcasted_iota(I32, (SUBLANES, tm), 0)
    lpos = jnp.zeros((SUBLANES, tm), I32)
    wrow = jnp.zeros((SUBLANES, tm), F32)
    for r in range(TOP_K):
        lp = jnp.sum(jnp.where(eiota == idxs[r], local_pos, 0.0), axis=0, keepdims=True)
        lpos = jnp.where(riota == r, lp.astype(I32), lpos)
        wrow = jnp.where(riota == r, wts[r] / wsum * ROUTE_SCALE, wrow)
    lpos_ref[j] = lpos
    wpad = jnp.concatenate([wrow, jnp.zeros((LANES - SUBLANES, tm), F32)], axis=0)
    wtok_ref[j * tm:(j + 1) * tm, :] = wpad.T


U32 = jnp.uint32
PACK_ROWS = D_MODEL // (2 * LANES)
HIGH_HALF = 0xFFFF0000


def _store_token_rows(rows_ref, val):
    half = D_MODEL // 2
    n = val.shape[0]
    bits = lax.bitcast_convert_type(val.astype(BF16).astype(F32), U32)
    words = (bits[:, half:] & jnp.uint32(HIGH_HALF)) | (bits[:, :half] >> 16)
    for c in range(PACK_ROWS):
        chunk = words[:, c * LANES:(c + 1) * LANES]
        if len(rows_ref.shape) == 2:
            rows_ref[pl.ds(c, n, stride=PACK_ROWS), :] = chunk
        else:
            rows_ref[:, c, :] = chunk


def _load_token_rows(rows_ref):
    if len(rows_ref.shape) == 2:
        n = rows_ref.shape[0] // PACK_ROWS
        chunks = [rows_ref[pl.ds(c, n, stride=PACK_ROWS), :] for c in range(PACK_ROWS)]
    else:
        chunks = [rows_ref[:, c, :] for c in range(PACK_ROWS)]
    words = jnp.concatenate(chunks, axis=-1)
    lo = lax.bitcast_convert_type(words << 16, F32)
    hi = lax.bitcast_convert_type(words & jnp.uint32(HIGH_HALF), F32)
    return jnp.concatenate([lo, hi], axis=-1)


def _merge_kernel(ya_ref, yr_ref, ga_ref, gb_ref, x_ref, gm_ref, shf_ref, scf_ref, gpost_ref, gpre_ref,
                  wba_ref, wbr_ref, wout_ref, wr_hi_ref, wr_mid_ref, rbias_ref,
                  x1_ref, h2rows_ref, lpos_ref, before_ref, ntile_ref, wtok_ref, cnt_ref, *, tm, tr):
    @pl.when((pl.program_id(0) == 0) & (pl.program_id(1) == 0))
    def _():
        cnt_ref[...] = jnp.zeros(cnt_ref.shape, F32)

    pa = _dot(ya_ref[0], wba_ref[...])
    pr = _dot(yr_ref[0], wbr_ref[...])
    merged = _sigmoid(ga_ref[0].astype(F32)) * pa + _sigmoid(gb_ref[0].astype(F32)) * pr
    o = _dot(merged.astype(BF16), wout_ref[...])
    x1 = x_ref[0] + gm_ref[0] * (_rms_scale(o) * gpost_ref[...])
    x1_ref[0] = x1
    h2 = (_rms_scale(x1) * gpre_ref[...]) * (1.0 + scf_ref[0]) + shf_ref[0]
    _store_token_rows(h2rows_ref, h2)
    for j in range(tm // tr):
        _route(h2[j * tr:(j + 1) * tr, :], wr_hi_ref, wr_mid_ref, rbias_ref, cnt_ref,
               lpos_ref, before_ref, ntile_ref, wtok_ref, j, tr)


def _merge_call(y_attn, y_rnn, gate_a, gate_r, x, gate_m, shift_f, scale_f, g_post, g_pre,
                wba, wbr, wout, wr_hi, wr_mid, rbias, tm, tr):
    bsz, s, d = x.shape
    nt = s // tm
    nr = tm // tr
    tok = lambda w: pl.BlockSpec((1, tm, w), lambda b, i: (b, i, 0))
    mod = pl.BlockSpec((1, 1, d), lambda b, i: (b, 0, 0))
    return pl.pallas_call(
        functools.partial(_merge_kernel, tm=tm, tr=tr),
        grid=(bsz, nt),
        in_specs=[tok(ATT_WIDTH), tok(d), tok(d), tok(d), tok(d), mod, mod, mod,
                  _const_spec((1, d)), _const_spec((1, d)),
                  _const_spec(wba.shape), _const_spec(wbr.shape), _const_spec(wout.shape),
                  _const_spec(wr_hi.shape), _const_spec(wr_mid.shape), _const_spec(rbias.shape)],
        out_specs=[tok(d),
                   pl.BlockSpec((tm * PACK_ROWS, LANES), lambda b, i: (b * nt + i, 0)),
                   pl.BlockSpec((nr, SUBLANES, tr), lambda b, i: (b * nt + i, 0, 0)),
                   pl.BlockSpec((nr, N_EXPERTS, 1), lambda b, i: (b * nt + i, 0, 0)),
                   pl.BlockSpec((nr, N_EXPERTS, 1), lambda b, i: (b * nt + i, 0, 0)),
                   pl.BlockSpec((tm, LANES), lambda b, i: (b * nt + i, 0)),
                   pl.BlockSpec((N_EXPERTS, 1), lambda b, i: (0, 0))],
        out_shape=[jax.ShapeDtypeStruct((bsz, s, d), F32),
                   jax.ShapeDtypeStruct((bsz * s * PACK_ROWS, LANES), U32),
                   jax.ShapeDtypeStruct((bsz * nt * nr, SUBLANES, tr), I32),
                   jax.ShapeDtypeStruct((bsz * nt * nr, N_EXPERTS, 1), I32),
                   jax.ShapeDtypeStruct((bsz * nt * nr, N_EXPERTS, 1), I32),
                   jax.ShapeDtypeStruct((bsz * s, LANES), F32),
                   jax.ShapeDtypeStruct((N_EXPERTS, 1), F32)],
        compiler_params=_params("arbitrary", "arbitrary"),
        name="merge_route",
    )(y_attn, y_rnn, gate_a, gate_r, x, gate_m, shift_f, scale_f, g_post, g_pre,
      wba, wbr, wout, wr_hi, wr_mid, rbias)


WIN_LIST = LANES
ROW_COPY_UNROLL = 8


def _local_rows(tm):
    rows = tm * TOP_K + N_EXPERTS * WINDOW_ROWS
    assert rows // WINDOW_ROWS < WIN_LIST
    return rows


def _for_each_window(wlist_ref, fn):
    def body(w, c):
        fn(pl.multiple_of(w * WINDOW_ROWS, WINDOW_ROWS), wlist_ref[0, 0, w])
        return c

    lax.fori_loop(0, wlist_ref[0, 0, WIN_LIST - 1], body, 0)


def _dispatch_kernel(lpos_ref, wlist_ref, wlistp_ref, wlistpp_ref, zfrom_ref, zto_ref, h_ref, h2d_ref,
                     wsg_ref, wsu_ref, wsd_ref, xs_hbm, shared_ref, loc_sc, zero_sc, sem, *, tm):
    g = pl.program_id(0)
    cur = lax.rem(g, 2)

    @pl.when(g == 0)
    def _():
        loc_sc[...] = jnp.zeros(loc_sc.shape, U32)
        zero_sc[...] = jnp.zeros(zero_sc.shape, U32)

        def for_each_zero_window(fn):
            def per_range(j, c):
                def per_window(w, c2):
                    r0 = pl.multiple_of(zfrom_ref[j] + w * WINDOW_ROWS, WINDOW_ROWS)
                    fn(pltpu.make_async_copy(zero_sc, xs_hbm.at[pl.ds(r0, WINDOW_ROWS)], sem))
                    return c2
                lax.fori_loop(0, (zto_ref[j] - zfrom_ref[j]) >> WINDOW_SHIFT, per_window, 0)
                return c
            lax.fori_loop(0, zfrom_ref.shape[0], per_range, 0)

        for_each_zero_window(lambda cp: cp.start())
        for_each_zero_window(lambda cp: cp.wait())

    half = _local_rows(tm)

    def window(base):
        return lambda row0, slot0: pltpu.make_async_copy(
            loc_sc.at[pl.ds(pl.multiple_of(base + row0, WINDOW_ROWS), WINDOW_ROWS)],
            xs_hbm.at[pl.ds(slot0, WINDOW_ROWS)], sem)

    def start_windows(wl_ref, buf):
        _for_each_window(wl_ref, lambda r, s: window(buf * half)(r, s).start())

    def wait_windows(wl_ref, buf):
        _for_each_window(wl_ref, lambda r, s: window(buf * half)(r, s).wait())

    @pl.when(g > 1)
    def _():
        wait_windows(wlistpp_ref, cur)

    @pl.when(g > 0)
    def _():
        start_windows(wlistp_ref, 1 - cur)

    for t in range(tm):
        row = h_ref[t]
        for k in range(TOP_K):
            loc_sc[lpos_ref[0, k, t]] = row

    hb = _load_token_rows(h2d_ref).astype(BF16)
    a = _dot(hb, wsg_ref[...])
    u = _dot(hb, wsu_ref[...])
    shared_ref[...] = _dot(((a * _sigmoid(a)) * u).astype(BF16), wsd_ref[...]).astype(BF16)

    @pl.when(g == pl.num_programs(0) - 1)
    def _():
        @pl.when(g > 0)
        def _():
            wait_windows(wlistp_ref, 1 - cur)

        start_windows(wlist_ref, cur)
        wait_windows(wlist_ref, cur)


def _dispatch_call(lpos, wlist, zero_from, zero_to, h2_rows, wsg, wsu, wsd, n_rows, tm):
    t = h2_rows.shape[0] // PACK_ROWS
    smem = functools.partial(pl.BlockSpec, memory_space=pltpu.SMEM)
    return pl.pallas_call(
        functools.partial(_dispatch_kernel, tm=tm),
        grid=(t // tm,),
        in_specs=[smem((1, SUBLANES, tm), lambda i: (i, 0, 0)),
                  smem((1, 1, WIN_LIST), lambda i: (i, 0, 0)),
                  smem((1, 1, WIN_LIST), lambda i: (jnp.maximum(i - 1, 0), 0, 0)),
                  smem((1, 1, WIN_LIST), lambda i: (jnp.maximum(i - 2, 0), 0, 0)),
                  smem(), smem(),
                  pl.BlockSpec((tm, PACK_ROWS, LANES), lambda i: (i, 0, 0)),
                  pl.BlockSpec((tm * PACK_ROWS, LANES), lambda i: (i, 0)),
                  _const_spec(wsg.shape), _const_spec(wsu.shape), _const_spec(wsd.shape)],
        out_specs=[pl.BlockSpec(memory_space=pl.ANY), pl.BlockSpec((tm, D_MODEL), lambda i: (i, 0))],
        out_shape=[jax.ShapeDtypeStruct((n_rows, PACK_ROWS, LANES), U32),
                   jax.ShapeDtypeStruct((t, D_MODEL), BF16)],
        scratch_shapes=[pltpu.VMEM((2 * _local_rows(tm), PACK_ROWS, LANES), U32),
                        pltpu.VMEM((WINDOW_ROWS, PACK_ROWS, LANES), U32), pltpu.SemaphoreType.DMA],
        compiler_params=_params("arbitrary"),
        name="moe_dispatch",
    )(lpos, wlist, wlist, wlist, zero_from, zero_to, h2_rows.reshape(-1, PACK_ROWS, LANES), h2_rows, wsg, wsu, wsd)


def _expert_kernel(be_ref, nused_ref, xs_ref, wg_ref, wu_ref, wd_ref, ys_ref, wgb_sc, wub_sc, wdb_sc, *, rows):
    i = pl.program_id(0)

    @pl.when((i == 0) | (be_ref[i] != be_ref[jnp.maximum(i - 1, 0)]))
    def _():
        wgb_sc[...] = wg_ref[0].astype(BF16)
        wub_sc[...] = wu_ref[0].astype(BF16)
        wdb_sc[...] = wd_ref[0].astype(BF16)

    @pl.when(i < nused_ref[0])
    def _():
        xb = _load_token_rows(xs_ref).astype(BF16)
        g = _dot(xb, wgb_sc[...])
        u = _dot(xb, wub_sc[...])
        act = (g * _sigmoid(g)) * u
        _store_token_rows(ys_ref, _dot(act.astype(BF16), wdb_sc[...]))

    @pl.when(pl.program_id(0) >= nused_ref[0])
    def _():
        ys_ref[...] = jnp.zeros(ys_ref.shape, U32)


def _expert_call(block_e, n_used, xs, wg, wu, wd, rows):
    n_rows = xs.shape[0] // PACK_ROWS
    _, d, ff = wg.shape
    row_spec = pl.BlockSpec((rows * PACK_ROWS, LANES), lambda i, be, nu: (jnp.minimum(i, nu[0] - 1), 0))
    grid_spec = pltpu.PrefetchScalarGridSpec(
        num_scalar_prefetch=2,
        grid=(n_rows // rows,),
        in_specs=[row_spec,
                  pl.BlockSpec((1, d, ff), lambda i, be, nu: (be[i], 0, 0)),
                  pl.BlockSpec((1, d, ff), lambda i, be, nu: (be[i], 0, 0)),
                  pl.BlockSpec((1, ff, d), lambda i, be, nu: (be[i], 0, 0))],
        out_specs=pl.BlockSpec((rows * PACK_ROWS, LANES), lambda i, be, nu: (i, 0)),
        scratch_shapes=[pltpu.VMEM((d, ff), BF16), pltpu.VMEM((d, ff), BF16), pltpu.VMEM((ff, d), BF16)],
    )
    return pl.pallas_call(
        functools.partial(_expert_kernel, rows=rows),
        grid_spec=grid_spec,
        out_shape=jax.ShapeDtypeStruct(xs.shape, U32),
        compiler_params=_params("arbitrary"),
        name="moe_experts",
    )(block_e, n_used, xs, wg, wu, wd)


def _final_kernel(lpos0_ref, lposn_ref, tab0_ref, tabw_ref, tabi_ref, x1_ref, shared_ref, wtok_ref, gf_ref,
                  gpost_ref, ys_hbm, o_ref, win_sc, gnext_sc, gcur_sc, sem, *, tm):
    nt = pl.num_programs(1)
    g = pl.program_id(0) * nt + pl.program_id(1)
    n_tiles = pl.num_programs(0) * nt
    buf_next = lax.rem(g + 1, 2)
    buf_after = lax.rem(g, 2)
    half = _local_rows(tm)

    def window(buf):
        return lambda row0, slot0: pltpu.make_async_copy(
            ys_hbm.at[pl.ds(slot0, WINDOW_ROWS)],
            win_sc.at[pl.ds(pl.multiple_of(buf * half + row0, WINDOW_ROWS), WINDOW_ROWS)], sem.at[buf])

    def start_windows(tab_ref, buf):
        _for_each_window(tab_ref, lambda r, s: window(buf)(r, s).start())

    def wait_windows(tab_ref, buf):
        _for_each_window(tab_ref, lambda r, s: window(buf)(r, s).wait())

    def pick_rows(lpos_ref, t):
        for k in range(TOP_K):
            gnext_sc[k, t] = win_sc[lpos_ref[0, k, t]]

    def stage():
        for k in range(TOP_K):
            gcur_sc[k] = gnext_sc[k].reshape(tm * PACK_ROWS, LANES)

    @pl.when(g == 0)
    def _():
        start_windows(tab0_ref, 0)
        wait_windows(tab0_ref, 0)

        def body(i, c):
            for u in range(ROW_COPY_UNROLL):
                pick_rows(lpos0_ref, i * ROW_COPY_UNROLL + u)
            return c

        lax.fori_loop(0, tm // ROW_COPY_UNROLL, body, 0)
        stage()

        @pl.when(n_tiles > 1)
        def _():
            start_windows(tabw_ref, 1)

    @pl.when(g + 1 < n_tiles)
    def _():
        wait_windows(tabw_ref, buf_next)

    @pl.when(g + 2 < n_tiles)
    def _():
        start_windows(tabi_ref, buf_after)

    for t in range(tm):
        pick_rows(lposn_ref, t)

    w = wtok_ref[...]
    routed = w[:, 0:1] * _load_token_rows(gcur_sc.at[0])
    for k in range(1, TOP_K):
        routed = routed + w[:, k:k + 1] * _load_token_rows(gcur_sc.at[k])
    tot = routed + shared_ref[0].astype(F32)
    o_ref[0] = x1_ref[0] + gf_ref[0] * (_rms_scale(tot) * gpost_ref[...])

    stage()


def _final_call(lpos, tab, x1, shared, wtok, gate_f, g_post, ys, tm):
    bsz, s, d = x1.shape
    nt = s // tm
    n_tiles = bsz * nt
    smem = functools.partial(pl.BlockSpec, memory_space=pltpu.SMEM)
    tok = pl.BlockSpec((1, tm, d), lambda b, i: (b, i, 0))
    tab_shape = (1, 1, WIN_LIST)
    ahead = lambda k: (lambda b, i: (jnp.minimum(b * nt + i + k, n_tiles - 1), 0, 0))
    return pl.pallas_call(
        functools.partial(_final_kernel, tm=tm),
        grid=(bsz, nt),
        in_specs=[smem((1, SUBLANES, tm), lambda b, i: (0, 0, 0)),
                  smem((1, SUBLANES, tm), ahead(1)),
                  smem(tab_shape, lambda b, i: (0, 0, 0)),
                  smem(tab_shape, ahead(1)),
                  smem(tab_shape, ahead(2)),
                  tok, tok,
                  pl.BlockSpec((tm, LANES), lambda b, i: (b * nt + i, 0)),
                  pl.BlockSpec((1, 1, d), lambda b, i: (b, 0, 0)),
                  _const_spec((1, d)),
                  pl.BlockSpec(memory_space=pl.ANY)],
        out_specs=tok,
        out_shape=jax.ShapeDtypeStruct((bsz, s, d), F32),
        scratch_shapes=[pltpu.VMEM((2 * _local_rows(tm), PACK_ROWS, LANES), U32),
                        pltpu.VMEM((TOP_K, tm, PACK_ROWS, LANES), U32),
                        pltpu.VMEM((TOP_K, tm * PACK_ROWS, LANES), U32), pltpu.SemaphoreType.DMA((2,))],
        compiler_params=_params("arbitrary", "arbitrary"),
        name="moe_combine",
    )(lpos, lpos, tab, tab, tab, x1, shared, wtok, gate_f, g_post, ys)


def _block_diag_tiles(w):
    per = MXU_DIM // RNN_BLOCK_DIM
    nt = RNN_BLOCKS // per
    w4 = w.reshape(nt, per, RNN_BLOCK_DIM, RNN_BLOCK_DIM)
    eye = jnp.eye(per, dtype=w.dtype)
    return jnp.einsum("tnij,nm->tnimj", w4, eye).reshape(nt, MXU_DIM, MXU_DIM)


def _tile(n, pref):
    t = min(n, pref)
    assert n % t == 0, (n, t)
    return t


def _layer(x, c, w_ada, b_ada, g_pre_mix, g_post_mix, g_pre_ffn, g_post_ffn, w_in, b_forget,
           w_conv, b_conv, w_rg, b_rg, w_ig, b_ig, lam, w_branch_attn, w_branch_rnn, w_out,
           w_router, router_bias, w_exp_gate, w_exp_up, w_exp_down, w_sh_gate, w_sh_up, w_sh_down):
    bsz, s, d = x.shape
    t = bsz * s
    row = lambda v: v.reshape(1, -1)

    mod = _ada_call(c, w_ada, b_ada)
    shift_m, scale_m, gate_m, shift_f, scale_f, gate_f = [
        mod[:, j * d:(j + 1) * d].reshape(bsz, 1, d) for j in range(6)]

    o_f = 3 * ATT_WIDTH
    o_r = o_f + ATT_HEADS
    wqkv = w_in[:, :o_f].astype(BF16)
    wf = jnp.pad(w_in[:, o_f:o_r], ((0, 0), (0, LANES - ATT_HEADS))).astype(BF16)
    wr = w_in[:, o_r:].astype(BF16)
    bf_pad = jnp.pad(b_forget, (0, LANES - ATT_HEADS)).reshape(1, LANES)
    q, k, v, kbias, x_rnn, g_rnn, gate_a, gate_r = _inproj_call(
        x, shift_m, scale_m, row(g_pre_mix), wqkv, wf, wr, bf_pad, _tile(s, 512))
    tq = _tile(s, 512)
    y_attn = _attn_call(q, k, v, kbias, tq, _tile(tq, 512))

    wbd = jnp.concatenate([_block_diag_tiles(w_rg), _block_diag_tiles(w_ig)], axis=-1).astype(BF16)
    y_rnn = _rnn_call(x_rnn, g_rnn, w_conv, row(b_conv), wbd, row(b_rg), row(b_ig), row(lam), _tile(s, 256))

    wr_t = w_router.T
    wr_hi = wr_t.astype(BF16)
    wr_mid = (wr_t - wr_hi.astype(F32)).astype(BF16)
    tm_m = _tile(s, 512)
    tm_r = _tile(tm_m, 256)
    x1, h2_rows, lpos, before, n_tile, wtok, counts = _merge_call(
        y_attn, y_rnn, gate_a, gate_r, x, gate_m, shift_f, scale_f, row(g_post_mix), row(g_pre_ffn),
        w_branch_attn.astype(BF16), w_branch_rnn.astype(BF16), w_out.astype(BF16),
        wr_hi, wr_mid, router_bias.reshape(N_EXPERTS, 1), tm_m, tm_r)

    rows = 1024
    n_blocks =-(-(t * TOP_K + N_EXPERTS * WINDOW_ROWS) // rows) + N_EXPERTS
    cnt = counts[:, 0].astype(I32)
    padded = ((cnt + WINDOW_ROWS + rows - 1) // rows) * rows
    pend = jnp.cumsum(padded).astype(I32)
    pstart = pend - padded
    n_used = jnp.maximum(pend[-1] // rows, 1)
    blk = jnp.minimum(jnp.arange(n_blocks, dtype=I32), n_used - 1)
    block_e = jnp.minimum(jnp.sum((pend[None, :] <= (blk * rows)[:, None]).astype(I32), axis=1), N_EXPERTS - 1)

    n_win = (n_tile[:, :, 0] + WINDOW_ROWS - 1) // WINDOW_ROWS
    win_end = jnp.cumsum(n_win, axis=1)
    w_id = jnp.arange(WIN_LIST, dtype=I32)[None, :, None]
    owner = (jnp.sum((win_end[:, None, :] <= w_id).astype(I32), axis=2, keepdims=True)
             == jnp.arange(N_EXPERTS, dtype=I32)[None, None, :]).astype(I32)
    first_slot = pstart[None, :] + before[:, :, 0] - (win_end - n_win) * WINDOW_ROWS
    wlist = jnp.sum(owner * first_slot[:, None, :], axis=2) + w_id[:, :, 0] * WINDOW_ROWS
    wlist = jnp.where(w_id[:, :, 0] == WIN_LIST - 1, win_end[:, -1:], wlist).astype(I32)[:, None, :]
    n_rt = lpos.shape[0]
    lpos = lpos + (jnp.arange(n_rt, dtype=I32) % 2)[:, None, None] * _local_rows(tm_r)

    rows3 = lambda a: a.reshape(-1, PACK_ROWS, LANES)
    rows2 = lambda a: a.reshape(-1, LANES)
    zero_from = jnp.concatenate([pstart + cnt // WINDOW_ROWS * WINDOW_ROWS, pend[-1:]]).astype(I32)
    zero_to = jnp.concatenate([pend, jnp.full((1,), n_blocks * rows, I32)]).astype(I32)
    xs, shared = _dispatch_call(lpos, wlist, zero_from, zero_to, h2_rows,
                                w_sh_gate.astype(BF16), w_sh_up.astype(BF16), w_sh_down.astype(BF16),
                                n_blocks * rows, tm_r)
    ys = rows3(_expert_call(block_e, n_used.reshape(1), rows2(xs),
                            w_exp_gate, w_exp_up, w_exp_down, rows))
    return _final_call(lpos, wlist, x1, shared.reshape(bsz, s, d), wtok, gate_f, row(g_post_ffn), ys, tm_r)


def kernel(x, c, w_ada, b_ada, g_pre_mix, g_post_mix, g_pre_ffn, g_post_ffn, w_in, b_forget, w_conv, b_conv, w_rg, b_rg, w_ig, b_ig, rglru_lambda, w_branch_attn, w_branch_rnn, w_out, w_router, router_bias, w_exp_gate, w_exp_up, w_exp_down, w_sh_gate, w_sh_up, w_sh_down):
    depth = w_ada.shape[0]
    for l in range(depth):
        x = _layer(x, c, w_ada[l], b_ada[l], g_pre_mix[l], g_post_mix[l], g_pre_ffn[l], g_post_ffn[l],
                   w_in[l], b_forget[l], w_conv[l], b_conv[l], w_rg[l], b_rg[l], w_ig[l], b_ig[l],
                   rglru_lambda[l], w_branch_attn[l], w_branch_rnn[l], w_out[l], w_router[l],
                   router_bias[l], w_exp_gate[l], w_exp_up[l], w_exp_down[l],
                   w_sh_gate[l], w_sh_up[l], w_sh_down[l])
    return x
```

```python
import functools

import jax
import jax.numpy as jnp
from jax import lax
from jax.experimental import pallas as pl
from jax.experimental.pallas import tpu as pltpu

F32 = jnp.float32
BF16 = jnp.bfloat16
I32 = jnp.int32

D_MODEL = 1024
ATT_HEADS = 8
ATT_HEAD_DIM = 64
ATT_WIDTH = ATT_HEADS * ATT_HEAD_DIM
RNN_WIDTH = D_MODEL
RNN_BLOCKS = 16
RNN_BLOCK_DIM = RNN_WIDTH // RNN_BLOCKS
CONV_WIDTH = 4
RGLRU_C = 8.0
N_EXPERTS = 64
TOP_K = 6
N_GROUPS = 8
GROUP_SIZE = N_EXPERTS // N_GROUPS
TOPK_GROUPS = 4
EXPERT_FF = D_MODEL // 4
ROUTE_SCALE = 2.5
NORM_EPS = 1e-6

LANES = 128
SUBLANES = 8
MXU_DIM = 256
VMEM_LIMIT_BYTES = 56 * 1024 * 1024

NEG_BIG = -1e30


def _sigmoid(x):
    return 0.5 * jnp.tanh(0.5 * x) + 0.5


def _rms_scale(x):
    return x * lax.rsqrt(jnp.mean(x * x, axis=-1, keepdims=True) + NORM_EPS)


def _dot(a, b):
    return jnp.dot(a, b, preferred_element_type=F32)


def _dot_nt(a, b):
    return lax.dot_general(a, b, (((1,), (1,)), ((), ())), preferred_element_type=F32)


def _split3(a):
    hi = a.astype(BF16)
    r1 = a - hi.astype(F32)
    mid = r1.astype(BF16)
    lo = (r1 - mid.astype(F32)).astype(BF16)
    return hi, mid, lo


def _params(*sem):
    return pltpu.CompilerParams(dimension_semantics=sem, vmem_limit_bytes=VMEM_LIMIT_BYTES)


def _const_spec(shape):
    nd = len(shape)
    return pl.BlockSpec(shape, lambda *_: (0,) * nd)


def _ada_kernel(c_ref, w_ref, b_ref, o_ref):
    c = c_ref[...]
    sc = c * _sigmoid(c)
    s_hi, s_mid, _ = _split3(sc)
    w_hi, w_mid, _ = _split3(w_ref[...])
    acc = _dot(s_hi, w_hi) + _dot(s_hi, w_mid) + _dot(s_mid, w_hi)
    o_ref[...] = acc + b_ref[...]


def _ada_call(c, w_ada, b_ada):
    bsz, d = c.shape
    n = w_ada.shape[1]
    tn = d
    return pl.pallas_call(
        _ada_kernel,
        grid=(n // tn,),
        in_specs=[
            pl.BlockSpec((bsz, d), lambda j: (0, 0)),
            pl.BlockSpec((d, tn), lambda j: (0, j)),
            pl.BlockSpec((1, tn), lambda j: (0, j)),
        ],
        out_specs=pl.BlockSpec((bsz, tn), lambda j: (0, j)),
        out_shape=jax.ShapeDtypeStruct((bsz, n), F32),
        compiler_params=_params("arbitrary"),
        name="ada_mod",
    )(c, w_ada, b_ada.reshape(1, n))


N_BIAS_PIECES = 3
CUM_BLOCK = 256


def _bias_placement():
    h = jnp.arange(LANES)[:, None]
    lane = jnp.arange(ATT_WIDTH)[None, :]
    mats = []
    for piece in range(N_BIAS_PIECES):
        target = (h // 2) * LANES + ATT_HEAD_DIM * (1 - h % 2) + piece
        mats.append(((lane == target) & (h < ATT_HEADS)).astype(BF16))
    return jnp.concatenate(mats, axis=0)


def _key_bias(f_logit, bf_ref, place_ref, kb_ref, cum_sc):
    tm = f_logit.shape[0]
    blk = min(tm, CUM_BLOCK)
    row = lax.broadcasted_iota(I32, (blk, blk), 0)
    col = lax.broadcasted_iota(I32, (blk, blk), 1)
    tri = (row >= col).astype(BF16)
    carry = cum_sc[...]
    for i in range(tm // blk):
        z = f_logit[i * blk:(i + 1) * blk, :] + bf_ref[...]
        lf = jnp.minimum(z, 0.0) - jnp.log1p(jnp.exp(-jnp.abs(z)))
        c3 = _dot(tri, jnp.concatenate(_split3(lf), axis=1))
        c = c3[:, :LANES] + c3[:, LANES:2 * LANES] + c3[:, 2 * LANES:] + carry
        carry = c[blk - 1:blk, :]
        kb = _dot(jnp.concatenate(_split3(-c), axis=1), place_ref[...])
        kb_ref[0, i * blk:(i + 1) * blk, :] = kb.astype(BF16)
    cum_sc[...] = carry


def _inproj_kernel(x_ref, sh_ref, sc_ref, g_ref, wqkv_ref, wf_ref, wr_ref, bf_ref, place_ref,
                   q_ref, k_ref, v_ref, kb_ref, xr_ref, gr_ref, ga_ref, gb_ref, cum_sc):
    @pl.when(pl.program_id(1) == 0)
    def _():
        cum_sc[...] = jnp.zeros(cum_sc.shape, F32)

    x = x_ref[0]
    h = (_rms_scale(x) * g_ref[...]) * (1.0 + sc_ref[0]) + sh_ref[0]
    hb = h.astype(BF16)
    for j, ref in enumerate((q_ref, k_ref, v_ref)):
        ref[0] = _dot(hb, wqkv_ref[:, j * ATT_WIDTH:(j + 1) * ATT_WIDTH]).astype(BF16)
    _key_bias(_dot(hb, wf_ref[...]), bf_ref, place_ref, kb_ref, cum_sc)
    half = D_MODEL // 2
    for j, ref in enumerate((xr_ref, gr_ref, ga_ref, gb_ref)):
        for c in range(2):
            lo = j * D_MODEL + c * half
            ref[0, :, c * half:(c + 1) * half] = _dot(hb, wr_ref[:, lo:lo + half]).astype(BF16)


def _inproj_call(x, shift, scale, g, wqkv, wf, wr, b_forget_pad, tm):
    bsz, s, d = x.shape
    nt = s // tm
    tok = lambda w: pl.BlockSpec((1, tm, w), lambda b, i: (b, i, 0))
    mod = pl.BlockSpec((1, 1, d), lambda b, i: (b, 0, 0))
    place = _bias_placement()
    out_w = (ATT_WIDTH,) * 4 + (d,) * 4
    return pl.pallas_call(
        _inproj_kernel,
        grid=(bsz, nt),
        in_specs=[tok(d), mod, mod, _const_spec((1, d)),
                  _const_spec(wqkv.shape), _const_spec(wf.shape), _const_spec(wr.shape),
                  _const_spec((1, LANES)), _const_spec(place.shape)],
        out_specs=[tok(w) for w in out_w],
        out_shape=[jax.ShapeDtypeStruct((bsz, s, w), BF16) for w in out_w],
        scratch_shapes=[pltpu.VMEM((1, LANES), F32)],
        compiler_params=_params("arbitrary", "arbitrary"),
        name="in_proj",
    )(x, shift, scale, g, wqkv, wf, wr, b_forget_pad, place)


ATT_VT_ROWS = ATT_HEAD_DIM + 16


def _attn_kernel(q_ref, k_ref, v_ref, kb_ref, o_ref, vt_sc, st_sc, m_sc, acc_sc, *, tq, tk):
    s = q_ref.shape[1]
    nq, nk, ratio = s // tq, s // tk, tq // tk
    lane = lax.broadcasted_iota(I32, (1, LANES), 1)
    head_lanes = (lane < ATT_HEAD_DIM, lane >= ATT_HEAD_DIM)
    one_lanes = ((lane >= ATT_HEAD_DIM) & (lane < ATT_HEAD_DIM + N_BIAS_PIECES), lane < N_BIAS_PIECES)
    key_minus_query = (lax.broadcasted_iota(I32, (tk, tq), 0) - lax.broadcasted_iota(I32, (tk, tq), 1))

    vt = v_ref[0].astype(F32).T
    ones = jnp.ones((ATT_VT_ROWS - ATT_HEAD_DIM, s), F32)
    for h in range(2):
        vth = jnp.concatenate([vt[h * ATT_HEAD_DIM:(h + 1) * ATT_HEAD_DIM, :], ones], axis=0).astype(BF16)
        for j in range(nk):
            vt_sc[h, j] = vth[:, j * tk:(j + 1) * tk]

    def q_block(qi):
        q0 = qi * tq
        qs = q_ref[0, pl.ds(q0, tq), :] * (ATT_HEAD_DIM ** -0.5)
        qa = [jnp.where(head_lanes[h], qs, one_lanes[h].astype(BF16)) for h in range(2)]
        m_sc[...] = jnp.full(m_sc.shape, NEG_BIG, F32)
        acc_sc[...] = jnp.zeros(acc_sc.shape, F32)

        def scores(kj):
            k0 = kj * tk
            kb = k_ref[0, pl.ds(k0, tk), :]
            bias = kb_ref[0, pl.ds(k0, tk), :]
            return [_dot_nt(jnp.where(head_lanes[h], kb, bias), qa[h]) for h in range(2)]

        def online_update(st, m_old, acc, vt):
            m_new = jnp.maximum(m_old, jnp.max(st, axis=0, keepdims=True))
            p = jnp.exp(st - m_new).astype(BF16)
            return m_new, jnp.exp(m_old - m_new) * acc + _dot(vt, p)

        def accumulate(kj, masked):
            for h in range(2):
                if not masked:
                    m_sc[h], acc_sc[h] = online_update(st_sc[h], m_sc[h], acc_sc[h], vt_sc[h, kj])
                elif ratio == 1:
                    hk = tk // 2
                    tri = key_minus_query[:hk, :] <= 0
                    st_a = jnp.where(tri, st_sc[h, :hk, :], NEG_BIG)
                    m_a, acc_a = online_update(st_a, m_sc[h], acc_sc[h], vt_sc[h, kj, :, :hk])
                    st_b = jnp.where(tri[:, :hk], st_sc[h, hk:, hk:], NEG_BIG)
                    m_b, acc_b = online_update(st_b, m_a[:, hk:], acc_a[:, hk:], vt_sc[h, kj, :, hk:])
                    m_sc[h] = jnp.concatenate([m_a[:, :hk], m_b], axis=1)
                    acc_sc[h] = jnp.concatenate([acc_a[:, :hk], acc_b], axis=1)
                else:
                    st = jnp.where(key_minus_query <= q0 - kj * tk, st_sc[h], NEG_BIG)
                    m_sc[h], acc_sc[h] = online_update(st, m_sc[h], acc_sc[h], vt_sc[h, kj])

        def put(sts):
            for h in range(2):
                st_sc[h] = sts[h]

        n_full = qi * ratio
        put(scores(0))

        for kj in range(n_full):
            nxt = scores(kj + 1)
            accumulate(kj, False)
            put(nxt)

        for d in range(ratio):
            nxt = scores(n_full + d + 1) if d + 1 < ratio else None
            accumulate(n_full + d, True)
            if nxt is not None:
                put(nxt)

        outs = []
        for h in range(2):
            acc = acc_sc[h]
            outs.append(acc[:ATT_HEAD_DIM, :] * (1.0 / acc[ATT_HEAD_DIM:ATT_HEAD_DIM + 1, :]))
        o_ref[0, pl.ds(q0, tq), :] = jnp.concatenate(outs, axis=0).T.astype(BF16)

    for qi in range(nq):
        q_block(qi)


def _attn_call(q, k, v, kbias, tq, tk):
    bsz, s, _ = q.shape
    qkv = pl.BlockSpec((1, s, LANES), lambda b, p: (b, 0, p))
    return pl.pallas_call(
        functools.partial(_attn_kernel, tq=tq, tk=tk),
        grid=(bsz, ATT_HEADS // 2),
        in_specs=[qkv, qkv, qkv, qkv],
        out_specs=qkv,
        out_shape=jax.ShapeDtypeStruct((bsz, s, ATT_WIDTH), BF16),
        scratch_shapes=[pltpu.VMEM((2, s // tk, ATT_VT_ROWS, tk), BF16), pltpu.VMEM((2, tk, tq), F32),
                        pltpu.VMEM((2, 1, tq), F32), pltpu.VMEM((2, ATT_VT_ROWS, tq), F32)],
        compiler_params=_params("arbitrary", "arbitrary"),
        name="fox_attention",
    )(q, k, v, kbias)


def _gelu_tanh(x):
    c = 0.7978845608028654
    return 0.5 * x * (1.0 + jnp.tanh(c * (x + 0.044715 * (x * x * x))))


def _rnn_kernel(xr_ref, gr_ref, wc_ref, bc_ref, wbd_ref, brg_ref, big_ref, lam_ref, o_ref,
                tail_sc, hc_sc, a_sc, b_sc, h_sc, *, ts):
    @pl.when(pl.program_id(1) == 0)
    def _():
        tail_sc[...] = jnp.zeros(tail_sc.shape, F32)
        hc_sc[...] = jnp.zeros(hc_sc.shape, F32)

    x = xr_ref[0].astype(F32)
    full = jnp.concatenate([tail_sc[...], x], axis=0)
    u = bc_ref[...] + wc_ref[CONV_WIDTH - 1:CONV_WIDTH, :] * x
    for j in range(CONV_WIDTH - 1):
        off = SUBLANES - (CONV_WIDTH - 1) + j
        u = u + wc_ref[j:j + 1, :] * full[off:off + ts, :]
    tail_sc[...] = x[ts - SUBLANES:ts, :]

    lam = lam_ref[...]
    neg_sp = -(jnp.maximum(-lam, 0.0) + jnp.log1p(jnp.exp(-jnp.abs(lam))))
    for jt in range(RNN_WIDTH // MXU_DIM):
        cs = slice(jt * MXU_DIM, (jt + 1) * MXU_DIM)
        uc = u[:, cs]
        g = _dot(uc.astype(BF16), wbd_ref[jt])
        r = _sigmoid(g[:, :MXU_DIM] + brg_ref[:, cs])
        ig = _sigmoid(g[:, MXU_DIM:] + big_ref[:, cs])
        a = jnp.exp(r * (RGLRU_C * neg_sp[:, cs]))
        a_sc[:, cs] = a
        v = 1.0 - a * a
        b_sc[:, cs] = jnp.where(v > 0.0, v * lax.rsqrt(v), 0.0) * (ig * uc)

    srow = lax.broadcasted_iota(I32, (SUBLANES, RNN_WIDTH), 0)

    def scan_body(i, carry):
        r0 = pl.multiple_of(i * SUBLANES, SUBLANES)
        a = a_sc[pl.ds(r0, SUBLANES), :]
        b = b_sc[pl.ds(r0, SUBLANES), :]
        b = b + jnp.where(srow == 0, a * carry, 0.0)
        for sh in (1, 2, 4):
            keep = srow >= sh
            b = jnp.where(keep, b + a * pltpu.roll(b, sh, 0), b)
            if sh < SUBLANES // 2:
                a = jnp.where(keep, a * pltpu.roll(a, sh, 0), a)
        h_sc[pl.ds(r0, SUBLANES), :] = b
        return jnp.broadcast_to(b[SUBLANES - 1:SUBLANES, :], (SUBLANES, RNN_WIDTH))

    hc_sc[...] = lax.fori_loop(0, ts // SUBLANES, scan_body, hc_sc[...])
    o_ref[0] = (h_sc[...] * _gelu_tanh(gr_ref[0].astype(F32))).astype(BF16)


def _rnn_call(x_rnn, g_rnn, w_conv, b_conv, wbd, b_rg, b_ig, lam, ts):
    bsz, s, w = x_rnn.shape
    tok = pl.BlockSpec((1, ts, w), lambda b, i: (b, i, 0))
    return pl.pallas_call(
        functools.partial(_rnn_kernel, ts=ts),
        grid=(bsz, s // ts),
        in_specs=[tok, tok, _const_spec(w_conv.shape), _const_spec((1, w)), _const_spec(wbd.shape),
                  _const_spec((1, w)), _const_spec((1, w)), _const_spec((1, w))],
        out_specs=tok,
        out_shape=jax.ShapeDtypeStruct((bsz, s, w), BF16),
        scratch_shapes=[pltpu.VMEM((SUBLANES, w), F32), pltpu.VMEM((SUBLANES, w), F32),
                        pltpu.VMEM((ts, w), F32), pltpu.VMEM((ts, w), F32), pltpu.VMEM((ts, w), F32)],
        compiler_params=_params("arbitrary", "arbitrary"),
        name="rglru",
    )(x_rnn, g_rnn, w_conv, b_conv, wbd, b_rg, b_ig, lam)


WINDOW_SHIFT = 5
WINDOW_ROWS = 1 << WINDOW_SHIFT


def _route(h2, wr_hi_ref, wr_mid_ref, rbias_ref, cnt_ref, lpos_ref, before_ref, ntile_ref, wtok_ref, j, tm):
    h_hi, h_mid, _ = _split3(h2)
    logits = _dot_nt(wr_hi_ref[...], h_hi) + _dot_nt(wr_hi_ref[...], h_mid) + _dot_nt(wr_mid_ref[...], h_hi)
    scores = _sigmoid(logits)
    sel = scores + rbias_ref[...]

    giota = lax.broadcasted_iota(I32, (N_GROUPS, tm), 0)
    gs = jnp.zeros((N_GROUPS, tm), F32)
    for g in range(N_GROUPS):
        blk = sel[g * GROUP_SIZE:(g + 1) * GROUP_SIZE, :]
        m1 = jnp.max(blk, axis=0, keepdims=True)
        i1 = jnp.min(jnp.where(blk == m1, giota, N_EXPERTS), axis=0, keepdims=True)
        m2 = jnp.max(jnp.where(giota == i1, -jnp.inf, blk), axis=0, keepdims=True)
        gs = jnp.where(giota == g, m1 + m2, gs)

    gsel = jnp.zeros((N_GROUPS, tm), F32)
    for _ in range(TOPK_GROUPS):
        m = jnp.max(gs, axis=0, keepdims=True)
        idx = jnp.min(jnp.where(gs == m, giota, N_EXPERTS), axis=0, keepdims=True)
        hit = giota == idx
        gsel = jnp.where(hit, 1.0, gsel)
        gs = jnp.where(hit, -jnp.inf, gs)

    masked = jnp.concatenate(
        [jnp.where(gsel[g:g + 1, :] > 0.0, sel[g * GROUP_SIZE:(g + 1) * GROUP_SIZE, :], -jnp.inf)
         for g in range(N_GROUPS)], axis=0)

    eiota = lax.broadcasted_iota(I32, (N_EXPERTS, tm), 0)
    chosen = jnp.zeros((N_EXPERTS, tm), F32)
    idxs, wts = [], []
    for _ in range(TOP_K):
        m = jnp.max(masked, axis=0, keepdims=True)
        idx = jnp.min(jnp.where(masked == m, eiota, N_EXPERTS), axis=0, keepdims=True)
        hit = eiota == idx
        wts.append(jnp.sum(jnp.where(hit, scores, 0.0), axis=0, keepdims=True))
        idxs.append(idx)
        chosen = jnp.where(hit, 1.0, chosen)
        masked = jnp.where(hit, -jnp.inf, masked)
    wsum = wts[0]
    for w in wts[1:]:
        wsum = wsum + w

    srow = lax.broadcasted_iota(I32, (tm, tm), 0)
    scol = lax.broadcasted_iota(I32, (tm, tm), 1)
    earlier = (srow < scol).astype(BF16)
    local_rank = _dot(chosen.astype(BF16), earlier)
    n_e = jnp.sum(chosen, axis=1, keepdims=True)
    before_ref[j] = cnt_ref[...].astype(I32)
    ntile_ref[j] = n_e.astype(I32)
    cnt_ref[...] = cnt_ref[...] + n_e
    n_pad = (((n_e.astype(I32) + (WINDOW_ROWS - 1)) >> WINDOW_SHIFT) << WINDOW_SHIFT).astype(F32)
    erow = lax.broadcasted_iota(I32, (N_EXPERTS, N_EXPERTS), 0)
    ecol = lax.broadcasted_iota(I32, (N_EXPERTS, N_EXPERTS), 1)
    off = _dot((ecol < erow).astype(BF16), jnp.broadcast_to(n_pad, (N_EXPERTS, LANES)).astype(BF16))[:, :1]
    local_pos = off + local_rank

    riota = lax.broadcasted_iota(I32, (SUBLANES, tm), 0)
    lpos = jnp.zeros((SUBLANES, tm), I32)
    wrow = jnp.zeros((SUBLANES, tm), F32)
    for r in range(TOP_K):
        lp = jnp.sum(jnp.where(eiota == idxs[r], local_pos, 0.0), axis=0, keepdims=True)
        lpos = jnp.where(riota == r, lp.astype(I32), lpos)
        wrow = jnp.where(riota == r, wts[r] / wsum * ROUTE_SCALE, wrow)
    lpos_ref[j] = lpos
    wpad = jnp.concatenate([wrow, jnp.zeros((LANES - SUBLANES, tm), F32)], axis=0)
    wtok_ref[j * tm:(j + 1) * tm, :] = wpad.T


U32 = jnp.uint32
PACK_ROWS = D_MODEL // (2 * LANES)
HIGH_HALF = 0xFFFF0000


def _store_token_rows(rows_ref, val):
    half = D_MODEL // 2
    n = val.shape[0]
    bits = lax.bitcast_convert_type(val.astype(BF16).astype(F32), U32)
    words = (bits[:, half:] & jnp.uint32(HIGH_HALF)) | (bits[:, :half] >> 16)
    for c in range(PACK_ROWS):
        chunk = words[:, c * LANES:(c + 1) * LANES]
        if len(rows_ref.shape) == 2:
            rows_ref[pl.ds(c, n, stride=PACK_ROWS), :] = chunk
        else:
            rows_ref[:, c, :] = chunk


def _load_token_rows(rows_ref):
    if len(rows_ref.shape) == 2:
        n = rows_ref.shape[0] // PACK_ROWS
        chunks = [rows_ref[pl.ds(c, n, stride=PACK_ROWS), :] for c in range(PACK_ROWS)]
    else:
        chunks = [rows_ref[:, c, :] for c in range(PACK_ROWS)]
    words = jnp.concatenate(chunks, axis=-1)
    lo = lax.bitcast_convert_type(words << 16, F32)
    hi = lax.bitcast_convert_type(words & jnp.uint32(HIGH_HALF), F32)
    return jnp.concatenate([lo, hi], axis=-1)


def _merge_kernel(ya_ref, yr_ref, ga_ref, gb_ref, x_ref, gm_ref, shf_ref, scf_ref, gpost_ref, gpre_ref,
                  wba_ref, wbr_ref, wout_ref, wr_hi_ref, wr_mid_ref, rbias_ref,
                  x1_ref, h2rows_ref, lpos_ref, before_ref, ntile_ref, wtok_ref, cnt_ref, *, tm, tr):
    @pl.when((pl.program_id(0) == 0) & (pl.program_id(1) == 0))
    def _():
        cnt_ref[...] = jnp.zeros(cnt_ref.shape, F32)

    pa = _dot(ya_ref[0], wba_ref[...])
    pr = _dot(yr_ref[0], wbr_ref[...])
    merged = _sigmoid(ga_ref[0].astype(F32)) * pa + _sigmoid(gb_ref[0].astype(F32)) * pr
    o = _dot(merged.astype(BF16), wout_ref[...])
    x1 = x_ref[0] + gm_ref[0] * (_rms_scale(o) * gpost_ref[...])
    x1_ref[0] = x1
    h2 = (_rms_scale(x1) * gpre_ref[...]) * (1.0 + scf_ref[0]) + shf_ref[0]
    _store_token_rows(h2rows_ref, h2)
    for j in range(tm // tr):
        _route(h2[j * tr:(j + 1) * tr, :], wr_hi_ref, wr_mid_ref, rbias_ref, cnt_ref,
               lpos_ref, before_ref, ntile_ref, wtok_ref, j, tr)


def _merge_call(y_attn, y_rnn, gate_a, gate_r, x, gate_m, shift_f, scale_f, g_post, g_pre,
                wba, wbr, wout, wr_hi, wr_mid, rbias, tm, tr):
    bsz, s, d = x.shape
    nt = s // tm
    nr = tm // tr
    tok = lambda w: pl.BlockSpec((1, tm, w), lambda b, i: (b, i, 0))
    mod = pl.BlockSpec((1, 1, d), lambda b, i: (b, 0, 0))
    return pl.pallas_call(
        functools.partial(_merge_kernel, tm=tm, tr=tr),
        grid=(bsz, nt),
        in_specs=[tok(ATT_WIDTH), tok(d), tok(d), tok(d), tok(d), mod, mod, mod,
                  _const_spec((1, d)), _const_spec((1, d)),
                  _const_spec(wba.shape), _const_spec(wbr.shape), _const_spec(wout.shape),
                  _const_spec(wr_hi.shape), _const_spec(wr_mid.shape), _const_spec(rbias.shape)],
        out_specs=[tok(d),
                   pl.BlockSpec((tm * PACK_ROWS, LANES), lambda b, i: (b * nt + i, 0)),
                   pl.BlockSpec((nr, SUBLANES, tr), lambda b, i: (b * nt + i, 0, 0)),
                   pl.BlockSpec((nr, N_EXPERTS, 1), lambda b, i: (b * nt + i, 0, 0)),
                   pl.BlockSpec((nr, N_EXPERTS, 1), lambda b, i: (b * nt + i, 0, 0)),
                   pl.BlockSpec((tm, LANES), lambda b, i: (b * nt + i, 0)),
                   pl.BlockSpec((N_EXPERTS, 1), lambda b, i: (0, 0))],
        out_shape=[jax.ShapeDtypeStruct((bsz, s, d), F32),
                   jax.ShapeDtypeStruct((bsz * s * PACK_ROWS, LANES), U32),
                   jax.ShapeDtypeStruct((bsz * nt * nr, SUBLANES, tr), I32),
                   jax.ShapeDtypeStruct((bsz * nt * nr, N_EXPERTS, 1), I32),
                   jax.ShapeDtypeStruct((bsz * nt * nr, N_EXPERTS, 1), I32),
                   jax.ShapeDtypeStruct((bsz * s, LANES), F32),
                   jax.ShapeDtypeStruct((N_EXPERTS, 1), F32)],
        compiler_params=_params("arbitrary", "arbitrary"),
        name="merge_route",
    )(y_attn, y_rnn, gate_a, gate_r, x, gate_m, shift_f, scale_f, g_post, g_pre,
      wba, wbr, wout, wr_hi, wr_mid, rbias)


WIN_LIST = LANES
ROW_COPY_UNROLL = 8


def _local_rows(tm):
    rows = tm * TOP_K + N_EXPERTS * WINDOW_ROWS
    assert rows // WINDOW_ROWS < WIN_LIST
    return rows


def _for_each_window(wlist_ref, fn):
    def body(w, c):
        fn(pl.multiple_of(w * WINDOW_ROWS, WINDOW_ROWS), wlist_ref[0, 0, w])
        return c

    lax.fori_loop(0, wlist_ref[0, 0, WIN_LIST - 1], body, 0)


def _dispatch_kernel(lpos_ref, wlist_ref, wlistp_ref, wlistpp_ref, zfrom_ref, zto_ref, h_ref, h2d_ref,
                     wsg_ref, wsu_ref, wsd_ref, xs_hbm, shared_ref, loc_sc, zero_sc, sem, *, tm):
    g = pl.program_id(0)
    cur = lax.rem(g, 2)

    @pl.when(g == 0)
    def _():
        loc_sc[...] = jnp.zeros(loc_sc.shape, U32)
        zero_sc[...] = jnp.zeros(zero_sc.shape, U32)

        def for_each_zero_window(fn):
            def per_range(j, c):
                def per_window(w, c2):
                    r0 = pl.multiple_of(zfrom_ref[j] + w * WINDOW_ROWS, WINDOW_ROWS)
                    fn(pltpu.make_async_copy(zero_sc, xs_hbm.at[pl.ds(r0, WINDOW_ROWS)], sem))
                    return c2
                lax.fori_loop(0, (zto_ref[j] - zfrom_ref[j]) >> WINDOW_SHIFT, per_window, 0)
                return c
            lax.fori_loop(0, zfrom_ref.shape[0], per_range, 0)

        for_each_zero_window(lambda cp: cp.start())
        for_each_zero_window(lambda cp: cp.wait())

    half = _local_rows(tm)

    def window(base):
        return lambda row0, slot0: pltpu.make_async_copy(
            loc_sc.at[pl.ds(pl.multiple_of(base + row0, WINDOW_ROWS), WINDOW_ROWS)],
            xs_hbm.at[pl.ds(slot0, WINDOW_ROWS)], sem)

    def start_windows(wl_ref, buf):
        _for_each_window(wl_ref, lambda r, s: window(buf * half)(r, s).start())

    def wait_windows(wl_ref, buf):
        _for_each_window(wl_ref, lambda r, s: window(buf * half)(r, s).wait())

    @pl.when(g > 1)
    def _():
        wait_windows(wlistpp_ref, cur)

    @pl.when(g > 0)
    def _():
        start_windows(wlistp_ref, 1 - cur)

    for t in range(tm):
        row = h_ref[t]
        for k in range(TOP_K):
            loc_sc[lpos_ref[0, k, t]] = row

    hb = _load_token_rows(h2d_ref).astype(BF16)
    a = _dot(hb, wsg_ref[...])
    u = _dot(hb, wsu_ref[...])
    shared_ref[...] = _dot(((a * _sigmoid(a)) * u).astype(BF16), wsd_ref[...]).astype(BF16)

    @pl.when(g == pl.num_programs(0) - 1)
    def _():
        @pl.when(g > 0)
        def _():
            wait_windows(wlistp_ref, 1 - cur)

        start_windows(wlist_ref, cur)
        wait_windows(wlist_ref, cur)


def _dispatch_call(lpos, wlist, zero_from, zero_to, h2_rows, wsg, wsu, wsd, n_rows, tm):
    t = h2_rows.shape[0] // PACK_ROWS
    smem = functools.partial(pl.BlockSpec, memory_space=pltpu.SMEM)
    return pl.pallas_call(
        functools.partial(_dispatch_kernel, tm=tm),
        grid=(t // tm,),
        in_specs=[smem((1, SUBLANES, tm), lambda i: (i, 0, 0)),
                  smem((1, 1, WIN_LIST), lambda i: (i, 0, 0)),
                  smem((1, 1, WIN_LIST), lambda i: (jnp.maximum(i - 1, 0), 0, 0)),
                  smem((1, 1, WIN_LIST), lambda i: (jnp.maximum(i - 2, 0), 0, 0)),
                  smem(), smem(),
                  pl.BlockSpec((tm, PACK_ROWS, LANES), lambda i: (i, 0, 0)),
                  pl.BlockSpec((tm * PACK_ROWS, LANES), lambda i: (i, 0)),
                  _const_spec(wsg.shape), _const_spec(wsu.shape), _const_spec(wsd.shape)],
        out_specs=[pl.BlockSpec(memory_space=pl.ANY), pl.BlockSpec((tm, D_MODEL), lambda i: (i, 0))],
        out_shape=[jax.ShapeDtypeStruct((n_rows, PACK_ROWS, LANES), U32),
                   jax.ShapeDtypeStruct((t, D_MODEL), BF16)],
        scratch_shapes=[pltpu.VMEM((2 * _local_rows(tm), PACK_ROWS, LANES), U32),
                        pltpu.VMEM((WINDOW_ROWS, PACK_ROWS, LANES), U32), pltpu.SemaphoreType.DMA],
        compiler_params=_params("arbitrary"),
        name="moe_dispatch",
    )(lpos, wlist, wlist, wlist, zero_from, zero_to, h2_rows.reshape(-1, PACK_ROWS, LANES), h2_rows, wsg, wsu, wsd)


def _expert_kernel(be_ref, nused_ref, xs_ref, wg_ref, wu_ref, wd_ref, ys_ref, wgb_sc, wub_sc, wdb_sc, *, rows):
    i = pl.program_id(0)

    @pl.when((i == 0) | (be_ref[i] != be_ref[jnp.maximum(i - 1, 0)]))
    def _():
        wgb_sc[...] = wg_ref[0].astype(BF16)
        wub_sc[...] = wu_ref[0].astype(BF16)
        wdb_sc[...] = wd_ref[0].astype(BF16)

    @pl.when(i < nused_ref[0])
    def _():
        xb = _load_token_rows(xs_ref).astype(BF16)
        g = _dot(xb, wgb_sc[...])
        u = _dot(xb, wub_sc[...])
        act = (g * _sigmoid(g)) * u
        _store_token_rows(ys_ref, _dot(act.astype(BF16), wdb_sc[...]))

    @pl.when(pl.program_id(0) >= nused_ref[0])
    def _():
        ys_ref[...] = jnp.zeros(ys_ref.shape, U32)


def _expert_call(block_e, n_used, xs, wg, wu, wd, rows):
    n_rows = xs.shape[0] // PACK_ROWS
    _, d, ff = wg.shape
    row_spec = pl.BlockSpec((rows * PACK_ROWS, LANES), lambda i, be, nu: (jnp.minimum(i, nu[0] - 1), 0))
    grid_spec = pltpu.PrefetchScalarGridSpec(
        num_scalar_prefetch=2,
        grid=(n_rows // rows,),
        in_specs=[row_spec,
                  pl.BlockSpec((1, d, ff), lambda i, be, nu: (be[i], 0, 0)),
                  pl.BlockSpec((1, d, ff), lambda i, be, nu: (be[i], 0, 0)),
                  pl.BlockSpec((1, ff, d), lambda i, be, nu: (be[i], 0, 0))],
        out_specs=pl.BlockSpec((rows * PACK_ROWS, LANES), lambda i, be, nu: (i, 0)),
        scratch_shapes=[pltpu.VMEM((d, ff), BF16), pltpu.VMEM((d, ff), BF16), pltpu.VMEM((ff, d), BF16)],
    )
    return pl.pallas_call(
        functools.partial(_expert_kernel, rows=rows),
        grid_spec=grid_spec,
        out_shape=jax.ShapeDtypeStruct(xs.shape, U32),
        compiler_params=_params("arbitrary"),
        name="moe_experts",
    )(block_e, n_used, xs, wg, wu, wd)


def _final_kernel(lpos0_ref, lposn_ref, tab0_ref, tabw_ref, tabi_ref, x1_ref, shared_ref, wtok_ref, gf_ref,
                  gpost_ref, ys_hbm, o_ref, win_sc, gnext_sc, gcur_sc, sem, *, tm):
    nt = pl.num_programs(1)
    g = pl.program_id(0) * nt + pl.program_id(1)
    n_tiles = pl.num_programs(0) * nt
    buf_next = lax.rem(g + 1, 2)
    buf_after = lax.rem(g, 2)
    half = _local_rows(tm)

    def window(buf):
        return lambda row0, slot0: pltpu.make_async_copy(
            ys_hbm.at[pl.ds(slot0, WINDOW_ROWS)],
            win_sc.at[pl.ds(pl.multiple_of(buf * half + row0, WINDOW_ROWS), WINDOW_ROWS)], sem.at[buf])

    def start_windows(tab_ref, buf):
        _for_each_window(tab_ref, lambda r, s: window(buf)(r, s).start())

    def wait_windows(tab_ref, buf):
        _for_each_window(tab_ref, lambda r, s: window(buf)(r, s).wait())

    def pick_rows(lpos_ref, t):
        for k in range(TOP_K):
            gnext_sc[k, t] = win_sc[lpos_ref[0, k, t]]

    def stage():
        for k in range(TOP_K):
            gcur_sc[k] = gnext_sc[k].reshape(tm * PACK_ROWS, LANES)

    @pl.when(g == 0)
    def _():
        start_windows(tab0_ref, 0)
        wait_windows(tab0_ref, 0)

        def body(i, c):
            for u in range(ROW_COPY_UNROLL):
                pick_rows(lpos0_ref, i * ROW_COPY_UNROLL + u)
            return c

        lax.fori_loop(0, tm // ROW_COPY_UNROLL, body, 0)
        stage()

        @pl.when(n_tiles > 1)
        def _():
            start_windows(tabw_ref, 1)

    @pl.when(g + 1 < n_tiles)
    def _():
        wait_windows(tabw_ref, buf_next)

    @pl.when(g + 2 < n_tiles)
    def _():
        start_windows(tabi_ref, buf_after)

    for t in range(tm):
        pick_rows(lposn_ref, t)

    w = wtok_ref[...]
    routed = w[:, 0:1] * _load_token_rows(gcur_sc.at[0])
    for k in range(1, TOP_K):
        routed = routed + w[:, k:k + 1] * _load_token_rows(gcur_sc.at[k])
    tot = routed + shared_ref[0].astype(F32)
    o_ref[0] = x1_ref[0] + gf_ref[0] * (_rms_scale(tot) * gpost_ref[...])

    stage()


def _final_call(lpos, tab, x1, shared, wtok, gate_f, g_post, ys, tm):
    bsz, s, d = x1.shape
    nt = s // tm
    n_tiles = bsz * nt
    smem = functools.partial(pl.BlockSpec, memory_space=pltpu.SMEM)
    tok = pl.BlockSpec((1, tm, d), lambda b, i: (b, i, 0))
    tab_shape = (1, 1, WIN_LIST)
    ahead = lambda k: (lambda b, i: (jnp.minimum(b * nt + i + k, n_tiles - 1), 0, 0))
    return pl.pallas_call(
        functools.partial(_final_kernel, tm=tm),
        grid=(bsz, nt),
        in_specs=[smem((1, SUBLANES, tm), lambda b, i: (0, 0, 0)),
                  smem((1, SUBLANES, tm), ahead(1)),
                  smem(tab_shape, lambda b, i: (0, 0, 0)),
                  smem(tab_shape, ahead(1)),
                  smem(tab_shape, ahead(2)),
                  tok, tok,
                  pl.BlockSpec((tm, LANES), lambda b, i: (b * nt + i, 0)),
                  pl.BlockSpec((1, 1, d), lambda b, i: (b, 0, 0)),
                  _const_spec((1, d)),
                  pl.BlockSpec(memory_space=pl.ANY)],
        out_specs=tok,
        out_shape=jax.ShapeDtypeStruct((bsz, s, d), F32),
        scratch_shapes=[pltpu.VMEM((2 * _local_rows(tm), PACK_ROWS, LANES), U32),
                        pltpu.VMEM((TOP_K, tm, PACK_ROWS, LANES), U32),
                        pltpu.VMEM((TOP_K, tm * PACK_ROWS, LANES), U32), pltpu.SemaphoreType.DMA((2,))],
        compiler_params=_params("arbitrary", "arbitrary"),
        name="moe_combine",
    )(lpos, lpos, tab, tab, tab, x1, shared, wtok, gate_f, g_post, ys)


def _block_diag_tiles(w):
    per = MXU_DIM // RNN_BLOCK_DIM
    nt = RNN_BLOCKS // per
    w4 = w.reshape(nt, per, RNN_BLOCK_DIM, RNN_BLOCK_DIM)
    eye = jnp.eye(per, dtype=w.dtype)
    return jnp.einsum("tnij,nm->tnimj", w4, eye).reshape(nt, MXU_DIM, MXU_DIM)


def _tile(n, pref):
    t = min(n, pref)
    assert n % t == 0, (n, t)
    return t


def _layer(x, c, w_ada, b_ada, g_pre_mix, g_post_mix, g_pre_ffn, g_post_ffn, w_in, b_forget,
           w_conv, b_conv, w_rg, b_rg, w_ig, b_ig, lam, w_branch_attn, w_branch_rnn, w_out,
           w_router, router_bias, w_exp_gate, w_exp_up, w_exp_down, w_sh_gate, w_sh_up, w_sh_down):
    bsz, s, d = x.shape
    t = bsz * s
    row = lambda v: v.reshape(1, -1)

    mod = _ada_call(c, w_ada, b_ada)
    shift_m, scale_m, gate_m, shift_f, scale_f, gate_f = [
        mod[:, j * d:(j + 1) * d].reshape(bsz, 1, d) for j in range(6)]

    o_f = 3 * ATT_WIDTH
    o_r = o_f + ATT_HEADS
    wqkv = w_in[:, :o_f].astype(BF16)
    wf = jnp.pad(w_in[:, o_f:o_r], ((0, 0), (0, LANES - ATT_HEADS))).astype(BF16)
    wr = w_in[:, o_r:].astype(BF16)
    bf_pad = jnp.pad(b_forget, (0, LANES - ATT_HEADS)).reshape(1, LANES)
    q, k, v, kbias, x_rnn, g_rnn, gate_a, gate_r = _inproj_call(
        x, shift_m, scale_m, row(g_pre_mix), wqkv, wf, wr, bf_pad, _tile(s, 512))
    tq = _tile(s, 512)
    y_attn = _attn_call(q, k, v, kbias, tq, _tile(tq, 512))

    wbd = jnp.concatenate([_block_diag_tiles(w_rg), _block_diag_tiles(w_ig)], axis=-1).astype(BF16)
    y_rnn = _rnn_call(x_rnn, g_rnn, w_conv, row(b_conv), wbd, row(b_rg), row(b_ig), row(lam), _tile(s, 256))

    wr_t = w_router.T
    wr_hi = wr_t.astype(BF16)
    wr_mid = (wr_t - wr_hi.astype(F32)).astype(BF16)
    tm_m = _tile(s, 512)
    tm_r = _tile(tm_m, 256)
    x1, h2_rows, lpos, before, n_tile, wtok, counts = _merge_call(
        y_attn, y_rnn, gate_a, gate_r, x, gate_m, shift_f, scale_f, row(g_post_mix), row(g_pre_ffn),
        w_branch_attn.astype(BF16), w_branch_rnn.astype(BF16), w_out.astype(BF16),
        wr_hi, wr_mid, router_bias.reshape(N_EXPERTS, 1), tm_m, tm_r)

    rows = 1024
    n_blocks =-(-(t * TOP_K + N_EXPERTS * WINDOW_ROWS) // rows) + N_EXPERTS
    cnt = counts[:, 0].astype(I32)
    padded = ((cnt + WINDOW_ROWS + rows - 1) // rows) * rows
    pend = jnp.cumsum(padded).astype(I32)
    pstart = pend - padded
    n_used = jnp.maximum(pend[-1] // rows, 1)
    blk = jnp.minimum(jnp.arange(n_blocks, dtype=I32), n_used - 1)
    block_e = jnp.minimum(jnp.sum((pend[None, :] <= (blk * rows)[:, None]).astype(I32), axis=1), N_EXPERTS - 1)

    n_win = (n_tile[:, :, 0] + WINDOW_ROWS - 1) // WINDOW_ROWS
    win_end = jnp.cumsum(n_win, axis=1)
    w_id = jnp.arange(WIN_LIST, dtype=I32)[None, :, None]
    owner = (jnp.sum((win_end[:, None, :] <= w_id).astype(I32), axis=2, keepdims=True)
             == jnp.arange(N_EXPERTS, dtype=I32)[None, None, :]).astype(I32)
    first_slot = pstart[None, :] + before[:, :, 0] - (win_end - n_win) * WINDOW_ROWS
    wlist = jnp.sum(owner * first_slot[:, None, :], axis=2) + w_id[:, :, 0] * WINDOW_ROWS
    wlist = jnp.where(w_id[:, :, 0] == WIN_LIST - 1, win_end[:, -1:], wlist).astype(I32)[:, None, :]
    n_rt = lpos.shape[0]
    lpos = lpos + (jnp.arange(n_rt, dtype=I32) % 2)[:, None, None] * _local_rows(tm_r)

    rows3 = lambda a: a.reshape(-1, PACK_ROWS, LANES)
    rows2 = lambda a: a.reshape(-1, LANES)
    zero_from = jnp.concatenate([pstart + cnt // WINDOW_ROWS * WINDOW_ROWS, pend[-1:]]).astype(I32)
    zero_to = jnp.concatenate([pend, jnp.full((1,), n_blocks * rows, I32)]).astype(I32)
    xs, shared = _dispatch_call(lpos, wlist, zero_from, zero_to, h2_rows,
                                w_sh_gate.astype(BF16), w_sh_up.astype(BF16), w_sh_down.astype(BF16),
                                n_blocks * rows, tm_r)
    ys = rows3(_expert_call(block_e, n_used.reshape(1), rows2(xs),
                            w_exp_gate, w_exp_up, w_exp_down, rows))
    return _final_call(lpos, wlist, x1, shared.reshape(bsz, s, d), wtok, gate_f, row(g_post_ffn), ys, tm_r)


def kernel(x, c, w_ada, b_ada, g_pre_mix, g_post_mix, g_pre_ffn, g_post_ffn, w_in, b_forget, w_conv, b_conv, w_rg, b_rg, w_ig, b_ig, rglru_lambda, w_branch_attn, w_branch_rnn, w_out, w_router, router_bias, w_exp_gate, w_exp_up, w_exp_down, w_sh_gate, w_sh_up, w_sh_down):
    depth = w_ada.shape[0]
    for l in range(depth):
        x = _layer(x, c, w_ada[l], b_ada[l], g_pre_mix[l], g_post_mix[l], g_pre_ffn[l], g_post_ffn[l],
                   w_in[l], b_forget[l], w_conv[l], b_conv[l], w_rg[l], b_rg[l], w_ig[l], b_ig[l],
                   rglru_lambda[l], w_branch_attn[l], w_branch_rnn[l], w_out[l], w_router[l],
                   router_bias[l], w_exp_gate[l], w_exp_up[l], w_exp_down[l],
                   w_sh_gate[l], w_sh_up[l], w_sh_down[l])
    return x
```

```python
import functools

import jax
import jax.numpy as jnp
from jax import lax
from jax.experimental import pallas as pl
from jax.experimental.pallas import tpu as pltpu

F32 = jnp.float32
BF16 = jnp.bfloat16
I32 = jnp.int32

D_MODEL = 1024
ATT_HEADS = 8
ATT_HEAD_DIM = 64
ATT_WIDTH = ATT_HEADS * ATT_HEAD_DIM
RNN_WIDTH = D_MODEL
RNN_BLOCKS = 16
RNN_BLOCK_DIM = RNN_WIDTH // RNN_BLOCKS
CONV_WIDTH = 4
RGLRU_C = 8.0
N_EXPERTS = 64
TOP_K = 6
N_GROUPS = 8
GROUP_SIZE = N_EXPERTS // N_GROUPS
TOPK_GROUPS = 4
EXPERT_FF = D_MODEL // 4
ROUTE_SCALE = 2.5
NORM_EPS = 1e-6

LANES = 128
SUBLANES = 8
MXU_DIM = 256
VMEM_LIMIT_BYTES = 56 * 1024 * 1024

NEG_BIG = -1e30


def _sigmoid(x):
    return 0.5 * jnp.tanh(0.5 * x) + 0.5


def _rms_scale(x):
    return x * lax.rsqrt(jnp.mean(x * x, axis=-1, keepdims=True) + NORM_EPS)


def _dot(a, b):
    return jnp.dot(a, b, preferred_element_type=F32)


def _dot_nt(a, b):
    return lax.dot_general(a, b, (((1,), (1,)), ((), ())), preferred_element_type=F32)


def _split3(a):
    hi = a.astype(BF16)
    r1 = a - hi.astype(F32)
    mid = r1.astype(BF16)
    lo = (r1 - mid.astype(F32)).astype(BF16)
    return hi, mid, lo


def _params(*sem):
    return pltpu.CompilerParams(dimension_semantics=sem, vmem_limit_bytes=VMEM_LIMIT_BYTES)


def _const_spec(shape):
    nd = len(shape)
    return pl.BlockSpec(shape, lambda *_: (0,) * nd)


def _ada_kernel(c_ref, w_ref, b_ref, o_ref):
    c = c_ref[...]
    sc = c * _sigmoid(c)
    s_hi, s_mid, _ = _split3(sc)
    w_hi, w_mid, _ = _split3(w_ref[...])
    acc = _dot(s_hi, w_hi) + _dot(s_hi, w_mid) + _dot(s_mid, w_hi)
    o_ref[...] = acc + b_ref[...]


def _ada_call(c, w_ada, b_ada):
    bsz, d = c.shape
    n = w_ada.shape[1]
    tn = d
    return pl.pallas_call(
        _ada_kernel,
        grid=(n // tn,),
        in_specs=[
            pl.BlockSpec((bsz, d), lambda j: (0, 0)),
            pl.BlockSpec((d, tn), lambda j: (0, j)),
            pl.BlockSpec((1, tn), lambda j: (0, j)),
        ],
        out_specs=pl.BlockSpec((bsz, tn), lambda j: (0, j)),
        out_shape=jax.ShapeDtypeStruct((bsz, n), F32),
        compiler_params=_params("arbitrary"),
        name="ada_mod",
    )(c, w_ada, b_ada.reshape(1, n))


N_BIAS_PIECES = 3
CUM_BLOCK = 256


def _bias_placement():
    h = jnp.arange(LANES)[:, None]
    lane = jnp.arange(ATT_WIDTH)[None, :]
    mats = []
    for piece in range(N_BIAS_PIECES):
        target = (h // 2) * LANES + ATT_HEAD_DIM * (1 - h % 2) + piece
        mats.append(((lane == target) & (h < ATT_HEADS)).astype(BF16))
    return jnp.concatenate(mats, axis=0)


def _key_bias(f_logit, bf_ref, place_ref, kb_ref, cum_sc):
    tm = f_logit.shape[0]
    blk = min(tm, CUM_BLOCK)
    row = lax.broadcasted_iota(I32, (blk, blk), 0)
    col = lax.broadcasted_iota(I32, (blk, blk), 1)
    tri = (row >= col).astype(BF16)
    carry = cum_sc[...]
    for i in range(tm // blk):
        z = f_logit[i * blk:(i + 1) * blk, :] + bf_ref[...]
        lf = jnp.minimum(z, 0.0) - jnp.log1p(jnp.exp(-jnp.abs(z)))
        c3 = _dot(tri, jnp.concatenate(_split3(lf), axis=1))
        c = c3[:, :LANES] + c3[:, LANES:2 * LANES] + c3[:, 2 * LANES:] + carry
        carry = c[blk - 1:blk, :]
        kb = _dot(jnp.concatenate(_split3(-c), axis=1), place_ref[...])
        kb_ref[0, i * blk:(i + 1) * blk, :] = kb.astype(BF16)
    cum_sc[...] = carry


def _inproj_kernel(x_ref, sh_ref, sc_ref, g_ref, wqkv_ref, wf_ref, wr_ref, bf_ref, place_ref,
                   q_ref, k_ref, v_ref, kb_ref, xr_ref, gr_ref, ga_ref, gb_ref, cum_sc):
    @pl.when(pl.program_id(1) == 0)
    def _():
        cum_sc[...] = jnp.zeros(cum_sc.shape, F32)

    x = x_ref[0]
    h = (_rms_scale(x) * g_ref[...]) * (1.0 + sc_ref[0]) + sh_ref[0]
    hb = h.astype(BF16)
    for j, ref in enumerate((q_ref, k_ref, v_ref)):
        ref[0] = _dot(hb, wqkv_ref[:, j * ATT_WIDTH:(j + 1) * ATT_WIDTH]).astype(BF16)
    _key_bias(_dot(hb, wf_ref[...]), bf_ref, place_ref, kb_ref, cum_sc)
    half = D_MODEL // 2
    for j, ref in enumerate((xr_ref, gr_ref, ga_ref, gb_ref)):
        for c in range(2):
            lo = j * D_MODEL + c * half
            ref[0, :, c * half:(c + 1) * half] = _dot(hb, wr_ref[:, lo:lo + half]).astype(BF16)


def _inproj_call(x, shift, scale, g, wqkv, wf, wr, b_forget_pad, tm):
    bsz, s, d = x.shape
    nt = s // tm
    tok = lambda w: pl.BlockSpec((1, tm, w), lambda b, i: (b, i, 0))
    mod = pl.BlockSpec((1, 1, d), lambda b, i: (b, 0, 0))
    place = _bias_placement()
    out_w = (ATT_WIDTH,) * 4 + (d,) * 4
    return pl.pallas_call(
        _inproj_kernel,
        grid=(bsz, nt),
        in_specs=[tok(d), mod, mod, _const_spec((1, d)),
                  _const_spec(wqkv.shape), _const_spec(wf.shape), _const_spec(wr.shape),
                  _const_spec((1, LANES)), _const_spec(place.shape)],
        out_specs=[tok(w) for w in out_w],
        out_shape=[jax.ShapeDtypeStruct((bsz, s, w), BF16) for w in out_w],
        scratch_shapes=[pltpu.VMEM((1, LANES), F32)],
        compiler_params=_params("arbitrary", "arbitrary"),
        name="in_proj",
    )(x, shift, scale, g, wqkv, wf, wr, b_forget_pad, place)


ATT_VT_ROWS = ATT_HEAD_DIM + 16


def _attn_kernel(q_ref, k_ref, v_ref, kb_ref, o_ref, vt_sc, st_sc, m_sc, acc_sc, *, tq, tk):
    s = q_ref.shape[1]
    nq, nk, ratio = s // tq, s // tk, tq // tk
    lane = lax.broadcasted_iota(I32, (1, LANES), 1)
    head_lanes = (lane < ATT_HEAD_DIM, lane >= ATT_HEAD_DIM)
    one_lanes = ((lane >= ATT_HEAD_DIM) & (lane < ATT_HEAD_DIM + N_BIAS_PIECES), lane < N_BIAS_PIECES)
    key_minus_query = (lax.broadcasted_iota(I32, (tk, tq), 0) - lax.broadcasted_iota(I32, (tk, tq), 1))

    vt = v_ref[0].astype(F32).T
    ones = jnp.ones((ATT_VT_ROWS - ATT_HEAD_DIM, s), F32)
    for h in range(2):
        vth = jnp.concatenate([vt[h * ATT_HEAD_DIM:(h + 1) * ATT_HEAD_DIM, :], ones], axis=0).astype(BF16)
        for j in range(nk):
            vt_sc[h, j] = vth[:, j * tk:(j + 1) * tk]

    def q_block(qi):
        q0 = qi * tq
        qs = q_ref[0, pl.ds(q0, tq), :] * (ATT_HEAD_DIM ** -0.5)
        qa = [jnp.where(head_lanes[h], qs, one_lanes[h].astype(BF16)) for h in range(2)]
        m_sc[...] = jnp.full(m_sc.shape, NEG_BIG, F32)
        acc_sc[...] = jnp.zeros(acc_sc.shape, F32)

        def scores(kj):
            k0 = kj * tk
            kb = k_ref[0, pl.ds(k0, tk), :]
            bias = kb_ref[0, pl.ds(k0, tk), :]
            return [_dot_nt(jnp.where(head_lanes[h], kb, bias), qa[h]) for h in range(2)]

        def online_update(st, m_old, acc, vt):
            m_new = jnp.maximum(m_old, jnp.max(st, axis=0, keepdims=True))
            p = jnp.exp(st - m_new).astype(BF16)
            return m_new, jnp.exp(m_old - m_new) * acc + _dot(vt, p)

        def accumulate(kj, masked):
            for h in range(2):
                if not masked:
                    m_sc[h], acc_sc[h] = online_update(st_sc[h], m_sc[h], acc_sc[h], vt_sc[h, kj])
                elif ratio == 1:
                    hk = tk // 2
                    tri = key_minus_query[:hk, :] <= 0
                    st_a = jnp.where(tri, st_sc[h, :hk, :], NEG_BIG)
                    m_a, acc_a = online_update(st_a, m_sc[h], acc_sc[h], vt_sc[h, kj, :, :hk])
                    st_b = jnp.where(tri[:, :hk], st_sc[h, hk:, hk:], NEG_BIG)
                    m_b, acc_b = online_update(st_b, m_a[:, hk:], acc_a[:, hk:], vt_sc[h, kj, :, hk:])
                    m_sc[h] = jnp.concatenate([m_a[:, :hk], m_b], axis=1)
                    acc_sc[h] = jnp.concatenate([acc_a[:, :hk], acc_b], axis=1)
                else:
                    st = jnp.where(key_minus_query <= q0 - kj * tk, st_sc[h], NEG_BIG)
                    m_sc[h], acc_sc[h] = online_update(st, m_sc[h], acc_sc[h], vt_sc[h, kj])

        def put(sts):
            for h in range(2):
                st_sc[h] = sts[h]

        n_full = qi * ratio
        put(scores(0))

        for kj in range(n_full):
            nxt = scores(kj + 1)
            accumulate(kj, False)
            put(nxt)

        for d in range(ratio):
            nxt = scores(n_full + d + 1) if d + 1 < ratio else None
            accumulate(n_full + d, True)
            if nxt is not None:
                put(nxt)

        outs = []
        for h in range(2):
            acc = acc_sc[h]
            outs.append(acc[:ATT_HEAD_DIM, :] * (1.0 / acc[ATT_HEAD_DIM:ATT_HEAD_DIM + 1, :]))
        o_ref[0, pl.ds(q0, tq), :] = jnp.concatenate(outs, axis=0).T.astype(BF16)

    for qi in range(nq):
        q_block(qi)


def _attn_call(q, k, v, kbias, tq, tk):
    bsz, s, _ = q.shape
    qkv = pl.BlockSpec((1, s, LANES), lambda b, p: (b, 0, p))
    return pl.pallas_call(
        functools.partial(_attn_kernel, tq=tq, tk=tk),
        grid=(bsz, ATT_HEADS // 2),
        in_specs=[qkv, qkv, qkv, qkv],
        out_specs=qkv,
        out_shape=jax.ShapeDtypeStruct((bsz, s, ATT_WIDTH), BF16),
        scratch_shapes=[pltpu.VMEM((2, s // tk, ATT_VT_ROWS, tk), BF16), pltpu.VMEM((2, tk, tq), F32),
                        pltpu.VMEM((2, 1, tq), F32), pltpu.VMEM((2, ATT_VT_ROWS, tq), F32)],
        compiler_params=_params("arbitrary", "arbitrary"),
        name="fox_attention",
    )(q, k, v, kbias)


def _gelu_tanh(x):
    c = 0.7978845608028654
    return 0.5 * x * (1.0 + jnp.tanh(c * (x + 0.044715 * (x * x * x))))


def _rnn_kernel(xr_ref, gr_ref, wc_ref, bc_ref, wbd_ref, brg_ref, big_ref, lam_ref, o_ref,
                tail_sc, hc_sc, a_sc, b_sc, h_sc, *, ts):
    @pl.when(pl.program_id(1) == 0)
    def _():
        tail_sc[...] = jnp.zeros(tail_sc.shape, F32)
        hc_sc[...] = jnp.zeros(hc_sc.shape, F32)

    x = xr_ref[0].astype(F32)
    full = jnp.concatenate([tail_sc[...], x], axis=0)
    u = bc_ref[...] + wc_ref[CONV_WIDTH - 1:CONV_WIDTH, :] * x
    for j in range(CONV_WIDTH - 1):
        off = SUBLANES - (CONV_WIDTH - 1) + j
        u = u + wc_ref[j:j + 1, :] * full[off:off + ts, :]
    tail_sc[...] = x[ts - SUBLANES:ts, :]

    lam = lam_ref[...]
    neg_sp = -(jnp.maximum(-lam, 0.0) + jnp.log1p(jnp.exp(-jnp.abs(lam))))
    for jt in range(RNN_WIDTH // MXU_DIM):
        cs = slice(jt * MXU_DIM, (jt + 1) * MXU_DIM)
        uc = u[:, cs]
        g = _dot(uc.astype(BF16), wbd_ref[jt])
        r = _sigmoid(g[:, :MXU_DIM] + brg_ref[:, cs])
        ig = _sigmoid(g[:, MXU_DIM:] + big_ref[:, cs])
        a = jnp.exp(r * (RGLRU_C * neg_sp[:, cs]))
        a_sc[:, cs] = a
        v = 1.0 - a * a
        b_sc[:, cs] = jnp.where(v > 0.0, v * lax.rsqrt(v), 0.0) * (ig * uc)

    srow = lax.broadcasted_iota(I32, (SUBLANES, RNN_WIDTH), 0)

    def scan_body(i, carry):
        r0 = pl.multiple_of(i * SUBLANES, SUBLANES)
        a = a_sc[pl.ds(r0, SUBLANES), :]
        b = b_sc[pl.ds(r0, SUBLANES), :]
        b = b + jnp.where(srow == 0, a * carry, 0.0)
        for sh in (1, 2, 4):
            keep = srow >= sh
            b = jnp.where(keep, b + a * pltpu.roll(b, sh, 0), b)
            if sh < SUBLANES // 2:
                a = jnp.where(keep, a * pltpu.roll(a, sh, 0), a)
        h_sc[pl.ds(r0, SUBLANES), :] = b
        return jnp.broadcast_to(b[SUBLANES - 1:SUBLANES, :], (SUBLANES, RNN_WIDTH))

    hc_sc[...] = lax.fori_loop(0, ts // SUBLANES, scan_body, hc_sc[...], unroll=True)
    o_ref[0] = (h_sc[...] * _gelu_tanh(gr_ref[0].astype(F32))).astype(BF16)


def _rnn_call(x_rnn, g_rnn, w_conv, b_conv, wbd, b_rg, b_ig, lam, ts):
    bsz, s, w = x_rnn.shape
    tok = pl.BlockSpec((1, ts, w), lambda b, i: (b, i, 0))
    return pl.pallas_call(
        functools.partial(_rnn_kernel, ts=ts),
        grid=(bsz, s // ts),
        in_specs=[tok, tok, _const_spec(w_conv.shape), _const_spec((1, w)), _const_spec(wbd.shape),
                  _const_spec((1, w)), _const_spec((1, w)), _const_spec((1, w))],
        out_specs=tok,
        out_shape=jax.ShapeDtypeStruct((bsz, s, w), BF16),
        scratch_shapes=[pltpu.VMEM((SUBLANES, w), F32), pltpu.VMEM((SUBLANES, w), F32),
                        pltpu.VMEM((ts, w), F32), pltpu.VMEM((ts, w), F32), pltpu.VMEM((ts, w), F32)],
        compiler_params=_params("arbitrary", "arbitrary"),
        name="rglru",
    )(x_rnn, g_rnn, w_conv, b_conv, wbd, b_rg, b_ig, lam)


WINDOW_SHIFT = 5
WINDOW_ROWS = 1 << WINDOW_SHIFT


def _route(h2, wr_hi_ref, wr_mid_ref, rbias_ref, cnt_ref, lpos_ref, before_ref, ntile_ref, wtok_ref, j, tm):
    h_hi, h_mid, _ = _split3(h2)
    logits = _dot_nt(wr_hi_ref[...], h_hi) + _dot_nt(wr_hi_ref[...], h_mid) + _dot_nt(wr_mid_ref[...], h_hi)
    scores = _sigmoid(logits)
    sel = scores + rbias_ref[...]

    giota = lax.broadcasted_iota(I32, (N_GROUPS, tm), 0)
    gs = jnp.zeros((N_GROUPS, tm), F32)
    for g in range(N_GROUPS):
        blk = sel[g * GROUP_SIZE:(g + 1) * GROUP_SIZE, :]
        m1 = jnp.max(blk, axis=0, keepdims=True)
        i1 = jnp.min(jnp.where(blk == m1, giota, N_EXPERTS), axis=0, keepdims=True)
        m2 = jnp.max(jnp.where(giota == i1, -jnp.inf, blk), axis=0, keepdims=True)
        gs = jnp.where(giota == g, m1 + m2, gs)

    gsel = jnp.zeros((N_GROUPS, tm), F32)
    for _ in range(TOPK_GROUPS):
        m = jnp.max(gs, axis=0, keepdims=True)
        idx = jnp.min(jnp.where(gs == m, giota, N_EXPERTS), axis=0, keepdims=True)
        hit = giota == idx
        gsel = jnp.where(hit, 1.0, gsel)
        gs = jnp.where(hit, -jnp.inf, gs)

    masked = jnp.concatenate(
        [jnp.where(gsel[g:g + 1, :] > 0.0, sel[g * GROUP_SIZE:(g + 1) * GROUP_SIZE, :], -jnp.inf)
         for g in range(N_GROUPS)], axis=0)

    eiota = lax.broadcasted_iota(I32, (N_EXPERTS, tm), 0)
    chosen = jnp.zeros((N_EXPERTS, tm), F32)
    idxs, wts = [], []
    for _ in range(TOP_K):
        m = jnp.max(masked, axis=0, keepdims=True)
        idx = jnp.min(jnp.where(masked == m, eiota, N_EXPERTS), axis=0, keepdims=True)
        hit = eiota == idx
        wts.append(jnp.sum(jnp.where(hit, scores, 0.0), axis=0, keepdims=True))
        idxs.append(idx)
        chosen = jnp.where(hit, 1.0, chosen)
        masked = jnp.where(hit, -jnp.inf, masked)
    wsum = wts[0]
    for w in wts[1:]:
        wsum = wsum + w

    srow = lax.broadcasted_iota(I32, (tm, tm), 0)
    scol = lax.broadcasted_iota(I32, (tm, tm), 1)
    earlier = (srow < scol).astype(BF16)
    local_rank = _dot(chosen.astype(BF16), earlier)
    n_e = jnp.sum(chosen, axis=1, keepdims=True)
    before_ref[j] = cnt_ref[...].astype(I32)
    ntile_ref[j] = n_e.astype(I32)
    cnt_ref[...] = cnt_ref[...] + n_e
    n_pad = (((n_e.astype(I32) + (WINDOW_ROWS - 1)) >> WINDOW_SHIFT) << WINDOW_SHIFT).astype(F32)
    erow = lax.broadcasted_iota(I32, (N_EXPERTS, N_EXPERTS), 0)
    ecol = lax.broadcasted_iota(I32, (N_EXPERTS, N_EXPERTS), 1)
    off = _dot((ecol < erow).astype(BF16), jnp.broadcast_to(n_pad, (N_EXPERTS, LANES)).astype(BF16))[:, :1]
    local_pos = off + local_rank

    riota = lax.broadcasted_iota(I32, (SUBLANES, tm), 0)
    lpos = jnp.zeros((SUBLANES, tm), I32)
    wrow = jnp.zeros((SUBLANES, tm), F32)
    for r in range(TOP_K):
        lp = jnp.sum(jnp.where(eiota == idxs[r], local_pos, 0.0), axis=0, keepdims=True)
        lpos = jnp.where(riota == r, lp.astype(I32), lpos)
        wrow = jnp.where(riota == r, wts[r] / wsum * ROUTE_SCALE, wrow)
    lpos_ref[j] = lpos
    wpad = jnp.concatenate([wrow, jnp.zeros((LANES - SUBLANES, tm), F32)], axis=0)
    wtok_ref[j * tm:(j + 1) * tm, :] = wpad.T


U32 = jnp.uint32
PACK_ROWS = D_MODEL // (2 * LANES)
HIGH_HALF = 0xFFFF0000


def _store_token_rows(rows_ref, val):
    half = D_MODEL // 2
    n = val.shape[0]
    bits = lax.bitcast_convert_type(val.astype(BF16).astype(F32), U32)
    words = (bits[:, half:] & jnp.uint32(HIGH_HALF)) | (bits[:, :half] >> 16)
    for c in range(PACK_ROWS):
        chunk = words[:, c * LANES:(c + 1) * LANES]
        if len(rows_ref.shape) == 2:
            rows_ref[pl.ds(c, n, stride=PACK_ROWS), :] = chunk
        else:
            rows_ref[:, c, :] = chunk


def _load_token_rows(rows_ref):
    if len(rows_ref.shape) == 2:
        n = rows_ref.shape[0] // PACK_ROWS
        chunks = [rows_ref[pl.ds(c, n, stride=PACK_ROWS), :] for c in range(PACK_ROWS)]
    else:
        chunks = [rows_ref[:, c, :] for c in range(PACK_ROWS)]
    words = jnp.concatenate(chunks, axis=-1)
    lo = lax.bitcast_convert_type(words << 16, F32)
    hi = lax.bitcast_convert_type(words & jnp.uint32(HIGH_HALF), F32)
    return jnp.concatenate([lo, hi], axis=-1)


def _merge_kernel(ya_ref, yr_ref, ga_ref, gb_ref, x_ref, gm_ref, shf_ref, scf_ref, gpost_ref, gpre_ref,
                  wba_ref, wbr_ref, wout_ref, wr_hi_ref, wr_mid_ref, rbias_ref,
                  x1_ref, h2rows_ref, lpos_ref, before_ref, ntile_ref, wtok_ref, cnt_ref, *, tm, tr):
    @pl.when((pl.program_id(0) == 0) & (pl.program_id(1) == 0))
    def _():
        cnt_ref[...] = jnp.zeros(cnt_ref.shape, F32)

    pa = _dot(ya_ref[0], wba_ref[...])
    pr = _dot(yr_ref[0], wbr_ref[...])
    merged = _sigmoid(ga_ref[0].astype(F32)) * pa + _sigmoid(gb_ref[0].astype(F32)) * pr
    o = _dot(merged.astype(BF16), wout_ref[...])
    x1 = x_ref[0] + gm_ref[0] * (_rms_scale(o) * gpost_ref[...])
    x1_ref[0] = x1
    h2 = (_rms_scale(x1) * gpre_ref[...]) * (1.0 + scf_ref[0]) + shf_ref[0]
    _store_token_rows(h2rows_ref, h2)
    for j in range(tm // tr):
        _route(h2[j * tr:(j + 1) * tr, :], wr_hi_ref, wr_mid_ref, rbias_ref, cnt_ref,
               lpos_ref, before_ref, ntile_ref, wtok_ref, j, tr)


def _merge_call(y_attn, y_rnn, gate_a, gate_r, x, gate_m, shift_f, scale_f, g_post, g_pre,
                wba, wbr, wout, wr_hi, wr_mid, rbias, tm, tr):
    bsz, s, d = x.shape
    nt = s // tm
    nr = tm // tr
    tok = lambda w: pl.BlockSpec((1, tm, w), lambda b, i: (b, i, 0))
    mod = pl.BlockSpec((1, 1, d), lambda b, i: (b, 0, 0))
    return pl.pallas_call(
        functools.partial(_merge_kernel, tm=tm, tr=tr),
        grid=(bsz, nt),
        in_specs=[tok(ATT_WIDTH), tok(d), tok(d), tok(d), tok(d), mod, mod, mod,
                  _const_spec((1, d)), _const_spec((1, d)),
                  _const_spec(wba.shape), _const_spec(wbr.shape), _const_spec(wout.shape),
                  _const_spec(wr_hi.shape), _const_spec(wr_mid.shape), _const_spec(rbias.shape)],
        out_specs=[tok(d),
                   pl.BlockSpec((tm * PACK_ROWS, LANES), lambda b, i: (b * nt + i, 0)),
                   pl.BlockSpec((nr, SUBLANES, tr), lambda b, i: (b * nt + i, 0, 0)),
                   pl.BlockSpec((nr, N_EXPERTS, 1), lambda b, i: (b * nt + i, 0, 0)),
                   pl.BlockSpec((nr, N_EXPERTS, 1), lambda b, i: (b * nt + i, 0, 0)),
                   pl.BlockSpec((tm, LANES), lambda b, i: (b * nt + i, 0)),
                   pl.BlockSpec((N_EXPERTS, 1), lambda b, i: (0, 0))],
        out_shape=[jax.ShapeDtypeStruct((bsz, s, d), F32),
                   jax.ShapeDtypeStruct((bsz * s * PACK_ROWS, LANES), U32),
                   jax.ShapeDtypeStruct((bsz * nt * nr, SUBLANES, tr), I32),
                   jax.ShapeDtypeStruct((bsz * nt * nr, N_EXPERTS, 1), I32),
                   jax.ShapeDtypeStruct((bsz * nt * nr, N_EXPERTS, 1), I32),
                   jax.ShapeDtypeStruct((bsz * s, LANES), F32),
                   jax.ShapeDtypeStruct((N_EXPERTS, 1), F32)],
        compiler_params=_params("arbitrary", "arbitrary"),
        name="merge_route",
    )(y_attn, y_rnn, gate_a, gate_r, x, gate_m, shift_f, scale_f, g_post, g_pre,
      wba, wbr, wout, wr_hi, wr_mid, rbias)


WIN_LIST = LANES
ROW_COPY_UNROLL = 8


def _local_rows(tm):
    rows = tm * TOP_K + N_EXPERTS * WINDOW_ROWS
    assert rows // WINDOW_ROWS < WIN_LIST
    return rows


def _for_each_window(wlist_ref, fn):
    def body(w, c):
        fn(pl.multiple_of(w * WINDOW_ROWS, WINDOW_ROWS), wlist_ref[0, 0, w])
        return c

    lax.fori_loop(0, wlist_ref[0, 0, WIN_LIST - 1], body, 0)


def _dispatch_kernel(lpos_ref, wlist_ref, wlistp_ref, wlistpp_ref, zfrom_ref, zto_ref, h_ref, h2d_ref,
                     wsg_ref, wsu_ref, wsd_ref, xs_hbm, shared_ref, loc_sc, zero_sc, sem, *, tm):
    g = pl.program_id(0)
    cur = lax.rem(g, 2)

    @pl.when(g == 0)
    def _():
        loc_sc[...] = jnp.zeros(loc_sc.shape, U32)
        zero_sc[...] = jnp.zeros(zero_sc.shape, U32)

        def for_each_zero_window(fn):
            def per_range(j, c):
                def per_window(w, c2):
                    r0 = pl.multiple_of(zfrom_ref[j] + w * WINDOW_ROWS, WINDOW_ROWS)
                    fn(pltpu.make_async_copy(zero_sc, xs_hbm.at[pl.ds(r0, WINDOW_ROWS)], sem))
                    return c2
                lax.fori_loop(0, (zto_ref[j] - zfrom_ref[j]) >> WINDOW_SHIFT, per_window, 0)
                return c
            lax.fori_loop(0, zfrom_ref.shape[0], per_range, 0)

        for_each_zero_window(lambda cp: cp.start())
        for_each_zero_window(lambda cp: cp.wait())

    half = _local_rows(tm)

    def window(base):
        return lambda row0, slot0: pltpu.make_async_copy(
            loc_sc.at[pl.ds(pl.multiple_of(base + row0, WINDOW_ROWS), WINDOW_ROWS)],
            xs_hbm.at[pl.ds(slot0, WINDOW_ROWS)], sem)

    def start_windows(wl_ref, buf):
        _for_each_window(wl_ref, lambda r, s: window(buf * half)(r, s).start())

    def wait_windows(wl_ref, buf):
        _for_each_window(wl_ref, lambda r, s: window(buf * half)(r, s).wait())

    @pl.when(g > 1)
    def _():
        wait_windows(wlistpp_ref, cur)

    @pl.when(g > 0)
    def _():
        start_windows(wlistp_ref, 1 - cur)

    for t in range(tm):
        row = h_ref[t]
        for k in range(TOP_K):
            loc_sc[lpos_ref[0, k, t]] = row

    hb = _load_token_rows(h2d_ref).astype(BF16)
    a = _dot(hb, wsg_ref[...])
    u = _dot(hb, wsu_ref[...])
    shared_ref[...] = _dot(((a * _sigmoid(a)) * u).astype(BF16), wsd_ref[...]).astype(BF16)

    @pl.when(g == pl.num_programs(0) - 1)
    def _():
        @pl.when(g > 0)
        def _():
            wait_windows(wlistp_ref, 1 - cur)

        start_windows(wlist_ref, cur)
        wait_windows(wlist_ref, cur)


def _dispatch_call(lpos, wlist, zero_from, zero_to, h2_rows, wsg, wsu, wsd, n_rows, tm):
    t = h2_rows.shape[0] // PACK_ROWS
    smem = functools.partial(pl.BlockSpec, memory_space=pltpu.SMEM)
    return pl.pallas_call(
        functools.partial(_dispatch_kernel, tm=tm),
        grid=(t // tm,),
        in_specs=[smem((1, SUBLANES, tm), lambda i: (i, 0, 0)),
                  smem((1, 1, WIN_LIST), lambda i: (i, 0, 0)),
                  smem((1, 1, WIN_LIST), lambda i: (jnp.maximum(i - 1, 0), 0, 0)),
                  smem((1, 1, WIN_LIST), lambda i: (jnp.maximum(i - 2, 0), 0, 0)),
                  smem(), smem(),
                  pl.BlockSpec((tm, PACK_ROWS, LANES), lambda i: (i, 0, 0)),
                  pl.BlockSpec((tm * PACK_ROWS, LANES), lambda i: (i, 0)),
                  _const_spec(wsg.shape), _const_spec(wsu.shape), _const_spec(wsd.shape)],
        out_specs=[pl.BlockSpec(memory_space=pl.ANY), pl.BlockSpec((tm, D_MODEL), lambda i: (i, 0))],
        out_shape=[jax.ShapeDtypeStruct((n_rows, PACK_ROWS, LANES), U32),
                   jax.ShapeDtypeStruct((t, D_MODEL), BF16)],
        scratch_shapes=[pltpu.VMEM((2 * _local_rows(tm), PACK_ROWS, LANES), U32),
                        pltpu.VMEM((WINDOW_ROWS, PACK_ROWS, LANES), U32), pltpu.SemaphoreType.DMA],
        compiler_params=_params("arbitrary"),
        name="moe_dispatch",
    )(lpos, wlist, wlist, wlist, zero_from, zero_to, h2_rows.reshape(-1, PACK_ROWS, LANES), h2_rows, wsg, wsu, wsd)


def _expert_kernel(be_ref, nused_ref, xs_ref, wg_ref, wu_ref, wd_ref, ys_ref, wgb_sc, wub_sc, wdb_sc, *, rows):
    i = pl.program_id(0)

    @pl.when((i == 0) | (be_ref[i] != be_ref[jnp.maximum(i - 1, 0)]))
    def _():
        wgb_sc[...] = wg_ref[0].astype(BF16)
        wub_sc[...] = wu_ref[0].astype(BF16)
        wdb_sc[...] = wd_ref[0].astype(BF16)

    @pl.when(i < nused_ref[0])
    def _():
        xb = _load_token_rows(xs_ref).astype(BF16)
        g = _dot(xb, wgb_sc[...])
        u = _dot(xb, wub_sc[...])
        act = (g * _sigmoid(g)) * u
        _store_token_rows(ys_ref, _dot(act.astype(BF16), wdb_sc[...]))

    @pl.when(pl.program_id(0) >= nused_ref[0])
    def _():
        ys_ref[...] = jnp.zeros(ys_ref.shape, U32)


def _expert_call(block_e, n_used, xs, wg, wu, wd, rows):
    n_rows = xs.shape[0] // PACK_ROWS
    _, d, ff = wg.shape
    row_spec = pl.BlockSpec((rows * PACK_ROWS, LANES), lambda i, be, nu: (jnp.minimum(i, nu[0] - 1), 0))
    grid_spec = pltpu.PrefetchScalarGridSpec(
        num_scalar_prefetch=2,
        grid=(n_rows // rows,),
        in_specs=[row_spec,
                  pl.BlockSpec((1, d, ff), lambda i, be, nu: (be[i], 0, 0)),
                  pl.BlockSpec((1, d, ff), lambda i, be, nu: (be[i], 0, 0)),
                  pl.BlockSpec((1, ff, d), lambda i, be, nu: (be[i], 0, 0))],
        out_specs=pl.BlockSpec((rows * PACK_ROWS, LANES), lambda i, be, nu: (i, 0)),
        scratch_shapes=[pltpu.VMEM((d, ff), BF16), pltpu.VMEM((d, ff), BF16), pltpu.VMEM((ff, d), BF16)],
    )
    return pl.pallas_call(
        functools.partial(_expert_kernel, rows=rows),
        grid_spec=grid_spec,
        out_shape=jax.ShapeDtypeStruct(xs.shape, U32),
        compiler_params=_params("arbitrary"),
        name="moe_experts",
    )(block_e, n_used, xs, wg, wu, wd)


def _final_kernel(lpos0_ref, lposn_ref, tab0_ref, tabw_ref, tabi_ref, x1_ref, shared_ref, wtok_ref, gf_ref,
                  gpost_ref, ys_hbm, o_ref, win_sc, gnext_sc, gcur_sc, sem, *, tm):
    nt = pl.num_programs(1)
    g = pl.program_id(0) * nt + pl.program_id(1)
    n_tiles = pl.num_programs(0) * nt
    buf_next = lax.rem(g + 1, 2)
    buf_after = lax.rem(g, 2)
    half = _local_rows(tm)

    def window(buf):
        return lambda row0, slot0: pltpu.make_async_copy(
            ys_hbm.at[pl.ds(slot0, WINDOW_ROWS)],
            win_sc.at[pl.ds(pl.multiple_of(buf * half + row0, WINDOW_ROWS), WINDOW_ROWS)], sem.at[buf])

    def start_windows(tab_ref, buf):
        _for_each_window(tab_ref, lambda r, s: window(buf)(r, s).start())

    def wait_windows(tab_ref, buf):
        _for_each_window(tab_ref, lambda r, s: window(buf)(r, s).wait())

    def pick_rows(lpos_ref, t):
        for k in range(TOP_K):
            gnext_sc[k, t] = win_sc[lpos_ref[0, k, t]]

    def stage():
        for k in range(TOP_K):
            gcur_sc[k] = gnext_sc[k].reshape(tm * PACK_ROWS, LANES)

    @pl.when(g == 0)
    def _():
        start_windows(tab0_ref, 0)
        wait_windows(tab0_ref, 0)

        def body(i, c):
            for u in range(ROW_COPY_UNROLL):
                pick_rows(lpos0_ref, i * ROW_COPY_UNROLL + u)
            return c

        lax.fori_loop(0, tm // ROW_COPY_UNROLL, body, 0)
        stage()

        @pl.when(n_tiles > 1)
        def _():
            start_windows(tabw_ref, 1)

    @pl.when(g + 1 < n_tiles)
    def _():
        wait_windows(tabw_ref, buf_next)

    @pl.when(g + 2 < n_tiles)
    def _():
        start_windows(tabi_ref, buf_after)

    for t in range(tm):
        pick_rows(lposn_ref, t)

    w = wtok_ref[...]
    routed = w[:, 0:1] * _load_token_rows(gcur_sc.at[0])
    for k in range(1, TOP_K):
        routed = routed + w[:, k:k + 1] * _load_token_rows(gcur_sc.at[k])
    tot = routed + shared_ref[0].astype(F32)
    o_ref[0] = x1_ref[0] + gf_ref[0] * (_rms_scale(tot) * gpost_ref[...])

    stage()


def _final_call(lpos, tab, x1, shared, wtok, gate_f, g_post, ys, tm):
    bsz, s, d = x1.shape
    nt = s // tm
    n_tiles = bsz * nt
    smem = functools.partial(pl.BlockSpec, memory_space=pltpu.SMEM)
    tok = pl.BlockSpec((1, tm, d), lambda b, i: (b, i, 0))
    tab_shape = (1, 1, WIN_LIST)
    ahead = lambda k: (lambda b, i: (jnp.minimum(b * nt + i + k, n_tiles - 1), 0, 0))
    return pl.pallas_call(
        functools.partial(_final_kernel, tm=tm),
        grid=(bsz, nt),
        in_specs=[smem((1, SUBLANES, tm), lambda b, i: (0, 0, 0)),
                  smem((1, SUBLANES, tm), ahead(1)),
                  smem(tab_shape, lambda b, i: (0, 0, 0)),
                  smem(tab_shape, ahead(1)),
                  smem(tab_shape, ahead(2)),
                  tok, tok,
                  pl.BlockSpec((tm, LANES), lambda b, i: (b * nt + i, 0)),
                  pl.BlockSpec((1, 1, d), lambda b, i: (b, 0, 0)),
                  _const_spec((1, d)),
                  pl.BlockSpec(memory_space=pl.ANY)],
        out_specs=tok,
        out_shape=jax.ShapeDtypeStruct((bsz, s, d), F32),
        scratch_shapes=[pltpu.VMEM((2 * _local_rows(tm), PACK_ROWS, LANES), U32),
                        pltpu.VMEM((TOP_K, tm, PACK_ROWS, LANES), U32),
                        pltpu.VMEM((TOP_K, tm * PACK_ROWS, LANES), U32), pltpu.SemaphoreType.DMA((2,))],
        compiler_params=_params("arbitrary", "arbitrary"),
        name="moe_combine",
    )(lpos, lpos, tab, tab, tab, x1, shared, wtok, gate_f, g_post, ys)


def _block_diag_tiles(w):
    per = MXU_DIM // RNN_BLOCK_DIM
    nt = RNN_BLOCKS // per
    w4 = w.reshape(nt, per, RNN_BLOCK_DIM, RNN_BLOCK_DIM)
    eye = jnp.eye(per, dtype=w.dtype)
    return jnp.einsum("tnij,nm->tnimj", w4, eye).reshape(nt, MXU_DIM, MXU_DIM)


def _tile(n, pref):
    t = min(n, pref)
    assert n % t == 0, (n, t)
    return t


def _layer(x, c, w_ada, b_ada, g_pre_mix, g_post_mix, g_pre_ffn, g_post_ffn, w_in, b_forget,
           w_conv, b_conv, w_rg, b_rg, w_ig, b_ig, lam, w_branch_attn, w_branch_rnn, w_out,
           w_router, router_bias, w_exp_gate, w_exp_up, w_exp_down, w_sh_gate, w_sh_up, w_sh_down):
    bsz, s, d = x.shape
    t = bsz * s
    row = lambda v: v.reshape(1, -1)

    mod = _ada_call(c, w_ada, b_ada)
    shift_m, scale_m, gate_m, shift_f, scale_f, gate_f = [
        mod[:, j * d:(j + 1) * d].reshape(bsz, 1, d) for j in range(6)]

    o_f = 3 * ATT_WIDTH
    o_r = o_f + ATT_HEADS
    wqkv = w_in[:, :o_f].astype(BF16)
    wf = jnp.pad(w_in[:, o_f:o_r], ((0, 0), (0, LANES - ATT_HEADS))).astype(BF16)
    wr = w_in[:, o_r:].astype(BF16)
    bf_pad = jnp.pad(b_forget, (0, LANES - ATT_HEADS)).reshape(1, LANES)
    q, k, v, kbias, x_rnn, g_rnn, gate_a, gate_r = _inproj_call(
        x, shift_m, scale_m, row(g_pre_mix), wqkv, wf, wr, bf_pad, _tile(s, 512))
    tq = _tile(s, 512)
    y_attn = _attn_call(q, k, v, kbias, tq, _tile(tq, 512))

    wbd = jnp.concatenate([_block_diag_tiles(w_rg), _block_diag_tiles(w_ig)], axis=-1).astype(BF16)
    y_rnn = _rnn_call(x_rnn, g_rnn, w_conv, row(b_conv), wbd, row(b_rg), row(b_ig), row(lam), _tile(s, 512))

    wr_t = w_router.T
    wr_hi = wr_t.astype(BF16)
    wr_mid = (wr_t - wr_hi.astype(F32)).astype(BF16)
    tm_m = _tile(s, 512)
    tm_r = _tile(tm_m, 256)
    x1, h2_rows, lpos, before, n_tile, wtok, counts = _merge_call(
        y_attn, y_rnn, gate_a, gate_r, x, gate_m, shift_f, scale_f, row(g_post_mix), row(g_pre_ffn),
        w_branch_attn.astype(BF16), w_branch_rnn.astype(BF16), w_out.astype(BF16),
        wr_hi, wr_mid, router_bias.reshape(N_EXPERTS, 1), tm_m, tm_r)

    rows = 1024
    n_blocks =-(-(t * TOP_K + N_EXPERTS * WINDOW_ROWS) // rows) + N_EXPERTS
    cnt = counts[:, 0].astype(I32)
    padded = ((cnt + WINDOW_ROWS + rows - 1) // rows) * rows
    pend = jnp.cumsum(padded).astype(I32)
    pstart = pend - padded
    n_used = jnp.maximum(pend[-1] // rows, 1)
    blk = jnp.minimum(jnp.arange(n_blocks, dtype=I32), n_used - 1)
    block_e = jnp.minimum(jnp.sum((pend[None, :] <= (blk * rows)[:, None]).astype(I32), axis=1), N_EXPERTS - 1)

    n_win = (n_tile[:, :, 0] + WINDOW_ROWS - 1) // WINDOW_ROWS
    win_end = jnp.cumsum(n_win, axis=1)
    w_id = jnp.arange(WIN_LIST, dtype=I32)[None, :, None]
    owner = (jnp.sum((win_end[:, None, :] <= w_id).astype(I32), axis=2, keepdims=True)
             == jnp.arange(N_EXPERTS, dtype=I32)[None, None, :]).astype(I32)
    first_slot = pstart[None, :] + before[:, :, 0] - (win_end - n_win) * WINDOW_ROWS
    wlist = jnp.sum(owner * first_slot[:, None, :], axis=2) + w_id[:, :, 0] * WINDOW_ROWS
    wlist = jnp.where(w_id[:, :, 0] == WIN_LIST - 1, win_end[:, -1:], wlist).astype(I32)[:, None, :]
    n_rt = lpos.shape[0]
    lpos = lpos + (jnp.arange(n_rt, dtype=I32) % 2)[:, None, None] * _local_rows(tm_r)

    rows3 = lambda a: a.reshape(-1, PACK_ROWS, LANES)
    rows2 = lambda a: a.reshape(-1, LANES)
    zero_from = jnp.concatenate([pstart + cnt // WINDOW_ROWS * WINDOW_ROWS, pend[-1:]]).astype(I32)
    zero_to = jnp.concatenate([pend, jnp.full((1,), n_blocks * rows, I32)]).astype(I32)
    xs, shared = _dispatch_call(lpos, wlist, zero_from, zero_to, h2_rows,
                                w_sh_gate.astype(BF16), w_sh_up.astype(BF16), w_sh_down.astype(BF16),
                                n_blocks * rows, tm_r)
    ys = rows3(_expert_call(block_e, n_used.reshape(1), rows2(xs),
                            w_exp_gate, w_exp_up, w_exp_down, rows))
    return _final_call(lpos, wlist, x1, shared.reshape(bsz, s, d), wtok, gate_f, row(g_post_ffn), ys, tm_r)


def kernel(x, c, w_ada, b_ada, g_pre_mix, g_post_mix, g_pre_ffn, g_post_ffn, w_in, b_forget, w_conv, b_conv, w_rg, b_rg, w_ig, b_ig, rglru_lambda, w_branch_attn, w_branch_rnn, w_out, w_router, router_bias, w_exp_gate, w_exp_up, w_exp_down, w_sh_gate, w_sh_up, w_sh_down):
    depth = w_ada.shape[0]
    for l in range(depth):
        x = _layer(x, c, w_ada[l], b_ada[l], g_pre_mix[l], g_post_mix[l], g_pre_ffn[l], g_post_ffn[l],
                   w_in[l], b_forget[l], w_conv[l], b_conv[l], w_rg[l], b_rg[l], w_ig[l], b_ig[l],
                   rglru_lambda[l], w_branch_attn[l], w_branch_rnn[l], w_out[l], w_router[l],
                   router_bias[l], w_exp_gate[l], w_exp_up[l], w_exp_down[l],
                   w_sh_gate[l], w_sh_up[l], w_sh_down[l])
    return x
```

```python
import functools

import jax
import jax.numpy as jnp
from jax import lax
from jax.experimental import pallas as pl
from jax.experimental.pallas import tpu as pltpu

F32 = jnp.float32
BF16 = jnp.bfloat16
I32 = jnp.int32

D_MODEL = 1024
ATT_HEADS = 8
ATT_HEAD_DIM = 64
ATT_WIDTH = ATT_HEADS * ATT_HEAD_DIM
RNN_WIDTH = D_MODEL
RNN_BLOCKS = 16
RNN_BLOCK_DIM = RNN_WIDTH // RNN_BLOCKS
CONV_WIDTH = 4
RGLRU_C = 8.0
N_EXPERTS = 64
TOP_K = 6
N_GROUPS = 8
GROUP_SIZE = N_EXPERTS // N_GROUPS
TOPK_GROUPS = 4
EXPERT_FF = D_MODEL // 4
ROUTE_SCALE = 2.5
NORM_EPS = 1e-6

LANES = 128
SUBLANES = 8
MXU_DIM = 256
VMEM_LIMIT_BYTES = 56 * 1024 * 1024

NEG_BIG = -1e30


def _sigmoid(x):
    return 0.5 * jnp.tanh(0.5 * x) + 0.5


def _rms_scale(x):
    return x * lax.rsqrt(jnp.mean(x * x, axis=-1, keepdims=True) + NORM_EPS)


def _dot(a, b):
    return jnp.dot(a, b, preferred_element_type=F32)


def _dot_nt(a, b):
    return lax.dot_general(a, b, (((1,), (1,)), ((), ())), preferred_element_type=F32)


def _split3(a):
    hi = a.astype(BF16)
    r1 = a - hi.astype(F32)
    mid = r1.astype(BF16)
    lo = (r1 - mid.astype(F32)).astype(BF16)
    return hi, mid, lo


def _params(*sem):
    return pltpu.CompilerParams(dimension_semantics=sem, vmem_limit_bytes=VMEM_LIMIT_BYTES)


def _const_spec(shape):
    nd = len(shape)
    return pl.BlockSpec(shape, lambda *_: (0,) * nd)


def _ada_kernel(c_ref, w_ref, b_ref, o_ref):
    c = c_ref[...]
    sc = c * _sigmoid(c)
    s_hi, s_mid, _ = _split3(sc)
    w_hi, w_mid, _ = _split3(w_ref[...])
    acc = _dot(s_hi, w_hi) + _dot(s_hi, w_mid) + _dot(s_mid, w_hi)
    o_ref[...] = acc + b_ref[...]


def _ada_call(c, w_ada, b_ada):
    bsz, d = c.shape
    n = w_ada.shape[1]
    tn = d
    return pl.pallas_call(
        _ada_kernel,
        grid=(n // tn,),
        in_specs=[
            pl.BlockSpec((bsz, d), lambda j: (0, 0)),
            pl.BlockSpec((d, tn), lambda j: (0, j)),
            pl.BlockSpec((1, tn), lambda j: (0, j)),
        ],
        out_specs=pl.BlockSpec((bsz, tn), lambda j: (0, j)),
        out_shape=jax.ShapeDtypeStruct((bsz, n), F32),
        compiler_params=_params("arbitrary"),
        name="ada_mod",
    )(c, w_ada, b_ada.reshape(1, n))


N_BIAS_PIECES = 3
CUM_BLOCK = 256


def _bias_placement():
    h = jnp.arange(LANES)[:, None]
    lane = jnp.arange(ATT_WIDTH)[None, :]
    mats = []
    for piece in range(N_BIAS_PIECES):
        target = (h // 2) * LANES + ATT_HEAD_DIM * (1 - h % 2) + piece
        mats.append(((lane == target) & (h < ATT_HEADS)).astype(BF16))
    return jnp.concatenate(mats, axis=0)


def _key_bias(f_logit, bf_ref, place_ref, kb_ref, cum_sc):
    tm = f_logit.shape[0]
    blk = min(tm, CUM_BLOCK)
    row = lax.broadcasted_iota(I32, (blk, blk), 0)
    col = lax.broadcasted_iota(I32, (blk, blk), 1)
    tri = (row >= col).astype(BF16)
    carry = cum_sc[...]
    for i in range(tm // blk):
        z = f_logit[i * blk:(i + 1) * blk, :] + bf_ref[...]
        lf = jnp.minimum(z, 0.0) - jnp.log1p(jnp.exp(-jnp.abs(z)))
        c3 = _dot(tri, jnp.concatenate(_split3(lf), axis=1))
        c = c3[:, :LANES] + c3[:, LANES:2 * LANES] + c3[:, 2 * LANES:] + carry
        carry = c[blk - 1:blk, :]
        kb = _dot(jnp.concatenate(_split3(-c), axis=1), place_ref[...])
        kb_ref[0, i * blk:(i + 1) * blk, :] = kb.astype(BF16)
    cum_sc[...] = carry


def _inproj_kernel(x_ref, sh_ref, sc_ref, g_ref, wqkv_ref, wf_ref, wr_ref, bf_ref, place_ref,
                   q_ref, k_ref, v_ref, kb_ref, xr_ref, gr_ref, ga_ref, gb_ref, cum_sc):
    @pl.when(pl.program_id(1) == 0)
    def _():
        cum_sc[...] = jnp.zeros(cum_sc.shape, F32)

    x = x_ref[0]
    h = (_rms_scale(x) * g_ref[...]) * (1.0 + sc_ref[0]) + sh_ref[0]
    hb = h.astype(BF16)
    for j, ref in enumerate((q_ref, k_ref, v_ref)):
        ref[0] = _dot(hb, wqkv_ref[:, j * ATT_WIDTH:(j + 1) * ATT_WIDTH]).astype(BF16)
    _key_bias(_dot(hb, wf_ref[...]), bf_ref, place_ref, kb_ref, cum_sc)
    half = D_MODEL // 2
    for j, ref in enumerate((xr_ref, gr_ref, ga_ref, gb_ref)):
        for c in range(2):
            lo = j * D_MODEL + c * half
            ref[0, :, c * half:(c + 1) * half] = _dot(hb, wr_ref[:, lo:lo + half]).astype(BF16)


def _inproj_call(x, shift, scale, g, wqkv, wf, wr, b_forget_pad, tm):
    bsz, s, d = x.shape
    nt = s // tm
    tok = lambda w: pl.BlockSpec((1, tm, w), lambda b, i: (b, i, 0))
    mod = pl.BlockSpec((1, 1, d), lambda b, i: (b, 0, 0))
    place = _bias_placement()
    out_w = (ATT_WIDTH,) * 4 + (d,) * 4
    return pl.pallas_call(
        _inproj_kernel,
        grid=(bsz, nt),
        in_specs=[tok(d), mod, mod, _const_spec((1, d)),
                  _const_spec(wqkv.shape), _const_spec(wf.shape), _const_spec(wr.shape),
                  _const_spec((1, LANES)), _const_spec(place.shape)],
        out_specs=[tok(w) for w in out_w],
        out_shape=[jax.ShapeDtypeStruct((bsz, s, w), BF16) for w in out_w],
        scratch_shapes=[pltpu.VMEM((1, LANES), F32)],
        compiler_params=_params("arbitrary", "arbitrary"),
        name="in_proj",
    )(x, shift, scale, g, wqkv, wf, wr, b_forget_pad, place)


ATT_VT_ROWS = ATT_HEAD_DIM + 16


def _attn_kernel(q_ref, k_ref, v_ref, kb_ref, o_ref, vt_sc, st_sc, m_sc, acc_sc, *, tq, tk):
    s = q_ref.shape[1]
    nq, nk, ratio = s // tq, s // tk, tq // tk
    lane = lax.broadcasted_iota(I32, (1, LANES), 1)
    head_lanes = (lane < ATT_HEAD_DIM, lane >= ATT_HEAD_DIM)
    one_lanes = ((lane >= ATT_HEAD_DIM) & (lane < ATT_HEAD_DIM + N_BIAS_PIECES), lane < N_BIAS_PIECES)
    key_minus_query = (lax.broadcasted_iota(I32, (tk, tq), 0) - lax.broadcasted_iota(I32, (tk, tq), 1))

    vt = v_ref[0].astype(F32).T
    ones = jnp.ones((ATT_VT_ROWS - ATT_HEAD_DIM, s), F32)
    for h in range(2):
        vth = jnp.concatenate([vt[h * ATT_HEAD_DIM:(h + 1) * ATT_HEAD_DIM, :], ones], axis=0).astype(BF16)
        for j in range(nk):
            vt_sc[h, j] = vth[:, j * tk:(j + 1) * tk]

    def q_block(qi):
        q0 = qi * tq
        qs = q_ref[0, pl.ds(q0, tq), :] * (ATT_HEAD_DIM ** -0.5)
        qa = [jnp.where(head_lanes[h], qs, one_lanes[h].astype(BF16)) for h in range(2)]
        m_sc[...] = jnp.full(m_sc.shape, NEG_BIG, F32)
        acc_sc[...] = jnp.zeros(acc_sc.shape, F32)

        def scores(kj):
            k0 = kj * tk
            kb = k_ref[0, pl.ds(k0, tk), :]
            bias = kb_ref[0, pl.ds(k0, tk), :]
            return [_dot_nt(jnp.where(head_lanes[h], kb, bias), qa[h]) for h in range(2)]

        def online_update(st, m_old, acc, vt):
            m_new = jnp.maximum(m_old, jnp.max(st, axis=0, keepdims=True))
            p = jnp.exp(st - m_new).astype(BF16)
            return m_new, jnp.exp(m_old - m_new) * acc + _dot(vt, p)

        def accumulate(kj, masked):
            for h in range(2):
                if not masked:
                    m_sc[h], acc_sc[h] = online_update(st_sc[h], m_sc[h], acc_sc[h], vt_sc[h, kj])
                elif ratio == 1:
                    hk = tk // 2
                    tri = key_minus_query[:hk, :] <= 0
                    st_a = jnp.where(tri, st_sc[h, :hk, :], NEG_BIG)
                    m_a, acc_a = online_update(st_a, m_sc[h], acc_sc[h], vt_sc[h, kj, :, :hk])
                    st_b = jnp.where(tri[:, :hk], st_sc[h, hk:, hk:], NEG_BIG)
                    m_b, acc_b = online_update(st_b, m_a[:, hk:], acc_a[:, hk:], vt_sc[h, kj, :, hk:])
                    m_sc[h] = jnp.concatenate([m_a[:, :hk], m_b], axis=1)
                    acc_sc[h] = jnp.concatenate([acc_a[:, :hk], acc_b], axis=1)
                else:
                    st = jnp.where(key_minus_query <= q0 - kj * tk, st_sc[h], NEG_BIG)
                    m_sc[h], acc_sc[h] = online_update(st, m_sc[h], acc_sc[h], vt_sc[h, kj])

        def put(sts):
            for h in range(2):
                st_sc[h] = sts[h]

        n_full = qi * ratio
        put(scores(0))

        for kj in range(n_full):
            nxt = scores(kj + 1)
            accumulate(kj, False)
            put(nxt)

        for d in range(ratio):
            nxt = scores(n_full + d + 1) if d + 1 < ratio else None
            accumulate(n_full + d, True)
            if nxt is not None:
                put(nxt)

        outs = []
        for h in range(2):
            acc = acc_sc[h]
            outs.append(acc[:ATT_HEAD_DIM, :] * (1.0 / acc[ATT_HEAD_DIM:ATT_HEAD_DIM + 1, :]))
        o_ref[0, pl.ds(q0, tq), :] = jnp.concatenate(outs, axis=0).T.astype(BF16)

    for qi in range(nq):
        q_block(qi)


def _attn_call(q, k, v, kbias, tq, tk):
    bsz, s, _ = q.shape
    qkv = pl.BlockSpec((1, s, LANES), lambda b, p: (b, 0, p))
    return pl.pallas_call(
        functools.partial(_attn_kernel, tq=tq, tk=tk),
        grid=(bsz, ATT_HEADS // 2),
        in_specs=[qkv, qkv, qkv, qkv],
        out_specs=qkv,
        out_shape=jax.ShapeDtypeStruct((bsz, s, ATT_WIDTH), BF16),
        scratch_shapes=[pltpu.VMEM((2, s // tk, ATT_VT_ROWS, tk), BF16), pltpu.VMEM((2, tk, tq), F32),
                        pltpu.VMEM((2, 1, tq), F32), pltpu.VMEM((2, ATT_VT_ROWS, tq), F32)],
        compiler_params=_params("arbitrary", "arbitrary"),
        name="fox_attention",
    )(q, k, v, kbias)


def _gelu_tanh(x):
    c = 0.7978845608028654
    return 0.5 * x * (1.0 + jnp.tanh(c * (x + 0.044715 * (x * x * x))))


def _rnn_kernel(xr_ref, gr_ref, wc_ref, bc_ref, wbd_ref, brg_ref, big_ref, lam_ref, o_ref,
                tail_sc, hc_sc, a_sc, b_sc, h_sc, *, ts):
    @pl.when(pl.program_id(1) == 0)
    def _():
        tail_sc[...] = jnp.zeros(tail_sc.shape, F32)
        hc_sc[...] = jnp.zeros(hc_sc.shape, F32)

    x = xr_ref[0].astype(F32)
    full = jnp.concatenate([tail_sc[...], x], axis=0)
    u = bc_ref[...] + wc_ref[CONV_WIDTH - 1:CONV_WIDTH, :] * x
    for j in range(CONV_WIDTH - 1):
        off = SUBLANES - (CONV_WIDTH - 1) + j
        u = u + wc_ref[j:j + 1, :] * full[off:off + ts, :]
    tail_sc[...] = x[ts - SUBLANES:ts, :]

    lam = lam_ref[...]
    neg_sp = -(jnp.maximum(-lam, 0.0) + jnp.log1p(jnp.exp(-jnp.abs(lam))))
    for jt in range(RNN_WIDTH // MXU_DIM):
        cs = slice(jt * MXU_DIM, (jt + 1) * MXU_DIM)
        uc = u[:, cs]
        g = _dot(uc.astype(BF16), wbd_ref[jt])
        r = _sigmoid(g[:, :MXU_DIM] + brg_ref[:, cs])
        ig = _sigmoid(g[:, MXU_DIM:] + big_ref[:, cs])
        a = jnp.exp(r * (RGLRU_C * neg_sp[:, cs]))
        a_sc[:, cs] = a
        v = 1.0 - a * a
        b_sc[:, cs] = jnp.where(v > 0.0, v * lax.rsqrt(v), 0.0) * (ig * uc)

    srow = lax.broadcasted_iota(I32, (SUBLANES, RNN_WIDTH), 0)

    def scan_body(i, carry):
        r0 = pl.multiple_of(i * SUBLANES, SUBLANES)
        a = a_sc[pl.ds(r0, SUBLANES), :]
        b = b_sc[pl.ds(r0, SUBLANES), :]
        b = b + jnp.where(srow == 0, a * carry, 0.0)
        for sh in (1, 2, 4):
            keep = srow >= sh
            b = jnp.where(keep, b + a * pltpu.roll(b, sh, 0), b)
            if sh < SUBLANES // 2:
                a = jnp.where(keep, a * pltpu.roll(a, sh, 0), a)
        h_sc[pl.ds(r0, SUBLANES), :] = b
        return jnp.broadcast_to(b[SUBLANES - 1:SUBLANES, :], (SUBLANES, RNN_WIDTH))

    hc_sc[...] = lax.fori_loop(0, ts // SUBLANES, scan_body, hc_sc[...], unroll=True)
    o_ref[0] = (h_sc[...] * _gelu_tanh(gr_ref[0].astype(F32))).astype(BF16)


def _rnn_call(x_rnn, g_rnn, w_conv, b_conv, wbd, b_rg, b_ig, lam, ts):
    bsz, s, w = x_rnn.shape
    tok = pl.BlockSpec((1, ts, w), lambda b, i: (b, i, 0))
    return pl.pallas_call(
        functools.partial(_rnn_kernel, ts=ts),
        grid=(bsz, s // ts),
        in_specs=[tok, tok, _const_spec(w_conv.shape), _const_spec((1, w)), _const_spec(wbd.shape),
                  _const_spec((1, w)), _const_spec((1, w)), _const_spec((1, w))],
        out_specs=tok,
        out_shape=jax.ShapeDtypeStruct((bsz, s, w), BF16),
        scratch_shapes=[pltpu.VMEM((SUBLANES, w), F32), pltpu.VMEM((SUBLANES, w), F32),
                        pltpu.VMEM((ts, w), F32), pltpu.VMEM((ts, w), F32), pltpu.VMEM((ts, w), F32)],
        compiler_params=_params("arbitrary", "arbitrary"),
        name="rglru",
    )(x_rnn, g_rnn, w_conv, b_conv, wbd, b_rg, b_ig, lam)


WINDOW_SHIFT = 5
WINDOW_ROWS = 1 << WINDOW_SHIFT


def _route(h2, wr_hi_ref, wr_mid_ref, rbias_ref, cnt_ref, lpos_ref, before_ref, ntile_ref, wtok_ref, j, tm):
    h_hi, h_mid, _ = _split3(h2)
    logits = _dot_nt(wr_hi_ref[...], h_hi) + _dot_nt(wr_hi_ref[...], h_mid) + _dot_nt(wr_mid_ref[...], h_hi)
    scores = _sigmoid(logits)
    sel = scores + rbias_ref[...]

    giota = lax.broadcasted_iota(I32, (N_GROUPS, tm), 0)
    gs = jnp.zeros((N_GROUPS, tm), F32)
    for g in range(N_GROUPS):
        blk = sel[g * GROUP_SIZE:(g + 1) * GROUP_SIZE, :]
        m1 = jnp.max(blk, axis=0, keepdims=True)
        i1 = jnp.min(jnp.where(blk == m1, giota, N_EXPERTS), axis=0, keepdims=True)
        m2 = jnp.max(jnp.where(giota == i1, -jnp.inf, blk), axis=0, keepdims=True)
        gs = jnp.where(giota == g, m1 + m2, gs)

    gsel = jnp.zeros((N_GROUPS, tm), F32)
    for _ in range(TOPK_GROUPS):
        m = jnp.max(gs, axis=0, keepdims=True)
        idx = jnp.min(jnp.where(gs == m, giota, N_EXPERTS), axis=0, keepdims=True)
        hit = giota == idx
        gsel = jnp.where(hit, 1.0, gsel)
        gs = jnp.where(hit, -jnp.inf, gs)

    masked = jnp.concatenate(
        [jnp.where(gsel[g:g + 1, :] > 0.0, sel[g * GROUP_SIZE:(g + 1) * GROUP_SIZE, :], -jnp.inf)
         for g in range(N_GROUPS)], axis=0)

    eiota = lax.broadcasted_iota(I32, (N_EXPERTS, tm), 0)
    chosen = jnp.zeros((N_EXPERTS, tm), F32)
    idxs, wts = [], []
    for _ in range(TOP_K):
        m = jnp.max(masked, axis=0, keepdims=True)
        idx = jnp.min(jnp.where(masked == m, eiota, N_EXPERTS), axis=0, keepdims=True)
        hit = eiota == idx
        wts.append(jnp.sum(jnp.where(hit, scores, 0.0), axis=0, keepdims=True))
        idxs.append(idx)
        chosen = jnp.where(hit, 1.0, chosen)
        masked = jnp.where(hit, -jnp.inf, masked)
    wsum = wts[0]
    for w in wts[1:]:
        wsum = wsum + w

    srow = lax.broadcasted_iota(I32, (tm, tm), 0)
    scol = lax.broadcasted_iota(I32, (tm, tm), 1)
    earlier = (srow < scol).astype(BF16)
    local_rank = _dot(chosen.astype(BF16), earlier)
    n_e = jnp.sum(chosen, axis=1, keepdims=True)
    before_ref[j] = cnt_ref[...].astype(I32)
    ntile_ref[j] = n_e.astype(I32)
    cnt_ref[...] = cnt_ref[...] + n_e
    n_pad = (((n_e.astype(I32) + (WINDOW_ROWS - 1)) >> WINDOW_SHIFT) << WINDOW_SHIFT).astype(F32)
    erow = lax.broadcasted_iota(I32, (N_EXPERTS, N_EXPERTS), 0)
    ecol = lax.broadcasted_iota(I32, (N_EXPERTS, N_EXPERTS), 1)
    off = _dot((ecol < erow).astype(BF16), jnp.broadcast_to(n_pad, (N_EXPERTS, LANES)).astype(BF16))[:, :1]
    local_pos = off + local_rank

    riota = lax.broadcasted_iota(I32, (SUBLANES, tm), 0)
    lpos = jnp.zeros((SUBLANES, tm), I32)
    wrow = jnp.zeros((SUBLANES, tm), F32)
    for r in range(TOP_K):
        lp = jnp.sum(jnp.where(eiota == idxs[r], local_pos, 0.0), axis=0, keepdims=True)
        lpos = jnp.where(riota == r, lp.astype(I32), lpos)
        wrow = jnp.where(riota == r, wts[r] / wsum * ROUTE_SCALE, wrow)
    lpos_ref[j] = lpos
    wpad = jnp.concatenate([wrow, jnp.zeros((LANES - SUBLANES, tm), F32)], axis=0)
    wtok_ref[j * tm:(j + 1) * tm, :] = wpad.T


U32 = jnp.uint32
PACK_ROWS = D_MODEL // (2 * LANES)
HIGH_HALF = 0xFFFF0000


def _store_token_rows(rows_ref, val):
    half = D_MODEL // 2
    n = val.shape[0]
    bits = lax.bitcast_convert_type(val.astype(BF16).astype(F32), U32)
    words = (bits[:, half:] & jnp.uint32(HIGH_HALF)) | (bits[:, :half] >> 16)
    for c in range(PACK_ROWS):
        chunk = words[:, c * LANES:(c + 1) * LANES]
        if len(rows_ref.shape) == 2:
            rows_ref[pl.ds(c, n, stride=PACK_ROWS), :] = chunk
        else:
            rows_ref[:, c, :] = chunk


def _load_token_rows(rows_ref):
    if len(rows_ref.shape) == 2:
        n = rows_ref.shape[0] // PACK_ROWS
        chunks = [rows_ref[pl.ds(c, n, stride=PACK_ROWS), :] for c in range(PACK_ROWS)]
    else:
        chunks = [rows_ref[:, c, :] for c in range(PACK_ROWS)]
    words = jnp.concatenate(chunks, axis=-1)
    lo = lax.bitcast_convert_type(words << 16, F32)
    hi = lax.bitcast_convert_type(words & jnp.uint32(HIGH_HALF), F32)
    return jnp.concatenate([lo, hi], axis=-1)


def _merge_kernel(ya_ref, yr_ref, ga_ref, gb_ref, x_ref, gm_ref, shf_ref, scf_ref, gpost_ref, gpre_ref,
                  wba_ref, wbr_ref, wout_ref, wr_hi_ref, wr_mid_ref, rbias_ref,
                  x1_ref, h2rows_ref, lpos_ref, before_ref, ntile_ref, wtok_ref, cnt_ref, *, tm, tr):
    @pl.when((pl.program_id(0) == 0) & (pl.program_id(1) == 0))
    def _():
        cnt_ref[...] = jnp.zeros(cnt_ref.shape, F32)

    pa = _dot(ya_ref[0], wba_ref[...])
    pr = _dot(yr_ref[0], wbr_ref[...])
    merged = _sigmoid(ga_ref[0].astype(F32)) * pa + _sigmoid(gb_ref[0].astype(F32)) * pr
    o = _dot(merged.astype(BF16), wout_ref[...])
    x1 = x_ref[0] + gm_ref[0] * (_rms_scale(o) * gpost_ref[...])
    x1_ref[0] = x1
    h2 = (_rms_scale(x1) * gpre_ref[...]) * (1.0 + scf_ref[0]) + shf_ref[0]
    _store_token_rows(h2rows_ref, h2)
    for j in range(tm // tr):
        _route(h2[j * tr:(j + 1) * tr, :], wr_hi_ref, wr_mid_ref, rbias_ref, cnt_ref,
               lpos_ref, before_ref, ntile_ref, wtok_ref, j, tr)


def _merge_call(y_attn, y_rnn, gate_a, gate_r, x, gate_m, shift_f, scale_f, g_post, g_pre,
                wba, wbr, wout, wr_hi, wr_mid, rbias, tm, tr):
    bsz, s, d = x.shape
    nt = s // tm
    nr = tm // tr
    tok = lambda w: pl.BlockSpec((1, tm, w), lambda b, i: (b, i, 0))
    mod = pl.BlockSpec((1, 1, d), lambda b, i: (b, 0, 0))
    return pl.pallas_call(
        functools.partial(_merge_kernel, tm=tm, tr=tr),
        grid=(bsz, nt),
        in_specs=[tok(ATT_WIDTH), tok(d), tok(d), tok(d), tok(d), mod, mod, mod,
                  _const_spec((1, d)), _const_spec((1, d)),
                  _const_spec(wba.shape), _const_spec(wbr.shape), _const_spec(wout.shape),
                  _const_spec(wr_hi.shape), _const_spec(wr_mid.shape), _const_spec(rbias.shape)],
        out_specs=[tok(d),
                   pl.BlockSpec((tm * PACK_ROWS, LANES), lambda b, i: (b * nt + i, 0)),
                   pl.BlockSpec((nr, SUBLANES, tr), lambda b, i: (b * nt + i, 0, 0)),
                   pl.BlockSpec((nr, N_EXPERTS, 1), lambda b, i: (b * nt + i, 0, 0)),
                   pl.BlockSpec((nr, N_EXPERTS, 1), lambda b, i: (b * nt + i, 0, 0)),
                   pl.BlockSpec((tm, LANES), lambda b, i: (b * nt + i, 0)),
                   pl.BlockSpec((N_EXPERTS, 1), lambda b, i: (0, 0))],
        out_shape=[jax.ShapeDtypeStruct((bsz, s, d), F32),
                   jax.ShapeDtypeStruct((bsz * s * PACK_ROWS, LANES), U32),
                   jax.ShapeDtypeStruct((bsz * nt * nr, SUBLANES, tr), I32),
                   jax.ShapeDtypeStruct((bsz * nt * nr, N_EXPERTS, 1), I32),
                   jax.ShapeDtypeStruct((bsz * nt * nr, N_EXPERTS, 1), I32),
                   jax.ShapeDtypeStruct((bsz * s, LANES), F32),
                   jax.ShapeDtypeStruct((N_EXPERTS, 1), F32)],
        compiler_params=_params("arbitrary", "arbitrary"),
        name="merge_route",
    )(y_attn, y_rnn, gate_a, gate_r, x, gate_m, shift_f, scale_f, g_post, g_pre,
      wba, wbr, wout, wr_hi, wr_mid, rbias)


WIN_LIST = 2 * LANES
ROW_COPY_UNROLL = 8


def _local_rows(tm):
    rows = tm * TOP_K + N_EXPERTS * WINDOW_ROWS
    assert rows // WINDOW_ROWS < WIN_LIST
    return rows


def _for_each_window(wlist_ref, fn):
    def body(w, c):
        fn(pl.multiple_of(w * WINDOW_ROWS, WINDOW_ROWS), wlist_ref[0, 0, w])
        return c

    lax.fori_loop(0, wlist_ref[0, 0, WIN_LIST - 1], body, 0)


def _dispatch_kernel(lpos_ref, wlist_ref, wlistp_ref, wlistpp_ref, zfrom_ref, zto_ref, h_ref, h2d_ref,
                     wsg_ref, wsu_ref, wsd_ref, xs_hbm, shared_ref, loc_sc, zero_sc, sem, *, tm):
    g = pl.program_id(0)
    cur = lax.rem(g, 2)

    @pl.when(g == 0)
    def _():
        loc_sc[...] = jnp.zeros(loc_sc.shape, U32)
        zero_sc[...] = jnp.zeros(zero_sc.shape, U32)

        def for_each_zero_window(fn):
            def per_range(j, c):
                def per_window(w, c2):
                    r0 = pl.multiple_of(zfrom_ref[j] + w * WINDOW_ROWS, WINDOW_ROWS)
                    fn(pltpu.make_async_copy(zero_sc, xs_hbm.at[pl.ds(r0, WINDOW_ROWS)], sem))
                    return c2
                lax.fori_loop(0, (zto_ref[j] - zfrom_ref[j]) >> WINDOW_SHIFT, per_window, 0)
                return c
            lax.fori_loop(0, zfrom_ref.shape[0], per_range, 0)

        for_each_zero_window(lambda cp: cp.start())
        for_each_zero_window(lambda cp: cp.wait())

    half = _local_rows(tm)

    def window(base):
        return lambda row0, slot0: pltpu.make_async_copy(
            loc_sc.at[pl.ds(pl.multiple_of(base + row0, WINDOW_ROWS), WINDOW_ROWS)],
            xs_hbm.at[pl.ds(slot0, WINDOW_ROWS)], sem)

    def start_windows(wl_ref, buf):
        _for_each_window(wl_ref, lambda r, s: window(buf * half)(r, s).start())

    def wait_windows(wl_ref, buf):
        _for_each_window(wl_ref, lambda r, s: window(buf * half)(r, s).wait())

    @pl.when(g > 1)
    def _():
        wait_windows(wlistpp_ref, cur)

    @pl.when(g > 0)
    def _():
        start_windows(wlistp_ref, 1 - cur)

    for t in range(tm):
        row = h_ref[t]
        for k in range(TOP_K):
            loc_sc[lpos_ref[0, k, t]] = row

    hb = _load_token_rows(h2d_ref).astype(BF16)
    a = _dot(hb, wsg_ref[...])
    u = _dot(hb, wsu_ref[...])
    shared_ref[...] = _dot(((a * _sigmoid(a)) * u).astype(BF16), wsd_ref[...]).astype(BF16)

    @pl.when(g == pl.num_programs(0) - 1)
    def _():
        @pl.when(g > 0)
        def _():
            wait_windows(wlistp_ref, 1 - cur)

        start_windows(wlist_ref, cur)
        wait_windows(wlist_ref, cur)


def _dispatch_call(lpos, wlist, zero_from, zero_to, h2_rows, wsg, wsu, wsd, n_rows, tm):
    t = h2_rows.shape[0] // PACK_ROWS
    smem = functools.partial(pl.BlockSpec, memory_space=pltpu.SMEM)
    return pl.pallas_call(
        functools.partial(_dispatch_kernel, tm=tm),
        grid=(t // tm,),
        in_specs=[smem((1, SUBLANES, tm), lambda i: (i, 0, 0)),
                  smem((1, 1, WIN_LIST), lambda i: (i, 0, 0)),
                  smem((1, 1, WIN_LIST), lambda i: (jnp.maximum(i - 1, 0), 0, 0)),
                  smem((1, 1, WIN_LIST), lambda i: (jnp.maximum(i - 2, 0), 0, 0)),
                  smem(), smem(),
                  pl.BlockSpec((tm, PACK_ROWS, LANES), lambda i: (i, 0, 0)),
                  pl.BlockSpec((tm * PACK_ROWS, LANES), lambda i: (i, 0)),
                  _const_spec(wsg.shape), _const_spec(wsu.shape), _const_spec(wsd.shape)],
        out_specs=[pl.BlockSpec(memory_space=pl.ANY), pl.BlockSpec((tm, D_MODEL), lambda i: (i, 0))],
        out_shape=[jax.ShapeDtypeStruct((n_rows, PACK_ROWS, LANES), U32),
                   jax.ShapeDtypeStruct((t, D_MODEL), BF16)],
        scratch_shapes=[pltpu.VMEM((2 * _local_rows(tm), PACK_ROWS, LANES), U32),
                        pltpu.VMEM((WINDOW_ROWS, PACK_ROWS, LANES), U32), pltpu.SemaphoreType.DMA],
        compiler_params=_params("arbitrary"),
        name="moe_dispatch",
    )(lpos, wlist, wlist, wlist, zero_from, zero_to, h2_rows.reshape(-1, PACK_ROWS, LANES), h2_rows, wsg, wsu, wsd)


def _expert_kernel(be_ref, nused_ref, xs_ref, wg_ref, wu_ref, wd_ref, ys_ref, wgb_sc, wub_sc, wdb_sc, *, rows):
    i = pl.program_id(0)

    @pl.when((i == 0) | (be_ref[i] != be_ref[jnp.maximum(i - 1, 0)]))
    def _():
        wgb_sc[...] = wg_ref[0].astype(BF16)
        wub_sc[...] = wu_ref[0].astype(BF16)
        wdb_sc[...] = wd_ref[0].astype(BF16)

    @pl.when(i < nused_ref[0])
    def _():
        xb = _load_token_rows(xs_ref).astype(BF16)
        g = _dot(xb, wgb_sc[...])
        u = _dot(xb, wub_sc[...])
        act = (g * _sigmoid(g)) * u
        _store_token_rows(ys_ref, _dot(act.astype(BF16), wdb_sc[...]))

    @pl.when(pl.program_id(0) >= nused_ref[0])
    def _():
        ys_ref[...] = jnp.zeros(ys_ref.shape, U32)


def _expert_call(block_e, n_used, xs, wg, wu, wd, rows):
    n_rows = xs.shape[0] // PACK_ROWS
    _, d, ff = wg.shape
    row_spec = pl.BlockSpec((rows * PACK_ROWS, LANES), lambda i, be, nu: (jnp.minimum(i, nu[0] - 1), 0))
    grid_spec = pltpu.PrefetchScalarGridSpec(
        num_scalar_prefetch=2,
        grid=(n_rows // rows,),
        in_specs=[row_spec,
                  pl.BlockSpec((1, d, ff), lambda i, be, nu: (be[i], 0, 0)),
                  pl.BlockSpec((1, d, ff), lambda i, be, nu: (be[i], 0, 0)),
                  pl.BlockSpec((1, ff, d), lambda i, be, nu: (be[i], 0, 0))],
        out_specs=pl.BlockSpec((rows * PACK_ROWS, LANES), lambda i, be, nu: (i, 0)),
        scratch_shapes=[pltpu.VMEM((d, ff), BF16), pltpu.VMEM((d, ff), BF16), pltpu.VMEM((ff, d), BF16)],
    )
    return pl.pallas_call(
        functools.partial(_expert_kernel, rows=rows),
        grid_spec=grid_spec,
        out_shape=jax.ShapeDtypeStruct(xs.shape, U32),
        compiler_params=_params("arbitrary"),
        name="moe_experts",
    )(block_e, n_used, xs, wg, wu, wd)


def _final_kernel(lpos0_ref, lposn_ref, tab0_ref, tabw_ref, tabi_ref, x1_ref, shared_ref, wtok_ref, gf_ref,
                  gpost_ref, ys_hbm, o_ref, win_sc, gnext_sc, gcur_sc, sem, *, tm):
    nt = pl.num_programs(1)
    g = pl.program_id(0) * nt + pl.program_id(1)
    n_tiles = pl.num_programs(0) * nt
    buf_next = lax.rem(g + 1, 2)
    buf_after = lax.rem(g, 2)
    half = _local_rows(tm)

    def window(buf):
        return lambda row0, slot0: pltpu.make_async_copy(
            ys_hbm.at[pl.ds(slot0, WINDOW_ROWS)],
            win_sc.at[pl.ds(pl.multiple_of(buf * half + row0, WINDOW_ROWS), WINDOW_ROWS)], sem.at[buf])

    def start_windows(tab_ref, buf):
        _for_each_window(tab_ref, lambda r, s: window(buf)(r, s).start())

    def wait_windows(tab_ref, buf):
        _for_each_window(tab_ref, lambda r, s: window(buf)(r, s).wait())

    def pick_rows(lpos_ref, t):
        for k in range(TOP_K):
            gnext_sc[k, t] = win_sc[lpos_ref[0, k, t]]

    def stage():
        for k in range(TOP_K):
            gcur_sc[k] = gnext_sc[k].reshape(tm * PACK_ROWS, LANES)

    @pl.when(g == 0)
    def _():
        start_windows(tab0_ref, 0)
        wait_windows(tab0_ref, 0)

        def body(i, c):
            for u in range(ROW_COPY_UNROLL):
                pick_rows(lpos0_ref, i * ROW_COPY_UNROLL + u)
            return c

        lax.fori_loop(0, tm // ROW_COPY_UNROLL, body, 0)
        stage()

        @pl.when(n_tiles > 1)
        def _():
            start_windows(tabw_ref, 1)

    @pl.when(g + 1 < n_tiles)
    def _():
        wait_windows(tabw_ref, buf_next)

    @pl.when(g + 2 < n_tiles)
    def _():
        start_windows(tabi_ref, buf_after)

    for t in range(tm):
        pick_rows(lposn_ref, t)

    w = wtok_ref[...]
    routed = w[:, 0:1] * _load_token_rows(gcur_sc.at[0])
    for k in range(1, TOP_K):
        routed = routed + w[:, k:k + 1] * _load_token_rows(gcur_sc.at[k])
    tot = routed + shared_ref[0].astype(F32)
    o_ref[0] = x1_ref[0] + gf_ref[0] * (_rms_scale(tot) * gpost_ref[...])

    stage()


def _final_call(lpos, tab, x1, shared, wtok, gate_f, g_post, ys, tm):
    bsz, s, d = x1.shape
    nt = s // tm
    n_tiles = bsz * nt
    smem = functools.partial(pl.BlockSpec, memory_space=pltpu.SMEM)
    tok = pl.BlockSpec((1, tm, d), lambda b, i: (b, i, 0))
    tab_shape = (1, 1, WIN_LIST)
    ahead = lambda k: (lambda b, i: (jnp.minimum(b * nt + i + k, n_tiles - 1), 0, 0))
    return pl.pallas_call(
        functools.partial(_final_kernel, tm=tm),
        grid=(bsz, nt),
        in_specs=[smem((1, SUBLANES, tm), lambda b, i: (0, 0, 0)),
                  smem((1, SUBLANES, tm), ahead(1)),
                  smem(tab_shape, lambda b, i: (0, 0, 0)),
                  smem(tab_shape, ahead(1)),
                  smem(tab_shape, ahead(2)),
                  tok, tok,
                  pl.BlockSpec((tm, LANES), lambda b, i: (b * nt + i, 0)),
                  pl.BlockSpec((1, 1, d), lambda b, i: (b, 0, 0)),
                  _const_spec((1, d)),
                  pl.BlockSpec(memory_space=pl.ANY)],
        out_specs=tok,
        out_shape=jax.ShapeDtypeStruct((bsz, s, d), F32),
        scratch_shapes=[pltpu.VMEM((2 * _local_rows(tm), PACK_ROWS, LANES), U32),
                        pltpu.VMEM((TOP_K, tm, PACK_ROWS, LANES), U32),
                        pltpu.VMEM((TOP_K, tm * PACK_ROWS, LANES), U32), pltpu.SemaphoreType.DMA((2,))],
        compiler_params=_params("arbitrary", "arbitrary"),
        name="moe_combine",
    )(lpos, lpos, tab, tab, tab, x1, shared, wtok, gate_f, g_post, ys)


def _block_diag_tiles(w):
    per = MXU_DIM // RNN_BLOCK_DIM
    nt = RNN_BLOCKS // per
    w4 = w.reshape(nt, per, RNN_BLOCK_DIM, RNN_BLOCK_DIM)
    eye = jnp.eye(per, dtype=w.dtype)
    return jnp.einsum("tnij,nm->tnimj", w4, eye).reshape(nt, MXU_DIM, MXU_DIM)


def _tile(n, pref):
    t = min(n, pref)
    assert n % t == 0, (n, t)
    return t


def _layer(x, c, w_ada, b_ada, g_pre_mix, g_post_mix, g_pre_ffn, g_post_ffn, w_in, b_forget,
           w_conv, b_conv, w_rg, b_rg, w_ig, b_ig, lam, w_branch_attn, w_branch_rnn, w_out,
           w_router, router_bias, w_exp_gate, w_exp_up, w_exp_down, w_sh_gate, w_sh_up, w_sh_down):
    bsz, s, d = x.shape
    t = bsz * s
    row = lambda v: v.reshape(1, -1)

    mod = _ada_call(c, w_ada, b_ada)
    shift_m, scale_m, gate_m, shift_f, scale_f, gate_f = [
        mod[:, j * d:(j + 1) * d].reshape(bsz, 1, d) for j in range(6)]

    o_f = 3 * ATT_WIDTH
    o_r = o_f + ATT_HEADS
    wqkv = w_in[:, :o_f].astype(BF16)
    wf = jnp.pad(w_in[:, o_f:o_r], ((0, 0), (0, LANES - ATT_HEADS))).astype(BF16)
    wr = w_in[:, o_r:].astype(BF16)
    bf_pad = jnp.pad(b_forget, (0, LANES - ATT_HEADS)).reshape(1, LANES)
    q, k, v, kbias, x_rnn, g_rnn, gate_a, gate_r = _inproj_call(
        x, shift_m, scale_m, row(g_pre_mix), wqkv, wf, wr, bf_pad, _tile(s, 512))
    tq = _tile(s, 512)
    y_attn = _attn_call(q, k, v, kbias, tq, _tile(tq, 512))

    wbd = jnp.concatenate([_block_diag_tiles(w_rg), _block_diag_tiles(w_ig)], axis=-1).astype(BF16)
    y_rnn = _rnn_call(x_rnn, g_rnn, w_conv, row(b_conv), wbd, row(b_rg), row(b_ig), row(lam), _tile(s, 512))

    wr_t = w_router.T
    wr_hi = wr_t.astype(BF16)
    wr_mid = (wr_t - wr_hi.astype(F32)).astype(BF16)
    tm_m = _tile(s, 512)
    tm_r = _tile(tm_m, 512)
    x1, h2_rows, lpos, before, n_tile, wtok, counts = _merge_call(
        y_attn, y_rnn, gate_a, gate_r, x, gate_m, shift_f, scale_f, row(g_post_mix), row(g_pre_ffn),
        w_branch_attn.astype(BF16), w_branch_rnn.astype(BF16), w_out.astype(BF16),
        wr_hi, wr_mid, router_bias.reshape(N_EXPERTS, 1), tm_m, tm_r)

    rows = 1024
    n_blocks =-(-(t * TOP_K + N_EXPERTS * WINDOW_ROWS) // rows) + N_EXPERTS
    cnt = counts[:, 0].astype(I32)
    padded = ((cnt + WINDOW_ROWS + rows - 1) // rows) * rows
    pend = jnp.cumsum(padded).astype(I32)
    pstart = pend - padded
    n_used = jnp.maximum(pend[-1] // rows, 1)
    blk = jnp.minimum(jnp.arange(n_blocks, dtype=I32), n_used - 1)
    block_e = jnp.minimum(jnp.sum((pend[None, :] <= (blk * rows)[:, None]).astype(I32), axis=1), N_EXPERTS - 1)

    n_win = (n_tile[:, :, 0] + WINDOW_ROWS - 1) // WINDOW_ROWS
    win_end = jnp.cumsum(n_win, axis=1)
    w_id = jnp.arange(WIN_LIST, dtype=I32)[None, :, None]
    owner = (jnp.sum((win_end[:, None, :] <= w_id).astype(I32), axis=2, keepdims=True)
             == jnp.arange(N_EXPERTS, dtype=I32)[None, None, :]).astype(I32)
    first_slot = pstart[None, :] + before[:, :, 0] - (win_end - n_win) * WINDOW_ROWS
    wlist = jnp.sum(owner * first_slot[:, None, :], axis=2) + w_id[:, :, 0] * WINDOW_ROWS
    wlist = jnp.where(w_id[:, :, 0] == WIN_LIST - 1, win_end[:, -1:], wlist).astype(I32)[:, None, :]
    n_rt = lpos.shape[0]
    lpos = lpos + (jnp.arange(n_rt, dtype=I32) % 2)[:, None, None] * _local_rows(tm_r)

    rows3 = lambda a: a.reshape(-1, PACK_ROWS, LANES)
    rows2 = lambda a: a.reshape(-1, LANES)
    zero_from = jnp.concatenate([pstart + cnt // WINDOW_ROWS * WINDOW_ROWS, pend[-1:]]).astype(I32)
    zero_to = jnp.concatenate([pend, jnp.full((1,), n_blocks * rows, I32)]).astype(I32)
    xs, shared = _dispatch_call(lpos, wlist, zero_from, zero_to, h2_rows,
                                w_sh_gate.astype(BF16), w_sh_up.astype(BF16), w_sh_down.astype(BF16),
                                n_blocks * rows, tm_r)
    ys = rows3(_expert_call(block_e, n_used.reshape(1), rows2(xs),
                            w_exp_gate, w_exp_up, w_exp_down, rows))
    return _final_call(lpos, wlist, x1, shared.reshape(bsz, s, d), wtok, gate_f, row(g_post_ffn), ys, tm_r)


def kernel(x, c, w_ada, b_ada, g_pre_mix, g_post_mix, g_pre_ffn, g_post_ffn, w_in, b_forget, w_conv, b_conv, w_rg, b_rg, w_ig, b_ig, rglru_lambda, w_branch_attn, w_branch_rnn, w_out, w_router, router_bias, w_exp_gate, w_exp_up, w_exp_down, w_sh_gate, w_sh_up, w_sh_down):
    depth = w_ada.shape[0]
    for l in range(depth):
        x = _layer(x, c, w_ada[l], b_ada[l], g_pre_mix[l], g_post_mix[l], g_pre_ffn[l], g_post_ffn[l],
                   w_in[l], b_forget[l], w_conv[l], b_conv[l], w_rg[l], b_rg[l], w_ig[l], b_ig[l],
                   rglru_lambda[l], w_branch_attn[l], w_branch_rnn[l], w_out[l], w_router[l],
                   router_bias[l], w_exp_gate[l], w_exp_up[l], w_exp_down[l],
                   w_sh_gate[l], w_sh_up[l], w_sh_down[l])
    return x
```
